```python
import jax, jax.numpy as jnp
from jax import lax
import numpy as np

D_MODEL = 1024
BATCH = 1
SEQ = 16384
DEPTH = 2
DEC_BATCH = 32
DEC_SEQ = 1
PAST_LEN = 16384
PAGE_SIZE = 128

N_EVEN = (DEPTH + 1) // 2
N_ODD = DEPTH // 2
RMS_EPS = 1e-6

FOX_HEADS = 8
FOX_HEAD_DIM = 64
FOX_WIDTH = FOX_HEADS * FOX_HEAD_DIM
Q_BLOCK = 128

RWKV_HEADS = 8
RWKV_HEAD_DIM = 64
RWKV_WIDTH = RWKV_HEADS * RWKV_HEAD_DIM
DECAY_LORA = 64
ICLR_LORA = 64
GATE_LORA = 128
RWKV_SPLITS = (RWKV_WIDTH, DECAY_LORA, RWKV_WIDTH, RWKV_WIDTH, ICLR_LORA, GATE_LORA)
RWKV_PROJ = sum(RWKV_SPLITS)
GN_EPS = 64e-5

EVEN_SPLITS = (FOX_WIDTH, FOX_WIDTH, FOX_WIDTH, FOX_HEADS, RWKV_PROJ)
EVEN_IN = sum(EVEN_SPLITS)
EVEN_MIX = FOX_WIDTH + RWKV_WIDTH

D_INNER = 2 * D_MODEL
SSM_HEAD_DIM = 64
SSM_HEADS = D_INNER // SSM_HEAD_DIM
D_STATE = 128
N_GROUPS = 8
HEADS_PER_GROUP = SSM_HEADS // N_GROUPS
CONV_W = 4
CONV_DIM = D_INNER + 2 * N_GROUPS * D_STATE
CHUNK = 128
ODD_IN = D_INNER + CONV_DIM + SSM_HEADS

N_KEYS = 128
N_EXPERTS = N_KEYS * N_KEYS
PEER_HEADS = 8
PEER_TOPK = 16
D_KEY = 128
PEER_BLOCK = 128

kernel_name = "fox_rwkv7_mamba2_peer_hybrid_step"


def _split(x, sizes):
    idx = np.cumsum(sizes)[:-1].tolist()
    return jnp.split(x, idx, axis=-1)


def rmsnorm(x, g):
    xf = x.astype(jnp.float32)
    y = xf * lax.rsqrt(jnp.mean(xf * xf, axis=-1, keepdims=True) + RMS_EPS)
    return (y * g.astype(jnp.float32)).astype(x.dtype)


def fox_attend(q, c_q, q_pos, k, v, c_k):
    B, Lq, H, E = q.shape
    Lk = k.shape[1]
    scale = E ** -0.5
    k_pos = jnp.arange(Lk)
    c_kT = c_k.transpose(0, 2, 1)

    def block(args):
        qb, cqb, pb = args
        s = jnp.einsum('bqhe,bkhe->bhqk', qb, k).astype(jnp.float32) * scale
        s = s + cqb.transpose(0, 2, 1)[..., None] - c_kT[:, :, None, :]
        s = jnp.where(k_pos[None, :] <= pb[:, None], s, -jnp.inf)
        p = jax.nn.softmax(s, axis=-1).astype(v.dtype)
        return jnp.einsum('bhqk,bkhe->bqhe', p, v)

    qb = Q_BLOCK if Lq % Q_BLOCK == 0 else Lq
    nb = Lq // qb
    if nb == 1:
        return block((q, c_q, q_pos))
    qs = q.reshape(B, nb, qb, H, E).swapaxes(0, 1)
    cs = c_q.reshape(B, nb, qb, H).swapaxes(0, 1)
    ps = q_pos.reshape(nb, qb)
    o = lax.map(block, (qs, cs, ps))
    return o.swapaxes(0, 1).reshape(B, Lq, H, E)


def rwkv_mix(p, p_prev0, S0, mu, w0, w2, a0, a2, g2, k_k, k_a, r_k, ln_w, ln_b):
    f32 = jnp.float32
    B, L, _ = p.shape
    H, E = RWKV_HEADS, RWKV_HEAD_DIM
    p = p.astype(f32)
    p_prev = jnp.concatenate([p_prev0[:, None].astype(f32), p[:, :-1]], axis=1)
    xs = p + mu * (p_prev - p)
    r, wl, k, v, al, gl = _split(xs, RWKV_SPLITS)
    w = -jax.nn.softplus(-(w0 + jnp.tanh(wl) @ w2)) - 0.5
    decay = jnp.exp(-jnp.exp(w))
    a = jax.nn.sigmoid(a0 + al @ a2)
    g = jax.nn.sigmoid(gl) @ g2
    heads = lambda t: t.reshape(B, L, H, E)
    kk = heads(k * k_k)
    kk = kk / jnp.maximum(jnp.linalg.norm(kk, axis=-1, keepdims=True), 1e-12)
    k = k * (1.0 + (a - 1.0) * k_a)
    r, k, v, a, decay = heads(r), heads(k), heads(v), heads(a), heads(decay)

    def step(S, inp):
        r_t, k_t, v_t, kk_t, a_t, d_t = inp
        sk = jnp.einsum('bhij,bhj->bhi', S, kk_t)
        S = (S * d_t[:, :, None, :] - sk[..., None] * (kk_t * a_t)[:, :, None, :]
             + v_t[..., None] * k_t[:, :, None, :])
        return S, jnp.einsum('bhij,bhj->bhi', S, r_t)

    seq = tuple(t.swapaxes(0, 1) for t in (r, k, v, kk, a, decay))
    S_T, o = lax.scan(step, S0.astype(f32), seq)
    o = o.swapaxes(0, 1)
    mean = jnp.mean(o, axis=-1, keepdims=True)
    var = jnp.mean(jnp.square(o - mean), axis=-1, keepdims=True)
    o = (o - mean) * lax.rsqrt(var + GN_EPS) * ln_w.reshape(H, E) + ln_b.reshape(H, E)
    o = o + jnp.sum(r * k * r_k, axis=-1, keepdims=True) * v
    out = o.reshape(B, L, RWKV_WIDTH) * g
    return out, p[:, -1], S_T


def even_mixer(h, past, shift0, S0, w_in, w_out, fb, rw):
    f32 = jnp.float32
    B, L, _ = h.shape
    q, k, v, fl, pr = _split(h @ w_in, EVEN_SPLITS)
    heads = lambda t: t.reshape(B, L, FOX_HEADS, FOX_HEAD_DIM)
    q, k, v = heads(q), heads(k), heads(v)
    logf = jax.nn.log_sigmoid(fl.astype(f32) + fb)
    if past is None:
        k_all, v_all, lf_all = k, v, logf
    else:
        k_all = jnp.concatenate([past[0].astype(k.dtype), k], axis=1)
        v_all = jnp.concatenate([past[1].astype(v.dtype), v], axis=1)
        lf_all = jnp.concatenate([past[2].astype(f32), logf], axis=1)
    n_past = k_all.shape[1] - L
    c = jnp.cumsum(lf_all, axis=1)
    o_fox = fox_attend(q, c[:, n_past:], n_past + jnp.arange(L), k_all, v_all, c)
    o_rwkv, shift_new, S_new = rwkv_mix(pr, shift0, S0, *rw)
    o = jnp.concatenate([o_fox.reshape(B, L, FOX_WIDTH), o_rwkv.astype(o_fox.dtype)], axis=-1)
    return o @ w_out, k, v, logf, S_new, shift_new


def ssd(x, dt, A, Bm, Cm, h0):
    f32 = jnp.float32
    Bsz, L = x.shape[:2]
    Q = CHUNK if L % CHUNK == 0 else L
    nc = L // Q
    resh = lambda t: t.reshape((Bsz, nc, Q) + t.shape[2:])
    xc = resh(x.astype(f32) * dt[..., None])
    acs = jnp.cumsum(resh(dt * A), axis=2)
    Bc, Cc = resh(Bm.astype(f32)), resh(Cm.astype(f32))
    tri = jnp.tril(jnp.ones((Q, Q), bool))
    diff = acs[:, :, :, None] - acs[:, :, None, :]
    Lm = jnp.exp(jnp.where(tri[:, :, None, None], diff, -jnp.inf))
    CB = jnp.einsum('bclgn,bcsgn->bclsg', Cc, Bc)
    y_diag = jnp.einsum('bclsg,bclsgr,bcsgrp->bclgrp', CB, Lm, xc)
    decay_states = jnp.exp(acs[:, :, -1:] - acs)
    states = jnp.einsum('bclgn,bclgr,bclgrp->bcgrpn', Bc, decay_states, xc)
    chunk_decay = jnp.exp(acs[:, :, -1])

    def step(h, inp):
        st, dec = inp
        return h * dec[..., None, None] + st, h

    hT, h_prev = lax.scan(step, h0, (states.swapaxes(0, 1), chunk_decay.swapaxes(0, 1)))
    h_prev = h_prev.swapaxes(0, 1)
    y_off = jnp.einsum('bclgn,bcgrpn,bclgr->bclgrp', Cc, h_prev, jnp.exp(acs))
    return (y_diag + y_off).reshape(x.shape), hT


def mamba_mix(u, conv_buf, h0, conv_w, conv_b, dt_bias, a_log, d_skip, norm_w):
    f32 = jnp.float32
    B, L, _ = u.shape
    z, xbc, dt = _split(u, (D_INNER, CONV_DIM, SSM_HEADS))
    full = jnp.concatenate([conv_buf.astype(xbc.dtype), xbc], axis=1)
    conv = conv_b + sum(full[:, j:j + L] * conv_w[j] for j in range(CONV_W))
    new_buf = full[:, L:]
    xbc = jax.nn.silu(conv.astype(f32))
    xs, Bm, Cm = _split(xbc, (D_INNER, N_GROUPS * D_STATE, N_GROUPS * D_STATE))
    xs = xs.reshape(B, L, N_GROUPS, HEADS_PER_GROUP, SSM_HEAD_DIM)
    Bm = Bm.reshape(B, L, N_GROUPS, D_STATE)
    Cm = Cm.reshape(B, L, N_GROUPS, D_STATE)
    dt = jax.nn.softplus(dt.astype(f32) + dt_bias).reshape(B, L, N_GROUPS, HEADS_PER_GROUP)
    A = -jnp.exp(a_log.astype(f32)).reshape(N_GROUPS, HEADS_PER_GROUP)
    h0 = h0.astype(f32).reshape(B, N_GROUPS, HEADS_PER_GROUP, SSM_HEAD_DIM, D_STATE)
    y, hT = ssd(xs, dt, A, Bm, Cm, h0)
    y = y + d_skip.reshape(N_GROUPS, HEADS_PER_GROUP)[..., None] * xs
    y = y.reshape(B, L, D_INNER) * jax.nn.silu(z.astype(f32))
    yg = y.reshape(B, L, N_GROUPS, D_INNER // N_GROUPS)
    yg = yg * lax.rsqrt(jnp.mean(yg * yg, axis=-1, keepdims=True) + RMS_EPS)
    y = yg.reshape(B, L, D_INNER) * norm_w
    return y.astype(u.dtype), new_buf, hT.reshape(B, SSM_HEADS, SSM_HEAD_DIM, D_STATE)


def peer(x, wq, keys, u_tab, v_tab):
    f32 = jnp.float32
    n = x.shape[0]
    blk = PEER_BLOCK if n % PEER_BLOCK == 0 else n

    def one(xb):
        q = (xb @ wq).astype(f32).reshape(-1, PEER_HEADS, 2, D_KEY // 2)
        s = jnp.einsum('nhpk,hpek->nhpe', q, keys.astype(f32))
        v1, i1 = lax.top_k(s[:, :, 0], PEER_TOPK)
        v2, i2 = lax.top_k(s[:, :, 1], PEER_TOPK)
        cand = (v1[..., :, None] + v2[..., None, :]).reshape(-1, PEER_HEADS, PEER_TOPK * PEER_TOPK)
        sc, pos = lax.top_k(cand, PEER_TOPK)
        e1 = jnp.take_along_axis(i1, pos // PEER_TOPK, axis=-1)
        e2 = jnp.take_along_axis(i2, pos % PEER_TOPK, axis=-1)
        idx = e1 * N_KEYS + e2
        g = jax.nn.softmax(sc, axis=-1)
        hid = jax.nn.gelu(jnp.einsum('nd,nhkd->nhk', xb, u_tab[idx]).astype(f32), approximate=False)
        return jnp.einsum('nhk,nhkd->nd', (g * hid).astype(xb.dtype), v_tab[idx])

    if blk == n:
        return one(x)
    return lax.map(one, x.reshape(n // blk, blk, x.shape[1])).reshape(n, x.shape[1])


def setup_inputs(seed: int = 0) -> dict:
    key = jax.random.key(seed)
    ks = iter(jax.random.split(key, 48))
    f32 = jnp.float32
    nrm = lambda shape, s: jax.random.normal(next(ks), shape, f32) * s
    uni = lambda shape, lo, hi: jax.random.uniform(next(ks), shape, f32, lo, hi)
    n_pages = PAST_LEN // PAGE_SIZE
    n_used = DEC_BATCH * n_pages
    n_pool = n_used + max(1, n_used // 4)
    page_table = jax.random.permutation(next(ks), n_pool)[:n_used].reshape(DEC_BATCH, n_pages).astype(jnp.int32)
    dt0 = jnp.exp(uni((N_ODD, SSM_HEADS), float(np.log(1e-3)), float(np.log(1e-1))))
    return {
        'x_prompt': nrm((BATCH, SEQ, D_MODEL), 1.0),
        'x_sample': nrm((DEC_BATCH, DEC_SEQ, D_MODEL), 1.0),
        'cache_k': nrm((N_EVEN, n_pool, PAGE_SIZE, FOX_HEADS, FOX_HEAD_DIM), 1.0),
        'cache_v': nrm((N_EVEN, n_pool, PAGE_SIZE, FOX_HEADS, FOX_HEAD_DIM), 1.0),
        'cache_logf': jax.nn.log_sigmoid(uni((N_EVEN, n_pool, PAGE_SIZE, FOX_HEADS), 1.0, 4.0)
                                         + nrm((N_EVEN, n_pool, PAGE_SIZE, FOX_HEADS), 1.0)),
        'page_table': page_table,
        'state_wkv': nrm((N_EVEN, DEC_BATCH, RWKV_HEADS, RWKV_HEAD_DIM, RWKV_HEAD_DIM), 0.5),
        'state_shift': nrm((N_EVEN, DEC_BATCH, RWKV_PROJ), 1.0),
        'state_conv': nrm((N_ODD, DEC_BATCH, CONV_W - 1, CONV_DIM), 1.0),
        'state_ssm': nrm((N_ODD, DEC_BATCH, SSM_HEADS, SSM_HEAD_DIM, D_STATE), 0.5),
        'norm_mix': 1.0 + nrm((DEPTH, D_MODEL), 0.1),
        'norm_ffn': 1.0 + nrm((DEPTH, D_MODEL), 0.1),
        'norm_final': 1.0 + nrm((D_MODEL,), 0.1),
        'w_in_even': nrm((N_EVEN, D_MODEL, EVEN_IN), D_MODEL ** -0.5),
        'w_out_even': nrm((N_EVEN, EVEN_MIX, D_MODEL), EVEN_MIX ** -0.5),
        'fox_fb': uni((N_EVEN, FOX_HEADS), 1.0, 4.0),
        'rwkv_mu': uni((N_EVEN, RWKV_PROJ), 0.0, 1.0),
        'rwkv_w0': uni((N_EVEN, RWKV_WIDTH), -3.0, 1.0),
        'rwkv_w2': nrm((N_EVEN, DECAY_LORA, RWKV_WIDTH), 0.1),
        'rwkv_a0': nrm((N_EVEN, RWKV_WIDTH), 0.1),
        'rwkv_a2': nrm((N_EVEN, ICLR_LORA, RWKV_WIDTH), 0.1),
        'rwkv_g2': nrm((N_EVEN, GATE_LORA, RWKV_WIDTH), GATE_LORA ** -0.5),
        'rwkv_kk': 1.0 + nrm((N_EVEN, RWKV_WIDTH), 0.1),
        'rwkv_ka': 1.0 + nrm((N_EVEN, RWKV_WIDTH), 0.1),
        'rwkv_rk': nrm((N_EVEN, RWKV_HEADS, RWKV_HEAD_DIM), 0.1),
        'rwkv_lnw': 1.0 + nrm((N_EVEN, RWKV_WIDTH), 0.1),
        'rwkv_lnb': nrm((N_EVEN, RWKV_WIDTH), 0.02),
        'w_in_odd': nrm((N_ODD, D_MODEL, ODD_IN), D_MODEL ** -0.5),
        'w_out_odd': nrm((N_ODD, D_INNER, D_MODEL), D_INNER ** -0.5),
        'ssm_conv_w': nrm((N_ODD, CONV_W, CONV_DIM), CONV_W ** -0.5),
        'ssm_conv_b': nrm((N_ODD, CONV_DIM), 0.02),
        'ssm_dt_bias': dt0 + jnp.log(-jnp.expm1(-dt0)),
        'ssm_a_log': jnp.log(uni((N_ODD, SSM_HEADS), 1.0, 16.0)),
        'ssm_d': 1.0 + nrm((N_ODD, SSM_HEADS), 0.1),
        'ssm_norm_w': 1.0 + nrm((N_ODD, D_INNER), 0.1),
        'peer_wq': nrm((DEPTH, D_MODEL, PEER_HEADS * D_KEY), D_MODEL ** -0.5),
        'peer_keys': nrm((DEPTH, PEER_HEADS, 2, N_KEYS, D_KEY // 2), (D_KEY // 2) ** -0.5),
        'peer_u': nrm((DEPTH, N_EXPERTS, D_MODEL), D_MODEL ** -0.5),
        'peer_v': nrm((DEPTH, N_EXPERTS, D_MODEL), PEER_HEADS ** -0.5),
    }


def reference(x_prompt, x_sample, cache_k, cache_v, cache_logf, page_table,
              state_wkv, state_shift, state_conv, state_ssm,
              norm_mix, norm_ffn, norm_final,
              w_in_even, w_out_even, fox_fb,
              rwkv_mu, rwkv_w0, rwkv_w2, rwkv_a0, rwkv_a2, rwkv_g2,
              rwkv_kk, rwkv_ka, rwkv_rk, rwkv_lnw, rwkv_lnb,
              w_in_odd, w_out_odd, ssm_conv_w, ssm_conv_b, ssm_dt_bias,
              ssm_a_log, ssm_d, ssm_norm_w,
              peer_wq, peer_keys, peer_u, peer_v):
    f32 = jnp.float32
    bp = x_prompt.shape[0]
    n_seq, n_pages = page_table.shape

    def gather_pages(pool):
        rows = pool[page_table]
        return rows.reshape((n_seq, n_pages * pool.shape[1]) + pool.shape[2:])

    yp, ys = x_prompt, x_sample
    kp, vp, lp, wp, sp, cp, hp = [], [], [], [], [], [], []
    ks_, vs_, ls_, ws_, ss_, cs_, hs_ = [], [], [], [], [], [], []
    for layer in range(DEPTH):
        np_ = rmsnorm(yp, norm_mix[layer])
        ns_ = rmsnorm(ys, norm_mix[layer])
        if layer % 2 == 0:
            e = layer // 2
            rw = (rwkv_mu[e], rwkv_w0[e], rwkv_w2[e], rwkv_a0[e], rwkv_a2[e], rwkv_g2[e],
                  rwkv_kk[e], rwkv_ka[e], rwkv_rk[e], rwkv_lnw[e], rwkv_lnb[e])
            zero_shift = jnp.zeros((bp, RWKV_PROJ), f32)
            zero_wkv = jnp.zeros((bp, RWKV_HEADS, RWKV_HEAD_DIM, RWKV_HEAD_DIM), f32)
            mp, k1, v1, l1, S1, sh1 = even_mixer(np_, None, zero_shift, zero_wkv,
                                                 w_in_even[e], w_out_even[e], fox_fb[e], rw)
            past = (gather_pages(cache_k[e]), gather_pages(cache_v[e]), gather_pages(cache_logf[e]))
            ms, k2, v2, l2, S2, sh2 = even_mixer(ns_, past, state_shift[e], state_wkv[e],
                                                 w_in_even[e], w_out_even[e], fox_fb[e], rw)
            kp.append(k1); vp.append(v1); lp.append(l1); wp.append(S1); sp.append(sh1)
            ks_.append(k2); vs_.append(v2); ls_.append(l2); ws_.append(S2); ss_.append(sh2)
        else:
            o = layer // 2
            mprm = (ssm_conv_w[o], ssm_conv_b[o], ssm_dt_bias[o], ssm_a_log[o], ssm_d[o], ssm_norm_w[o])
            zero_conv = jnp.zeros((bp, CONV_W - 1, CONV_DIM), x_prompt.dtype)
            zero_ssm = jnp.zeros((bp, SSM_HEADS, SSM_HEAD_DIM, D_STATE), f32)
            yp_m, c1, h1 = mamba_mix(np_ @ w_in_odd[o], zero_conv, zero_ssm, *mprm)
            ys_m, c2, h2 = mamba_mix(ns_ @ w_in_odd[o], state_conv[o], state_ssm[o], *mprm)
            mp = yp_m @ w_out_odd[o]
            ms = ys_m @ w_out_odd[o]
            cp.append(c1); hp.append(h1); cs_.append(c2); hs_.append(h2)
        yp = yp + mp
        ys = ys + ms
        fp = rmsnorm(yp, norm_ffn[layer]).reshape(-1, D_MODEL)
        fs = rmsnorm(ys, norm_ffn[layer]).reshape(-1, D_MODEL)
        yp = yp + peer(fp, peer_wq[layer], peer_keys[layer], peer_u[layer], peer_v[layer]).reshape(yp.shape)
        ys = ys + peer(fs, peer_wq[layer], peer_keys[layer], peer_u[layer], peer_v[layer]).reshape(ys.shape)
    y_prompt = rmsnorm(yp, norm_final)
    y_sample = rmsnorm(ys, norm_final)
    return (y_prompt, y_sample,
            jnp.stack(kp), jnp.stack(vp), jnp.stack(lp), jnp.stack(wp), jnp.stack(sp),
            jnp.stack(cp), jnp.stack(hp),
            jnp.stack(ks_), jnp.stack(vs_), jnp.stack(ls_), jnp.stack(ws_), jnp.stack(ss_),
            jnp.stack(cs_), jnp.stack(hs_))
```

```python
import functools
import math

import numpy as np
import jax
import jax.numpy as jnp
from jax import lax
from jax.experimental import pallas as pl
from jax.experimental.pallas import tpu as pltpu

F32 = jnp.float32
BF16 = jnp.bfloat16
I32 = jnp.int32

LANES = 128
SUBLANES = 8
V7X_VMEM_BYTES = 64 * 1024 * 1024

RMS_EPS = 1e-6
GN_EPS = 64e-5

FOX_HEADS = 8
FOX_HEAD_DIM = 64
FOX_WIDTH = FOX_HEADS * FOX_HEAD_DIM
RWKV_HEADS = 8
RWKV_HEAD_DIM = 64
RWKV_WIDTH = RWKV_HEADS * RWKV_HEAD_DIM
DECAY_LORA = 64
ICLR_LORA = 64
GATE_LORA = 128
RWKV_PROJ = 4 * RWKV_WIDTH - RWKV_WIDTH + DECAY_LORA + ICLR_LORA + GATE_LORA
HEAD_PAIRS = RWKV_HEADS // 2

SSM_HEAD_DIM = 64
D_STATE = 128
N_GROUPS = 8
CONV_W = 4
CHUNK = 128

N_KEYS = 128
PEER_HEADS = 8
PEER_TOPK = 16
D_KEY = 128


def _cparams(semantics, vmem_mb=48):
    return pltpu.CompilerParams(dimension_semantics=semantics, vmem_limit_bytes=vmem_mb * 1024 * 1024)


def _split3(x):
    hi = x.astype(BF16)
    r1 = x - hi.astype(F32)
    mid = r1.astype(BF16)
    lo = (r1 - mid.astype(F32)).astype(BF16)
    return hi, mid, lo


def _dot_f32_right(x, m_bf16):
    hi, mid, lo = _split3(x)
    d = lambda a: jnp.dot(a, m_bf16, preferred_element_type=F32)
    return d(hi) + d(mid) + d(lo)


def _dot_f32_left(m_bf16, x):
    hi, mid, lo = _split3(x)
    d = lambda a: jnp.dot(m_bf16, a, preferred_element_type=F32)
    return d(hi) + d(mid) + d(lo)


def _sigmoid(x):
    return 1.0 / (1.0 + jnp.exp(-x))


def _softplus(x):
    return jnp.maximum(x, 0.0) + jnp.log1p(jnp.exp(-jnp.abs(x)))


def _silu(x):
    return x * _sigmoid(x)


def _norm_matmul_kernel(x_ref, g_ref, w_ref, o_ref, xn_ref):
    @pl.when(pl.program_id(1) == 0)
    def _():
        x = x_ref[...]
        ms = jnp.mean(x * x, axis=-1, keepdims=True)
        xn_ref[...] = (x * lax.rsqrt(ms + RMS_EPS) * g_ref[...]).astype(BF16)

    o_ref[...] = jnp.dot(xn_ref[...], w_ref[...], preferred_element_type=F32).astype(o_ref.dtype)


def _norm_matmul(x, g, w_bf16, *, tm, tn, out_dtype=F32):
    m, k = x.shape
    n = w_bf16.shape[1]
    assert m % tm == 0 and n % tn == 0
    return pl.pallas_call(
        _norm_matmul_kernel,
        grid=(m // tm, n // tn),
        in_specs=[
            pl.BlockSpec((tm, k), lambda i, j: (i, 0)),
            pl.BlockSpec((1, k), lambda i, j: (0, 0)),
            pl.BlockSpec((k, tn), lambda i, j: (0, j)),
        ],
        out_specs=pl.BlockSpec((tm, tn), lambda i, j: (i, j)),
        out_shape=jax.ShapeDtypeStruct((m, n), out_dtype),
        scratch_shapes=[pltpu.VMEM((tm, k), BF16)],
        compiler_params=_cparams(("parallel", "arbitrary")),
        name="norm_matmul",
    )(x, g.reshape(1, k), w_bf16)


def _matmul_res_kernel(*refs, n_in):
    a_refs = refs[:n_in]
    w_refs = refs[n_in:2 * n_in]
    r_ref, o_ref = refs[2 * n_in], refs[2 * n_in + 1]
    acc = r_ref[...]
    for a_ref, w_ref in zip(a_refs, w_refs):
        acc = acc + jnp.dot(a_ref[...].astype(BF16), w_ref[...], preferred_element_type=F32)
    o_ref[...] = acc


def _matmul_res(a_list, w_list, res, *, tm, tn):
    m, n = res.shape
    n_in = len(a_list)
    assert m % tm == 0 and n % tn == 0
    in_specs = ([pl.BlockSpec((tm, a.shape[1]), lambda i, j: (i, 0)) for a in a_list]
                + [pl.BlockSpec((w.shape[0], tn), lambda i, j: (0, j)) for w in w_list]
                + [pl.BlockSpec((tm, tn), lambda i, j: (i, j))])
    return pl.pallas_call(
        functools.partial(_matmul_res_kernel, n_in=n_in),
        grid=(m // tm, n // tn),
        in_specs=in_specs,
        out_specs=pl.BlockSpec((tm, tn), lambda i, j: (i, j)),
        out_shape=jax.ShapeDtypeStruct((m, n), F32),
        compiler_params=_cparams(("parallel", "parallel")),
        name="matmul_res",
    )(*a_list, *w_list, res)


def _rmsnorm_kernel(x_ref, g_ref, o_ref):
    x = x_ref[...]
    ms = jnp.mean(x * x, axis=-1, keepdims=True)
    o_ref[...] = x * lax.rsqrt(ms + RMS_EPS) * g_ref[...]


def _rmsnorm(x, g, *, tm):
    m, k = x.shape
    return pl.pallas_call(
        _rmsnorm_kernel,
        grid=(m // tm,),
        in_specs=[pl.BlockSpec((tm, k), lambda i: (i, 0)), pl.BlockSpec((1, k), lambda i: (0, 0))],
        out_specs=pl.BlockSpec((tm, k), lambda i: (i, 0)),
        out_shape=jax.ShapeDtypeStruct((m, k), F32),
        compiler_params=_cparams(("parallel",)),
        name="rmsnorm",
    )(x, g.reshape(1, k))


def _fox_prep_kernel(fl_ref, fb_ref, lf_ref, ct_ref, carry_ref, *, t):
    @pl.when(pl.program_id(0) == 0)
    def _():
        carry_ref[...] = jnp.zeros_like(carry_ref)

    x = fl_ref[...] + fb_ref[...]
    lf = jnp.minimum(x, 0.0) - jnp.log1p(jnp.exp(-jnp.abs(x)))
    lf_ref[...] = lf[:, :FOX_HEADS]
    row = lax.broadcasted_iota(I32, (t, t), 0)
    col = lax.broadcasted_iota(I32, (t, t), 1)
    tri = jnp.where(col <= row, 1.0, 0.0).astype(BF16)
    c = _dot_f32_left(tri, lf) + carry_ref[...]
    carry_ref[...] = c[t - 1:t, :]
    ct_ref[...] = c.T[:FOX_HEADS, :]


def _fox_prep(proj, fl_block, fb_pad, *, t):
    n = proj.shape[0]
    return pl.pallas_call(
        functools.partial(_fox_prep_kernel, t=t),
        grid=(n // t,),
        in_specs=[pl.BlockSpec((t, LANES), lambda i: (i, fl_block)),
                  pl.BlockSpec((1, LANES), lambda i: (0, 0))],
        out_specs=[pl.BlockSpec((t, FOX_HEADS), lambda i: (i, 0)),
                   pl.BlockSpec((FOX_HEADS, t), lambda i: (0, i))],
        out_shape=[jax.ShapeDtypeStruct((n, FOX_HEADS), F32),
                   jax.ShapeDtypeStruct((FOX_HEADS, n), F32)],
        scratch_shapes=[pltpu.VMEM((1, LANES), F32)],
        compiler_params=_cparams(("arbitrary",)),
        name="fox_prep",
    )(proj, fb_pad)


def _fox_attn_kernel(q_ref, k_ref, v_ref, ct_ref, o_ref, m_ref, l_ref, acc_ref, *, tq):
    i = pl.program_id(1)
    lane = lax.broadcasted_iota(I32, (1, LANES), 1)
    left = lane < FOX_HEAD_DIM
    q = q_ref[...] * jnp.asarray(FOX_HEAD_DIM ** -0.5, BF16)
    zero = jnp.zeros_like(q)
    qs = (jnp.where(left, q, zero), jnp.where(left, zero, q))
    q0 = pl.multiple_of(i * tq, tq)
    cref = [ct_ref[h:h + 1, pl.ds(q0, LANES)][:, 0:1] for h in range(2)]

    m_ref[...] = jnp.full_like(m_ref, -jnp.inf)
    l_ref[...] = jnp.zeros_like(l_ref)
    acc_ref[...] = jnp.zeros_like(acc_ref)

    def step(j, masked):
        k0 = pl.multiple_of(j * tq, tq)
        kb = k_ref[pl.ds(k0, tq), :]
        vb = v_ref[pl.ds(k0, tq), :]
        pv = []
        alphas = []
        for h in range(2):
            s = lax.dot_general(qs[h], kb, (((1,), (1,)), ((), ())), preferred_element_type=F32)
            s = s + (cref[h] - ct_ref[h:h + 1, pl.ds(k0, tq)])
            if masked:
                r = lax.broadcasted_iota(I32, (tq, tq), 0)
                c = lax.broadcasted_iota(I32, (tq, tq), 1)
                s = jnp.where(c <= r, s, -jnp.inf)
            m_old = m_ref[h]
            m_new = jnp.maximum(m_old, jnp.max(s, axis=-1, keepdims=True))
            alpha = jnp.exp(m_old - m_new)
            p = jnp.exp(s - m_new[:, 0:1])
            l_ref[h] = alpha * l_ref[h] + jnp.sum(p, axis=-1, keepdims=True)
            m_ref[h] = m_new
            pv.append(jnp.dot(p.astype(BF16), vb, preferred_element_type=F32))
            alphas.append(alpha)
        acc_ref[...] = jnp.where(left, alphas[0], alphas[1]) * acc_ref[...] + jnp.where(left, pv[0], pv[1])

    def body(j, carry):
        step(j, False)
        return carry

    lax.fori_loop(0, i, body, 0)
    step(i, True)
    o_ref[...] = (acc_ref[...] / jnp.where(left, l_ref[0], l_ref[1])).astype(o_ref.dtype)


def _fox_attn(qkv_bf16, ct, *, tq):
    n = qkv_bf16.shape[0]
    nb = FOX_WIDTH // LANES
    return pl.pallas_call(
        functools.partial(_fox_attn_kernel, tq=tq),
        grid=(nb, n // tq),
        in_specs=[
            pl.BlockSpec((tq, LANES), lambda p, i: (i, p)),
            pl.BlockSpec((n, LANES), lambda p, i: (0, nb + p)),
            pl.BlockSpec((n, LANES), lambda p, i: (0, 2 * nb + p)),
            pl.BlockSpec((None, 2, n), lambda p, i: (p, 0, 0)),
        ],
        out_specs=pl.BlockSpec((tq, LANES), lambda p, i: (i, p)),
        out_shape=jax.ShapeDtypeStruct((n, FOX_WIDTH), BF16),
        scratch_shapes=[pltpu.VMEM((2, tq, LANES), F32), pltpu.VMEM((2, tq, LANES), F32),
                        pltpu.VMEM((tq, LANES), F32)],
        compiler_params=_cparams(("arbitrary", "arbitrary")),
        name="fox_attn",
    )(qkv_bf16, qkv_bf16, qkv_bf16, ct.reshape(nb, 2, n))


_RW = RWKV_WIDTH
_RWKV_PERM = np.concatenate([
    np.arange(0, _RW),
    np.arange(_RW + DECAY_LORA, 2 * _RW + DECAY_LORA),
    np.arange(2 * _RW + DECAY_LORA, 3 * _RW + DECAY_LORA),
    np.arange(3 * _RW + DECAY_LORA + ICLR_LORA, RWKV_PROJ),
    np.arange(_RW, _RW + DECAY_LORA),
    np.arange(3 * _RW + DECAY_LORA, 3 * _RW + DECAY_LORA + ICLR_LORA),
])
_RWKV_INV_PERM = np.argsort(_RWKV_PERM)


def _head_block_ones(width, head_dim):
    idx = np.arange(width) // head_dim
    return jnp.asarray((idx[:, None] == idx[None, :]).astype(np.float32), BF16)


def _rwkv_prep_math(p, p_prev, mu, w0, w2p, a0, a2p, g2, k_k, k_a, bones):
    xs = p + mu * (p_prev - p)
    r = xs[:, 0:_RW]
    k = xs[:, _RW:2 * _RW]
    v = xs[:, 2 * _RW:3 * _RW]
    gl = xs[:, 3 * _RW:3 * _RW + GATE_LORA]
    wa = xs[:, 3 * _RW + GATE_LORA:]
    w = -_softplus(-(w0 + jnp.dot(jnp.tanh(wa).astype(BF16), w2p, preferred_element_type=F32))) - 0.5
    decay = jnp.exp(-jnp.exp(w))
    a = _sigmoid(a0 + jnp.dot(wa.astype(BF16), a2p, preferred_element_type=F32))
    g = jnp.dot(_sigmoid(gl).astype(BF16), g2, preferred_element_type=F32)
    kkr = k * k_k
    ss = _dot_f32_right(kkr * kkr, bones)
    kk = kkr / jnp.maximum(jnp.sqrt(ss), 1e-12)
    k2 = k * (1.0 + (a - 1.0) * k_a)
    return r, k2, v, kk, kk * a, decay, g


def _rwkv_prep_kernel(p_ref, prev_ref, mu_ref, w0_ref, w2_ref, a0_ref, a2_ref, g2_ref, kk_ref, ka_ref, bones_ref,
                      r_o, k_o, v_o, kk_o, b_o, d_o, g_o, last_o, buf_ref, *, t, shifted):
    p = p_ref[...]
    if shifted:
        @pl.when(pl.program_id(0) == 0)
        def _():
            buf_ref[SUBLANES - 1:SUBLANES, :] = prev_ref[...]

        buf_ref[SUBLANES:SUBLANES + t, :] = p
        p_prev = buf_ref[SUBLANES - 1:SUBLANES - 1 + t, :]
        buf_ref[SUBLANES - 1:SUBLANES, :] = p[t - 1:t, :]
        last_o[...] = p[t - 1:t, :]
    else:
        p_prev = prev_ref[...]
        last_o[...] = p
    outs = _rwkv_prep_math(p, p_prev, mu_ref[...], w0_ref[...], w2_ref[...], a0_ref[...], a2_ref[...], g2_ref[...],
                           kk_ref[...], ka_ref[...], bones_ref[...])
    for o_ref, val in zip((r_o, k_o, v_o, kk_o, b_o, d_o, g_o), outs):
        o_ref[...] = val


def _rwkv_prep(proj, prev, mu, w0, w2p, a0, a2p, g2, k_k, k_a, bones, *, t, shifted):
    n = proj.shape[0]
    row = lambda w: pl.BlockSpec((1, w), lambda i: (0, 0))
    full = lambda a: pl.BlockSpec(a.shape, lambda i: (0, 0))
    tok = lambda w: pl.BlockSpec((t, w), lambda i: (i, 0))
    prev_spec = row(RWKV_PROJ) if shifted else tok(RWKV_PROJ)
    last_spec = row(RWKV_PROJ) if shifted else tok(RWKV_PROJ)
    last_shape = (1, RWKV_PROJ) if shifted else (n, RWKV_PROJ)
    return pl.pallas_call(
        functools.partial(_rwkv_prep_kernel, t=t, shifted=shifted),
        grid=(n // t,),
        in_specs=[tok(RWKV_PROJ), prev_spec, row(RWKV_PROJ), row(_RW), full(w2p), row(_RW), full(a2p), full(g2),
                  row(_RW), row(_RW), full(bones)],
        out_specs=[tok(_RW)] * 7 + [last_spec],
        out_shape=[jax.ShapeDtypeStruct((n, _RW), F32)] * 7 + [jax.ShapeDtypeStruct(last_shape, F32)],
        scratch_shapes=[pltpu.VMEM((SUBLANES + t, RWKV_PROJ), F32)],
        compiler_params=_cparams(("arbitrary",)),
        name="rwkv_prep",
    )(proj, prev, mu, w0, w2p, a0, a2p, g2, k_k, k_a, bones)


def _rwkv_scan_kernel(r_ref, k_ref, v_ref, kk_ref, b_ref, d_ref, s0_ref, o_ref, st_ref, s_ref, *, t):
    j = pl.program_id(1)

    @pl.when(j == 0)
    def _():
        s_ref[...] = s0_ref[...]

    shape = (RWKV_HEAD_DIM, LANES)
    lane = lax.broadcasted_iota(I32, shape, 1)
    sub = lax.broadcasted_iota(I32, shape, 0)
    left = lane < RWKV_HEAD_DIM
    eye2 = (lane & (RWKV_HEAD_DIM - 1)) == sub

    def seg(x):
        s0 = jnp.sum(jnp.where(left, x, 0.0), axis=-1, keepdims=True)
        s1 = jnp.sum(jnp.where(left, 0.0, x), axis=-1, keepdims=True)
        return jnp.where(left, s0, s1)

    group = SUBLANES if t % SUBLANES == 0 else t

    def body(gi, states):
        base = pl.multiple_of(gi * group, group)
        new_states = []
        for p in range(HEAD_PAIRS):
            sl = slice(p * LANES, (p + 1) * LANES)
            s = states[p]
            tile = lambda ref: ref[pl.ds(base, group), sl]
            kk_t, v_t, d_t, b_t, k_t, r_t = (tile(ref) for ref in (kk_ref, v_ref, d_ref, b_ref, k_ref, r_ref))
            o_rows = []
            for u in range(group):
                row = lambda x: x[u:u + 1, :]
                sk = seg(s * row(kk_t))
                vcol = seg(jnp.where(eye2, row(v_t), 0.0))
                s = s * row(d_t) - sk * row(b_t) + vcol * row(k_t)
                ocol = seg(s * row(r_t))
                o_rows.append(jnp.sum(jnp.where(eye2, ocol, 0.0), axis=0, keepdims=True))
            o_ref[pl.ds(base, group), sl] = jnp.concatenate(o_rows, axis=0) if group > 1 else o_rows[0]
            new_states.append(s)
        return tuple(new_states)

    states = lax.fori_loop(0, t // group, body, tuple(s_ref[p] for p in range(HEAD_PAIRS)))
    for p in range(HEAD_PAIRS):
        s_ref[p] = states[p]

    @pl.when(j == pl.num_programs(1) - 1)
    def _():
        st_ref[...] = s_ref[...]


def _rwkv_scan(r, k, v, kk, b, d, s0_pairs, *, batch, t):
    n = r.shape[0]
    seq = n // batch
    tok = pl.BlockSpec((None, t, _RW), lambda bi, j: (bi, j, 0))
    st = pl.BlockSpec((None, HEAD_PAIRS, RWKV_HEAD_DIM, LANES), lambda bi, j: (bi, 0, 0, 0))
    o, s_t = pl.pallas_call(
        functools.partial(_rwkv_scan_kernel, t=t),
        grid=(batch, seq // t),
        in_specs=[tok] * 6 + [st],
        out_specs=[tok, st],
        out_shape=[jax.ShapeDtypeStruct((batch, seq, _RW), F32),
                   jax.ShapeDtypeStruct((batch, HEAD_PAIRS, RWKV_HEAD_DIM, LANES), F32)],
        scratch_shapes=[pltpu.VMEM((HEAD_PAIRS, RWKV_HEAD_DIM, LANES), F32)],
        compiler_params=_cparams(("arbitrary", "arbitrary")),
        name="rwkv_scan",
    )(*(x.reshape(batch, seq, _RW) for x in (r, k, v, kk, b, d)), s0_pairs)
    return o.reshape(n, _RW), s_t


def _rwkv_post_kernel(o_ref, r_ref, k_ref, v_ref, g_ref, lnw_ref, lnb_ref, rk_ref, bones_ref, out_ref):
    bones = bones_ref[...]
    inv = 1.0 / RWKV_HEAD_DIM
    o = o_ref[...]
    mean = _dot_f32_right(o, bones) * inv
    cen = o - mean
    var = _dot_f32_right(cen * cen, bones) * inv
    gn = cen * lax.rsqrt(var + GN_EPS) * lnw_ref[...] + lnb_ref[...]
    bonus = _dot_f32_right(r_ref[...] * k_ref[...] * rk_ref[...], bones) * v_ref[...]
    out_ref[...] = ((gn + bonus) * g_ref[...]).astype(out_ref.dtype)


def _rwkv_post(o, r, k, v, g, lnw, lnb, rk, bones, *, t):
    n = o.shape[0]
    tok = pl.BlockSpec((t, _RW), lambda i: (i, 0))
    row = pl.BlockSpec((1, _RW), lambda i: (0, 0))
    return pl.pallas_call(
        _rwkv_post_kernel,
        grid=(n // t,),
        in_specs=[tok] * 5 + [row] * 3 + [pl.BlockSpec(bones.shape, lambda i: (0, 0))],
        out_specs=tok,
        out_shape=jax.ShapeDtypeStruct((n, _RW), BF16),
        compiler_params=_cparams(("parallel",)),
        name="rwkv_post",
    )(o, r, k, v, g, lnw, lnb, rk, bones)


def _to_pairs(s):
    b = s.shape[0]
    return s.reshape(b, HEAD_PAIRS, 2, RWKV_HEAD_DIM, RWKV_HEAD_DIM).transpose(0, 1, 3, 2, 4).reshape(
        b, HEAD_PAIRS, RWKV_HEAD_DIM, LANES)


def _from_pairs(s):
    b = s.shape[0]
    return s.reshape(b, HEAD_PAIRS, RWKV_HEAD_DIM, 2, RWKV_HEAD_DIM).transpose(0, 1, 3, 2, 4).reshape(
        b, RWKV_HEADS, RWKV_HEAD_DIM, RWKV_HEAD_DIM)


def _rwkv_mix(proj, prev, s0, rw, *, batch, t_prep, t_scan, shifted):
    mu, w0, w2, a0, a2, g2, k_k, k_a, r_k, ln_w, ln_b = rw
    bones = _head_block_ones(_RW, RWKV_HEAD_DIM)
    zeros = jnp.zeros((LANES - DECAY_LORA, _RW), F32)
    w2p = jnp.concatenate([w2, zeros], axis=0).astype(BF16)
    a2p = jnp.concatenate([zeros, a2], axis=0).astype(BF16)
    row = lambda x: x.reshape(1, -1)
    r, k, v, kk, b, d, g, last = _rwkv_prep(
        proj, prev, row(mu[_RWKV_PERM]), row(w0), w2p, row(a0), a2p, g2.astype(BF16), row(k_k), row(k_a), bones,
        t=t_prep, shifted=shifted)
    o, s_t = _rwkv_scan(r, k, v, kk, b, d, _to_pairs(s0), batch=batch, t=t_scan)
    out = _rwkv_post(o, r, k, v, g, row(ln_w), row(ln_b), row(r_k), bones, t=t_prep)
    return out, last[:, _RWKV_INV_PERM], _from_pairs(s_t)


def _top_rows(s, k):
    n_rows = s.shape[0]
    iota = lax.broadcasted_iota(I32, s.shape, 0)
    vals, idxs = [], []
    for _ in range(k):
        m = jnp.max(s, axis=0, keepdims=True)
        idx = jnp.min(jnp.where(s == m, iota, n_rows), axis=0, keepdims=True)
        vals.append(m)
        idxs.append(idx)
        s = jnp.where(iota == idx, -jnp.inf, s)
    return jnp.concatenate(vals, axis=0), jnp.concatenate(idxs, axis=0)


def _take_rows(table, pos):
    out = jnp.zeros(pos.shape, table.dtype)
    for i in range(table.shape[0]):
        out = jnp.where(pos == i, table[i:i + 1, :], out)
    return out


def _peer_route_kernel(q_ref, keys_ref, g_ref, e1_ref, e2_ref, gv_ref, *, tm):
    k = PEER_TOPK
    e1s, e2s, gs = [], [], []
    for h in range(PEER_HEADS):
        qh = q_ref[:, h * D_KEY:(h + 1) * D_KEY].astype(BF16)
        score = lambda p: lax.dot_general(keys_ref[h, p], qh, (((1,), (1,)), ((), ())), preferred_element_type=F32)
        v1, i1 = _top_rows(score(0), k)
        v2, i2 = _top_rows(score(1), k)
        cand = jnp.concatenate([v1[i:i + 1, :] + v2 for i in range(k)], axis=0)
        sc, pos = _top_rows(cand, k)
        e1s.append(_take_rows(i1, pos >> 4))
        e2s.append(_take_rows(i2, pos & (k - 1)))
        ex = jnp.exp(sc - sc[0:1, :])
        gs.append(ex / jnp.sum(ex, axis=0, keepdims=True))
    e1_ref[...] = jnp.concatenate(e1s, axis=0).astype(F32).T
    e2_ref[...] = jnp.concatenate(e2s, axis=0).astype(F32).T
    gv_ref[...] = jnp.concatenate(gs, axis=0).T

    eid = lax.broadcasted_iota(I32, (N_KEYS, PEER_HEADS * k), 0).astype(F32)

    def body(gi, carry):
        base = pl.multiple_of(gi * SUBLANES, SUBLANES)
        e1_t = e1_ref[pl.ds(base, SUBLANES), :]
        e2_t = e2_ref[pl.ds(base, SUBLANES), :]
        gv_t = gv_ref[pl.ds(base, SUBLANES), :]
        for u in range(SUBLANES):
            a = jnp.where(eid == e1_t[u:u + 1, :], gv_t[u:u + 1, :], 0.0).astype(BF16)
            b = jnp.where(eid == e2_t[u:u + 1, :], 1.0, 0.0).astype(BF16)
            gmap = lax.dot_general(a, b, (((1,), (1,)), ((), ())), preferred_element_type=F32)
            g_ref[base + u] = gmap.astype(g_ref.dtype)
        return carry

    lax.fori_loop(0, tm // SUBLANES, body, 0)


def _peer_route(q, keys_pad, *, tm):
    n = q.shape[0]
    width = PEER_HEADS * PEER_TOPK
    return pl.pallas_call(
        functools.partial(_peer_route_kernel, tm=tm),
        grid=(n // tm,),
        in_specs=[pl.BlockSpec((tm, PEER_HEADS * D_KEY), lambda i: (i, 0)),
                  pl.BlockSpec(keys_pad.shape, lambda i: (0, 0, 0, 0))],
        out_specs=pl.BlockSpec((tm, N_KEYS, N_KEYS), lambda i: (i, 0, 0)),
        out_shape=jax.ShapeDtypeStruct((n, N_KEYS, N_KEYS), BF16),
        scratch_shapes=[pltpu.VMEM((tm, width), F32)] * 3,
        compiler_params=_cparams(("parallel",)),
        name="peer_route",
    )(q, keys_pad)


def _peer_mlp_kernel(x_ref, gain_ref, u_ref, v_ref, g_ref, o_ref, xn_ref, acc_ref):
    j = pl.program_id(1)

    @pl.when(j == 0)
    def _():
        x = x_ref[...]
        ms = jnp.mean(x * x, axis=-1, keepdims=True)
        xn_ref[...] = (x * lax.rsqrt(ms + RMS_EPS) * gain_ref[...]).astype(BF16)
        acc_ref[...] = jnp.zeros_like(acc_ref)

    h = lax.dot_general(xn_ref[...], u_ref[...], (((1,), (1,)), ((), ())), preferred_element_type=F32)
    act = 0.5 * h * (1.0 + lax.erf(h * (2.0 ** -0.5)))
    w = (act * g_ref[...].astype(F32)).astype(BF16)
    acc_ref[...] += jnp.dot(w, v_ref[...], preferred_element_type=F32)

    @pl.when(j == pl.num_programs(1) - 1)
    def _():
        o_ref[...] = x_ref[...] + acc_ref[...]


def _peer_mlp(x, gain, u_bf16, v_bf16, gmap, *, tm, te):
    n, dm = x.shape
    ne = u_bf16.shape[0]
    return pl.pallas_call(
        _peer_mlp_kernel,
        grid=(n // tm, ne // te),
        in_specs=[pl.BlockSpec((tm, dm), lambda i, j: (i, 0)),
                  pl.BlockSpec((1, dm), lambda i, j: (0, 0)),
                  pl.BlockSpec((te, dm), lambda i, j: (j, 0)),
                  pl.BlockSpec((te, dm), lambda i, j: (j, 0)),
                  pl.BlockSpec((tm, te), lambda i, j: (i, j))],
        out_specs=pl.BlockSpec((tm, dm), lambda i, j: (i, 0)),
        out_shape=jax.ShapeDtypeStruct((n, dm), F32),
        scratch_shapes=[pltpu.VMEM((tm, dm), BF16), pltpu.VMEM((tm, dm), F32)],
        compiler_params=_cparams(("parallel", "arbitrary")),
        name="peer_mlp",
    )(x, gain.reshape(1, dm), u_bf16, v_bf16, gmap)


def _peer_keys_padded(keys):
    z = jnp.zeros_like(keys[:, 0])
    first = jnp.concatenate([keys[:, 0], z], axis=-1)
    second = jnp.concatenate([z, keys[:, 1]], axis=-1)
    return jnp.stack([first, second], axis=1).astype(BF16)


def _peer(y, gain, wq_bf16, keys_pad, u_bf16, v_bf16, *, tm_proj, tm_route, tm_mlp, te):
    n = y.shape[0]
    q = _norm_matmul(y, gain, wq_bf16, tm=tm_proj, tn=wq_bf16.shape[1])
    gmap = _peer_route(q, keys_pad, tm=tm_route).reshape(n, N_KEYS * N_KEYS)
    return _peer_mlp(y, gain, u_bf16, v_bf16, gmap, tm=tm_mlp, te=te)


def _conv_silu(x, buf_ref, w_ref, b_ref, q):
    buf_ref[SUBLANES:SUBLANES + q, :] = x
    acc = b_ref[...] + x * w_ref[CONV_W - 1:CONV_W, :]
    for back in range(1, CONV_W):
        acc = acc + buf_ref[SUBLANES - back:SUBLANES - back + q, :] * w_ref[CONV_W - 1 - back:CONV_W - back, :]
    tail = buf_ref[q + SUBLANES - (CONV_W - 1):q + SUBLANES, :]
    buf_ref[SUBLANES - (CONV_W - 1):SUBLANES, :] = tail
    return _silu(acc), tail


def _ssd_kernel(z_ref, x_ref, bc_ref, dt_ref, cwx_ref, cwbc_ref, cbx_ref, cbbc_ref, dtb_ref, alog_ref, dskip_ref,
                nw_ref, expand_ref, y_ref, convx_ref, convbc_ref, ht_ref, bufx_ref, bufbc_ref, h_ref, *, q, heads):
    hpg = heads // N_GROUPS
    gw = hpg * SSM_HEAD_DIM

    @pl.when(pl.program_id(0) == 0)
    def _():
        bufx_ref[...] = jnp.zeros_like(bufx_ref)
        bufbc_ref[...] = jnp.zeros_like(bufbc_ref)
        h_ref[...] = jnp.zeros_like(h_ref)

    xs, tail_x = _conv_silu(x_ref[...], bufx_ref, cwx_ref, cbx_ref, q)
    bc, tail_bc = _conv_silu(bc_ref[...], bufbc_ref, cwbc_ref, cbbc_ref, q)
    convx_ref[...] = tail_x
    convbc_ref[...] = tail_bc

    dt = _softplus(dt_ref[...] + dtb_ref[...])
    a = -jnp.exp(alog_ref[...])
    row = lax.broadcasted_iota(I32, (q, q), 0)
    col = lax.broadcasted_iota(I32, (q, q), 1)
    lower = col <= row
    tri = jnp.where(lower, 1.0, 0.0).astype(BF16)
    acs = _dot_f32_left(tri, dt * a)
    acs_t = acs.T
    expand = expand_ref[...]
    dt_e = _dot_f32_right(dt, expand)
    acs_e = _dot_f32_right(acs, expand)
    last_e = acs_e[q - 1:q, :]
    xc = xs * dt_e
    xcd = (xc * jnp.exp(last_e - acs_e)).astype(BF16)
    xc_b = xc.astype(BF16)
    grow = jnp.exp(acs_e)
    chunk_decay = jnp.exp(last_e)

    nbc = N_GROUPS * D_STATE
    for g in range(N_GROUPS):
        bm = bc[:, g * D_STATE:(g + 1) * D_STATE].astype(BF16)
        cm = bc[:, nbc + g * D_STATE:nbc + (g + 1) * D_STATE].astype(BF16)
        cb = lax.dot_general(cm, bm, (((1,), (1,)), ((), ())), preferred_element_type=F32)
        sl = slice(g * gw, (g + 1) * gw)
        h_prev = h_ref[g]
        y_off = jnp.dot(cm, h_prev.astype(BF16), preferred_element_type=F32) * grow[:, sl]
        y_diag = []
        for r in range(hpg):
            hd = g * hpg + r
            diff = acs[:, hd:hd + 1] - acs_t[hd:hd + 1, :]
            m = (cb * jnp.exp(jnp.where(lower, diff, -jnp.inf))).astype(BF16)
            y_diag.append(jnp.dot(m, xc_b[:, hd * SSM_HEAD_DIM:(hd + 1) * SSM_HEAD_DIM], preferred_element_type=F32))
        y_g = jnp.concatenate(y_diag, axis=1) + y_off + dskip_ref[:, sl] * xs[:, sl]
        states = jnp.dot(bm.T, xcd[:, sl], preferred_element_type=F32)
        h_ref[g] = h_prev * chunk_decay[:, sl] + states
        zg = z_ref[:, sl]
        y_g = y_g * _silu(zg)
        ms = jnp.mean(y_g * y_g, axis=-1, keepdims=True)
        y_ref[:, sl] = (y_g * lax.rsqrt(ms + RMS_EPS) * nw_ref[:, sl]).astype(y_ref.dtype)

    @pl.when(pl.program_id(0) == pl.num_programs(0) - 1)
    def _():
        ht_ref[...] = h_ref[...]


def _ssd(proj, conv_w, conv_b, dt_bias, a_log, d_skip, norm_w, *, heads):
    n = proj.shape[0]
    d_inner = heads * SSM_HEAD_DIM
    q = CHUNK
    nbc = 2 * N_GROUPS * D_STATE
    assert nbc == d_inner and n % q == 0
    gw = d_inner // N_GROUPS
    pad = lambda x: jnp.pad(x, (0, LANES - heads)).reshape(1, LANES)
    expand = jnp.asarray((np.arange(LANES)[:, None] == (np.arange(d_inner) // SSM_HEAD_DIM)[None, :]).astype(np.float32), BF16)
    blk = lambda w, j: pl.BlockSpec((q, w), lambda i, j=j: (i, j))
    full = lambda a: pl.BlockSpec(a.shape, lambda i: (0,) * a.ndim)
    small = [conv_w[:, :d_inner], conv_w[:, d_inner:], conv_b[:d_inner].reshape(1, -1), conv_b[d_inner:].reshape(1, -1),
             pad(dt_bias), pad(a_log), jnp.repeat(d_skip, SSM_HEAD_DIM).reshape(1, -1), norm_w.reshape(1, -1), expand]
    y, cx, cbc, ht = pl.pallas_call(
        functools.partial(_ssd_kernel, q=q, heads=heads),
        grid=(n // q,),
        in_specs=[blk(d_inner, 0), blk(d_inner, 1), blk(d_inner, 2), blk(LANES, 3 * d_inner // LANES)]
                 + [full(a) for a in small],
        out_specs=[pl.BlockSpec((q, d_inner), lambda i: (i, 0)),
                   pl.BlockSpec((CONV_W - 1, d_inner), lambda i: (0, 0)),
                   pl.BlockSpec((CONV_W - 1, nbc), lambda i: (0, 0)),
                   pl.BlockSpec((N_GROUPS, D_STATE, gw), lambda i: (0, 0, 0))],
        out_shape=[jax.ShapeDtypeStruct((n, d_inner), BF16),
                   jax.ShapeDtypeStruct((CONV_W - 1, d_inner), F32),
                   jax.ShapeDtypeStruct((CONV_W - 1, nbc), F32),
                   jax.ShapeDtypeStruct((N_GROUPS, D_STATE, gw), F32)],
        scratch_shapes=[pltpu.VMEM((SUBLANES + q, d_inner), F32), pltpu.VMEM((SUBLANES + q, nbc), F32),
                        pltpu.VMEM((N_GROUPS, D_STATE, gw), F32)],
        compiler_params=_cparams(("arbitrary",)),
        name="ssd",
    )(proj, proj, proj, proj, *small)
    hpg = heads // N_GROUPS
    h_t = ht.reshape(N_GROUPS, D_STATE, hpg, SSM_HEAD_DIM).transpose(0, 2, 3, 1).reshape(heads, SSM_HEAD_DIM, D_STATE)
    return y, jnp.concatenate([cx, cbc], axis=1), h_t


def _ssm_step_kernel(z_ref, x_ref, bc_ref, dt_ref, bufx_ref, bufbc_ref, h0_ref, cwx_ref, cwbc_ref, cbx_ref, cbbc_ref,
                     dtb_ref, alog_ref, dskip_ref, nw_ref, y_ref, nbufx_ref, nbufbc_ref, h_ref, *, heads):
    hpg = heads // N_GROUPS
    gw = hpg * SSM_HEAD_DIM
    nbc = N_GROUPS * D_STATE
    fullx = jnp.concatenate([bufx_ref[...], x_ref[...]], axis=1)
    nbufx_ref[...] = fullx[:, 1:]
    xs = _silu(cbx_ref[...] + jnp.sum(fullx * cwx_ref[...], axis=1, keepdims=True))
    fullbc = jnp.concatenate([bufbc_ref[...], bc_ref[...]], axis=0)
    nbufbc_ref[...] = fullbc[1:, :]
    bc = _silu(cbbc_ref[...] + jnp.sum(fullbc * cwbc_ref[...], axis=0, keepdims=True))
    dt = _softplus(dt_ref[...] + dtb_ref[...])
    dec = jnp.exp(dt * (-jnp.exp(alog_ref[...])))
    xdt = xs * dt
    for g in range(N_GROUPS):
        rows = slice(g * gw, (g + 1) * gw)
        bm = bc[:, g * D_STATE:(g + 1) * D_STATE]
        cm = bc[:, nbc + g * D_STATE:nbc + (g + 1) * D_STATE]
        h_new = h0_ref[rows, :] * dec[rows, :] + xdt[rows, :] * bm
        h_ref[rows, :] = h_new
        y = jnp.sum(h_new * cm, axis=-1, keepdims=True) + dskip_ref[rows, :] * xs[rows, :]
        y = y * _silu(z_ref[rows, :])
        ms = jnp.mean(y * y, axis=0, keepdims=True)
        y_ref[rows, :] = y * lax.rsqrt(ms + RMS_EPS) * nw_ref[rows, :]


def _ssm_step(proj, conv_buf, h0, conv_w, conv_b, dt_bias, a_log, d_skip, norm_w, *, heads):
    b = proj.shape[0]
    d_inner = heads * SSM_HEAD_DIM
    nbc2 = 2 * N_GROUPS * D_STATE
    col = lambda x: x.reshape(b, d_inner, 1)
    per_head = lambda v: jnp.repeat(v, SSM_HEAD_DIM, axis=-1)
    z = col(proj[:, :d_inner])
    x = col(proj[:, d_inner:2 * d_inner])
    bc = proj[:, 2 * d_inner:2 * d_inner + nbc2].reshape(b, 1, nbc2)
    dt = col(per_head(proj[:, 2 * d_inner + nbc2:2 * d_inner + nbc2 + heads]))
    bufx = conv_buf[:, :, :d_inner].transpose(0, 2, 1)
    bufbc = conv_buf[:, :, d_inner:]
    pcol = lambda v: v.reshape(d_inner, 1)
    params = [conv_w[:, :d_inner].T, conv_w[:, d_inner:], pcol(conv_b[:d_inner]), conv_b[d_inner:].reshape(1, nbc2),
              pcol(per_head(dt_bias)), pcol(per_head(a_log)), pcol(per_head(d_skip)), pcol(norm_w)]
    seq = lambda *shape: pl.BlockSpec((None,) + shape, lambda i: (i,) + (0,) * len(shape))
    full = lambda a: pl.BlockSpec(a.shape, lambda i: (0,) * a.ndim)
    y, nbx, nbbc, h = pl.pallas_call(
        functools.partial(_ssm_step_kernel, heads=heads),
        grid=(b,),
        in_specs=[seq(d_inner, 1), seq(d_inner, 1), seq(1, nbc2), seq(d_inner, 1), seq(d_inner, CONV_W - 1),
                  seq(CONV_W - 1, nbc2), seq(d_inner, D_STATE)] + [full(a) for a in params],
        out_specs=[seq(d_inner, 1), seq(d_inner, CONV_W - 1), seq(CONV_W - 1, nbc2), seq(d_inner, D_STATE)],
        out_shape=[jax.ShapeDtypeStruct((b, d_inner, 1), F32), jax.ShapeDtypeStruct((b, d_inner, CONV_W - 1), F32),
                   jax.ShapeDtypeStruct((b, CONV_W - 1, nbc2), F32), jax.ShapeDtypeStruct((b, d_inner, D_STATE), F32)],
        compiler_params=_cparams(("parallel",)),
        name="ssm_step",
    )(z, x, bc, dt, bufx, bufbc, h0.reshape(b, d_inner, D_STATE), *params)
    new_buf = jnp.concatenate([nbx.transpose(0, 2, 1), nbbc], axis=2)
    return y.reshape(b, d_inner), new_buf, h.reshape(b, heads, SSM_HEAD_DIM, D_STATE)


_DEC_ROWS = 16


def _fox_decode_kernel(pt_ref, q_ref, kc_ref, vc_ref, lfc_ref, *refs, pages):
    k_refs = refs[:pages]
    v_refs = refs[pages:2 * pages]
    lf_refs = refs[2 * pages:3 * pages]
    o_ref, m_ref, l_ref, acc_ref, later_ref = refs[3 * pages:]
    j = pl.program_id(1)
    hrow = lax.broadcasted_iota(I32, (_DEC_ROWS, FOX_WIDTH), 0)
    hlane = lax.broadcasted_iota(I32, (_DEC_ROWS, FOX_WIDTH), 1) // FOX_HEAD_DIM
    own = hrow == hlane
    q = (q_ref[...].astype(BF16) * jnp.asarray(FOX_HEAD_DIM ** -0.5, BF16)).astype(F32)
    qbd = jnp.where(own, q, 0.0).astype(BF16)

    def to_head_rows(row_tile):
        return jnp.broadcast_to(row_tile, (LANES, LANES)).T[:_DEC_ROWS, :]

    @pl.when(j == 0)
    def _():
        kc = kc_ref[...].astype(BF16).astype(F32)
        s_cur = jnp.sum(qbd.astype(F32) * kc, axis=-1, keepdims=True)
        m_ref[...] = jnp.broadcast_to(s_cur, m_ref.shape)
        l_ref[...] = jnp.ones_like(l_ref)
        acc_ref[...] = jnp.broadcast_to(vc_ref[...].astype(BF16).astype(F32), acc_ref.shape)
        later_ref[...] = to_head_rows(lfc_ref[...])

    urow = lax.broadcasted_iota(I32, (LANES, LANES), 0)
    scol = lax.broadcasted_iota(I32, (LANES, LANES), 1)
    after = jnp.where(urow > scol, 1.0, 0.0).astype(BF16)
    for u in range(pages):
        kb = k_refs[u][...].astype(BF16)
        vb = v_refs[u][...].astype(BF16)
        lf = jnp.concatenate([lf_refs[u][...], jnp.zeros((LANES, LANES - FOX_HEADS), F32)], axis=1)
        lf_t = lf.T
        later = later_ref[...]
        bias = _dot_f32_right(lf_t, after)[:_DEC_ROWS, :] + later
        s = lax.dot_general(qbd, kb, (((1,), (1,)), ((), ())), preferred_element_type=F32) + bias
        m_old = m_ref[...]
        m_new = jnp.maximum(m_old, jnp.max(s, axis=-1, keepdims=True))
        alpha = jnp.exp(m_old - m_new)
        p = jnp.exp(s - m_new)
        l_ref[...] = alpha * l_ref[...] + jnp.sum(p, axis=-1, keepdims=True)
        m_ref[...] = m_new
        acc_ref[...] = alpha[:, 0:1] * acc_ref[...] + jnp.dot(p.astype(BF16), vb, preferred_element_type=F32)
        later_ref[...] = later + jnp.sum(lf_t[:_DEC_ROWS, :], axis=-1, keepdims=True)

    @pl.when(j == pl.num_programs(1) - 1)
    def _():
        o = acc_ref[...] / l_ref[:, 0:1]
        o_ref[...] = jnp.sum(jnp.where(own, o, 0.0), axis=0, keepdims=True)


def _fox_decode(q, k_cur, v_cur, lf_cur_pad, cache_k, cache_v, cache_logf, page_table, *, pages):
    b, n_pages = page_table.shape
    pool, page = cache_k.shape[:2]
    assert page == LANES and n_pages % pages == 0
    ck = cache_k.reshape(pool, page, FOX_WIDTH)
    cv = cache_v.reshape(pool, page, FOX_WIDTH)
    tok = lambda w: pl.BlockSpec((None, 1, w), lambda bi, j, pt: (bi, 0, 0))

    def paged(width, u):
        return pl.BlockSpec((None, page, width), lambda bi, j, pt, u=u: (pt[bi, n_pages - 1 - (j * pages + u)], 0, 0))

    r3 = lambda x: x.reshape(b, 1, x.shape[-1])
    out = pl.pallas_call(
        functools.partial(_fox_decode_kernel, pages=pages),
        grid_spec=pltpu.PrefetchScalarGridSpec(
            num_scalar_prefetch=1,
            grid=(b, n_pages // pages),
            in_specs=[tok(FOX_WIDTH), tok(FOX_WIDTH), tok(FOX_WIDTH), tok(LANES)]
                     + [paged(FOX_WIDTH, u) for u in range(pages)] * 2
                     + [paged(FOX_HEADS, u) for u in range(pages)],
            out_specs=tok(FOX_WIDTH),
            scratch_shapes=[pltpu.VMEM((_DEC_ROWS, LANES), F32), pltpu.VMEM((_DEC_ROWS, LANES), F32),
                            pltpu.VMEM((_DEC_ROWS, FOX_WIDTH), F32), pltpu.VMEM((_DEC_ROWS, LANES), F32)],
        ),
        out_shape=jax.ShapeDtypeStruct((b, 1, FOX_WIDTH), F32),
        compiler_params=_cparams(("parallel", "arbitrary")),
        name="fox_decode",
    )(page_table, r3(q), r3(k_cur), r3(v_cur), r3(lf_cur_pad), *([ck] * pages), *([cv] * pages), *([cache_logf] * pages))
    return out.reshape(b, FOX_WIDTH)


_COL_Q = RWKV_PROJ
_COL_K = _COL_Q + FOX_WIDTH
_COL_V = _COL_K + FOX_WIDTH
_COL_FL = _COL_V + FOX_WIDTH
_EVEN_COLS = _COL_FL + LANES


def _even_weights(w_in):
    q, k, v = (w_in[:, i * FOX_WIDTH:(i + 1) * FOX_WIDTH] for i in range(3))
    fl = w_in[:, 3 * FOX_WIDTH:3 * FOX_WIDTH + FOX_HEADS]
    rw = w_in[:, 3 * FOX_WIDTH + FOX_HEADS:][:, _RWKV_PERM]
    fl_pad = jnp.pad(fl, ((0, 0), (0, LANES - FOX_HEADS)))
    return jnp.concatenate([rw, q, k, v, fl_pad], axis=1).astype(BF16)


def _even_layer(y, gain, w_cat, w_out, fb, rw, *, batch, seq, past, shift0, wkv0):
    n = y.shape[0]
    tm = min(512, n)
    proj = _norm_matmul(y, gain, w_cat, tm=tm, tn=_EVEN_COLS // 3)
    fb_pad = jnp.pad(fb, (0, LANES - FOX_HEADS)).reshape(1, LANES)
    logf, ct = _fox_prep(proj, _COL_FL // LANES, fb_pad, t=min(256, n))
    k_new = proj[:, _COL_K:_COL_V]
    v_new = proj[:, _COL_V:_COL_FL]
    if past is None:
        qkv = proj[:, _COL_Q:_COL_FL].astype(BF16)
        o_fox = _fox_attn(qkv, ct, tq=512)
        o_rwkv, shift_new, wkv_new = _rwkv_mix(proj, shift0[:, _RWKV_PERM], wkv0, rw, batch=batch,
                                               t_prep=256, t_scan=256, shifted=True)
    else:
        cache_k, cache_v, cache_logf, page_table = past
        lf_pad = jnp.pad(logf, ((0, 0), (0, LANES - FOX_HEADS)))
        o_fox = _fox_decode(proj[:, _COL_Q:_COL_K], k_new, v_new, lf_pad, cache_k, cache_v, cache_logf, page_table,
                            pages=4)
        o_rwkv, shift_new, wkv_new = _rwkv_mix(proj, shift0[:, _RWKV_PERM], wkv0, rw, batch=batch,
                                               t_prep=n, t_scan=1, shifted=False)
    w_out_b = w_out.astype(BF16)
    y = _matmul_res([o_fox, o_rwkv], [w_out_b[:FOX_WIDTH], w_out_b[FOX_WIDTH:]], y, tm=tm, tn=y.shape[1])
    heads = lambda t: t.reshape(batch, seq, FOX_HEADS, FOX_HEAD_DIM)
    return y, heads(k_new), heads(v_new), logf.reshape(batch, seq, FOX_HEADS), wkv_new, shift_new


def kernel(x_prompt, x_sample, cache_k, cache_v, cache_logf, page_table, state_wkv, state_shift, state_conv, state_ssm, norm_mix, norm_ffn, norm_final, w_in_even, w_out_even, fox_fb, rwkv_mu, rwkv_w0, rwkv_w2, rwkv_a0, rwkv_a2, rwkv_g2, rwkv_kk, rwkv_ka, rwkv_rk, rwkv_lnw, rwkv_lnb, w_in_odd, w_out_odd, ssm_conv_w, ssm_conv_b, ssm_dt_bias, ssm_a_log, ssm_d, ssm_norm_w, peer_wq, peer_keys, peer_u, peer_v):
    bp, seq_p, dm = x_prompt.shape
    bs, seq_s, _ = x_sample.shape
    assert bp == 1 and seq_s == 1
    depth = norm_mix.shape[0]
    ssm_heads = ssm_dt_bias.shape[1]
    yp = x_prompt.reshape(bp * seq_p, dm)
    ys = x_sample.reshape(bs * seq_s, dm)
    outs_p = {name: [] for name in ("k", "v", "logf", "wkv", "shift", "conv", "ssm")}
    outs_s = {name: [] for name in outs_p}
    for layer in range(depth):
        if layer % 2 == 0:
            e = layer // 2
            rw = (rwkv_mu[e], rwkv_w0[e], rwkv_w2[e], rwkv_a0[e], rwkv_a2[e], rwkv_g2[e],
                  rwkv_kk[e], rwkv_ka[e], rwkv_rk[e], rwkv_lnw[e], rwkv_lnb[e])
            w_cat = _even_weights(w_in_even[e])
            yp, *res_p = _even_layer(
                yp, norm_mix[layer], w_cat, w_out_even[e], fox_fb[e], rw, batch=bp, seq=seq_p, past=None,
                shift0=jnp.zeros((bp, RWKV_PROJ), F32),
                wkv0=jnp.zeros((bp, RWKV_HEADS, RWKV_HEAD_DIM, RWKV_HEAD_DIM), F32))
            ys, *res_s = _even_layer(
                ys, norm_mix[layer], w_cat, w_out_even[e], fox_fb[e], rw, batch=bs, seq=seq_s,
                past=(cache_k[e], cache_v[e], cache_logf[e], page_table), shift0=state_shift[e], wkv0=state_wkv[e])
            for outs, res in ((outs_p, res_p), (outs_s, res_s)):
                for name, val in zip(("k", "v", "logf", "wkv", "shift"), res):
                    outs[name].append(val)
        else:
            o = layer // 2
            mprm = (ssm_conv_w[o], ssm_conv_b[o], ssm_dt_bias[o], ssm_a_log[o], ssm_d[o], ssm_norm_w[o])
            w_cat = jnp.pad(w_in_odd[o], ((0, 0), (0, LANES - ssm_heads))).astype(BF16)
            w_out = w_out_odd[o].astype(BF16)
            tn = w_cat.shape[1] // 7
            proj_p = _norm_matmul(yp, norm_mix[layer], w_cat, tm=512, tn=tn)
            y_m, conv_p, h_p = _ssd(proj_p, *mprm, heads=ssm_heads)
            yp = _matmul_res([y_m], [w_out], yp, tm=512, tn=dm)
            proj_s = _norm_matmul(ys, norm_mix[layer], w_cat, tm=bs, tn=tn)
            y_s, conv_s, h_s = _ssm_step(proj_s, state_conv[o], state_ssm[o], *mprm, heads=ssm_heads)
            ys = _matmul_res([y_s], [w_out], ys, tm=bs, tn=dm)
            outs_p["conv"].append(conv_p[None])
            outs_p["ssm"].append(h_p[None])
            outs_s["conv"].append(conv_s)
            outs_s["ssm"].append(h_s)
        wq = peer_wq[layer].astype(BF16)
        keys_pad = _peer_keys_padded(peer_keys[layer])
        u_b = peer_u[layer].astype(BF16)
        v_b = peer_v[layer].astype(BF16)
        yp = _peer(yp, norm_ffn[layer], wq, keys_pad, u_b, v_b, tm_proj=512, tm_route=128, tm_mlp=1024, te=1024)
        ys = _peer(ys, norm_ffn[layer], wq, keys_pad, u_b, v_b, tm_proj=bs, tm_route=bs, tm_mlp=bs, te=2048)
    y_prompt = _rmsnorm(yp, norm_final, tm=512).reshape(bp, seq_p, dm)
    y_sample = _rmsnorm(ys, norm_final, tm=bs).reshape(bs, seq_s, dm)
    order = ("k", "v", "logf", "wkv", "shift", "conv", "ssm")
    return ((y_prompt, y_sample) + tuple(jnp.stack(outs_p[name]) for name in order)
            + tuple(jnp.stack(outs_s[name]) for name in order))
```

```python
import functools
import math

import numpy as np
import jax
import jax.numpy as jnp
from jax import lax
from jax.experimental import pallas as pl
from jax.experimental.pallas import tpu as pltpu

F32 = jnp.float32
BF16 = jnp.bfloat16
I32 = jnp.int32

LANES = 128
SUBLANES = 8
V7X_VMEM_BYTES = 64 * 1024 * 1024

RMS_EPS = 1e-6
GN_EPS = 64e-5

FOX_HEADS = 8
FOX_HEAD_DIM = 64
FOX_WIDTH = FOX_HEADS * FOX_HEAD_DIM
RWKV_HEADS = 8
RWKV_HEAD_DIM = 64
RWKV_WIDTH = RWKV_HEADS * RWKV_HEAD_DIM
DECAY_LORA = 64
ICLR_LORA = 64
GATE_LORA = 128
RWKV_PROJ = 4 * RWKV_WIDTH - RWKV_WIDTH + DECAY_LORA + ICLR_LORA + GATE_LORA
HEAD_PAIRS = RWKV_HEADS // 2

SSM_HEAD_DIM = 64
D_STATE = 128
N_GROUPS = 8
CONV_W = 4
CHUNK = 128

N_KEYS = 128
PEER_HEADS = 8
PEER_TOPK = 16
D_KEY = 128


def _cparams(semantics, vmem_mb=48):
    return pltpu.CompilerParams(dimension_semantics=semantics, vmem_limit_bytes=vmem_mb * 1024 * 1024)


def _split3(x):
    hi = x.astype(BF16)
    r1 = x - hi.astype(F32)
    mid = r1.astype(BF16)
    lo = (r1 - mid.astype(F32)).astype(BF16)
    return hi, mid, lo


def _dot_f32_right(x, m_bf16):
    hi, mid, lo = _split3(x)
    d = lambda a: jnp.dot(a, m_bf16, preferred_element_type=F32)
    return d(hi) + d(mid) + d(lo)


def _dot_f32_left(m_bf16, x):
    hi, mid, lo = _split3(x)
    d = lambda a: jnp.dot(m_bf16, a, preferred_element_type=F32)
    return d(hi) + d(mid) + d(lo)


def _sigmoid(x):
    return 1.0 / (1.0 + jnp.exp(-x))


def _softplus(x):
    return jnp.maximum(x, 0.0) + jnp.log1p(jnp.exp(-jnp.abs(x)))


def _silu(x):
    return x * _sigmoid(x)


def _norm_matmul_kernel(x_ref, g_ref, w_ref, o_ref, xn_ref):
    @pl.when(pl.program_id(1) == 0)
    def _():
        x = x_ref[...]
        ms = jnp.mean(x * x, axis=-1, keepdims=True)
        xn_ref[...] = (x * lax.rsqrt(ms + RMS_EPS) * g_ref[...]).astype(BF16)

    o_ref[...] = jnp.dot(xn_ref[...], w_ref[...], preferred_element_type=F32).astype(o_ref.dtype)


def _norm_matmul(x, g, w_bf16, *, tm, tn, out_dtype=F32):
    m, k = x.shape
    n = w_bf16.shape[1]
    assert m % tm == 0 and n % tn == 0
    return pl.pallas_call(
        _norm_matmul_kernel,
        grid=(m // tm, n // tn),
        in_specs=[
            pl.BlockSpec((tm, k), lambda i, j: (i, 0)),
            pl.BlockSpec((1, k), lambda i, j: (0, 0)),
            pl.BlockSpec((k, tn), lambda i, j: (0, j)),
        ],
        out_specs=pl.BlockSpec((tm, tn), lambda i, j: (i, j)),
        out_shape=jax.ShapeDtypeStruct((m, n), out_dtype),
        scratch_shapes=[pltpu.VMEM((tm, k), BF16)],
        compiler_params=_cparams(("parallel", "arbitrary")),
        name="norm_matmul",
    )(x, g.reshape(1, k), w_bf16)


def _matmul_res_kernel(*refs, n_in):
    a_refs = refs[:n_in]
    w_refs = refs[n_in:2 * n_in]
    r_ref, o_ref = refs[2 * n_in], refs[2 * n_in + 1]
    acc = r_ref[...]
    for a_ref, w_ref in zip(a_refs, w_refs):
        acc = acc + jnp.dot(a_ref[...].astype(BF16), w_ref[...], preferred_element_type=F32)
    o_ref[...] = acc


def _matmul_res(a_list, w_list, res, *, tm, tn):
    m, n = res.shape
    n_in = len(a_list)
    assert m % tm == 0 and n % tn == 0
    in_specs = ([pl.BlockSpec((tm, a.shape[1]), lambda i, j: (i, 0)) for a in a_list]
                + [pl.BlockSpec((w.shape[0], tn), lambda i, j: (0, j)) for w in w_list]
                + [pl.BlockSpec((tm, tn), lambda i, j: (i, j))])
    return pl.pallas_call(
        functools.partial(_matmul_res_kernel, n_in=n_in),
        grid=(m // tm, n // tn),
        in_specs=in_specs,
        out_specs=pl.BlockSpec((tm, tn), lambda i, j: (i, j)),
        out_shape=jax.ShapeDtypeStruct((m, n), F32),
        compiler_params=_cparams(("parallel", "parallel")),
        name="matmul_res",
    )(*a_list, *w_list, res)


def _rmsnorm_kernel(x_ref, g_ref, o_ref):
    x = x_ref[...]
    ms = jnp.mean(x * x, axis=-1, keepdims=True)
    o_ref[...] = x * lax.rsqrt(ms + RMS_EPS) * g_ref[...]


def _rmsnorm(x, g, *, tm):
    m, k = x.shape
    return pl.pallas_call(
        _rmsnorm_kernel,
        grid=(m // tm,),
        in_specs=[pl.BlockSpec((tm, k), lambda i: (i, 0)), pl.BlockSpec((1, k), lambda i: (0, 0))],
        out_specs=pl.BlockSpec((tm, k), lambda i: (i, 0)),
        out_shape=jax.ShapeDtypeStruct((m, k), F32),
        compiler_params=_cparams(("parallel",)),
        name="rmsnorm",
    )(x, g.reshape(1, k))


def _fox_prep_kernel(fl_ref, fb_ref, lf_ref, ct_ref, carry_ref, *, t):
    @pl.when(pl.program_id(0) == 0)
    def _():
        carry_ref[...] = jnp.zeros_like(carry_ref)

    x = fl_ref[...] + fb_ref[...]
    lf = jnp.minimum(x, 0.0) - jnp.log1p(jnp.exp(-jnp.abs(x)))
    lf_ref[...] = lf[:, :FOX_HEADS]
    row = lax.broadcasted_iota(I32, (t, t), 0)
    col = lax.broadcasted_iota(I32, (t, t), 1)
    tri = jnp.where(col <= row, 1.0, 0.0).astype(BF16)
    c = _dot_f32_left(tri, lf) + carry_ref[...]
    carry_ref[...] = c[t - 1:t, :]
    ct_ref[...] = c.T[:FOX_HEADS, :]


def _fox_prep(proj, fl_block, fb_pad, *, t):
    n = proj.shape[0]
    return pl.pallas_call(
        functools.partial(_fox_prep_kernel, t=t),
        grid=(n // t,),
        in_specs=[pl.BlockSpec((t, LANES), lambda i: (i, fl_block)),
                  pl.BlockSpec((1, LANES), lambda i: (0, 0))],
        out_specs=[pl.BlockSpec((t, FOX_HEADS), lambda i: (i, 0)),
                   pl.BlockSpec((FOX_HEADS, t), lambda i: (0, i))],
        out_shape=[jax.ShapeDtypeStruct((n, FOX_HEADS), F32),
                   jax.ShapeDtypeStruct((FOX_HEADS, n), F32)],
        scratch_shapes=[pltpu.VMEM((1, LANES), F32)],
        compiler_params=_cparams(("arbitrary",)),
        name="fox_prep",
    )(proj, fb_pad)


def _fox_attn_kernel(q_ref, k_ref, v_ref, ct_ref, o_ref, qs_ref, m_ref, l_ref, acc_ref, *, tq):
    i = pl.program_id(1)
    lane = lax.broadcasted_iota(I32, (1, LANES), 1)
    left = lane < FOX_HEAD_DIM
    q = q_ref[...] * jnp.asarray(FOX_HEAD_DIM ** -0.5, BF16)
    zero = jnp.zeros_like(q)
    qs_ref[0] = jnp.where(left, q, zero)
    qs_ref[1] = jnp.where(left, zero, q)
    q0 = pl.multiple_of(i * tq, tq)
    cref = [ct_ref[h:h + 1, pl.ds(q0, LANES)][:, 0:1] for h in range(2)]

    m_ref[...] = jnp.full_like(m_ref, -jnp.inf)
    l_ref[...] = jnp.zeros_like(l_ref)
    acc_ref[...] = jnp.zeros_like(acc_ref)
    ncol = tq // LANES

    def step(j, masked):
        k0 = pl.multiple_of(j * tq, tq)
        kb = k_ref[pl.ds(k0, tq), :]
        vb = v_ref[pl.ds(k0, tq), :]
        bias = [cref[h] - ct_ref[h:h + 1, pl.ds(k0, tq)] for h in range(2)]
        for h in range(2):
            s = lax.dot_general(qs_ref[h], kb, (((1,), (1,)), ((), ())), preferred_element_type=F32)
            s = s + bias[h]
            if masked:
                r = lax.broadcasted_iota(I32, (tq, tq), 0)
                c = lax.broadcasted_iota(I32, (tq, tq), 1)
                s = jnp.where(c <= r, s, -jnp.inf)
            cols = [s[:, c * LANES:(c + 1) * LANES] for c in range(ncol)]
            lane_max = functools.reduce(jnp.maximum, cols)
            m_old = m_ref[h]
            m_new = jnp.maximum(m_old, jnp.max(lane_max, axis=-1, keepdims=True))
            alpha = jnp.exp(m_old - m_new)
            ps = [jnp.exp(c - m_new) for c in cols]
            l_ref[h] = alpha * l_ref[h] + functools.reduce(jnp.add, ps)
            m_ref[h] = m_new
            p = jnp.concatenate([x.astype(BF16) for x in ps], axis=1)
            acc_ref[h] = alpha * acc_ref[h] + jnp.dot(p, vb, preferred_element_type=F32)

    def body(j, carry):
        step(j, False)
        return carry

    lax.fori_loop(0, i, body, 0)
    step(i, True)
    l0 = jnp.sum(l_ref[0], axis=-1, keepdims=True)
    l1 = jnp.sum(l_ref[1], axis=-1, keepdims=True)
    o_ref[...] = jnp.where(left, acc_ref[0] / l0, acc_ref[1] / l1).astype(o_ref.dtype)


def _fox_attn(qkv_bf16, ct, *, tq):
    n = qkv_bf16.shape[0]
    nb = FOX_WIDTH // LANES
    return pl.pallas_call(
        functools.partial(_fox_attn_kernel, tq=tq),
        grid=(nb, n // tq),
        in_specs=[
            pl.BlockSpec((tq, LANES), lambda p, i: (i, p)),
            pl.BlockSpec((n, LANES), lambda p, i: (0, nb + p)),
            pl.BlockSpec((n, LANES), lambda p, i: (0, 2 * nb + p)),
            pl.BlockSpec((None, 2, n), lambda p, i: (p, 0, 0)),
        ],
        out_specs=pl.BlockSpec((tq, LANES), lambda p, i: (i, p)),
        out_shape=jax.ShapeDtypeStruct((n, FOX_WIDTH), BF16),
        scratch_shapes=[pltpu.VMEM((2, tq, LANES), BF16), pltpu.VMEM((2, tq, LANES), F32),
                        pltpu.VMEM((2, tq, LANES), F32), pltpu.VMEM((2, tq, LANES), F32)],
        compiler_params=_cparams(("arbitrary", "arbitrary")),
        name="fox_attn",
    )(qkv_bf16, qkv_bf16, qkv_bf16, ct.reshape(nb, 2, n))


_RW = RWKV_WIDTH
_RWKV_PERM = np.concatenate([
    np.arange(0, _RW),
    np.arange(_RW + DECAY_LORA, 2 * _RW + DECAY_LORA),
    np.arange(2 * _RW + DECAY_LORA, 3 * _RW + DECAY_LORA),
    np.arange(3 * _RW + DECAY_LORA + ICLR_LORA, RWKV_PROJ),
    np.arange(_RW, _RW + DECAY_LORA),
    np.arange(3 * _RW + DECAY_LORA, 3 * _RW + DECAY_LORA + ICLR_LORA),
])
_RWKV_INV_PERM = np.argsort(_RWKV_PERM)


def _head_block_ones(width, head_dim):
    idx = np.arange(width) // head_dim
    return jnp.asarray((idx[:, None] == idx[None, :]).astype(np.float32), BF16)


def _rwkv_prep_math(p, p_prev, mu, w0, w2p, a0, a2p, g2, k_k, k_a, bones):
    xs = p + mu * (p_prev - p)
    r = xs[:, 0:_RW]
    k = xs[:, _RW:2 * _RW]
    v = xs[:, 2 * _RW:3 * _RW]
    gl = xs[:, 3 * _RW:3 * _RW + GATE_LORA]
    wa = xs[:, 3 * _RW + GATE_LORA:]
    w = -_softplus(-(w0 + jnp.dot(jnp.tanh(wa).astype(BF16), w2p, preferred_element_type=F32))) - 0.5
    decay = jnp.exp(-jnp.exp(w))
    a = _sigmoid(a0 + jnp.dot(wa.astype(BF16), a2p, preferred_element_type=F32))
    g = jnp.dot(_sigmoid(gl).astype(BF16), g2, preferred_element_type=F32)
    kkr = k * k_k
    ss = _dot_f32_right(kkr * kkr, bones)
    kk = kkr / jnp.maximum(jnp.sqrt(ss), 1e-12)
    k2 = k * (1.0 + (a - 1.0) * k_a)
    return r, k2, v, kk, kk * a, decay, g


def _rwkv_prep_kernel(p_ref, prev_ref, mu_ref, w0_ref, w2_ref, a0_ref, a2_ref, g2_ref, kk_ref, ka_ref, bones_ref,
                      r_o, k_o, v_o, kk_o, b_o, d_o, g_o, last_o, buf_ref, *, t, shifted):
    p = p_ref[...]
    if shifted:
        @pl.when(pl.program_id(0) == 0)
        def _():
            buf_ref[SUBLANES - 1:SUBLANES, :] = prev_ref[...]

        buf_ref[SUBLANES:SUBLANES + t, :] = p
        p_prev = buf_ref[SUBLANES - 1:SUBLANES - 1 + t, :]
        buf_ref[SUBLANES - 1:SUBLANES, :] = p[t - 1:t, :]
        last_o[...] = p[t - 1:t, :]
    else:
        p_prev = prev_ref[...]
        last_o[...] = p
    outs = _rwkv_prep_math(p, p_prev, mu_ref[...], w0_ref[...], w2_ref[...], a0_ref[...], a2_ref[...], g2_ref[...],
                           kk_ref[...], ka_ref[...], bones_ref[...])
    for o_ref, val in zip((r_o, k_o, v_o, kk_o, b_o, d_o, g_o), outs):
        o_ref[...] = val


def _rwkv_prep(proj, prev, mu, w0, w2p, a0, a2p, g2, k_k, k_a, bones, *, t, shifted):
    n = proj.shape[0]
    row = lambda w: pl.BlockSpec((1, w), lambda i: (0, 0))
    full = lambda a: pl.BlockSpec(a.shape, lambda i: (0, 0))
    tok = lambda w: pl.BlockSpec((t, w), lambda i: (i, 0))
    prev_spec = row(RWKV_PROJ) if shifted else tok(RWKV_PROJ)
    last_spec = row(RWKV_PROJ) if shifted else tok(RWKV_PROJ)
    last_shape = (1, RWKV_PROJ) if shifted else (n, RWKV_PROJ)
    return pl.pallas_call(
        functools.partial(_rwkv_prep_kernel, t=t, shifted=shifted),
        grid=(n // t,),
        in_specs=[tok(RWKV_PROJ), prev_spec, row(RWKV_PROJ), row(_RW), full(w2p), row(_RW), full(a2p), full(g2),
                  row(_RW), row(_RW), full(bones)],
        out_specs=[tok(_RW)] * 7 + [last_spec],
        out_shape=[jax.ShapeDtypeStruct((n, _RW), F32)] * 7 + [jax.ShapeDtypeStruct(last_shape, F32)],
        scratch_shapes=[pltpu.VMEM((SUBLANES + t, RWKV_PROJ), F32)],
        compiler_params=_cparams(("arbitrary",)),
        name="rwkv_prep",
    )(proj, prev, mu, w0, w2p, a0, a2p, g2, k_k, k_a, bones)


def _rwkv_scan_kernel(r_ref, k_ref, v_ref, kk_ref, b_ref, d_ref, s0_ref, o_ref, st_ref, s_ref, *, t):
    j = pl.program_id(1)

    @pl.when(j == 0)
    def _():
        s_ref[...] = s0_ref[...]

    shape = (RWKV_HEAD_DIM, LANES)
    lane = lax.broadcasted_iota(I32, shape, 1)
    sub = lax.broadcasted_iota(I32, shape, 0)
    left = lane < RWKV_HEAD_DIM
    eye2 = (lane & (RWKV_HEAD_DIM - 1)) == sub

    def seg(x):
        s0 = jnp.sum(jnp.where(left, x, 0.0), axis=-1, keepdims=True)
        s1 = jnp.sum(jnp.where(left, 0.0, x), axis=-1, keepdims=True)
        return jnp.where(left, s0, s1)

    group = SUBLANES if t % SUBLANES == 0 else t

    def readout(s, r_rows):
        s_bd = jnp.concatenate([jnp.where(left, s, 0.0), jnp.where(left, 0.0, s)], axis=0).astype(BF16)
        return lax.dot_general(r_rows, s_bd, (((1,), (1,)), ((), ())), preferred_element_type=F32)

    def body(gi, states):
        base = pl.multiple_of(gi * group, group)
        states = list(states)
        tiles = []
        for p in range(HEAD_PAIRS):
            sl = slice(p * LANES, (p + 1) * LANES)
            tiles.append(tuple(ref[pl.ds(base, group), sl] for ref in (kk_ref, v_ref, d_ref, b_ref, k_ref, r_ref)))
        o_rows = [[] for _ in range(HEAD_PAIRS)]
        for u in range(group):
            row = lambda x: x[u:u + 1, :]
            for p in range(HEAD_PAIRS):
                kk_t, v_t, d_t, b_t, k_t, r_t = tiles[p]
                s = states[p]
                sk = seg(s * row(kk_t))
                vcol = seg(jnp.where(eye2, row(v_t), 0.0))
                s = s * row(d_t) - sk * row(b_t) + vcol * row(k_t)
                states[p] = s
                r_rows = jnp.broadcast_to(r_t, (SUBLANES, LANES)) if group == 1 else r_t
                o_rows[p].append(readout(s, r_rows.astype(BF16))[u:u + 1, :])
        for p in range(HEAD_PAIRS):
            sl = slice(p * LANES, (p + 1) * LANES)
            o_ref[pl.ds(base, group), sl] = jnp.concatenate(o_rows[p], axis=0) if group > 1 else o_rows[p][0]
        return tuple(states)

    states = lax.fori_loop(0, t // group, body, tuple(s_ref[p] for p in range(HEAD_PAIRS)))
    for p in range(HEAD_PAIRS):
        s_ref[p] = states[p]

    @pl.when(j == pl.num_programs(1) - 1)
    def _():
        st_ref[...] = s_ref[...]


def _rwkv_scan(r, k, v, kk, b, d, s0_pairs, *, batch, t):
    n = r.shape[0]
    seq = n // batch
    tok = pl.BlockSpec((None, t, _RW), lambda bi, j: (bi, j, 0))
    st = pl.BlockSpec((None, HEAD_PAIRS, RWKV_HEAD_DIM, LANES), lambda bi, j: (bi, 0, 0, 0))
    o, s_t = pl.pallas_call(
        functools.partial(_rwkv_scan_kernel, t=t),
        grid=(batch, seq // t),
        in_specs=[tok] * 6 + [st],
        out_specs=[tok, st],
        out_shape=[jax.ShapeDtypeStruct((batch, seq, _RW), F32),
                   jax.ShapeDtypeStruct((batch, HEAD_PAIRS, RWKV_HEAD_DIM, LANES), F32)],
        scratch_shapes=[pltpu.VMEM((HEAD_PAIRS, RWKV_HEAD_DIM, LANES), F32)],
        compiler_params=_cparams(("arbitrary", "arbitrary")),
        name="rwkv_scan",
    )(*(x.reshape(batch, seq, _RW) for x in (r, k, v, kk, b, d)), s0_pairs)
    return o.reshape(n, _RW), s_t


def _rwkv_post_kernel(o_ref, r_ref, k_ref, v_ref, g_ref, lnw_ref, lnb_ref, rk_ref, bones_ref, out_ref):
    bones = bones_ref[...]
    inv = 1.0 / RWKV_HEAD_DIM
    o = o_ref[...]
    mean = _dot_f32_right(o, bones) * inv
    cen = o - mean
    var = _dot_f32_right(cen * cen, bones) * inv
    gn = cen * lax.rsqrt(var + GN_EPS) * lnw_ref[...] + lnb_ref[...]
    bonus = _dot_f32_right(r_ref[...] * k_ref[...] * rk_ref[...], bones) * v_ref[...]
    out_ref[...] = ((gn + bonus) * g_ref[...]).astype(out_ref.dtype)


def _rwkv_post(o, r, k, v, g, lnw, lnb, rk, bones, *, t):
    n = o.shape[0]
    tok = pl.BlockSpec((t, _RW), lambda i: (i, 0))
    row = pl.BlockSpec((1, _RW), lambda i: (0, 0))
    return pl.pallas_call(
        _rwkv_post_kernel,
        grid=(n // t,),
        in_specs=[tok] * 5 + [row] * 3 + [pl.BlockSpec(bones.shape, lambda i: (0, 0))],
        out_specs=tok,
        out_shape=jax.ShapeDtypeStruct((n, _RW), BF16),
        compiler_params=_cparams(("parallel",)),
        name="rwkv_post",
    )(o, r, k, v, g, lnw, lnb, rk, bones)


def _to_pairs(s):
    b = s.shape[0]
    return s.reshape(b, HEAD_PAIRS, 2, RWKV_HEAD_DIM, RWKV_HEAD_DIM).transpose(0, 1, 3, 2, 4).reshape(
        b, HEAD_PAIRS, RWKV_HEAD_DIM, LANES)


def _from_pairs(s):
    b = s.shape[0]
    return s.reshape(b, HEAD_PAIRS, RWKV_HEAD_DIM, 2, RWKV_HEAD_DIM).transpose(0, 1, 3, 2, 4).reshape(
        b, RWKV_HEADS, RWKV_HEAD_DIM, RWKV_HEAD_DIM)


def _rwkv_mix(proj, prev, s0, rw, *, batch, t_prep, t_scan, shifted):
    mu, w0, w2, a0, a2, g2, k_k, k_a, r_k, ln_w, ln_b = rw
    bones = _head_block_ones(_RW, RWKV_HEAD_DIM)
    zeros = jnp.zeros((LANES - DECAY_LORA, _RW), F32)
    w2p = jnp.concatenate([w2, zeros], axis=0).astype(BF16)
    a2p = jnp.concatenate([zeros, a2], axis=0).astype(BF16)
    row = lambda x: x.reshape(1, -1)
    r, k, v, kk, b, d, g, last = _rwkv_prep(
        proj, prev, row(mu[_RWKV_PERM]), row(w0), w2p, row(a0), a2p, g2.astype(BF16), row(k_k), row(k_a), bones,
        t=t_prep, shifted=shifted)
    o, s_t = _rwkv_scan(r, k, v, kk, b, d, _to_pairs(s0), batch=batch, t=t_scan)
    out = _rwkv_post(o, r, k, v, g, row(ln_w), row(ln_b), row(r_k), bones, t=t_prep)
    return out, last[:, _RWKV_INV_PERM], _from_pairs(s_t)


def _top_rows(s, k, order):
    vals, ords = [], []
    for _ in range(k):
        m = jnp.max(s, axis=0, keepdims=True)
        pick = jnp.min(jnp.where(s == m, order, jnp.inf), axis=0, keepdims=True)
        vals.append(m)
        ords.append(pick)
        s = jnp.where(order == pick, -jnp.inf, s)
    return jnp.concatenate(vals, axis=0), jnp.concatenate(ords, axis=0)


def _take_rows(table, pos):
    out = jnp.zeros(pos.shape, table.dtype)
    for i in range(table.shape[0]):
        out = jnp.where(pos == float(i), table[i:i + 1, :], out)
    return out


_CAND_GROUPS = ((0, 2, 16), (2, 4, 8))
_CAND_TAIL_I0, _CAND_TAIL_J = 4, 3
_CAND_HEAD_ROWS = sum((hi - lo) * nj for lo, hi, nj in _CAND_GROUPS)
_CAND_ROWS = _CAND_HEAD_ROWS + _CAND_TAIL_J * PEER_TOPK


def _candidate_tables(tm):
    k = PEER_TOPK
    assert k == 16 and _CAND_GROUPS == ((0, 2, 16), (2, 4, 8))
    r = lax.broadcasted_iota(I32, (_CAND_ROWS, tm), 0)
    r2 = r - 2 * k
    flat2 = (2 + (r2 >> 3)) * k + (r2 & 7)
    r3 = r - _CAND_HEAD_ROWS
    i3 = r3 & (k - 1)
    flat3 = i3 * k + (r3 >> 4)
    flat = jnp.where(r < 2 * k, r, jnp.where(r < _CAND_HEAD_ROWS, flat2, flat3))
    valid = (r < _CAND_HEAD_ROWS) | (i3 >= _CAND_TAIL_I0)
    return flat.astype(F32), valid


def _peer_route_kernel(q_ref, keys_ref, g_ref, e1_ref, e2_ref, gv_ref, gt_ref, *, tm):
    k = PEER_TOPK
    flat, valid = _candidate_tables(tm)
    key_order = lax.broadcasted_iota(I32, (N_KEYS, tm), 0).astype(F32)

    e1s, e2s, gs = [], [], []
    for h in range(PEER_HEADS):
        qh = q_ref[:, h * D_KEY:(h + 1) * D_KEY].astype(BF16)
        score = lambda p: lax.dot_general(keys_ref[h, p], qh, (((1,), (1,)), ((), ())), preferred_element_type=F32)
        v1, i1 = _top_rows(score(0), k, key_order)
        v2, i2 = _top_rows(score(1), k, key_order)
        parts = [v1[i:i + 1, :] + v2[:nj, :] for lo, hi, nj in _CAND_GROUPS for i in range(lo, hi)]
        parts += [v1 + v2[j:j + 1, :] for j in range(_CAND_TAIL_J)]
        cand = jnp.where(valid, jnp.concatenate(parts, axis=0), -jnp.inf)
        sc, pos = _top_rows(cand, k, flat)
        pos_i = jnp.floor(pos * (1.0 / k))
        e1s.append(_take_rows(i1, pos_i))
        e2s.append(_take_rows(i2, pos - k * pos_i))
        ex = jnp.exp(sc - sc[0:1, :])
        gs.append(ex / jnp.sum(ex, axis=0, keepdims=True))
    e1_ref[...] = jnp.concatenate(e1s, axis=0).T
    e2_ref[...] = jnp.concatenate(e2s, axis=0).T
    gv_ref[...] = jnp.concatenate(gs, axis=0).T

    eid = lax.broadcasted_iota(I32, (N_KEYS, PEER_HEADS * k), 0).astype(F32)
    rows = _ROUTE_GROUP

    def body(gi, carry):
        base = pl.multiple_of(gi * rows, rows)
        e1_t = e1_ref[pl.ds(base, rows), :]
        e2_t = e2_ref[pl.ds(base, rows), :]
        gv_t = gv_ref[pl.ds(base, rows), :]
        for u in range(rows):
            a = jnp.where(eid == e1_t[u:u + 1, :], gv_t[u:u + 1, :], 0.0).astype(BF16)
            b = jnp.where(eid == e2_t[u:u + 1, :], 1.0, 0.0).astype(BF16)
            gt_ref[u * _ROUTE_PITCH:u * _ROUTE_PITCH + N_KEYS, :] = lax.dot_general(
                a, b, (((1,), (1,)), ((), ())), preferred_element_type=F32)
        for e1 in range(N_KEYS):
            slab = gt_ref[pl.ds(e1, rows, stride=_ROUTE_PITCH), :]
            g_ref[pl.ds(base, rows), e1 * N_KEYS:(e1 + 1) * N_KEYS] = slab.astype(g_ref.dtype)
        return carry

    lax.fori_loop(0, tm // rows, body, 0)


_ROUTE_GROUP = 2 * SUBLANES
_ROUTE_PITCH = N_KEYS + SUBLANES


def _peer_route(q, keys_pad, *, tm):
    n = q.shape[0]
    width = PEER_HEADS * PEER_TOPK
    assert tm % _ROUTE_GROUP == 0
    return pl.pallas_call(
        functools.partial(_peer_route_kernel, tm=tm),
        grid=(n // tm,),
        in_specs=[pl.BlockSpec((tm, PEER_HEADS * D_KEY), lambda i: (i, 0)),
                  pl.BlockSpec(keys_pad.shape, lambda i: (0, 0, 0, 0))],
        out_specs=pl.BlockSpec((tm, N_KEYS * N_KEYS), lambda i: (i, 0)),
        out_shape=jax.ShapeDtypeStruct((n, N_KEYS * N_KEYS), BF16),
        scratch_shapes=[pltpu.VMEM((tm, width), F32)] * 3 + [pltpu.VMEM((_ROUTE_GROUP * _ROUTE_PITCH, N_KEYS), F32)],
        compiler_params=_cparams(("parallel",)),
        name="peer_route",
    )(q, keys_pad)


def _peer_mlp_kernel(x_ref, gain_ref, u_ref, v_ref, g_ref, o_ref, xn_ref, acc_ref):
    j = pl.program_id(1)

    @pl.when(j == 0)
    def _():
        x = x_ref[...]
        ms = jnp.mean(x * x, axis=-1, keepdims=True)
        xn_ref[...] = (x * lax.rsqrt(ms + RMS_EPS) * gain_ref[...]).astype(BF16)
        acc_ref[...] = jnp.zeros_like(acc_ref)

    h = lax.dot_general(xn_ref[...], u_ref[...], (((1,), (1,)), ((), ())), preferred_element_type=F32)
    act = 0.5 * h * (1.0 + lax.erf(h * (2.0 ** -0.5)))
    w = (act * g_ref[...].astype(F32)).astype(BF16)
    acc_ref[...] += jnp.dot(w, v_ref[...], preferred_element_type=F32)

    @pl.when(j == pl.num_programs(1) - 1)
    def _():
        o_ref[...] = x_ref[...] + acc_ref[...]


def _peer_mlp(x, gain, u_bf16, v_bf16, gmap, *, tm, te):
    n, dm = x.shape
    ne = u_bf16.shape[0]
    return pl.pallas_call(
        _peer_mlp_kernel,
        grid=(n // tm, ne // te),
        in_specs=[pl.BlockSpec((tm, dm), lambda i, j: (i, 0)),
                  pl.BlockSpec((1, dm), lambda i, j: (0, 0)),
                  pl.BlockSpec((te, dm), lambda i, j: (j, 0)),
                  pl.BlockSpec((te, dm), lambda i, j: (j, 0)),
                  pl.BlockSpec((tm, te), lambda i, j: (i, j))],
        out_specs=pl.BlockSpec((tm, dm), lambda i, j: (i, 0)),
        out_shape=jax.ShapeDtypeStruct((n, dm), F32),
        scratch_shapes=[pltpu.VMEM((tm, dm), BF16), pltpu.VMEM((tm, dm), F32)],
        compiler_params=_cparams(("parallel", "arbitrary")),
        name="peer_mlp",
    )(x, gain.reshape(1, dm), u_bf16, v_bf16, gmap)


def _peer_keys_padded(keys):
    z = jnp.zeros_like(keys[:, 0])
    first = jnp.concatenate([keys[:, 0], z], axis=-1)
    second = jnp.concatenate([z, keys[:, 1]], axis=-1)
    return jnp.stack([first, second], axis=1).astype(BF16)


def _peer(y, gain, wq_bf16, keys_pad, u_bf16, v_bf16, *, tm_proj, tm_route, tm_mlp, te):
    n = y.shape[0]
    q = _norm_matmul(y, gain, wq_bf16, tm=tm_proj, tn=wq_bf16.shape[1])
    gmap = _peer_route(q, keys_pad, tm=tm_route)
    return _peer_mlp(y, gain, u_bf16, v_bf16, gmap, tm=tm_mlp, te=te)


def _conv_silu(x, buf_ref, w_ref, b_ref, q):
    buf_ref[SUBLANES:SUBLANES + q, :] = x
    acc = b_ref[...] + x * w_ref[CONV_W - 1:CONV_W, :]
    for back in range(1, CONV_W):
        acc = acc + buf_ref[SUBLANES - back:SUBLANES - back + q, :] * w_ref[CONV_W - 1 - back:CONV_W - back, :]
    tail = buf_ref[q + SUBLANES - (CONV_W - 1):q + SUBLANES, :]
    buf_ref[SUBLANES - (CONV_W - 1):SUBLANES, :] = tail
    return _silu(acc), tail


def _ssd_kernel(z_ref, x_ref, bc_ref, dt_ref, cwx_ref, cwbc_ref, cbx_ref, cbbc_ref, dtb_ref, alog_ref, dskip_ref,
                nw_ref, expand_ref, y_ref, convx_ref, convbc_ref, ht_ref, bufx_ref, bufbc_ref, h_ref, *, q, heads):
    hpg = heads // N_GROUPS
    gw = hpg * SSM_HEAD_DIM

    @pl.when(pl.program_id(0) == 0)
    def _():
        bufx_ref[...] = jnp.zeros_like(bufx_ref)
        bufbc_ref[...] = jnp.zeros_like(bufbc_ref)
        h_ref[...] = jnp.zeros_like(h_ref)

    xs, tail_x = _conv_silu(x_ref[...], bufx_ref, cwx_ref, cbx_ref, q)
    bc, tail_bc = _conv_silu(bc_ref[...], bufbc_ref, cwbc_ref, cbbc_ref, q)
    convx_ref[...] = tail_x
    convbc_ref[...] = tail_bc

    dt = _softplus(dt_ref[...] + dtb_ref[...])
    a = -jnp.exp(alog_ref[...])
    row = lax.broadcasted_iota(I32, (q, q), 0)
    col = lax.broadcasted_iota(I32, (q, q), 1)
    lower = col <= row
    tri = jnp.where(lower, 1.0, 0.0).astype(BF16)
    acs = _dot_f32_left(tri, dt * a)
    acs_t = acs.T
    expand = expand_ref[...]
    dt_e = _dot_f32_right(dt, expand)
    acs_e = _dot_f32_right(acs, expand)
    last_e = acs_e[q - 1:q, :]
    xc = xs * dt_e
    xcd = (xc * jnp.exp(last_e - acs_e)).astype(BF16)
    xc_b = xc.astype(BF16)
    grow = jnp.exp(acs_e)
    chunk_decay = jnp.exp(last_e)

    nbc = N_GROUPS * D_STATE
    for g in range(N_GROUPS):
        bm = bc[:, g * D_STATE:(g + 1) * D_STATE].astype(BF16)
        cm = bc[:, nbc + g * D_STATE:nbc + (g + 1) * D_STATE].astype(BF16)
        cb = lax.dot_general(cm, bm, (((1,), (1,)), ((), ())), preferred_element_type=F32)
        sl = slice(g * gw, (g + 1) * gw)
        h_prev = h_ref[g]
        y_off = jnp.dot(cm, h_prev.astype(BF16), preferred_element_type=F32) * grow[:, sl]
        y_diag = []
        for r in range(hpg):
            hd = g * hpg + r
            diff = acs[:, hd:hd + 1] - acs_t[hd:hd + 1, :]
            m = (cb * jnp.exp(jnp.where(lower, diff, -jnp.inf))).astype(BF16)
            y_diag.append(jnp.dot(m, xc_b[:, hd * SSM_HEAD_DIM:(hd + 1) * SSM_HEAD_DIM], preferred_element_type=F32))
        y_g = jnp.concatenate(y_diag, axis=1) + y_off + dskip_ref[:, sl] * xs[:, sl]
        states = jnp.dot(bm.T, xcd[:, sl], preferred_element_type=F32)
        h_ref[g] = h_prev * chunk_decay[:, sl] + states
        zg = z_ref[:, sl]
        y_g = y_g * _silu(zg)
        ms = jnp.mean(y_g * y_g, axis=-1, keepdims=True)
        y_ref[:, sl] = (y_g * lax.rsqrt(ms + RMS_EPS) * nw_ref[:, sl]).astype(y_ref.dtype)

    @pl.when(pl.program_id(0) == pl.num_programs(0) - 1)
    def _():
        ht_ref[...] = h_ref[...]


def _ssd(proj, conv_w, conv_b, dt_bias, a_log, d_skip, norm_w, *, heads):
    n = proj.shape[0]
    d_inner = heads * SSM_HEAD_DIM
    q = CHUNK
    nbc = 2 * N_GROUPS * D_STATE
    assert nbc == d_inner and n % q == 0
    gw = d_inner // N_GROUPS
    pad = lambda x: jnp.pad(x, (0, LANES - heads)).reshape(1, LANES)
    expand = jnp.asarray((np.arange(LANES)[:, None] == (np.arange(d_inner) // SSM_HEAD_DIM)[None, :]).astype(np.float32), BF16)
    blk = lambda w, j: pl.BlockSpec((q, w), lambda i, j=j: (i, j))
    full = lambda a: pl.BlockSpec(a.shape, lambda i: (0,) * a.ndim)
    small = [conv_w[:, :d_inner], conv_w[:, d_inner:], conv_b[:d_inner].reshape(1, -1), conv_b[d_inner:].reshape(1, -1),
             pad(dt_bias), pad(a_log), jnp.repeat(d_skip, SSM_HEAD_DIM).reshape(1, -1), norm_w.reshape(1, -1), expand]
    y, cx, cbc, ht = pl.pallas_call(
        functools.partial(_ssd_kernel, q=q, heads=heads),
        grid=(n // q,),
        in_specs=[blk(d_inner, 0), blk(d_inner, 1), blk(d_inner, 2), blk(LANES, 3 * d_inner // LANES)]
                 + [full(a) for a in small],
        out_specs=[pl.BlockSpec((q, d_inner), lambda i: (i, 0)),
                   pl.BlockSpec((CONV_W - 1, d_inner), lambda i: (0, 0)),
                   pl.BlockSpec((CONV_W - 1, nbc), lambda i: (0, 0)),
                   pl.BlockSpec((N_GROUPS, D_STATE, gw), lambda i: (0, 0, 0))],
        out_shape=[jax.ShapeDtypeStruct((n, d_inner), BF16),
                   jax.ShapeDtypeStruct((CONV_W - 1, d_inner), F32),
                   jax.ShapeDtypeStruct((CONV_W - 1, nbc), F32),
                   jax.ShapeDtypeStruct((N_GROUPS, D_STATE, gw), F32)],
        scratch_shapes=[pltpu.VMEM((SUBLANES + q, d_inner), F32), pltpu.VMEM((SUBLANES + q, nbc), F32),
                        pltpu.VMEM((N_GROUPS, D_STATE, gw), F32)],
        compiler_params=_cparams(("arbitrary",)),
        name="ssd",
    )(proj, proj, proj, proj, *small)
    hpg = heads // N_GROUPS
    h_t = ht.reshape(N_GROUPS, D_STATE, hpg, SSM_HEAD_DIM).transpose(0, 2, 3, 1).reshape(heads, SSM_HEAD_DIM, D_STATE)
    return y, jnp.concatenate([cx, cbc], axis=1), h_t


def _ssm_step_kernel(z_ref, x_ref, bc_ref, dt_ref, bufx_ref, bufbc_ref, h0_ref, cwx_ref, cwbc_ref, cbx_ref, cbbc_ref,
                     dtb_ref, alog_ref, dskip_ref, nw_ref, y_ref, nbufx_ref, nbufbc_ref, h_ref, *, heads):
    hpg = heads // N_GROUPS
    gw = hpg * SSM_HEAD_DIM
    nbc = N_GROUPS * D_STATE
    fullx = jnp.concatenate([bufx_ref[...], x_ref[...]], axis=1)
    nbufx_ref[...] = fullx[:, 1:]
    xs = _silu(cbx_ref[...] + jnp.sum(fullx * cwx_ref[...], axis=1, keepdims=True))
    fullbc = jnp.concatenate([bufbc_ref[...], bc_ref[...]], axis=0)
    nbufbc_ref[...] = fullbc[1:, :]
    bc = _silu(cbbc_ref[...] + jnp.sum(fullbc * cwbc_ref[...], axis=0, keepdims=True))
    dt = _softplus(dt_ref[...] + dtb_ref[...])
    dec = jnp.exp(dt * (-jnp.exp(alog_ref[...])))
    xdt = xs * dt
    for g in range(N_GROUPS):
        rows = slice(g * gw, (g + 1) * gw)
        bm = bc[:, g * D_STATE:(g + 1) * D_STATE]
        cm = bc[:, nbc + g * D_STATE:nbc + (g + 1) * D_STATE]
        h_new = h0_ref[rows, :] * dec[rows, :] + xdt[rows, :] * bm
        h_ref[rows, :] = h_new
        y = jnp.sum(h_new * cm, axis=-1, keepdims=True) + dskip_ref[rows, :] * xs[rows, :]
        y = y * _silu(z_ref[rows, :])
        ms = jnp.mean(y * y, axis=0, keepdims=True)
        y_ref[rows, :] = y * lax.rsqrt(ms + RMS_EPS) * nw_ref[rows, :]


def _ssm_step(proj, conv_buf, h0, conv_w, conv_b, dt_bias, a_log, d_skip, norm_w, *, heads):
    b = proj.shape[0]
    d_inner = heads * SSM_HEAD_DIM
    nbc2 = 2 * N_GROUPS * D_STATE
    col = lambda x: x.reshape(b, d_inner, 1)
    per_head = lambda v: jnp.repeat(v, SSM_HEAD_DIM, axis=-1)
    z = col(proj[:, :d_inner])
    x = col(proj[:, d_inner:2 * d_inner])
    bc = proj[:, 2 * d_inner:2 * d_inner + nbc2].reshape(b, 1, nbc2)
    dt = col(per_head(proj[:, 2 * d_inner + nbc2:2 * d_inner + nbc2 + heads]))
    bufx = conv_buf[:, :, :d_inner].transpose(0, 2, 1)
    bufbc = conv_buf[:, :, d_inner:]
    pcol = lambda v: v.reshape(d_inner, 1)
    params = [conv_w[:, :d_inner].T, conv_w[:, d_inner:], pcol(conv_b[:d_inner]), conv_b[d_inner:].reshape(1, nbc2),
              pcol(per_head(dt_bias)), pcol(per_head(a_log)), pcol(per_head(d_skip)), pcol(norm_w)]
    seq = lambda *shape: pl.BlockSpec((None,) + shape, lambda i: (i,) + (0,) * len(shape))
    full = lambda a: pl.BlockSpec(a.shape, lambda i: (0,) * a.ndim)
    y, nbx, nbbc, h = pl.pallas_call(
        functools.partial(_ssm_step_kernel, heads=heads),
        grid=(b,),
        in_specs=[seq(d_inner, 1), seq(d_inner, 1), seq(1, nbc2), seq(d_inner, 1), seq(d_inner, CONV_W - 1),
                  seq(CONV_W - 1, nbc2), seq(d_inner, D_STATE)] + [full(a) for a in params],
        out_specs=[seq(d_inner, 1), seq(d_inner, CONV_W - 1), seq(CONV_W - 1, nbc2), seq(d_inner, D_STATE)],
        out_shape=[jax.ShapeDtypeStruct((b, d_inner, 1), F32), jax.ShapeDtypeStruct((b, d_inner, CONV_W - 1), F32),
                   jax.ShapeDtypeStruct((b, CONV_W - 1, nbc2), F32), jax.ShapeDtypeStruct((b, d_inner, D_STATE), F32)],
        compiler_params=_cparams(("parallel",)),
        name="ssm_step",
    )(z, x, bc, dt, bufx, bufbc, h0.reshape(b, d_inner, D_STATE), *params)
    new_buf = jnp.concatenate([nbx.transpose(0, 2, 1), nbbc], axis=2)
    return y.reshape(b, d_inner), new_buf, h.reshape(b, heads, SSM_HEAD_DIM, D_STATE)


def _fox_decode_kernel(pt_ref, q_ref, kc_ref, vc_ref, lfc_ref, *refs, pages):
    k_refs = refs[:pages]
    v_refs = refs[pages:2 * pages]
    lf_refs = refs[2 * pages:3 * pages]
    o_ref, m_ref, l_ref, acc_ref, later_ref = refs[3 * pages:]
    j = pl.program_id(1)
    q = q_ref[...] * (FOX_HEAD_DIM ** -0.5)
    own_lane = (lax.broadcasted_iota(I32, (FOX_HEADS, FOX_HEAD_DIM), 0)
                == lax.broadcasted_iota(I32, (FOX_HEADS, FOX_HEAD_DIM), 1))

    @pl.when(j == 0)
    def _():
        m_ref[...] = jnp.sum(q * kc_ref[...], axis=-1, keepdims=True)
        l_ref[...] = jnp.ones_like(l_ref)
        acc_ref[...] = vc_ref[...]
        later_ref[...] = lfc_ref[...]

    trow = lax.broadcasted_iota(I32, (LANES, LANES), 0)
    ucol = lax.broadcasted_iota(I32, (LANES, LANES), 1)
    after = jnp.where(ucol > trow, 1.0, 0.0).astype(BF16)
    for u in range(pages):
        lf = jnp.concatenate([lf_refs[u][...], jnp.zeros((LANES, LANES - FOX_HEADS), F32)], axis=1)
        later = later_ref[...]
        bias = _dot_f32_left(after, lf) + later
        bias_tiles = jnp.where(own_lane[None], bias[:, None, :FOX_HEAD_DIM], 0.0)
        s = jnp.sum(k_refs[u][...] * q[None] + bias_tiles, axis=-1, keepdims=True)
        m_old = m_ref[...]
        m_new = jnp.maximum(m_old, jnp.max(s, axis=0))
        alpha = jnp.exp(m_old - m_new)
        p = jnp.exp(s - m_new[None])
        l_ref[...] = alpha * l_ref[...] + jnp.sum(p, axis=0)
        m_ref[...] = m_new
        acc_ref[...] = alpha * acc_ref[...] + jnp.sum(p * v_refs[u][...], axis=0)
        later_ref[...] = later + jnp.sum(lf, axis=0, keepdims=True)

    @pl.when(j == pl.num_programs(1) - 1)
    def _():
        o_ref[...] = acc_ref[...] / l_ref[...]


def _fox_decode(q, k_cur, v_cur, lf_cur_pad, cache_k, cache_v, cache_logf, page_table, *, pages):
    b, n_pages = page_table.shape
    page = cache_k.shape[1]
    assert page == LANES and n_pages % pages == 0
    tok = pl.BlockSpec((None, FOX_HEADS, FOX_HEAD_DIM), lambda bi, j, pt: (bi, 0, 0))
    newest_first = lambda bi, j, pt, u: pt[bi, n_pages - 1 - (j * pages + u)]
    kv_page = lambda u: pl.BlockSpec((None, page, FOX_HEADS, FOX_HEAD_DIM),
                                     lambda bi, j, pt, u=u: (newest_first(bi, j, pt, u), 0, 0, 0))
    lf_page = lambda u: pl.BlockSpec((None, page, FOX_HEADS), lambda bi, j, pt, u=u: (newest_first(bi, j, pt, u), 0, 0))
    return pl.pallas_call(
        functools.partial(_fox_decode_kernel, pages=pages),
        grid_spec=pltpu.PrefetchScalarGridSpec(
            num_scalar_prefetch=1,
            grid=(b, n_pages // pages),
            in_specs=[tok, tok, tok, pl.BlockSpec((None, 1, LANES), lambda bi, j, pt: (bi, 0, 0))]
                     + [kv_page(u) for u in range(pages)] * 2 + [lf_page(u) for u in range(pages)],
            out_specs=tok,
            scratch_shapes=[pltpu.VMEM((FOX_HEADS, 1), F32), pltpu.VMEM((FOX_HEADS, 1), F32),
                            pltpu.VMEM((FOX_HEADS, FOX_HEAD_DIM), F32), pltpu.VMEM((1, LANES), F32)],
        ),
        out_shape=jax.ShapeDtypeStruct((b, FOX_HEADS, FOX_HEAD_DIM), F32),
        compiler_params=_cparams(("parallel", "arbitrary")),
        name="fox_decode",
    )(page_table, q, k_cur, v_cur, lf_cur_pad.reshape(b, 1, LANES),
      *([cache_k] * pages), *([cache_v] * pages), *([cache_logf] * pages))


_COL_Q = RWKV_PROJ
_COL_K = _COL_Q + FOX_WIDTH
_COL_V = _COL_K + FOX_WIDTH
_COL_FL = _COL_V + FOX_WIDTH
_EVEN_COLS = _COL_FL + LANES


def _even_weights(w_in):
    q, k, v = (w_in[:, i * FOX_WIDTH:(i + 1) * FOX_WIDTH] for i in range(3))
    fl = w_in[:, 3 * FOX_WIDTH:3 * FOX_WIDTH + FOX_HEADS]
    rw = w_in[:, 3 * FOX_WIDTH + FOX_HEADS:][:, _RWKV_PERM]
    fl_pad = jnp.pad(fl, ((0, 0), (0, LANES - FOX_HEADS)))
    return jnp.concatenate([rw, q, k, v, fl_pad], axis=1).astype(BF16)


def _even_layer(y, gain, w_cat, w_out, fb, rw, *, batch, seq, past, shift0, wkv0):
    n = y.shape[0]
    tm = min(512, n)
    proj = _norm_matmul(y, gain, w_cat, tm=tm, tn=_EVEN_COLS // 3)
    fb_pad = jnp.pad(fb, (0, LANES - FOX_HEADS)).reshape(1, LANES)
    logf, ct = _fox_prep(proj, _COL_FL // LANES, fb_pad, t=min(256, n))
    k_new = proj[:, _COL_K:_COL_V]
    v_new = proj[:, _COL_V:_COL_FL]
    if past is None:
        qkv = proj[:, _COL_Q:_COL_FL].astype(BF16)
        o_fox = _fox_attn(qkv, ct, tq=1024)
        o_rwkv, shift_new, wkv_new = _rwkv_mix(proj, shift0[:, _RWKV_PERM], wkv0, rw, batch=batch,
                                               t_prep=256, t_scan=256, shifted=True)
    else:
        cache_k, cache_v, cache_logf, page_table = past
        lf_pad = jnp.pad(logf, ((0, 0), (0, LANES - FOX_HEADS)))
        heads3 = lambda t: t.reshape(n, FOX_HEADS, FOX_HEAD_DIM)
        o_fox = _fox_decode(heads3(proj[:, _COL_Q:_COL_K]), heads3(k_new), heads3(v_new), lf_pad,
                            cache_k, cache_v, cache_logf, page_table, pages=4).reshape(n, FOX_WIDTH)
        o_rwkv, shift_new, wkv_new = _rwkv_mix(proj, shift0[:, _RWKV_PERM], wkv0, rw, batch=batch,
                                               t_prep=n, t_scan=1, shifted=False)
    w_out_b = w_out.astype(BF16)
    y = _matmul_res([o_fox, o_rwkv], [w_out_b[:FOX_WIDTH], w_out_b[FOX_WIDTH:]], y, tm=tm, tn=y.shape[1])
    heads = lambda t: t.reshape(batch, seq, FOX_HEADS, FOX_HEAD_DIM)
    return y, heads(k_new), heads(v_new), logf.reshape(batch, seq, FOX_HEADS), wkv_new, shift_new


def kernel(x_prompt, x_sample, cache_k, cache_v, cache_logf, page_table, state_wkv, state_shift, state_conv, state_ssm, norm_mix, norm_ffn, norm_final, w_in_even, w_out_even, fox_fb, rwkv_mu, rwkv_w0, rwkv_w2, rwkv_a0, rwkv_a2, rwkv_g2, rwkv_kk, rwkv_ka, rwkv_rk, rwkv_lnw, rwkv_lnb, w_in_odd, w_out_odd, ssm_conv_w, ssm_conv_b, ssm_dt_bias, ssm_a_log, ssm_d, ssm_norm_w, peer_wq, peer_keys, peer_u, peer_v):
    bp, seq_p, dm = x_prompt.shape
    bs, seq_s, _ = x_sample.shape
    assert bp == 1 and seq_s == 1
    depth = norm_mix.shape[0]
    ssm_heads = ssm_dt_bias.shape[1]
    yp = x_prompt.reshape(bp * seq_p, dm)
    ys = x_sample.reshape(bs * seq_s, dm)
    outs_p = {name: [] for name in ("k", "v", "logf", "wkv", "shift", "conv", "ssm")}
    outs_s = {name: [] for name in outs_p}
    for layer in range(depth):
        if layer % 2 == 0:
            e = layer // 2
            rw = (rwkv_mu[e], rwkv_w0[e], rwkv_w2[e], rwkv_a0[e], rwkv_a2[e], rwkv_g2[e],
                  rwkv_kk[e], rwkv_ka[e], rwkv_rk[e], rwkv_lnw[e], rwkv_lnb[e])
            w_cat = _even_weights(w_in_even[e])
            yp, *res_p = _even_layer(
                yp, norm_mix[layer], w_cat, w_out_even[e], fox_fb[e], rw, batch=bp, seq=seq_p, past=None,
                shift0=jnp.zeros((bp, RWKV_PROJ), F32),
                wkv0=jnp.zeros((bp, RWKV_HEADS, RWKV_HEAD_DIM, RWKV_HEAD_DIM), F32))
            ys, *res_s = _even_layer(
                ys, norm_mix[layer], w_cat, w_out_even[e], fox_fb[e], rw, batch=bs, seq=seq_s,
                past=(cache_k[e], cache_v[e], cache_logf[e], page_table), shift0=state_shift[e], wkv0=state_wkv[e])
            for outs, res in ((outs_p, res_p), (outs_s, res_s)):
                for name, val in zip(("k", "v", "logf", "wkv", "shift"), res):
                    outs[name].append(val)
        else:
            o = layer // 2
            mprm = (ssm_conv_w[o], ssm_conv_b[o], ssm_dt_bias[o], ssm_a_log[o], ssm_d[o], ssm_norm_w[o])
            w_cat = jnp.pad(w_in_odd[o], ((0, 0), (0, LANES - ssm_heads))).astype(BF16)
            w_out = w_out_odd[o].astype(BF16)
            tn = w_cat.shape[1] // 7
            proj_p = _norm_matmul(yp, norm_mix[layer], w_cat, tm=512, tn=tn)
            y_m, conv_p, h_p = _ssd(proj_p, *mprm, heads=ssm_heads)
            yp = _matmul_res([y_m], [w_out], yp, tm=512, tn=dm)
            proj_s = _norm_matmul(ys, norm_mix[layer], w_cat, tm=bs, tn=tn)
            y_s, conv_s, h_s = _ssm_step(proj_s, state_conv[o], state_ssm[o], *mprm, heads=ssm_heads)
            ys = _matmul_res([y_s], [w_out], ys, tm=bs, tn=dm)
            outs_p["conv"].append(conv_p[None])
            outs_p["ssm"].append(h_p[None])
            outs_s["conv"].append(conv_s)
            outs_s["ssm"].append(h_s)
        wq = peer_wq[layer].astype(BF16)
        keys_pad = _peer_keys_padded(peer_keys[layer])
        u_b = peer_u[layer].astype(BF16)
        v_b = peer_v[layer].astype(BF16)
        yp = _peer(yp, norm_ffn[layer], wq, keys_pad, u_b, v_b, tm_proj=512, tm_route=128, tm_mlp=1024, te=1024)
        ys = _peer(ys, norm_ffn[layer], wq, keys_pad, u_b, v_b, tm_proj=bs, tm_route=bs, tm_mlp=bs, te=2048)
    y_prompt = _rmsnorm(yp, norm_final, tm=512).reshape(bp, seq_p, dm)
    y_sample = _rmsnorm(ys, norm_final, tm=bs).reshape(bs, seq_s, dm)
    order = ("k", "v", "logf", "wkv", "shift", "conv", "ssm")
    return ((y_prompt, y_sample) + tuple(jnp.stack(outs_p[name]) for name in order)
            + tuple(jnp.stack(outs_s[name]) for name in order))
```

```python
import functools
import math

import numpy as np
import jax
import jax.numpy as jnp
from jax import lax
from jax.experimental import pallas as pl
from jax.experimental.pallas import tpu as pltpu

F32 = jnp.float32
BF16 = jnp.bfloat16
I32 = jnp.int32

LANES = 128
SUBLANES = 8
V7X_VMEM_BYTES = 64 * 1024 * 1024

RMS_EPS = 1e-6
GN_EPS = 64e-5

FOX_HEADS = 8
FOX_HEAD_DIM = 64
FOX_WIDTH = FOX_HEADS * FOX_HEAD_DIM
RWKV_HEADS = 8
RWKV_HEAD_DIM = 64
RWKV_WIDTH = RWKV_HEADS * RWKV_HEAD_DIM
DECAY_LORA = 64
ICLR_LORA = 64
GATE_LORA = 128
RWKV_PROJ = 4 * RWKV_WIDTH - RWKV_WIDTH + DECAY_LORA + ICLR_LORA + GATE_LORA
HEAD_PAIRS = RWKV_HEADS // 2

SSM_HEAD_DIM = 64
D_STATE = 128
N_GROUPS = 8
CONV_W = 4
CHUNK = 128

N_KEYS = 128
PEER_HEADS = 8
PEER_TOPK = 16
D_KEY = 128


def _cparams(semantics, vmem_mb=48):
    return pltpu.CompilerParams(dimension_semantics=semantics, vmem_limit_bytes=vmem_mb * 1024 * 1024)


def _split3(x):
    hi = x.astype(BF16)
    r1 = x - hi.astype(F32)
    mid = r1.astype(BF16)
    lo = (r1 - mid.astype(F32)).astype(BF16)
    return hi, mid, lo


def _dot_f32_right(x, m_bf16):
    hi, mid, lo = _split3(x)
    d = lambda a: jnp.dot(a, m_bf16, preferred_element_type=F32)
    return d(hi) + d(mid) + d(lo)


def _dot_f32_left(m_bf16, x):
    hi, mid, lo = _split3(x)
    d = lambda a: jnp.dot(m_bf16, a, preferred_element_type=F32)
    return d(hi) + d(mid) + d(lo)


def _sigmoid(x):
    return 1.0 / (1.0 + jnp.exp(-x))


def _softplus(x):
    return jnp.maximum(x, 0.0) + jnp.log1p(jnp.exp(-jnp.abs(x)))


def _silu(x):
    return x * _sigmoid(x)


def _norm_matmul_kernel(x_ref, g_ref, w_ref, o_ref, xn_ref):
    @pl.when(pl.program_id(1) == 0)
    def _():
        x = x_ref[...]
        ms = jnp.mean(x * x, axis=-1, keepdims=True)
        xn_ref[...] = (x * lax.rsqrt(ms + RMS_EPS) * g_ref[...]).astype(BF16)

    o_ref[...] = jnp.dot(xn_ref[...], w_ref[...], preferred_element_type=F32).astype(o_ref.dtype)


def _norm_matmul(x, g, w_bf16, *, tm, tn, out_dtype=F32):
    m, k = x.shape
    n = w_bf16.shape[1]
    assert m % tm == 0 and n % tn == 0
    return pl.pallas_call(
        _norm_matmul_kernel,
        grid=(m // tm, n // tn),
        in_specs=[
            pl.BlockSpec((tm, k), lambda i, j: (i, 0)),
            pl.BlockSpec((1, k), lambda i, j: (0, 0)),
            pl.BlockSpec((k, tn), lambda i, j: (0, j)),
        ],
        out_specs=pl.BlockSpec((tm, tn), lambda i, j: (i, j)),
        out_shape=jax.ShapeDtypeStruct((m, n), out_dtype),
        scratch_shapes=[pltpu.VMEM((tm, k), BF16)],
        compiler_params=_cparams(("parallel", "arbitrary")),
        name="norm_matmul",
    )(x, g.reshape(1, k), w_bf16)


def _matmul_res_kernel(*refs, n_in):
    a_refs = refs[:n_in]
    w_refs = refs[n_in:2 * n_in]
    r_ref, o_ref = refs[2 * n_in], refs[2 * n_in + 1]
    acc = r_ref[...]
    for a_ref, w_ref in zip(a_refs, w_refs):
        acc = acc + jnp.dot(a_ref[...].astype(BF16), w_ref[...], preferred_element_type=F32)
    o_ref[...] = acc


def _matmul_res(a_list, w_list, res, *, tm, tn):
    m, n = res.shape
    n_in = len(a_list)
    assert m % tm == 0 and n % tn == 0
    in_specs = ([pl.BlockSpec((tm, a.shape[1]), lambda i, j: (i, 0)) for a in a_list]
                + [pl.BlockSpec((w.shape[0], tn), lambda i, j: (0, j)) for w in w_list]
                + [pl.BlockSpec((tm, tn), lambda i, j: (i, j))])
    return pl.pallas_call(
        functools.partial(_matmul_res_kernel, n_in=n_in),
        grid=(m // tm, n // tn),
        in_specs=in_specs,
        out_specs=pl.BlockSpec((tm, tn), lambda i, j: (i, j)),
        out_shape=jax.ShapeDtypeStruct((m, n), F32),
        compiler_params=_cparams(("parallel", "parallel")),
        name="matmul_res",
    )(*a_list, *w_list, res)


def _rmsnorm_kernel(x_ref, g_ref, o_ref):
    x = x_ref[...]
    ms = jnp.mean(x * x, axis=-1, keepdims=True)
    o_ref[...] = x * lax.rsqrt(ms + RMS_EPS) * g_ref[...]


def _rmsnorm(x, g, *, tm):
    m, k = x.shape
    return pl.pallas_call(
        _rmsnorm_kernel,
        grid=(m // tm,),
        in_specs=[pl.BlockSpec((tm, k), lambda i: (i, 0)), pl.BlockSpec((1, k), lambda i: (0, 0))],
        out_specs=pl.BlockSpec((tm, k), lambda i: (i, 0)),
        out_shape=jax.ShapeDtypeStruct((m, k), F32),
        compiler_params=_cparams(("parallel",)),
        name="rmsnorm",
    )(x, g.reshape(1, k))


def _fox_prep_kernel(fl_ref, fb_ref, lf_ref, ct_ref, carry_ref, *, t):
    @pl.when(pl.program_id(0) == 0)
    def _():
        carry_ref[...] = jnp.zeros_like(carry_ref)

    x = fl_ref[...] + fb_ref[...]
    lf = jnp.minimum(x, 0.0) - jnp.log1p(jnp.exp(-jnp.abs(x)))
    lf_ref[...] = lf[:, :FOX_HEADS]
    row = lax.broadcasted_iota(I32, (t, t), 0)
    col = lax.broadcasted_iota(I32, (t, t), 1)
    tri = jnp.where(col <= row, 1.0, 0.0).astype(BF16)
    c = _dot_f32_left(tri, lf) + carry_ref[...]
    carry_ref[...] = c[t - 1:t, :]
    ct_ref[...] = c.T[:FOX_HEADS, :]


def _fox_prep(proj, fl_block, fb_pad, *, t):
    n = proj.shape[0]
    return pl.pallas_call(
        functools.partial(_fox_prep_kernel, t=t),
        grid=(n // t,),
        in_specs=[pl.BlockSpec((t, LANES), lambda i: (i, fl_block)),
                  pl.BlockSpec((1, LANES), lambda i: (0, 0))],
        out_specs=[pl.BlockSpec((t, FOX_HEADS), lambda i: (i, 0)),
                   pl.BlockSpec((FOX_HEADS, t), lambda i: (0, i))],
        out_shape=[jax.ShapeDtypeStruct((n, FOX_HEADS), F32),
                   jax.ShapeDtypeStruct((FOX_HEADS, n), F32)],
        scratch_shapes=[pltpu.VMEM((1, LANES), F32)],
        compiler_params=_cparams(("arbitrary",)),
        name="fox_prep",
    )(proj, fb_pad)


def _fox_attn_kernel(q_ref, k_ref, v_ref, ct_ref, o_ref, qs_ref, m_ref, l_ref, acc_ref, *, tq):
    i = pl.program_id(1)
    lane = lax.broadcasted_iota(I32, (1, LANES), 1)
    left = lane < FOX_HEAD_DIM
    q = q_ref[...] * jnp.asarray(FOX_HEAD_DIM ** -0.5, BF16)
    zero = jnp.zeros_like(q)
    qs_ref[0] = jnp.where(left, q, zero)
    qs_ref[1] = jnp.where(left, zero, q)
    q0 = pl.multiple_of(i * tq, tq)
    cref = [ct_ref[h:h + 1, pl.ds(q0, LANES)][:, 0:1] for h in range(2)]

    m_ref[...] = jnp.full_like(m_ref, -jnp.inf)
    l_ref[...] = jnp.zeros_like(l_ref)
    acc_ref[...] = jnp.zeros_like(acc_ref)
    ncol = tq // LANES

    def step(j, masked):
        k0 = pl.multiple_of(j * tq, tq)
        kb = k_ref[pl.ds(k0, tq), :]
        vb = v_ref[pl.ds(k0, tq), :]
        bias = [cref[h] - ct_ref[h:h + 1, pl.ds(k0, tq)] for h in range(2)]
        for h in range(2):
            s = lax.dot_general(qs_ref[h], kb, (((1,), (1,)), ((), ())), preferred_element_type=F32)
            s = s + bias[h]
            if masked:
                r = lax.broadcasted_iota(I32, (tq, tq), 0)
                c = lax.broadcasted_iota(I32, (tq, tq), 1)
                s = jnp.where(c <= r, s, -jnp.inf)
            cols = [s[:, c * LANES:(c + 1) * LANES] for c in range(ncol)]
            lane_max = functools.reduce(jnp.maximum, cols)
            m_old = m_ref[h]
            m_new = jnp.maximum(m_old, jnp.max(lane_max, axis=-1, keepdims=True))
            alpha = jnp.exp(m_old - m_new)
            ps = [jnp.exp(c - m_new) for c in cols]
            l_ref[h] = alpha * l_ref[h] + functools.reduce(jnp.add, ps)
            m_ref[h] = m_new
            p = jnp.concatenate([x.astype(BF16) for x in ps], axis=1)
            acc_ref[h] = alpha * acc_ref[h] + jnp.dot(p, vb, preferred_element_type=F32)

    def body(j, carry):
        step(j, False)
        return carry

    lax.fori_loop(0, i, body, 0)
    step(i, True)
    l0 = jnp.sum(l_ref[0], axis=-1, keepdims=True)
    l1 = jnp.sum(l_ref[1], axis=-1, keepdims=True)
    o_ref[...] = jnp.where(left, acc_ref[0] / l0, acc_ref[1] / l1).astype(o_ref.dtype)


def _fox_attn(qkv_bf16, ct, *, tq):
    n = qkv_bf16.shape[0]
    nb = FOX_WIDTH // LANES
    return pl.pallas_call(
        functools.partial(_fox_attn_kernel, tq=tq),
        grid=(nb, n // tq),
        in_specs=[
            pl.BlockSpec((tq, LANES), lambda p, i: (i, p)),
            pl.BlockSpec((n, LANES), lambda p, i: (0, nb + p)),
            pl.BlockSpec((n, LANES), lambda p, i: (0, 2 * nb + p)),
            pl.BlockSpec((None, 2, n), lambda p, i: (p, 0, 0)),
        ],
        out_specs=pl.BlockSpec((tq, LANES), lambda p, i: (i, p)),
        out_shape=jax.ShapeDtypeStruct((n, FOX_WIDTH), BF16),
        scratch_shapes=[pltpu.VMEM((2, tq, LANES), BF16), pltpu.VMEM((2, tq, LANES), F32),
                        pltpu.VMEM((2, tq, LANES), F32), pltpu.VMEM((2, tq, LANES), F32)],
        compiler_params=_cparams(("arbitrary", "arbitrary")),
        name="fox_attn",
    )(qkv_bf16, qkv_bf16, qkv_bf16, ct.reshape(nb, 2, n))


_RW = RWKV_WIDTH
_RWKV_PERM = np.concatenate([
    np.arange(0, _RW),
    np.arange(_RW + DECAY_LORA, 2 * _RW + DECAY_LORA),
    np.arange(2 * _RW + DECAY_LORA, 3 * _RW + DECAY_LORA),
    np.arange(3 * _RW + DECAY_LORA + ICLR_LORA, RWKV_PROJ),
    np.arange(_RW, _RW + DECAY_LORA),
    np.arange(3 * _RW + DECAY_LORA, 3 * _RW + DECAY_LORA + ICLR_LORA),
])
_RWKV_INV_PERM = np.argsort(_RWKV_PERM)


def _head_block_ones(width, head_dim):
    idx = np.arange(width) // head_dim
    return jnp.asarray((idx[:, None] == idx[None, :]).astype(np.float32), BF16)


def _rwkv_prep_math(p, p_prev, mu, w0, w2p, a0, a2p, g2, k_k, k_a, bones):
    xs = p + mu * (p_prev - p)
    r = xs[:, 0:_RW]
    k = xs[:, _RW:2 * _RW]
    v = xs[:, 2 * _RW:3 * _RW]
    gl = xs[:, 3 * _RW:3 * _RW + GATE_LORA]
    wa = xs[:, 3 * _RW + GATE_LORA:]
    w = -_softplus(-(w0 + jnp.dot(jnp.tanh(wa).astype(BF16), w2p, preferred_element_type=F32))) - 0.5
    decay = jnp.exp(-jnp.exp(w))
    a = _sigmoid(a0 + jnp.dot(wa.astype(BF16), a2p, preferred_element_type=F32))
    g = jnp.dot(_sigmoid(gl).astype(BF16), g2, preferred_element_type=F32)
    kkr = k * k_k
    ss = _dot_f32_right(kkr * kkr, bones)
    kk = kkr / jnp.maximum(jnp.sqrt(ss), 1e-12)
    k2 = k * (1.0 + (a - 1.0) * k_a)
    return r, k2, v, kk, kk * a, decay, g


def _rwkv_prep_kernel(p_ref, prev_ref, mu_ref, w0_ref, w2_ref, a0_ref, a2_ref, g2_ref, kk_ref, ka_ref, bones_ref,
                      r_o, k_o, v_o, kk_o, b_o, d_o, g_o, last_o, buf_ref, *, t, shifted):
    p = p_ref[...]
    if shifted:
        @pl.when(pl.program_id(0) == 0)
        def _():
            buf_ref[SUBLANES - 1:SUBLANES, :] = prev_ref[...]

        buf_ref[SUBLANES:SUBLANES + t, :] = p
        p_prev = buf_ref[SUBLANES - 1:SUBLANES - 1 + t, :]
        buf_ref[SUBLANES - 1:SUBLANES, :] = p[t - 1:t, :]
        last_o[...] = p[t - 1:t, :]
    else:
        p_prev = prev_ref[...]
        last_o[...] = p
    outs = _rwkv_prep_math(p, p_prev, mu_ref[...], w0_ref[...], w2_ref[...], a0_ref[...], a2_ref[...], g2_ref[...],
                           kk_ref[...], ka_ref[...], bones_ref[...])
    for o_ref, val in zip((r_o, k_o, v_o, kk_o, b_o, d_o, g_o), outs):
        o_ref[...] = val


def _rwkv_prep(proj, prev, mu, w0, w2p, a0, a2p, g2, k_k, k_a, bones, *, t, shifted):
    n = proj.shape[0]
    row = lambda w: pl.BlockSpec((1, w), lambda i: (0, 0))
    full = lambda a: pl.BlockSpec(a.shape, lambda i: (0, 0))
    tok = lambda w: pl.BlockSpec((t, w), lambda i: (i, 0))
    prev_spec = row(RWKV_PROJ) if shifted else tok(RWKV_PROJ)
    last_spec = row(RWKV_PROJ) if shifted else tok(RWKV_PROJ)
    last_shape = (1, RWKV_PROJ) if shifted else (n, RWKV_PROJ)
    return pl.pallas_call(
        functools.partial(_rwkv_prep_kernel, t=t, shifted=shifted),
        grid=(n // t,),
        in_specs=[tok(RWKV_PROJ), prev_spec, row(RWKV_PROJ), row(_RW), full(w2p), row(_RW), full(a2p), full(g2),
                  row(_RW), row(_RW), full(bones)],
        out_specs=[tok(_RW)] * 7 + [last_spec],
        out_shape=[jax.ShapeDtypeStruct((n, _RW), F32)] * 7 + [jax.ShapeDtypeStruct(last_shape, F32)],
        scratch_shapes=[pltpu.VMEM((SUBLANES + t, RWKV_PROJ), F32)],
        compiler_params=_cparams(("arbitrary",)),
        name="rwkv_prep",
    )(proj, prev, mu, w0, w2p, a0, a2p, g2, k_k, k_a, bones)


def _rwkv_scan_kernel(r_ref, k_ref, v_ref, kk_ref, b_ref, d_ref, s0_ref, o_ref, st_ref, s_ref, *, t):
    j = pl.program_id(1)

    @pl.when(j == 0)
    def _():
        s_ref[...] = s0_ref[...]

    shape = (RWKV_HEAD_DIM, LANES)
    lane = lax.broadcasted_iota(I32, shape, 1)
    sub = lax.broadcasted_iota(I32, shape, 0)
    left = lane < RWKV_HEAD_DIM
    eye2 = (lane & (RWKV_HEAD_DIM - 1)) == sub

    def seg(x):
        s0 = jnp.sum(jnp.where(left, x, 0.0), axis=-1, keepdims=True)
        s1 = jnp.sum(jnp.where(left, 0.0, x), axis=-1, keepdims=True)
        return jnp.where(left, s0, s1)

    group = SUBLANES if t % SUBLANES == 0 else t

    def readout(s, r_rows):
        s_bd = jnp.concatenate([jnp.where(left, s, 0.0), jnp.where(left, 0.0, s)], axis=0).astype(BF16)
        return lax.dot_general(r_rows, s_bd, (((1,), (1,)), ((), ())), preferred_element_type=F32)

    def body(gi, states):
        base = pl.multiple_of(gi * group, group)
        states = list(states)
        tiles = []
        for p in range(HEAD_PAIRS):
            sl = slice(p * LANES, (p + 1) * LANES)
            tiles.append(tuple(ref[pl.ds(base, group), sl] for ref in (kk_ref, v_ref, d_ref, b_ref, k_ref, r_ref)))
        o_rows = [[] for _ in range(HEAD_PAIRS)]
        for u in range(group):
            row = lambda x: x[u:u + 1, :]
            for p in range(HEAD_PAIRS):
                kk_t, v_t, d_t, b_t, k_t, r_t = tiles[p]
                s = states[p]
                sk = seg(s * row(kk_t))
                vcol = seg(jnp.where(eye2, row(v_t), 0.0))
                s = s * row(d_t) - sk * row(b_t) + vcol * row(k_t)
                states[p] = s
                r_rows = jnp.broadcast_to(r_t, (SUBLANES, LANES)) if group == 1 else r_t
                o_rows[p].append(readout(s, r_rows.astype(BF16))[u:u + 1, :])
        for p in range(HEAD_PAIRS):
            sl = slice(p * LANES, (p + 1) * LANES)
            o_ref[pl.ds(base, group), sl] = jnp.concatenate(o_rows[p], axis=0) if group > 1 else o_rows[p][0]
        return tuple(states)

    states = lax.fori_loop(0, t // group, body, tuple(s_ref[p] for p in range(HEAD_PAIRS)))
    for p in range(HEAD_PAIRS):
        s_ref[p] = states[p]

    @pl.when(j == pl.num_programs(1) - 1)
    def _():
        st_ref[...] = s_ref[...]


def _rwkv_scan(r, k, v, kk, b, d, s0_pairs, *, batch, t):
    n = r.shape[0]
    seq = n // batch
    tok = pl.BlockSpec((None, t, _RW), lambda bi, j: (bi, j, 0))
    st = pl.BlockSpec((None, HEAD_PAIRS, RWKV_HEAD_DIM, LANES), lambda bi, j: (bi, 0, 0, 0))
    o, s_t = pl.pallas_call(
        functools.partial(_rwkv_scan_kernel, t=t),
        grid=(batch, seq // t),
        in_specs=[tok] * 6 + [st],
        out_specs=[tok, st],
        out_shape=[jax.ShapeDtypeStruct((batch, seq, _RW), F32),
                   jax.ShapeDtypeStruct((batch, HEAD_PAIRS, RWKV_HEAD_DIM, LANES), F32)],
        scratch_shapes=[pltpu.VMEM((HEAD_PAIRS, RWKV_HEAD_DIM, LANES), F32)],
        compiler_params=_cparams(("arbitrary", "arbitrary")),
        name="rwkv_scan",
    )(*(x.reshape(batch, seq, _RW) for x in (r, k, v, kk, b, d)), s0_pairs)
    return o.reshape(n, _RW), s_t


def _rwkv_post_kernel(o_ref, r_ref, k_ref, v_ref, g_ref, lnw_ref, lnb_ref, rk_ref, bones_ref, out_ref):
    bones = bones_ref[...]
    inv = 1.0 / RWKV_HEAD_DIM
    o = o_ref[...]
    mean = _dot_f32_right(o, bones) * inv
    cen = o - mean
    var = _dot_f32_right(cen * cen, bones) * inv
    gn = cen * lax.rsqrt(var + GN_EPS) * lnw_ref[...] + lnb_ref[...]
    bonus = _dot_f32_right(r_ref[...] * k_ref[...] * rk_ref[...], bones) * v_ref[...]
    out_ref[...] = ((gn + bonus) * g_ref[...]).astype(out_ref.dtype)


def _rwkv_post(o, r, k, v, g, lnw, lnb, rk, bones, *, t):
    n = o.shape[0]
    tok = pl.BlockSpec((t, _RW), lambda i: (i, 0))
    row = pl.BlockSpec((1, _RW), lambda i: (0, 0))
    return pl.pallas_call(
        _rwkv_post_kernel,
        grid=(n // t,),
        in_specs=[tok] * 5 + [row] * 3 + [pl.BlockSpec(bones.shape, lambda i: (0, 0))],
        out_specs=tok,
        out_shape=jax.ShapeDtypeStruct((n, _RW), BF16),
        compiler_params=_cparams(("parallel",)),
        name="rwkv_post",
    )(o, r, k, v, g, lnw, lnb, rk, bones)


def _to_pairs(s):
    b = s.shape[0]
    return s.reshape(b, HEAD_PAIRS, 2, RWKV_HEAD_DIM, RWKV_HEAD_DIM).transpose(0, 1, 3, 2, 4).reshape(
        b, HEAD_PAIRS, RWKV_HEAD_DIM, LANES)


def _from_pairs(s):
    b = s.shape[0]
    return s.reshape(b, HEAD_PAIRS, RWKV_HEAD_DIM, 2, RWKV_HEAD_DIM).transpose(0, 1, 3, 2, 4).reshape(
        b, RWKV_HEADS, RWKV_HEAD_DIM, RWKV_HEAD_DIM)


def _rwkv_mix(proj, prev, s0, rw, *, batch, t_prep, t_scan, shifted):
    mu, w0, w2, a0, a2, g2, k_k, k_a, r_k, ln_w, ln_b = rw
    bones = _head_block_ones(_RW, RWKV_HEAD_DIM)
    zeros = jnp.zeros((LANES - DECAY_LORA, _RW), F32)
    w2p = jnp.concatenate([w2, zeros], axis=0).astype(BF16)
    a2p = jnp.concatenate([zeros, a2], axis=0).astype(BF16)
    row = lambda x: x.reshape(1, -1)
    r, k, v, kk, b, d, g, last = _rwkv_prep(
        proj, prev, row(mu[_RWKV_PERM]), row(w0), w2p, row(a0), a2p, g2.astype(BF16), row(k_k), row(k_a), bones,
        t=t_prep, shifted=shifted)
    o, s_t = _rwkv_scan(r, k, v, kk, b, d, _to_pairs(s0), batch=batch, t=t_scan)
    out = _rwkv_post(o, r, k, v, g, row(ln_w), row(ln_b), row(r_k), bones, t=t_prep)
    return out, last[:, _RWKV_INV_PERM], _from_pairs(s_t)


def _top_rows(s, k, order):
    vals, ords = [], []
    for _ in range(k):
        m = jnp.max(s, axis=0, keepdims=True)
        pick = jnp.min(jnp.where(s == m, order, jnp.inf), axis=0, keepdims=True)
        vals.append(m)
        ords.append(pick)
        s = jnp.where(order == pick, -jnp.inf, s)
    return jnp.concatenate(vals, axis=0), jnp.concatenate(ords, axis=0)


def _take_rows(table, pos):
    out = jnp.zeros(pos.shape, table.dtype)
    for i in range(table.shape[0]):
        out = jnp.where(pos == float(i), table[i:i + 1, :], out)
    return out


_CAND_GROUPS = ((0, 2, 16), (2, 4, 8))
_CAND_TAIL_I0, _CAND_TAIL_J = 4, 3
_CAND_HEAD_ROWS = sum((hi - lo) * nj for lo, hi, nj in _CAND_GROUPS)
_CAND_ROWS = _CAND_HEAD_ROWS + _CAND_TAIL_J * PEER_TOPK


def _candidate_tables(tm):
    k = PEER_TOPK
    assert k == 16 and _CAND_GROUPS == ((0, 2, 16), (2, 4, 8))
    r = lax.broadcasted_iota(I32, (_CAND_ROWS, tm), 0)
    r2 = r - 2 * k
    flat2 = (2 + (r2 >> 3)) * k + (r2 & 7)
    r3 = r - _CAND_HEAD_ROWS
    i3 = r3 & (k - 1)
    flat3 = i3 * k + (r3 >> 4)
    flat = jnp.where(r < 2 * k, r, jnp.where(r < _CAND_HEAD_ROWS, flat2, flat3))
    valid = (r < _CAND_HEAD_ROWS) | (i3 >= _CAND_TAIL_I0)
    return flat.astype(F32), valid


def _peer_route_kernel(q_ref, keys_ref, g_ref, e1_ref, e2_ref, gv_ref, gt_ref, *, tm):
    k = PEER_TOPK
    flat, valid = _candidate_tables(tm)
    key_order = lax.broadcasted_iota(I32, (N_KEYS, tm), 0).astype(F32)

    e1s, e2s, gs = [], [], []
    for h in range(PEER_HEADS):
        qh = q_ref[:, h * D_KEY:(h + 1) * D_KEY].astype(BF16)
        score = lambda p: lax.dot_general(keys_ref[h, p], qh, (((1,), (1,)), ((), ())), preferred_element_type=F32)
        v1, i1 = _top_rows(score(0), k, key_order)
        v2, i2 = _top_rows(score(1), k, key_order)
        parts = [v1[i:i + 1, :] + v2[:nj, :] for lo, hi, nj in _CAND_GROUPS for i in range(lo, hi)]
        parts += [v1 + v2[j:j + 1, :] for j in range(_CAND_TAIL_J)]
        cand = jnp.where(valid, jnp.concatenate(parts, axis=0), -jnp.inf)
        sc, pos = _top_rows(cand, k, flat)
        pos_i = jnp.floor(pos * (1.0 / k))
        e1s.append(_take_rows(i1, pos_i))
        e2s.append(_take_rows(i2, pos - k * pos_i))
        ex = jnp.exp(sc - sc[0:1, :])
        gs.append(ex / jnp.sum(ex, axis=0, keepdims=True))
    e1_ref[...] = jnp.concatenate(e1s, axis=0).T
    e2_ref[...] = jnp.concatenate(e2s, axis=0).T
    gv_ref[...] = jnp.concatenate(gs, axis=0).T

    eid = lax.broadcasted_iota(I32, (N_KEYS, PEER_HEADS * k), 0).astype(F32)
    rows = _ROUTE_GROUP

    def body(gi, carry):
        base = pl.multiple_of(gi * rows, rows)
        e1_t = e1_ref[pl.ds(base, rows), :]
        e2_t = e2_ref[pl.ds(base, rows), :]
        gv_t = gv_ref[pl.ds(base, rows), :]
        for u in range(rows):
            a = jnp.where(eid == e1_t[u:u + 1, :], gv_t[u:u + 1, :], 0.0).astype(BF16)
            b = jnp.where(eid == e2_t[u:u + 1, :], 1.0, 0.0).astype(BF16)
            gt_ref[u * _ROUTE_PITCH:u * _ROUTE_PITCH + N_KEYS, :] = lax.dot_general(
                a, b, (((1,), (1,)), ((), ())), preferred_element_type=F32)
        for e1 in range(N_KEYS):
            slab = gt_ref[pl.ds(e1, rows, stride=_ROUTE_PITCH), :]
            g_ref[pl.ds(base, rows), e1 * N_KEYS:(e1 + 1) * N_KEYS] = slab.astype(g_ref.dtype)
        return carry

    lax.fori_loop(0, tm // rows, body, 0)


_ROUTE_GROUP = 2 * SUBLANES
_ROUTE_PITCH = N_KEYS + SUBLANES


def _peer_route(q, keys_pad, *, tm):
    n = q.shape[0]
    width = PEER_HEADS * PEER_TOPK
    assert tm % _ROUTE_GROUP == 0
    return pl.pallas_call(
        functools.partial(_peer_route_kernel, tm=tm),
        grid=(n // tm,),
        in_specs=[pl.BlockSpec((tm, PEER_HEADS * D_KEY), lambda i: (i, 0)),
                  pl.BlockSpec(keys_pad.shape, lambda i: (0, 0, 0, 0))],
        out_specs=pl.BlockSpec((tm, N_KEYS * N_KEYS), lambda i: (i, 0)),
        out_shape=jax.ShapeDtypeStruct((n, N_KEYS * N_KEYS), BF16),
        scratch_shapes=[pltpu.VMEM((tm, width), F32)] * 3 + [pltpu.VMEM((_ROUTE_GROUP * _ROUTE_PITCH, N_KEYS), F32)],
        compiler_params=_cparams(("parallel",)),
        name="peer_route",
    )(q, keys_pad)


def _peer_mlp_kernel(x_ref, gain_ref, u_ref, v_ref, g_ref, o_ref, xn_ref, acc_ref):
    j = pl.program_id(1)

    @pl.when(j == 0)
    def _():
        x = x_ref[...]
        ms = jnp.mean(x * x, axis=-1, keepdims=True)
        xn_ref[...] = (x * lax.rsqrt(ms + RMS_EPS) * gain_ref[...]).astype(BF16)
        acc_ref[...] = jnp.zeros_like(acc_ref)

    h = lax.dot_general(xn_ref[...], u_ref[...], (((1,), (1,)), ((), ())), preferred_element_type=F32)
    act = 0.5 * h * (1.0 + lax.erf(h * (2.0 ** -0.5)))
    w = (act * g_ref[...].astype(F32)).astype(BF16)
    acc_ref[...] += jnp.dot(w, v_ref[...], preferred_element_type=F32)

    @pl.when(j == pl.num_programs(1) - 1)
    def _():
        o_ref[...] = x_ref[...] + acc_ref[...]


def _peer_mlp(x, gain, u_bf16, v_bf16, gmap, *, tm, te):
    n, dm = x.shape
    ne = u_bf16.shape[0]
    return pl.pallas_call(
        _peer_mlp_kernel,
        grid=(n // tm, ne // te),
        in_specs=[pl.BlockSpec((tm, dm), lambda i, j: (i, 0)),
                  pl.BlockSpec((1, dm), lambda i, j: (0, 0)),
                  pl.BlockSpec((te, dm), lambda i, j: (j, 0)),
                  pl.BlockSpec((te, dm), lambda i, j: (j, 0)),
                  pl.BlockSpec((tm, te), lambda i, j: (i, j))],
        out_specs=pl.BlockSpec((tm, dm), lambda i, j: (i, 0)),
        out_shape=jax.ShapeDtypeStruct((n, dm), F32),
        scratch_shapes=[pltpu.VMEM((tm, dm), BF16), pltpu.VMEM((tm, dm), F32)],
        compiler_params=_cparams(("parallel", "arbitrary")),
        name="peer_mlp",
    )(x, gain.reshape(1, dm), u_bf16, v_bf16, gmap)


def _peer_keys_padded(keys):
    z = jnp.zeros_like(keys[:, 0])
    first = jnp.concatenate([keys[:, 0], z], axis=-1)
    second = jnp.concatenate([z, keys[:, 1]], axis=-1)
    return jnp.stack([first, second], axis=1).astype(BF16)


def _peer(y, gain, wq_bf16, keys_pad, u_bf16, v_bf16, *, tm_proj, tm_route, tm_mlp, te):
    n = y.shape[0]
    q = _norm_matmul(y, gain, wq_bf16, tm=tm_proj, tn=wq_bf16.shape[1])
    gmap = _peer_route(q, keys_pad, tm=tm_route)
    return _peer_mlp(y, gain, u_bf16, v_bf16, gmap, tm=tm_mlp, te=te)


def _conv_silu(x, buf_ref, w_ref, b_ref, q):
    buf_ref[SUBLANES:SUBLANES + q, :] = x
    acc = b_ref[...] + x * w_ref[CONV_W - 1:CONV_W, :]
    for back in range(1, CONV_W):
        acc = acc + buf_ref[SUBLANES - back:SUBLANES - back + q, :] * w_ref[CONV_W - 1 - back:CONV_W - back, :]
    tail = buf_ref[q + SUBLANES - (CONV_W - 1):q + SUBLANES, :]
    buf_ref[SUBLANES - (CONV_W - 1):SUBLANES, :] = tail
    return _silu(acc), tail


def _ssd_kernel(z_ref, x_ref, bc_ref, dt_ref, cwx_ref, cwbc_ref, cbx_ref, cbbc_ref, dtb_ref, alog_ref, dskip_ref,
                nw_ref, expand_ref, y_ref, convx_ref, convbc_ref, ht_ref, bufx_ref, bufbc_ref, h_ref, *, q, heads):
    hpg = heads // N_GROUPS
    gw = hpg * SSM_HEAD_DIM

    @pl.when(pl.program_id(0) == 0)
    def _():
        bufx_ref[...] = jnp.zeros_like(bufx_ref)
        bufbc_ref[...] = jnp.zeros_like(bufbc_ref)
        h_ref[...] = jnp.zeros_like(h_ref)

    xs, tail_x = _conv_silu(x_ref[...], bufx_ref, cwx_ref, cbx_ref, q)
    bc, tail_bc = _conv_silu(bc_ref[...], bufbc_ref, cwbc_ref, cbbc_ref, q)
    convx_ref[...] = tail_x
    convbc_ref[...] = tail_bc

    dt = _softplus(dt_ref[...] + dtb_ref[...])
    a = -jnp.exp(alog_ref[...])
    row = lax.broadcasted_iota(I32, (q, q), 0)
    col = lax.broadcasted_iota(I32, (q, q), 1)
    lower = col <= row
    tri = jnp.where(lower, 1.0, 0.0).astype(BF16)
    acs = _dot_f32_left(tri, dt * a)
    acs_t = acs.T
    expand = expand_ref[...]
    dt_e = _dot_f32_right(dt, expand)
    acs_e = _dot_f32_right(acs, expand)
    last_e = acs_e[q - 1:q, :]
    xc = xs * dt_e
    xcd = (xc * jnp.exp(last_e - acs_e)).astype(BF16)
    xc_b = xc.astype(BF16)
    grow = jnp.exp(acs_e)
    chunk_decay = jnp.exp(last_e)

    nbc = N_GROUPS * D_STATE
    for g in range(N_GROUPS):
        bm = bc[:, g * D_STATE:(g + 1) * D_STATE].astype(BF16)
        cm = bc[:, nbc + g * D_STATE:nbc + (g + 1) * D_STATE].astype(BF16)
        cb = lax.dot_general(cm, bm, (((1,), (1,)), ((), ())), preferred_element_type=F32)
        sl = slice(g * gw, (g + 1) * gw)
        h_prev = h_ref[g]
        y_off = jnp.dot(cm, h_prev.astype(BF16), preferred_element_type=F32) * grow[:, sl]
        y_diag = []
        for r in range(hpg):
            hd = g * hpg + r
            diff = acs[:, hd:hd + 1] - acs_t[hd:hd + 1, :]
            m = (cb * jnp.exp(jnp.where(lower, diff, -jnp.inf))).astype(BF16)
            y_diag.append(jnp.dot(m, xc_b[:, hd * SSM_HEAD_DIM:(hd + 1) * SSM_HEAD_DIM], preferred_element_type=F32))
        y_g = jnp.concatenate(y_diag, axis=1) + y_off + dskip_ref[:, sl] * xs[:, sl]
        states = jnp.dot(bm.T, xcd[:, sl], preferred_element_type=F32)
        h_ref[g] = h_prev * chunk_decay[:, sl] + states
        zg = z_ref[:, sl]
        y_g = y_g * _silu(zg)
        ms = jnp.mean(y_g * y_g, axis=-1, keepdims=True)
        y_ref[:, sl] = (y_g * lax.rsqrt(ms + RMS_EPS) * nw_ref[:, sl]).astype(y_ref.dtype)

    @pl.when(pl.program_id(0) == pl.num_programs(0) - 1)
    def _():
        ht_ref[...] = h_ref[...]


def _ssd(proj, conv_w, conv_b, dt_bias, a_log, d_skip, norm_w, *, heads):
    n = proj.shape[0]
    d_inner = heads * SSM_HEAD_DIM
    q = CHUNK
    nbc = 2 * N_GROUPS * D_STATE
    assert nbc == d_inner and n % q == 0
    gw = d_inner // N_GROUPS
    pad = lambda x: jnp.pad(x, (0, LANES - heads)).reshape(1, LANES)
    expand = jnp.asarray((np.arange(LANES)[:, None] == (np.arange(d_inner) // SSM_HEAD_DIM)[None, :]).astype(np.float32), BF16)
    blk = lambda w, j: pl.BlockSpec((q, w), lambda i, j=j: (i, j))
    full = lambda a: pl.BlockSpec(a.shape, lambda i: (0,) * a.ndim)
    small = [conv_w[:, :d_inner], conv_w[:, d_inner:], conv_b[:d_inner].reshape(1, -1), conv_b[d_inner:].reshape(1, -1),
             pad(dt_bias), pad(a_log), jnp.repeat(d_skip, SSM_HEAD_DIM).reshape(1, -1), norm_w.reshape(1, -1), expand]
    y, cx, cbc, ht = pl.pallas_call(
        functools.partial(_ssd_kernel, q=q, heads=heads),
        grid=(n // q,),
        in_specs=[blk(d_inner, 0), blk(d_inner, 1), blk(d_inner, 2), blk(LANES, 3 * d_inner // LANES)]
                 + [full(a) for a in small],
        out_specs=[pl.BlockSpec((q, d_inner), lambda i: (i, 0)),
                   pl.BlockSpec((CONV_W - 1, d_inner), lambda i: (0, 0)),
                   pl.BlockSpec((CONV_W - 1, nbc), lambda i: (0, 0)),
                   pl.BlockSpec((N_GROUPS, D_STATE, gw), lambda i: (0, 0, 0))],
        out_shape=[jax.ShapeDtypeStruct((n, d_inner), BF16),
                   jax.ShapeDtypeStruct((CONV_W - 1, d_inner), F32),
                   jax.ShapeDtypeStruct((CONV_W - 1, nbc), F32),
                   jax.ShapeDtypeStruct((N_GROUPS, D_STATE, gw), F32)],
        scratch_shapes=[pltpu.VMEM((SUBLANES + q, d_inner), F32), pltpu.VMEM((SUBLANES + q, nbc), F32),
                        pltpu.VMEM((N_GROUPS, D_STATE, gw), F32)],
        compiler_params=_cparams(("arbitrary",)),
        name="ssd",
    )(proj, proj, proj, proj, *small)
    hpg = heads // N_GROUPS
    h_t = ht.reshape(N_GROUPS, D_STATE, hpg, SSM_HEAD_DIM).transpose(0, 2, 3, 1).reshape(heads, SSM_HEAD_DIM, D_STATE)
    return y, jnp.concatenate([cx, cbc], axis=1), h_t


def _ssm_step_kernel(z_ref, x_ref, bc_ref, dt_ref, bufx_ref, bufbc_ref, h0_ref, cwx_ref, cwbc_ref, cbx_ref, cbbc_ref,
                     dtb_ref, alog_ref, dskip_ref, nw_ref, y_ref, nbufx_ref, nbufbc_ref, h_ref, *, heads):
    hpg = heads // N_GROUPS
    gw = hpg * SSM_HEAD_DIM
    nbc = N_GROUPS * D_STATE
    fullx = jnp.concatenate([bufx_ref[...], x_ref[...]], axis=1)
    nbufx_ref[...] = fullx[:, 1:]
    xs = _silu(cbx_ref[...] + jnp.sum(fullx * cwx_ref[...], axis=1, keepdims=True))
    fullbc = jnp.concatenate([bufbc_ref[...], bc_ref[...]], axis=0)
    nbufbc_ref[...] = fullbc[1:, :]
    bc = _silu(cbbc_ref[...] + jnp.sum(fullbc * cwbc_ref[...], axis=0, keepdims=True))
    dt = _softplus(dt_ref[...] + dtb_ref[...])
    dec = jnp.exp(dt * (-jnp.exp(alog_ref[...])))
    xdt = xs * dt
    for g in range(N_GROUPS):
        rows = slice(g * gw, (g + 1) * gw)
        bm = bc[:, g * D_STATE:(g + 1) * D_STATE]
        cm = bc[:, nbc + g * D_STATE:nbc + (g + 1) * D_STATE]
        h_new = h0_ref[rows, :] * dec[rows, :] + xdt[rows, :] * bm
        h_ref[rows, :] = h_new
        y = jnp.sum(h_new * cm, axis=-1, keepdims=True) + dskip_ref[rows, :] * xs[rows, :]
        y = y * _silu(z_ref[rows, :])
        ms = jnp.mean(y * y, axis=0, keepdims=True)
        y_ref[rows, :] = y * lax.rsqrt(ms + RMS_EPS) * nw_ref[rows, :]


def _ssm_step(proj, conv_buf, h0, conv_w, conv_b, dt_bias, a_log, d_skip, norm_w, *, heads):
    b = proj.shape[0]
    d_inner = heads * SSM_HEAD_DIM
    nbc2 = 2 * N_GROUPS * D_STATE
    col = lambda x: x.reshape(b, d_inner, 1)
    per_head = lambda v: jnp.repeat(v, SSM_HEAD_DIM, axis=-1)
    z = col(proj[:, :d_inner])
    x = col(proj[:, d_inner:2 * d_inner])
    bc = proj[:, 2 * d_inner:2 * d_inner + nbc2].reshape(b, 1, nbc2)
    dt = col(per_head(proj[:, 2 * d_inner + nbc2:2 * d_inner + nbc2 + heads]))
    bufx = conv_buf[:, :, :d_inner].transpose(0, 2, 1)
    bufbc = conv_buf[:, :, d_inner:]
    pcol = lambda v: v.reshape(d_inner, 1)
    params = [conv_w[:, :d_inner].T, conv_w[:, d_inner:], pcol(conv_b[:d_inner]), conv_b[d_inner:].reshape(1, nbc2),
              pcol(per_head(dt_bias)), pcol(per_head(a_log)), pcol(per_head(d_skip)), pcol(norm_w)]
    seq = lambda *shape: pl.BlockSpec((None,) + shape, lambda i: (i,) + (0,) * len(shape))
    full = lambda a: pl.BlockSpec(a.shape, lambda i: (0,) * a.ndim)
    y, nbx, nbbc, h = pl.pallas_call(
        functools.partial(_ssm_step_kernel, heads=heads),
        grid=(b,),
        in_specs=[seq(d_inner, 1), seq(d_inner, 1), seq(1, nbc2), seq(d_inner, 1), seq(d_inner, CONV_W - 1),
                  seq(CONV_W - 1, nbc2), seq(d_inner, D_STATE)] + [full(a) for a in params],
        out_specs=[seq(d_inner, 1), seq(d_inner, CONV_W - 1), seq(CONV_W - 1, nbc2), seq(d_inner, D_STATE)],
        out_shape=[jax.ShapeDtypeStruct((b, d_inner, 1), F32), jax.ShapeDtypeStruct((b, d_inner, CONV_W - 1), F32),
                   jax.ShapeDtypeStruct((b, CONV_W - 1, nbc2), F32), jax.ShapeDtypeStruct((b, d_inner, D_STATE), F32)],
        compiler_params=_cparams(("parallel",)),
        name="ssm_step",
    )(z, x, bc, dt, bufx, bufbc, h0.reshape(b, d_inner, D_STATE), *params)
    new_buf = jnp.concatenate([nbx.transpose(0, 2, 1), nbbc], axis=2)
    return y.reshape(b, d_inner), new_buf, h.reshape(b, heads, SSM_HEAD_DIM, D_STATE)


def _fox_decode_kernel(pt_ref, q_ref, kc_ref, vc_ref, lfc_ref, *refs, pages):
    k_refs = refs[:pages]
    v_refs = refs[pages:2 * pages]
    lf_refs = refs[2 * pages:3 * pages]
    o_ref, qb_ref, m_ref, l_ref, acc_ref, later_ref = refs[3 * pages:]
    j = pl.program_id(1)
    first_lane = lax.broadcasted_iota(I32, (1, LANES), 1) == 0

    @pl.when(j == 0)
    def _():
        s_rows = []
        for h in range(FOX_HEADS):
            qh = q_ref[h] * (FOX_HEAD_DIM ** -0.5)
            qb_ref[h] = jnp.broadcast_to(qh, (FOX_HEAD_DIM, LANES))
            s_rows.append(jnp.sum(qh * kc_ref[h], axis=0, keepdims=True))
            acc_ref[h] = jnp.where(first_lane, vc_ref[h], 0.0)
        m_ref[...] = jnp.broadcast_to(jnp.concatenate(s_rows, axis=0), m_ref.shape)
        l_ref[...] = jnp.broadcast_to(jnp.where(first_lane, 1.0, 0.0), l_ref.shape)
        later_ref[...] = jnp.broadcast_to(lfc_ref[...], later_ref.shape)

    urow = lax.broadcasted_iota(I32, (LANES, LANES), 0)
    tcol = lax.broadcasted_iota(I32, (LANES, LANES), 1)
    after = jnp.where(urow > tcol, 1.0, 0.0).astype(BF16)
    for u in range(pages):
        lf = lf_refs[u][...]
        later = later_ref[...]
        bias = _dot_f32_right(lf, after) + later
        rows = [jnp.sum(k_refs[u][h] * qb_ref[h], axis=0, keepdims=True) for h in range(FOX_HEADS)]
        s = jnp.concatenate(rows, axis=0) + bias
        m_old = m_ref[...]
        m_new = jnp.maximum(m_old, jnp.max(s, axis=-1, keepdims=True))
        alpha = jnp.exp(m_old - m_new)
        p = jnp.exp(s - m_new)
        l_ref[...] = alpha * l_ref[...] + p
        m_ref[...] = m_new
        for h in range(FOX_HEADS):
            acc_ref[h] = alpha[h:h + 1, :] * acc_ref[h] + p[h:h + 1, :] * v_refs[u][h]
        later_ref[...] = later + jnp.sum(lf, axis=-1, keepdims=True)

    @pl.when(j == pl.num_programs(1) - 1)
    def _():
        l = jnp.sum(l_ref[...], axis=-1, keepdims=True)
        for h in range(FOX_HEADS):
            o_ref[h] = jnp.sum(acc_ref[h], axis=-1, keepdims=True) / l[h:h + 1, :]


def _fox_decode(q, k_cur, v_cur, logf_cur, cache_k, cache_v, cache_logf, page_table, *, pages):
    b, n_pages = page_table.shape
    page = cache_k.shape[1]
    assert page == LANES and n_pages % pages == 0
    kt = cache_k.transpose(0, 2, 3, 1)
    vt = cache_v.transpose(0, 2, 3, 1)
    lft = cache_logf.transpose(0, 2, 1)
    col = lambda x: x.reshape(b, FOX_HEADS, FOX_HEAD_DIM, 1)
    tok = pl.BlockSpec((None, FOX_HEADS, FOX_HEAD_DIM, 1), lambda bi, j, pt: (bi, 0, 0, 0))
    newest_first = lambda bi, j, pt, u: pt[bi, n_pages - 1 - (j * pages + u)]
    kv_page = lambda u: pl.BlockSpec((None, FOX_HEADS, FOX_HEAD_DIM, page),
                                     lambda bi, j, pt, u=u: (newest_first(bi, j, pt, u), 0, 0, 0))
    lf_page = lambda u: pl.BlockSpec((None, FOX_HEADS, page), lambda bi, j, pt, u=u: (newest_first(bi, j, pt, u), 0, 0))
    out = pl.pallas_call(
        functools.partial(_fox_decode_kernel, pages=pages),
        grid_spec=pltpu.PrefetchScalarGridSpec(
            num_scalar_prefetch=1,
            grid=(b, n_pages // pages),
            in_specs=[tok, tok, tok, pl.BlockSpec((None, FOX_HEADS, 1), lambda bi, j, pt: (bi, 0, 0))]
                     + [kv_page(u) for u in range(pages)] * 2 + [lf_page(u) for u in range(pages)],
            out_specs=tok,
            scratch_shapes=[pltpu.VMEM((FOX_HEADS, FOX_HEAD_DIM, LANES), F32), pltpu.VMEM((FOX_HEADS, LANES), F32),
                            pltpu.VMEM((FOX_HEADS, LANES), F32), pltpu.VMEM((FOX_HEADS, FOX_HEAD_DIM, LANES), F32),
                            pltpu.VMEM((FOX_HEADS, LANES), F32)],
        ),
        out_shape=jax.ShapeDtypeStruct((b, FOX_HEADS, FOX_HEAD_DIM, 1), F32),
        compiler_params=_cparams(("parallel", "arbitrary")),
        name="fox_decode",
    )(page_table, col(q), col(k_cur), col(v_cur), logf_cur.reshape(b, FOX_HEADS, 1),
      *([kt] * pages), *([vt] * pages), *([lft] * pages))
    return out.reshape(b, FOX_WIDTH)


_COL_Q = RWKV_PROJ
_COL_K = _COL_Q + FOX_WIDTH
_COL_V = _COL_K + FOX_WIDTH
_COL_FL = _COL_V + FOX_WIDTH
_EVEN_COLS = _COL_FL + LANES


def _even_weights(w_in):
    q, k, v = (w_in[:, i * FOX_WIDTH:(i + 1) * FOX_WIDTH] for i in range(3))
    fl = w_in[:, 3 * FOX_WIDTH:3 * FOX_WIDTH + FOX_HEADS]
    rw = w_in[:, 3 * FOX_WIDTH + FOX_HEADS:][:, _RWKV_PERM]
    fl_pad = jnp.pad(fl, ((0, 0), (0, LANES - FOX_HEADS)))
    return jnp.concatenate([rw, q, k, v, fl_pad], axis=1).astype(BF16)


def _even_layer(y, gain, w_cat, w_out, fb, rw, *, batch, seq, past, shift0, wkv0):
    n = y.shape[0]
    tm = min(512, n)
    proj = _norm_matmul(y, gain, w_cat, tm=tm, tn=_EVEN_COLS // 3)
    fb_pad = jnp.pad(fb, (0, LANES - FOX_HEADS)).reshape(1, LANES)
    logf, ct = _fox_prep(proj, _COL_FL // LANES, fb_pad, t=min(256, n))
    k_new = proj[:, _COL_K:_COL_V]
    v_new = proj[:, _COL_V:_COL_FL]
    if past is None:
        qkv = proj[:, _COL_Q:_COL_FL].astype(BF16)
        o_fox = _fox_attn(qkv, ct, tq=1024)
        o_rwkv, shift_new, wkv_new = _rwkv_mix(proj, shift0[:, _RWKV_PERM], wkv0, rw, batch=batch,
                                               t_prep=256, t_scan=256, shifted=True)
    else:
        cache_k, cache_v, cache_logf, page_table = past
        o_fox = _fox_decode(proj[:, _COL_Q:_COL_K], k_new, v_new, logf, cache_k, cache_v, cache_logf, page_table,
                            pages=8)
        o_rwkv, shift_new, wkv_new = _rwkv_mix(proj, shift0[:, _RWKV_PERM], wkv0, rw, batch=batch,
                                               t_prep=n, t_scan=1, shifted=False)
    w_out_b = w_out.astype(BF16)
    y = _matmul_res([o_fox, o_rwkv], [w_out_b[:FOX_WIDTH], w_out_b[FOX_WIDTH:]], y, tm=tm, tn=y.shape[1])
    heads = lambda t: t.reshape(batch, seq, FOX_HEADS, FOX_HEAD_DIM)
    return y, heads(k_new), heads(v_new), logf.reshape(batch, seq, FOX_HEADS), wkv_new, shift_new


def kernel(x_prompt, x_sample, cache_k, cache_v, cache_logf, page_table, state_wkv, state_shift, state_conv, state_ssm, norm_mix, norm_ffn, norm_final, w_in_even, w_out_even, fox_fb, rwkv_mu, rwkv_w0, rwkv_w2, rwkv_a0, rwkv_a2, rwkv_g2, rwkv_kk, rwkv_ka, rwkv_rk, rwkv_lnw, rwkv_lnb, w_in_odd, w_out_odd, ssm_conv_w, ssm_conv_b, ssm_dt_bias, ssm_a_log, ssm_d, ssm_norm_w, peer_wq, peer_keys, peer_u, peer_v):
    bp, seq_p, dm = x_prompt.shape
    bs, seq_s, _ = x_sample.shape
    assert bp == 1 and seq_s == 1
    depth = norm_mix.shape[0]
    ssm_heads = ssm_dt_bias.shape[1]
    yp = x_prompt.reshape(bp * seq_p, dm)
    ys = x_sample.reshape(bs * seq_s, dm)
    outs_p = {name: [] for name in ("k", "v", "logf", "wkv", "shift", "conv", "ssm")}
    outs_s = {name: [] for name in outs_p}
    for layer in range(depth):
        if layer % 2 == 0:
            e = layer // 2
            rw = (rwkv_mu[e], rwkv_w0[e], rwkv_w2[e], rwkv_a0[e], rwkv_a2[e], rwkv_g2[e],
                  rwkv_kk[e], rwkv_ka[e], rwkv_rk[e], rwkv_lnw[e], rwkv_lnb[e])
            w_cat = _even_weights(w_in_even[e])
            yp, *res_p = _even_layer(
                yp, norm_mix[layer], w_cat, w_out_even[e], fox_fb[e], rw, batch=bp, seq=seq_p, past=None,
                shift0=jnp.zeros((bp, RWKV_PROJ), F32),
                wkv0=jnp.zeros((bp, RWKV_HEADS, RWKV_HEAD_DIM, RWKV_HEAD_DIM), F32))
            ys, *res_s = _even_layer(
                ys, norm_mix[layer], w_cat, w_out_even[e], fox_fb[e], rw, batch=bs, seq=seq_s,
                past=(cache_k[e], cache_v[e], cache_logf[e], page_table), shift0=state_shift[e], wkv0=state_wkv[e])
            for outs, res in ((outs_p, res_p), (outs_s, res_s)):
                for name, val in zip(("k", "v", "logf", "wkv", "shift"), res):
                    outs[name].append(val)
        else:
            o = layer // 2
            mprm = (ssm_conv_w[o], ssm_conv_b[o], ssm_dt_bias[o], ssm_a_log[o], ssm_d[o], ssm_norm_w[o])
            w_cat = jnp.pad(w_in_odd[o], ((0, 0), (0, LANES - ssm_heads))).astype(BF16)
            w_out = w_out_odd[o].astype(BF16)
            tn = w_cat.shape[1] // 7
            proj_p = _norm_matmul(yp, norm_mix[layer], w_cat, tm=512, tn=tn)
            y_m, conv_p, h_p = _ssd(proj_p, *mprm, heads=ssm_heads)
            yp = _matmul_res([y_m], [w_out], yp, tm=512, tn=dm)
            proj_s = _norm_matmul(ys, norm_mix[layer], w_cat, tm=bs, tn=tn)
            y_s, conv_s, h_s = _ssm_step(proj_s, state_conv[o], state_ssm[o], *mprm, heads=ssm_heads)
            ys = _matmul_res([y_s], [w_out], ys, tm=bs, tn=dm)
            outs_p["conv"].append(conv_p[None])
            outs_p["ssm"].append(h_p[None])
            outs_s["conv"].append(conv_s)
            outs_s["ssm"].append(h_s)
        wq = peer_wq[layer].astype(BF16)
        keys_pad = _peer_keys_padded(peer_keys[layer])
        u_b = peer_u[layer].astype(BF16)
        v_b = peer_v[layer].astype(BF16)
        yp = _peer(yp, norm_ffn[layer], wq, keys_pad, u_b, v_b, tm_proj=512, tm_route=128, tm_mlp=1024, te=1024)
        ys = _peer(ys, norm_ffn[layer], wq, keys_pad, u_b, v_b, tm_proj=bs, tm_route=bs, tm_mlp=bs, te=2048)
    y_prompt = _rmsnorm(yp, norm_final, tm=512).reshape(bp, seq_p, dm)
    y_sample = _rmsnorm(ys, norm_final, tm=bs).reshape(bs, seq_s, dm)
    order = ("k", "v", "logf", "wkv", "shift", "conv", "ssm")
    return ((y_prompt, y_sample) + tuple(jnp.stack(outs_p[name]) for name in order)
            + tuple(jnp.stack(outs_s[name]) for name in order))
```

```python
import functools
import math

import numpy as np
import jax
import jax.numpy as jnp
from jax import lax
from jax.experimental import pallas as pl
from jax.experimental.pallas import tpu as pltpu

F32 = jnp.float32
BF16 = jnp.bfloat16
I32 = jnp.int32

LANES = 128
SUBLANES = 8
V7X_VMEM_BYTES = 64 * 1024 * 1024

RMS_EPS = 1e-6
GN_EPS = 64e-5

FOX_HEADS = 8
FOX_HEAD_DIM = 64
FOX_WIDTH = FOX_HEADS * FOX_HEAD_DIM
RWKV_HEADS = 8
RWKV_HEAD_DIM = 64
RWKV_WIDTH = RWKV_HEADS * RWKV_HEAD_DIM
DECAY_LORA = 64
ICLR_LORA = 64
GATE_LORA = 128
RWKV_PROJ = 4 * RWKV_WIDTH - RWKV_WIDTH + DECAY_LORA + ICLR_LORA + GATE_LORA
HEAD_PAIRS = RWKV_HEADS // 2

SSM_HEAD_DIM = 64
D_STATE = 128
N_GROUPS = 8
CONV_W = 4
CHUNK = 128

N_KEYS = 128
PEER_HEADS = 8
PEER_TOPK = 16
D_KEY = 128


def _cparams(semantics, vmem_mb=48):
    return pltpu.CompilerParams(dimension_semantics=semantics, vmem_limit_bytes=vmem_mb * 1024 * 1024)


def _split3(x):
    hi = x.astype(BF16)
    r1 = x - hi.astype(F32)
    mid = r1.astype(BF16)
    lo = (r1 - mid.astype(F32)).astype(BF16)
    return hi, mid, lo


def _dot_f32_right(x, m_bf16):
    hi, mid, lo = _split3(x)
    d = lambda a: jnp.dot(a, m_bf16, preferred_element_type=F32)
    return d(hi) + d(mid) + d(lo)


def _dot_f32_left(m_bf16, x):
    hi, mid, lo = _split3(x)
    d = lambda a: jnp.dot(m_bf16, a, preferred_element_type=F32)
    return d(hi) + d(mid) + d(lo)


def _sigmoid(x):
    return 1.0 / (1.0 + jnp.exp(-x))


def _softplus(x):
    return jnp.maximum(x, 0.0) + jnp.log1p(jnp.exp(-jnp.abs(x)))


def _silu(x):
    return x * _sigmoid(x)


def _norm_matmul_kernel(x_ref, g_ref, w_ref, o_ref, xn_ref):
    @pl.when(pl.program_id(1) == 0)
    def _():
        x = x_ref[...]
        ms = jnp.mean(x * x, axis=-1, keepdims=True)
        xn_ref[...] = (x * lax.rsqrt(ms + RMS_EPS) * g_ref[...]).astype(BF16)

    o_ref[...] = jnp.dot(xn_ref[...], w_ref[...], preferred_element_type=F32).astype(o_ref.dtype)


def _norm_matmul(x, g, w_bf16, *, tm, tn, out_dtype=F32):
    m, k = x.shape
    n = w_bf16.shape[1]
    assert m % tm == 0 and n % tn == 0
    return pl.pallas_call(
        _norm_matmul_kernel,
        grid=(m // tm, n // tn),
        in_specs=[
            pl.BlockSpec((tm, k), lambda i, j: (i, 0)),
            pl.BlockSpec((1, k), lambda i, j: (0, 0)),
            pl.BlockSpec((k, tn), lambda i, j: (0, j)),
        ],
        out_specs=pl.BlockSpec((tm, tn), lambda i, j: (i, j)),
        out_shape=jax.ShapeDtypeStruct((m, n), out_dtype),
        scratch_shapes=[pltpu.VMEM((tm, k), BF16)],
        compiler_params=_cparams(("parallel", "arbitrary")),
        name="norm_matmul",
    )(x, g.reshape(1, k), w_bf16)


def _matmul_res_kernel(*refs, n_in):
    a_refs = refs[:n_in]
    w_refs = refs[n_in:2 * n_in]
    r_ref, o_ref = refs[2 * n_in], refs[2 * n_in + 1]
    acc = r_ref[...]
    for a_ref, w_ref in zip(a_refs, w_refs):
        acc = acc + jnp.dot(a_ref[...].astype(BF16), w_ref[...], preferred_element_type=F32)
    o_ref[...] = acc


def _matmul_res(a_list, w_list, res, *, tm, tn):
    m, n = res.shape
    n_in = len(a_list)
    assert m % tm == 0 and n % tn == 0
    in_specs = ([pl.BlockSpec((tm, a.shape[1]), lambda i, j: (i, 0)) for a in a_list]
                + [pl.BlockSpec((w.shape[0], tn), lambda i, j: (0, j)) for w in w_list]
                + [pl.BlockSpec((tm, tn), lambda i, j: (i, j))])
    return pl.pallas_call(
        functools.partial(_matmul_res_kernel, n_in=n_in),
        grid=(m // tm, n // tn),
        in_specs=in_specs,
        out_specs=pl.BlockSpec((tm, tn), lambda i, j: (i, j)),
        out_shape=jax.ShapeDtypeStruct((m, n), F32),
        compiler_params=_cparams(("parallel", "parallel")),
        name="matmul_res",
    )(*a_list, *w_list, res)


def _rmsnorm_kernel(x_ref, g_ref, o_ref):
    x = x_ref[...]
    ms = jnp.mean(x * x, axis=-1, keepdims=True)
    o_ref[...] = x * lax.rsqrt(ms + RMS_EPS) * g_ref[...]


def _rmsnorm(x, g, *, tm):
    m, k = x.shape
    return pl.pallas_call(
        _rmsnorm_kernel,
        grid=(m // tm,),
        in_specs=[pl.BlockSpec((tm, k), lambda i: (i, 0)), pl.BlockSpec((1, k), lambda i: (0, 0))],
        out_specs=pl.BlockSpec((tm, k), lambda i: (i, 0)),
        out_shape=jax.ShapeDtypeStruct((m, k), F32),
        compiler_params=_cparams(("parallel",)),
        name="rmsnorm",
    )(x, g.reshape(1, k))


def _fox_prep_kernel(fl_ref, fb_ref, lf_ref, ct_ref, carry_ref, *, t):
    @pl.when(pl.program_id(0) == 0)
    def _():
        carry_ref[...] = jnp.zeros_like(carry_ref)

    x = fl_ref[...] + fb_ref[...]
    lf = jnp.minimum(x, 0.0) - jnp.log1p(jnp.exp(-jnp.abs(x)))
    lf_ref[...] = lf[:, :FOX_HEADS]
    row = lax.broadcasted_iota(I32, (t, t), 0)
    col = lax.broadcasted_iota(I32, (t, t), 1)
    tri = jnp.where(col <= row, 1.0, 0.0).astype(BF16)
    c = _dot_f32_left(tri, lf) + carry_ref[...]
    carry_ref[...] = c[t - 1:t, :]
    ct_ref[...] = c.T[:FOX_HEADS, :]


def _fox_prep(proj, fl_block, fb_pad, *, t):
    n = proj.shape[0]
    return pl.pallas_call(
        functools.partial(_fox_prep_kernel, t=t),
        grid=(n // t,),
        in_specs=[pl.BlockSpec((t, LANES), lambda i: (i, fl_block)),
                  pl.BlockSpec((1, LANES), lambda i: (0, 0))],
        out_specs=[pl.BlockSpec((t, FOX_HEADS), lambda i: (i, 0)),
                   pl.BlockSpec((FOX_HEADS, t), lambda i: (0, i))],
        out_shape=[jax.ShapeDtypeStruct((n, FOX_HEADS), F32),
                   jax.ShapeDtypeStruct((FOX_HEADS, n), F32)],
        scratch_shapes=[pltpu.VMEM((1, LANES), F32)],
        compiler_params=_cparams(("arbitrary",)),
        name="fox_prep",
    )(proj, fb_pad)


_SKIP_LOGIT_GAP = 110.0
_SKIP_NORM_SLACK = 1.05


def _fox_bounds_kernel(q_ref, k_ref, ct_ref, sel_ref, jmin_ref, qmax_ref, kmax_ref, cs_ref, ce_ref, *, tq, nq):
    i = pl.program_id(0)
    lane = lax.broadcasted_iota(I32, (FOX_HEADS, LANES), 1)
    sub = lax.broadcasted_iota(I32, (FOX_HEADS, LANES), 0)

    @pl.when(i == 0)
    def _():
        qmax_ref[...] = jnp.zeros_like(qmax_ref)
        kmax_ref[...] = jnp.zeros_like(kmax_ref)
        cs_ref[...] = jnp.zeros_like(cs_ref)
        ce_ref[...] = jnp.zeros_like(ce_ref)

    def head_sq_norm_max(x_ref):
        x = x_ref[...].astype(F32)
        sq = jnp.dot((x * x).astype(BF16), sel_ref[...], preferred_element_type=F32)
        return jnp.max(sq, axis=0, keepdims=True)

    qmax_ref[...] = jnp.maximum(qmax_ref[...], head_sq_norm_max(q_ref))
    kmax_ref[...] = jnp.maximum(kmax_ref[...], head_sq_norm_max(k_ref))
    cs_ref[...] = jnp.where(lane == i, ct_ref[:, 0:1], cs_ref[...])
    ce_ref[...] = jnp.where(lane == i, ct_ref[:, tq - 1:tq], ce_ref[...])

    @pl.when(i == nq - 1)
    def _():
        def to_col(row):
            return jnp.sum(jnp.where(sub == lane, row, 0.0), axis=-1, keepdims=True)

        b2 = (2.0 * _SKIP_NORM_SLACK * FOX_HEAD_DIM ** -0.5) * jnp.sqrt(to_col(qmax_ref[...]) * to_col(kmax_ref[...]))
        ce = ce_ref[...]
        out = jnp.zeros((FOX_HEADS, LANES), F32)
        for qi in range(nq):
            gap = b2 + cs_ref[:, qi:qi + 1] - ce
            skip = jnp.where((gap < -_SKIP_LOGIT_GAP) & (lane < qi), 1.0, 0.0)
            for p in range(FOX_HEADS // 2):
                both = skip[2 * p:2 * p + 1, :] * skip[2 * p + 1:2 * p + 2, :]
                count = jnp.sum(both, axis=-1, keepdims=True)
                out = jnp.where((sub == p) & (lane == qi), count, out)
        jmin_ref[...] = out.astype(I32)


def _fox_bounds(qkv_bf16, ct, *, tq):
    n = qkv_bf16.shape[0]
    nq = n // tq
    assert nq <= LANES
    sel = jnp.asarray((np.arange(FOX_WIDTH)[:, None] // FOX_HEAD_DIM == np.arange(LANES)[None, :]).astype(np.float32), BF16)
    table = pl.pallas_call(
        functools.partial(_fox_bounds_kernel, tq=tq, nq=nq),
        grid=(nq,),
        in_specs=[pl.BlockSpec((tq, FOX_WIDTH), lambda i: (i, 0)),
                  pl.BlockSpec((tq, FOX_WIDTH), lambda i: (i, 1)),
                  pl.BlockSpec((FOX_HEADS, tq), lambda i: (0, i)),
                  pl.BlockSpec(sel.shape, lambda i: (0, 0))],
        out_specs=pl.BlockSpec((FOX_HEADS, LANES), lambda i: (0, 0)),
        out_shape=jax.ShapeDtypeStruct((FOX_HEADS, LANES), I32),
        scratch_shapes=[pltpu.VMEM((1, LANES), F32), pltpu.VMEM((1, LANES), F32),
                        pltpu.VMEM((FOX_HEADS, LANES), F32), pltpu.VMEM((FOX_HEADS, LANES), F32)],
        compiler_params=_cparams(("arbitrary",)),
        name="fox_bounds",
    )(qkv_bf16, qkv_bf16, ct, sel)
    return table[:FOX_HEADS // 2, :nq]


def _fox_attn_kernel(jmin_ref, q_ref, k_ref, v_ref, ct_ref, o_ref, qs_ref, m_ref, l_ref, acc_ref, *, tq):
    i = pl.program_id(1)
    lane = lax.broadcasted_iota(I32, (1, LANES), 1)
    left = lane < FOX_HEAD_DIM
    q = q_ref[...] * jnp.asarray(FOX_HEAD_DIM ** -0.5, BF16)
    zero = jnp.zeros_like(q)
    qs_ref[0] = jnp.where(left, q, zero)
    qs_ref[1] = jnp.where(left, zero, q)
    q0 = pl.multiple_of(i * tq, tq)
    cref = [ct_ref[h:h + 1, pl.ds(q0, LANES)][:, 0:1] for h in range(2)]

    m_ref[...] = jnp.full_like(m_ref, -jnp.inf)
    l_ref[...] = jnp.zeros_like(l_ref)
    acc_ref[...] = jnp.zeros_like(acc_ref)
    ncol = tq // LANES

    def step(j, masked):
        k0 = pl.multiple_of(j * tq, tq)
        kb = k_ref[pl.ds(k0, tq), :]
        vb = v_ref[pl.ds(k0, tq), :]
        bias = [cref[h] - ct_ref[h:h + 1, pl.ds(k0, tq)] for h in range(2)]
        for h in range(2):
            s = lax.dot_general(qs_ref[h], kb, (((1,), (1,)), ((), ())), preferred_element_type=F32)
            s = s + bias[h]
            if masked:
                r = lax.broadcasted_iota(I32, (tq, tq), 0)
                c = lax.broadcasted_iota(I32, (tq, tq), 1)
                s = jnp.where(c <= r, s, -jnp.inf)
            cols = [s[:, c * LANES:(c + 1) * LANES] for c in range(ncol)]
            lane_max = functools.reduce(jnp.maximum, cols)
            m_old = m_ref[h]
            m_new = jnp.maximum(m_old, jnp.max(lane_max, axis=-1, keepdims=True))
            alpha = jnp.exp(m_old - m_new)
            ps = [jnp.exp(c - m_new) for c in cols]
            l_ref[h] = alpha * l_ref[h] + functools.reduce(jnp.add, ps)
            m_ref[h] = m_new
            p = jnp.concatenate([x.astype(BF16) for x in ps], axis=1)
            acc_ref[h] = alpha * acc_ref[h] + jnp.dot(p, vb, preferred_element_type=F32)

    def body(j, carry):
        step(j, False)
        return carry

    lax.fori_loop(jmin_ref[pl.program_id(0), i], i, body, 0)
    step(i, True)
    l0 = jnp.sum(l_ref[0], axis=-1, keepdims=True)
    l1 = jnp.sum(l_ref[1], axis=-1, keepdims=True)
    o_ref[...] = jnp.where(left, acc_ref[0] / l0, acc_ref[1] / l1).astype(o_ref.dtype)


def _fox_attn(qkv_bf16, ct, *, tq):
    n = qkv_bf16.shape[0]
    nb = FOX_WIDTH // LANES
    jmin = _fox_bounds(qkv_bf16, ct, tq=tq)
    return pl.pallas_call(
        functools.partial(_fox_attn_kernel, tq=tq),
        grid_spec=pltpu.PrefetchScalarGridSpec(
            num_scalar_prefetch=1,
            grid=(nb, n // tq),
            in_specs=[
                pl.BlockSpec((tq, LANES), lambda p, i, jm: (i, p)),
                pl.BlockSpec((n, LANES), lambda p, i, jm: (0, nb + p)),
                pl.BlockSpec((n, LANES), lambda p, i, jm: (0, 2 * nb + p)),
                pl.BlockSpec((None, 2, n), lambda p, i, jm: (p, 0, 0)),
            ],
            out_specs=pl.BlockSpec((tq, LANES), lambda p, i, jm: (i, p)),
            scratch_shapes=[pltpu.VMEM((2, tq, LANES), BF16), pltpu.VMEM((2, tq, LANES), F32),
                            pltpu.VMEM((2, tq, LANES), F32), pltpu.VMEM((2, tq, LANES), F32)],
        ),
        out_shape=jax.ShapeDtypeStruct((n, FOX_WIDTH), BF16),
        compiler_params=_cparams(("arbitrary", "arbitrary")),
        name="fox_attn",
    )(jmin, qkv_bf16, qkv_bf16, qkv_bf16, ct.reshape(nb, 2, n))


_RW = RWKV_WIDTH
_RWKV_PERM = np.concatenate([
    np.arange(0, _RW),
    np.arange(_RW + DECAY_LORA, 2 * _RW + DECAY_LORA),
    np.arange(2 * _RW + DECAY_LORA, 3 * _RW + DECAY_LORA),
    np.arange(3 * _RW + DECAY_LORA + ICLR_LORA, RWKV_PROJ),
    np.arange(_RW, _RW + DECAY_LORA),
    np.arange(3 * _RW + DECAY_LORA, 3 * _RW + DECAY_LORA + ICLR_LORA),
])
_RWKV_INV_PERM = np.argsort(_RWKV_PERM)


def _head_block_ones(width, head_dim):
    idx = np.arange(width) // head_dim
    return jnp.asarray((idx[:, None] == idx[None, :]).astype(np.float32), BF16)


def _rwkv_prep_math(p, p_prev, mu, w0, w2p, a0, a2p, g2, k_k, k_a, bones):
    xs = p + mu * (p_prev - p)
    r = xs[:, 0:_RW]
    k = xs[:, _RW:2 * _RW]
    v = xs[:, 2 * _RW:3 * _RW]
    gl = xs[:, 3 * _RW:3 * _RW + GATE_LORA]
    wa = xs[:, 3 * _RW + GATE_LORA:]
    w = -_softplus(-(w0 + jnp.dot(jnp.tanh(wa).astype(BF16), w2p, preferred_element_type=F32))) - 0.5
    decay = jnp.exp(-jnp.exp(w))
    a = _sigmoid(a0 + jnp.dot(wa.astype(BF16), a2p, preferred_element_type=F32))
    g = jnp.dot(_sigmoid(gl).astype(BF16), g2, preferred_element_type=F32)
    kkr = k * k_k
    ss = _dot_f32_right(kkr * kkr, bones)
    kk = kkr / jnp.maximum(jnp.sqrt(ss), 1e-12)
    k2 = k * (1.0 + (a - 1.0) * k_a)
    return r, k2, v, kk, kk * a, decay, g


def _rwkv_prep_kernel(p_ref, prev_ref, mu_ref, w0_ref, w2_ref, a0_ref, a2_ref, g2_ref, kk_ref, ka_ref, bones_ref,
                      r_o, k_o, v_o, kk_o, b_o, d_o, g_o, last_o, buf_ref, *, t, shifted):
    p = p_ref[...]
    if shifted:
        @pl.when(pl.program_id(0) == 0)
        def _():
            buf_ref[SUBLANES - 1:SUBLANES, :] = prev_ref[...]

        buf_ref[SUBLANES:SUBLANES + t, :] = p
        p_prev = buf_ref[SUBLANES - 1:SUBLANES - 1 + t, :]
        buf_ref[SUBLANES - 1:SUBLANES, :] = p[t - 1:t, :]
        last_o[...] = p[t - 1:t, :]
    else:
        p_prev = prev_ref[...]
        last_o[...] = p
    outs = _rwkv_prep_math(p, p_prev, mu_ref[...], w0_ref[...], w2_ref[...], a0_ref[...], a2_ref[...], g2_ref[...],
                           kk_ref[...], ka_ref[...], bones_ref[...])
    for o_ref, val in zip((r_o, k_o, v_o, kk_o, b_o, d_o, g_o), outs):
        o_ref[...] = val


def _rwkv_prep(proj, prev, mu, w0, w2p, a0, a2p, g2, k_k, k_a, bones, *, t, shifted):
    n = proj.shape[0]
    row = lambda w: pl.BlockSpec((1, w), lambda i: (0, 0))
    full = lambda a: pl.BlockSpec(a.shape, lambda i: (0, 0))
    tok = lambda w: pl.BlockSpec((t, w), lambda i: (i, 0))
    prev_spec = row(RWKV_PROJ) if shifted else tok(RWKV_PROJ)
    last_spec = row(RWKV_PROJ) if shifted else tok(RWKV_PROJ)
    last_shape = (1, RWKV_PROJ) if shifted else (n, RWKV_PROJ)
    return pl.pallas_call(
        functools.partial(_rwkv_prep_kernel, t=t, shifted=shifted),
        grid=(n // t,),
        in_specs=[tok(RWKV_PROJ), prev_spec, row(RWKV_PROJ), row(_RW), full(w2p), row(_RW), full(a2p), full(g2),
                  row(_RW), row(_RW), full(bones)],
        out_specs=[tok(_RW)] * 7 + [last_spec],
        out_shape=[jax.ShapeDtypeStruct((n, _RW), F32)] * 7 + [jax.ShapeDtypeStruct(last_shape, F32)],
        scratch_shapes=[pltpu.VMEM((SUBLANES + t, RWKV_PROJ), F32)],
        compiler_params=_cparams(("arbitrary",)),
        name="rwkv_prep",
    )(proj, prev, mu, w0, w2p, a0, a2p, g2, k_k, k_a, bones)


def _rwkv_scan_kernel(r_ref, k_ref, v_ref, kk_ref, b_ref, d_ref, s0_ref, o_ref, st_ref, s_ref, *, t):
    j = pl.program_id(1)

    @pl.when(j == 0)
    def _():
        s_ref[...] = s0_ref[...]

    shape = (RWKV_HEAD_DIM, LANES)
    lane = lax.broadcasted_iota(I32, shape, 1)
    sub = lax.broadcasted_iota(I32, shape, 0)
    left = lane < RWKV_HEAD_DIM
    eye2 = (lane & (RWKV_HEAD_DIM - 1)) == sub

    def seg(x):
        s0 = jnp.sum(jnp.where(left, x, 0.0), axis=-1, keepdims=True)
        s1 = jnp.sum(jnp.where(left, 0.0, x), axis=-1, keepdims=True)
        return jnp.where(left, s0, s1)

    group = SUBLANES if t % SUBLANES == 0 else t

    def readout(s, r_rows):
        s_bd = jnp.concatenate([jnp.where(left, s, 0.0), jnp.where(left, 0.0, s)], axis=0).astype(BF16)
        return lax.dot_general(r_rows, s_bd, (((1,), (1,)), ((), ())), preferred_element_type=F32)

    def body(gi, states):
        base = pl.multiple_of(gi * group, group)
        states = list(states)
        tiles = []
        for p in range(HEAD_PAIRS):
            sl = slice(p * LANES, (p + 1) * LANES)
            tiles.append(tuple(ref[pl.ds(base, group), sl] for ref in (kk_ref, v_ref, d_ref, b_ref, k_ref, r_ref)))
        o_rows = [[] for _ in range(HEAD_PAIRS)]
        for u in range(group):
            row = lambda x: x[u:u + 1, :]
            for p in range(HEAD_PAIRS):
                kk_t, v_t, d_t, b_t, k_t, r_t = tiles[p]
                s = states[p]
                sk = seg(s * row(kk_t))
                vcol = seg(jnp.where(eye2, row(v_t), 0.0))
                s = s * row(d_t) - sk * row(b_t) + vcol * row(k_t)
                states[p] = s
                r_rows = jnp.broadcast_to(r_t, (SUBLANES, LANES)) if group == 1 else r_t
                o_rows[p].append(readout(s, r_rows.astype(BF16))[u:u + 1, :])
        for p in range(HEAD_PAIRS):
            sl = slice(p * LANES, (p + 1) * LANES)
            o_ref[pl.ds(base, group), sl] = jnp.concatenate(o_rows[p], axis=0) if group > 1 else o_rows[p][0]
        return tuple(states)

    states = lax.fori_loop(0, t // group, body, tuple(s_ref[p] for p in range(HEAD_PAIRS)))
    for p in range(HEAD_PAIRS):
        s_ref[p] = states[p]

    @pl.when(j == pl.num_programs(1) - 1)
    def _():
        st_ref[...] = s_ref[...]


def _rwkv_scan(r, k, v, kk, b, d, s0_pairs, *, batch, t):
    n = r.shape[0]
    seq = n // batch
    tok = pl.BlockSpec((None, t, _RW), lambda bi, j: (bi, j, 0))
    st = pl.BlockSpec((None, HEAD_PAIRS, RWKV_HEAD_DIM, LANES), lambda bi, j: (bi, 0, 0, 0))
    o, s_t = pl.pallas_call(
        functools.partial(_rwkv_scan_kernel, t=t),
        grid=(batch, seq // t),
        in_specs=[tok] * 6 + [st],
        out_specs=[tok, st],
        out_shape=[jax.ShapeDtypeStruct((batch, seq, _RW), F32),
                   jax.ShapeDtypeStruct((batch, HEAD_PAIRS, RWKV_HEAD_DIM, LANES), F32)],
        scratch_shapes=[pltpu.VMEM((HEAD_PAIRS, RWKV_HEAD_DIM, LANES), F32)],
        compiler_params=_cparams(("arbitrary", "arbitrary")),
        name="rwkv_scan",
    )(*(x.reshape(batch, seq, _RW) for x in (r, k, v, kk, b, d)), s0_pairs)
    return o.reshape(n, _RW), s_t


def _rwkv_post_kernel(o_ref, r_ref, k_ref, v_ref, g_ref, lnw_ref, lnb_ref, rk_ref, bones_ref, out_ref):
    bones = bones_ref[...]
    inv = 1.0 / RWKV_HEAD_DIM
    o = o_ref[...]
    mean = _dot_f32_right(o, bones) * inv
    cen = o - mean
    var = _dot_f32_right(cen * cen, bones) * inv
    gn = cen * lax.rsqrt(var + GN_EPS) * lnw_ref[...] + lnb_ref[...]
    bonus = _dot_f32_right(r_ref[...] * k_ref[...] * rk_ref[...], bones) * v_ref[...]
    out_ref[...] = ((gn + bonus) * g_ref[...]).astype(out_ref.dtype)


def _rwkv_post(o, r, k, v, g, lnw, lnb, rk, bones, *, t):
    n = o.shape[0]
    tok = pl.BlockSpec((t, _RW), lambda i: (i, 0))
    row = pl.BlockSpec((1, _RW), lambda i: (0, 0))
    return pl.pallas_call(
        _rwkv_post_kernel,
        grid=(n // t,),
        in_specs=[tok] * 5 + [row] * 3 + [pl.BlockSpec(bones.shape, lambda i: (0, 0))],
        out_specs=tok,
        out_shape=jax.ShapeDtypeStruct((n, _RW), BF16),
        compiler_params=_cparams(("parallel",)),
        name="rwkv_post",
    )(o, r, k, v, g, lnw, lnb, rk, bones)


def _to_pairs(s):
    b = s.shape[0]
    return s.reshape(b, HEAD_PAIRS, 2, RWKV_HEAD_DIM, RWKV_HEAD_DIM).transpose(0, 1, 3, 2, 4).reshape(
        b, HEAD_PAIRS, RWKV_HEAD_DIM, LANES)


def _from_pairs(s):
    b = s.shape[0]
    return s.reshape(b, HEAD_PAIRS, RWKV_HEAD_DIM, 2, RWKV_HEAD_DIM).transpose(0, 1, 3, 2, 4).reshape(
        b, RWKV_HEADS, RWKV_HEAD_DIM, RWKV_HEAD_DIM)


def _rwkv_mix(proj, prev, s0, rw, *, batch, t_prep, t_scan, shifted):
    mu, w0, w2, a0, a2, g2, k_k, k_a, r_k, ln_w, ln_b = rw
    bones = _head_block_ones(_RW, RWKV_HEAD_DIM)
    zeros = jnp.zeros((LANES - DECAY_LORA, _RW), F32)
    w2p = jnp.concatenate([w2, zeros], axis=0).astype(BF16)
    a2p = jnp.concatenate([zeros, a2], axis=0).astype(BF16)
    row = lambda x: x.reshape(1, -1)
    r, k, v, kk, b, d, g, last = _rwkv_prep(
        proj, prev, row(mu[_RWKV_PERM]), row(w0), w2p, row(a0), a2p, g2.astype(BF16), row(k_k), row(k_a), bones,
        t=t_prep, shifted=shifted)
    o, s_t = _rwkv_scan(r, k, v, kk, b, d, _to_pairs(s0), batch=batch, t=t_scan)
    out = _rwkv_post(o, r, k, v, g, row(ln_w), row(ln_b), row(r_k), bones, t=t_prep)
    return out, last[:, _RWKV_INV_PERM], _from_pairs(s_t)


def _top_rows(s, k, order):
    vals, ords = [], []
    for _ in range(k):
        m = jnp.max(s, axis=0, keepdims=True)
        pick = jnp.min(jnp.where(s == m, order, jnp.inf), axis=0, keepdims=True)
        vals.append(m)
        ords.append(pick)
        s = jnp.where(order == pick, -jnp.inf, s)
    return jnp.concatenate(vals, axis=0), jnp.concatenate(ords, axis=0)


def _take_rows(table, pos):
    out = jnp.zeros(pos.shape, table.dtype)
    for i in range(table.shape[0]):
        out = jnp.where(pos == float(i), table[i:i + 1, :], out)
    return out


_CAND_GROUPS = ((0, 2, 16), (2, 4, 8))
_CAND_TAIL_I0, _CAND_TAIL_J = 4, 3
_CAND_HEAD_ROWS = sum((hi - lo) * nj for lo, hi, nj in _CAND_GROUPS)
_CAND_ROWS = _CAND_HEAD_ROWS + _CAND_TAIL_J * PEER_TOPK


def _candidate_tables(tm):
    k = PEER_TOPK
    assert k == 16 and _CAND_GROUPS == ((0, 2, 16), (2, 4, 8))
    r = lax.broadcasted_iota(I32, (_CAND_ROWS, tm), 0)
    r2 = r - 2 * k
    flat2 = (2 + (r2 >> 3)) * k + (r2 & 7)
    r3 = r - _CAND_HEAD_ROWS
    i3 = r3 & (k - 1)
    flat3 = i3 * k + (r3 >> 4)
    flat = jnp.where(r < 2 * k, r, jnp.where(r < _CAND_HEAD_ROWS, flat2, flat3))
    valid = (r < _CAND_HEAD_ROWS) | (i3 >= _CAND_TAIL_I0)
    return flat.astype(F32), valid


def _peer_route_kernel(q_ref, keys_ref, g_ref, e1_ref, e2_ref, gv_ref, gt_ref, *, tm):
    k = PEER_TOPK
    flat, valid = _candidate_tables(tm)
    key_order = lax.broadcasted_iota(I32, (N_KEYS, tm), 0).astype(F32)

    e1s, e2s, gs = [], [], []
    for h in range(PEER_HEADS):
        qh = q_ref[:, h * D_KEY:(h + 1) * D_KEY].astype(BF16)
        score = lambda p: lax.dot_general(keys_ref[h, p], qh, (((1,), (1,)), ((), ())), preferred_element_type=F32)
        v1, i1 = _top_rows(score(0), k, key_order)
        v2, i2 = _top_rows(score(1), k, key_order)
        parts = [v1[i:i + 1, :] + v2[:nj, :] for lo, hi, nj in _CAND_GROUPS for i in range(lo, hi)]
        parts += [v1 + v2[j:j + 1, :] for j in range(_CAND_TAIL_J)]
        cand = jnp.where(valid, jnp.concatenate(parts, axis=0), -jnp.inf)
        sc, pos = _top_rows(cand, k, flat)
        pos_i = jnp.floor(pos * (1.0 / k))
        e1s.append(_take_rows(i1, pos_i))
        e2s.append(_take_rows(i2, pos - k * pos_i))
        ex = jnp.exp(sc - sc[0:1, :])
        gs.append(ex / jnp.sum(ex, axis=0, keepdims=True))
    e1_ref[...] = jnp.concatenate(e1s, axis=0).T
    e2_ref[...] = jnp.concatenate(e2s, axis=0).T
    gv_ref[...] = jnp.concatenate(gs, axis=0).T

    eid = lax.broadcasted_iota(I32, (N_KEYS, PEER_HEADS * k), 0).astype(F32)
    rows = _ROUTE_GROUP

    def body(gi, carry):
        base = pl.multiple_of(gi * rows, rows)
        e1_t = e1_ref[pl.ds(base, rows), :]
        e2_t = e2_ref[pl.ds(base, rows), :]
        gv_t = gv_ref[pl.ds(base, rows), :]
        for u in range(rows):
            a = jnp.where(eid == e1_t[u:u + 1, :], gv_t[u:u + 1, :], 0.0).astype(BF16)
            b = jnp.where(eid == e2_t[u:u + 1, :], 1.0, 0.0).astype(BF16)
            gt_ref[u * _ROUTE_PITCH:u * _ROUTE_PITCH + N_KEYS, :] = lax.dot_general(
                a, b, (((1,), (1,)), ((), ())), preferred_element_type=F32)
        for e1 in range(N_KEYS):
            slab = gt_ref[pl.ds(e1, rows, stride=_ROUTE_PITCH), :]
            g_ref[pl.ds(base, rows), e1 * N_KEYS:(e1 + 1) * N_KEYS] = slab.astype(g_ref.dtype)
        return carry

    lax.fori_loop(0, tm // rows, body, 0)


_ROUTE_GROUP = 2 * SUBLANES
_ROUTE_PITCH = N_KEYS + SUBLANES


def _peer_route(q, keys_pad, *, tm):
    n = q.shape[0]
    width = PEER_HEADS * PEER_TOPK
    assert tm % _ROUTE_GROUP == 0
    return pl.pallas_call(
        functools.partial(_peer_route_kernel, tm=tm),
        grid=(n // tm,),
        in_specs=[pl.BlockSpec((tm, PEER_HEADS * D_KEY), lambda i: (i, 0)),
                  pl.BlockSpec(keys_pad.shape, lambda i: (0, 0, 0, 0))],
        out_specs=pl.BlockSpec((tm, N_KEYS * N_KEYS), lambda i: (i, 0)),
        out_shape=jax.ShapeDtypeStruct((n, N_KEYS * N_KEYS), BF16),
        scratch_shapes=[pltpu.VMEM((tm, width), F32)] * 3 + [pltpu.VMEM((_ROUTE_GROUP * _ROUTE_PITCH, N_KEYS), F32)],
        compiler_params=_cparams(("parallel",)),
        name="peer_route",
    )(q, keys_pad)


def _peer_mlp_kernel(x_ref, gain_ref, u_ref, v_ref, g_ref, o_ref, xn_ref, acc_ref):
    j = pl.program_id(1)

    @pl.when(j == 0)
    def _():
        x = x_ref[...]
        ms = jnp.mean(x * x, axis=-1, keepdims=True)
        xn_ref[...] = (x * lax.rsqrt(ms + RMS_EPS) * gain_ref[...]).astype(BF16)
        acc_ref[...] = jnp.zeros_like(acc_ref)

    h = lax.dot_general(xn_ref[...], u_ref[...], (((1,), (1,)), ((), ())), preferred_element_type=F32)
    act = 0.5 * h * (1.0 + lax.erf(h * (2.0 ** -0.5)))
    w = (act * g_ref[...].astype(F32)).astype(BF16)
    acc_ref[...] += jnp.dot(w, v_ref[...], preferred_element_type=F32)

    @pl.when(j == pl.num_programs(1) - 1)
    def _():
        o_ref[...] = x_ref[...] + acc_ref[...]


def _peer_mlp(x, gain, u_bf16, v_bf16, gmap, *, tm, te):
    n, dm = x.shape
    ne = u_bf16.shape[0]
    return pl.pallas_call(
        _peer_mlp_kernel,
        grid=(n // tm, ne // te),
        in_specs=[pl.BlockSpec((tm, dm), lambda i, j: (i, 0)),
                  pl.BlockSpec((1, dm), lambda i, j: (0, 0)),
                  pl.BlockSpec((te, dm), lambda i, j: (j, 0)),
                  pl.BlockSpec((te, dm), lambda i, j: (j, 0)),
                  pl.BlockSpec((tm, te), lambda i, j: (i, j))],
        out_specs=pl.BlockSpec((tm, dm), lambda i, j: (i, 0)),
        out_shape=jax.ShapeDtypeStruct((n, dm), F32),
        scratch_shapes=[pltpu.VMEM((tm, dm), BF16), pltpu.VMEM((tm, dm), F32)],
        compiler_params=_cparams(("parallel", "arbitrary")),
        name="peer_mlp",
    )(x, gain.reshape(1, dm), u_bf16, v_bf16, gmap)


def _peer_keys_padded(keys):
    z = jnp.zeros_like(keys[:, 0])
    first = jnp.concatenate([keys[:, 0], z], axis=-1)
    second = jnp.concatenate([z, keys[:, 1]], axis=-1)
    return jnp.stack([first, second], axis=1).astype(BF16)


def _peer(y, gain, wq_bf16, keys_pad, u_bf16, v_bf16, *, tm_proj, tm_route, tm_mlp, te):
    n = y.shape[0]
    q = _norm_matmul(y, gain, wq_bf16, tm=tm_proj, tn=wq_bf16.shape[1])
    gmap = _peer_route(q, keys_pad, tm=tm_route)
    return _peer_mlp(y, gain, u_bf16, v_bf16, gmap, tm=tm_mlp, te=te)


def _conv_silu(x, buf_ref, w_ref, b_ref, q):
    buf_ref[SUBLANES:SUBLANES + q, :] = x
    acc = b_ref[...] + x * w_ref[CONV_W - 1:CONV_W, :]
    for back in range(1, CONV_W):
        acc = acc + buf_ref[SUBLANES - back:SUBLANES - back + q, :] * w_ref[CONV_W - 1 - back:CONV_W - back, :]
    tail = buf_ref[q + SUBLANES - (CONV_W - 1):q + SUBLANES, :]
    buf_ref[SUBLANES - (CONV_W - 1):SUBLANES, :] = tail
    return _silu(acc), tail


def _ssd_kernel(z_ref, x_ref, bc_ref, dt_ref, cwx_ref, cwbc_ref, cbx_ref, cbbc_ref, dtb_ref, alog_ref, dskip_ref,
                nw_ref, expand_ref, y_ref, convx_ref, convbc_ref, ht_ref, bufx_ref, bufbc_ref, h_ref, *, q, heads):
    hpg = heads // N_GROUPS
    gw = hpg * SSM_HEAD_DIM

    @pl.when(pl.program_id(0) == 0)
    def _():
        bufx_ref[...] = jnp.zeros_like(bufx_ref)
        bufbc_ref[...] = jnp.zeros_like(bufbc_ref)
        h_ref[...] = jnp.zeros_like(h_ref)

    xs, tail_x = _conv_silu(x_ref[...], bufx_ref, cwx_ref, cbx_ref, q)
    bc, tail_bc = _conv_silu(bc_ref[...], bufbc_ref, cwbc_ref, cbbc_ref, q)
    convx_ref[...] = tail_x
    convbc_ref[...] = tail_bc

    dt = _softplus(dt_ref[...] + dtb_ref[...])
    a = -jnp.exp(alog_ref[...])
    row = lax.broadcasted_iota(I32, (q, q), 0)
    col = lax.broadcasted_iota(I32, (q, q), 1)
    lower = col <= row
    tri = jnp.where(lower, 1.0, 0.0).astype(BF16)
    acs = _dot_f32_left(tri, dt * a)
    acs_t = acs.T
    expand = expand_ref[...]
    dt_e = _dot_f32_right(dt, expand)
    acs_e = _dot_f32_right(acs, expand)
    last_e = acs_e[q - 1:q, :]
    xc = xs * dt_e
    xcd = (xc * jnp.exp(last_e - acs_e)).astype(BF16)
    xc_b = xc.astype(BF16)
    grow = jnp.exp(acs_e)
    chunk_decay = jnp.exp(last_e)

    nbc = N_GROUPS * D_STATE
    for g in range(N_GROUPS):
        bm = bc[:, g * D_STATE:(g + 1) * D_STATE].astype(BF16)
        cm = bc[:, nbc + g * D_STATE:nbc + (g + 1) * D_STATE].astype(BF16)
        cb = lax.dot_general(cm, bm, (((1,), (1,)), ((), ())), preferred_element_type=F32)
        sl = slice(g * gw, (g + 1) * gw)
        h_prev = h_ref[g]
        y_off = jnp.dot(cm, h_prev.astype(BF16), preferred_element_type=F32) * grow[:, sl]
        y_diag = []
        for r in range(hpg):
            hd = g * hpg + r
            diff = acs[:, hd:hd + 1] - acs_t[hd:hd + 1, :]
            m = (cb * jnp.exp(jnp.where(lower, diff, -jnp.inf))).astype(BF16)
            y_diag.append(jnp.dot(m, xc_b[:, hd * SSM_HEAD_DIM:(hd + 1) * SSM_HEAD_DIM], preferred_element_type=F32))
        y_g = jnp.concatenate(y_diag, axis=1) + y_off + dskip_ref[:, sl] * xs[:, sl]
        states = jnp.dot(bm.T, xcd[:, sl], preferred_element_type=F32)
        h_ref[g] = h_prev * chunk_decay[:, sl] + states
        zg = z_ref[:, sl]
        y_g = y_g * _silu(zg)
        ms = jnp.mean(y_g * y_g, axis=-1, keepdims=True)
        y_ref[:, sl] = (y_g * lax.rsqrt(ms + RMS_EPS) * nw_ref[:, sl]).astype(y_ref.dtype)

    @pl.when(pl.program_id(0) == pl.num_programs(0) - 1)
    def _():
        ht_ref[...] = h_ref[...]


def _ssd(proj, conv_w, conv_b, dt_bias, a_log, d_skip, norm_w, *, heads):
    n = proj.shape[0]
    d_inner = heads * SSM_HEAD_DIM
    q = CHUNK
    nbc = 2 * N_GROUPS * D_STATE
    assert nbc == d_inner and n % q == 0
    gw = d_inner // N_GROUPS
    pad = lambda x: jnp.pad(x, (0, LANES - heads)).reshape(1, LANES)
    expand = jnp.asarray((np.arange(LANES)[:, None] == (np.arange(d_inner) // SSM_HEAD_DIM)[None, :]).astype(np.float32), BF16)
    blk = lambda w, j: pl.BlockSpec((q, w), lambda i, j=j: (i, j))
    full = lambda a: pl.BlockSpec(a.shape, lambda i: (0,) * a.ndim)
    small = [conv_w[:, :d_inner], conv_w[:, d_inner:], conv_b[:d_inner].reshape(1, -1), conv_b[d_inner:].reshape(1, -1),
             pad(dt_bias), pad(a_log), jnp.repeat(d_skip, SSM_HEAD_DIM).reshape(1, -1), norm_w.reshape(1, -1), expand]
    y, cx, cbc, ht = pl.pallas_call(
        functools.partial(_ssd_kernel, q=q, heads=heads),
        grid=(n // q,),
        in_specs=[blk(d_inner, 0), blk(d_inner, 1), blk(d_inner, 2), blk(LANES, 3 * d_inner // LANES)]
                 + [full(a) for a in small],
        out_specs=[pl.BlockSpec((q, d_inner), lambda i: (i, 0)),
                   pl.BlockSpec((CONV_W - 1, d_inner), lambda i: (0, 0)),
                   pl.BlockSpec((CONV_W - 1, nbc), lambda i: (0, 0)),
                   pl.BlockSpec((N_GROUPS, D_STATE, gw), lambda i: (0, 0, 0))],
        out_shape=[jax.ShapeDtypeStruct((n, d_inner), BF16),
                   jax.ShapeDtypeStruct((CONV_W - 1, d_inner), F32),
                   jax.ShapeDtypeStruct((CONV_W - 1, nbc), F32),
                   jax.ShapeDtypeStruct((N_GROUPS, D_STATE, gw), F32)],
        scratch_shapes=[pltpu.VMEM((SUBLANES + q, d_inner), F32), pltpu.VMEM((SUBLANES + q, nbc), F32),
                        pltpu.VMEM((N_GROUPS, D_STATE, gw), F32)],
        compiler_params=_cparams(("arbitrary",)),
        name="ssd",
    )(proj, proj, proj, proj, *small)
    hpg = heads // N_GROUPS
    h_t = ht.reshape(N_GROUPS, D_STATE, hpg, SSM_HEAD_DIM).transpose(0, 2, 3, 1).reshape(heads, SSM_HEAD_DIM, D_STATE)
    return y, jnp.concatenate([cx, cbc], axis=1), h_t


def _ssm_step_kernel(z_ref, x_ref, bc_ref, dt_ref, bufx_ref, bufbc_ref, h0_ref, cwx_ref, cwbc_ref, cbx_ref, cbbc_ref,
                     dtb_ref, alog_ref, dskip_ref, nw_ref, y_ref, nbufx_ref, nbufbc_ref, h_ref, *, heads):
    hpg = heads // N_GROUPS
    gw = hpg * SSM_HEAD_DIM
    nbc = N_GROUPS * D_STATE
    fullx = jnp.concatenate([bufx_ref[...], x_ref[...]], axis=1)
    nbufx_ref[...] = fullx[:, 1:]
    xs = _silu(cbx_ref[...] + jnp.sum(fullx * cwx_ref[...], axis=1, keepdims=True))
    fullbc = jnp.concatenate([bufbc_ref[...], bc_ref[...]], axis=0)
    nbufbc_ref[...] = fullbc[1:, :]
    bc = _silu(cbbc_ref[...] + jnp.sum(fullbc * cwbc_ref[...], axis=0, keepdims=True))
    dt = _softplus(dt_ref[...] + dtb_ref[...])
    dec = jnp.exp(dt * (-jnp.exp(alog_ref[...])))
    xdt = xs * dt
    for g in range(N_GROUPS):
        rows = slice(g * gw, (g + 1) * gw)
        bm = bc[:, g * D_STATE:(g + 1) * D_STATE]
        cm = bc[:, nbc + g * D_STATE:nbc + (g + 1) * D_STATE]
        h_new = h0_ref[rows, :] * dec[rows, :] + xdt[rows, :] * bm
        h_ref[rows, :] = h_new
        y = jnp.sum(h_new * cm, axis=-1, keepdims=True) + dskip_ref[rows, :] * xs[rows, :]
        y = y * _silu(z_ref[rows, :])
        ms = jnp.mean(y * y, axis=0, keepdims=True)
        y_ref[rows, :] = y * lax.rsqrt(ms + RMS_EPS) * nw_ref[rows, :]


def _ssm_step(proj, conv_buf, h0, conv_w, conv_b, dt_bias, a_log, d_skip, norm_w, *, heads):
    b = proj.shape[0]
    d_inner = heads * SSM_HEAD_DIM
    nbc2 = 2 * N_GROUPS * D_STATE
    col = lambda x: x.reshape(b, d_inner, 1)
    per_head = lambda v: jnp.repeat(v, SSM_HEAD_DIM, axis=-1)
    z = col(proj[:, :d_inner])
    x = col(proj[:, d_inner:2 * d_inner])
    bc = proj[:, 2 * d_inner:2 * d_inner + nbc2].reshape(b, 1, nbc2)
    dt = col(per_head(proj[:, 2 * d_inner + nbc2:2 * d_inner + nbc2 + heads]))
    bufx = conv_buf[:, :, :d_inner].transpose(0, 2, 1)
    bufbc = conv_buf[:, :, d_inner:]
    pcol = lambda v: v.reshape(d_inner, 1)
    params = [conv_w[:, :d_inner].T, conv_w[:, d_inner:], pcol(conv_b[:d_inner]), conv_b[d_inner:].reshape(1, nbc2),
              pcol(per_head(dt_bias)), pcol(per_head(a_log)), pcol(per_head(d_skip)), pcol(norm_w)]
    seq = lambda *shape: pl.BlockSpec((None,) + shape, lambda i: (i,) + (0,) * len(shape))
    full = lambda a: pl.BlockSpec(a.shape, lambda i: (0,) * a.ndim)
    y, nbx, nbbc, h = pl.pallas_call(
        functools.partial(_ssm_step_kernel, heads=heads),
        grid=(b,),
        in_specs=[seq(d_inner, 1), seq(d_inner, 1), seq(1, nbc2), seq(d_inner, 1), seq(d_inner, CONV_W - 1),
                  seq(CONV_W - 1, nbc2), seq(d_inner, D_STATE)] + [full(a) for a in params],
        out_specs=[seq(d_inner, 1), seq(d_inner, CONV_W - 1), seq(CONV_W - 1, nbc2), seq(d_inner, D_STATE)],
        out_shape=[jax.ShapeDtypeStruct((b, d_inner, 1), F32), jax.ShapeDtypeStruct((b, d_inner, CONV_W - 1), F32),
                   jax.ShapeDtypeStruct((b, CONV_W - 1, nbc2), F32), jax.ShapeDtypeStruct((b, d_inner, D_STATE), F32)],
        compiler_params=_cparams(("parallel",)),
        name="ssm_step",
    )(z, x, bc, dt, bufx, bufbc, h0.reshape(b, d_inner, D_STATE), *params)
    new_buf = jnp.concatenate([nbx.transpose(0, 2, 1), nbbc], axis=2)
    return y.reshape(b, d_inner), new_buf, h.reshape(b, heads, SSM_HEAD_DIM, D_STATE)


def _fox_decode_kernel(pt_ref, q_ref, kc_ref, vc_ref, lfc_ref, *refs, pages):
    k_refs = refs[:pages]
    v_refs = refs[pages:2 * pages]
    lf_refs = refs[2 * pages:3 * pages]
    o_ref, qb_ref, m_ref, l_ref, acc_ref, later_ref = refs[3 * pages:]
    j = pl.program_id(1)
    first_lane = lax.broadcasted_iota(I32, (1, LANES), 1) == 0

    @pl.when(j == 0)
    def _():
        s_rows = []
        for h in range(FOX_HEADS):
            qh = q_ref[h] * (FOX_HEAD_DIM ** -0.5)
            qb_ref[h] = jnp.broadcast_to(qh, (FOX_HEAD_DIM, LANES))
            s_rows.append(jnp.sum(qh * kc_ref[h], axis=0, keepdims=True))
            acc_ref[h] = jnp.where(first_lane, vc_ref[h], 0.0)
        m_ref[...] = jnp.broadcast_to(jnp.concatenate(s_rows, axis=0), m_ref.shape)
        l_ref[...] = jnp.broadcast_to(jnp.where(first_lane, 1.0, 0.0), l_ref.shape)
        later_ref[...] = jnp.broadcast_to(lfc_ref[...], later_ref.shape)

    urow = lax.broadcasted_iota(I32, (LANES, LANES), 0)
    tcol = lax.broadcasted_iota(I32, (LANES, LANES), 1)
    after = jnp.where(urow > tcol, 1.0, 0.0).astype(BF16)
    for u in range(pages):
        lf = lf_refs[u][...]
        later = later_ref[...]
        bias = _dot_f32_right(lf, after) + later
        rows = [jnp.sum(k_refs[u][h] * qb_ref[h], axis=0, keepdims=True) for h in range(FOX_HEADS)]
        s = jnp.concatenate(rows, axis=0) + bias
        m_old = m_ref[...]
        m_new = jnp.maximum(m_old, jnp.max(s, axis=-1, keepdims=True))
        alpha = jnp.exp(m_old - m_new)
        p = jnp.exp(s - m_new)
        l_ref[...] = alpha * l_ref[...] + p
        m_ref[...] = m_new
        for h in range(FOX_HEADS):
            acc_ref[h] = alpha[h:h + 1, :] * acc_ref[h] + p[h:h + 1, :] * v_refs[u][h]
        later_ref[...] = later + jnp.sum(lf, axis=-1, keepdims=True)

    @pl.when(j == pl.num_programs(1) - 1)
    def _():
        l = jnp.sum(l_ref[...], axis=-1, keepdims=True)
        for h in range(FOX_HEADS):
            o_ref[h] = jnp.sum(acc_ref[h], axis=-1, keepdims=True) / l[h:h + 1, :]


def _fox_decode(q, k_cur, v_cur, logf_cur, cache_k, cache_v, cache_logf, page_table, *, pages):
    b, n_pages = page_table.shape
    page = cache_k.shape[1]
    assert page == LANES and n_pages % pages == 0
    kt = cache_k.transpose(0, 2, 3, 1)
    vt = cache_v.transpose(0, 2, 3, 1)
    lft = cache_logf.transpose(0, 2, 1)
    col = lambda x: x.reshape(b, FOX_HEADS, FOX_HEAD_DIM, 1)
    tok = pl.BlockSpec((None, FOX_HEADS, FOX_HEAD_DIM, 1), lambda bi, j, pt: (bi, 0, 0, 0))
    newest_first = lambda bi, j, pt, u: pt[bi, n_pages - 1 - (j * pages + u)]
    kv_page = lambda u: pl.BlockSpec((None, FOX_HEADS, FOX_HEAD_DIM, page),
                                     lambda bi, j, pt, u=u: (newest_first(bi, j, pt, u), 0, 0, 0))
    lf_page = lambda u: pl.BlockSpec((None, FOX_HEADS, page), lambda bi, j, pt, u=u: (newest_first(bi, j, pt, u), 0, 0))
    out = pl.pallas_call(
        functools.partial(_fox_decode_kernel, pages=pages),
        grid_spec=pltpu.PrefetchScalarGridSpec(
            num_scalar_prefetch=1,
            grid=(b, n_pages // pages),
            in_specs=[tok, tok, tok, pl.BlockSpec((None, FOX_HEADS, 1), lambda bi, j, pt: (bi, 0, 0))]
                     + [kv_page(u) for u in range(pages)] * 2 + [lf_page(u) for u in range(pages)],
            out_specs=tok,
            scratch_shapes=[pltpu.VMEM((FOX_HEADS, FOX_HEAD_DIM, LANES), F32), pltpu.VMEM((FOX_HEADS, LANES), F32),
                            pltpu.VMEM((FOX_HEADS, LANES), F32), pltpu.VMEM((FOX_HEADS, FOX_HEAD_DIM, LANES), F32),
                            pltpu.VMEM((FOX_HEADS, LANES), F32)],
        ),
        out_shape=jax.ShapeDtypeStruct((b, FOX_HEADS, FOX_HEAD_DIM, 1), F32),
        compiler_params=_cparams(("parallel", "arbitrary")),
        name="fox_decode",
    )(page_table, col(q), col(k_cur), col(v_cur), logf_cur.reshape(b, FOX_HEADS, 1),
      *([kt] * pages), *([vt] * pages), *([lft] * pages))
    return out.reshape(b, FOX_WIDTH)


_COL_Q = RWKV_PROJ
_COL_K = _COL_Q + FOX_WIDTH
_COL_V = _COL_K + FOX_WIDTH
_COL_FL = _COL_V + FOX_WIDTH
_EVEN_COLS = _COL_FL + LANES


def _even_weights(w_in):
    q, k, v = (w_in[:, i * FOX_WIDTH:(i + 1) * FOX_WIDTH] for i in range(3))
    fl = w_in[:, 3 * FOX_WIDTH:3 * FOX_WIDTH + FOX_HEADS]
    rw = w_in[:, 3 * FOX_WIDTH + FOX_HEADS:][:, _RWKV_PERM]
    fl_pad = jnp.pad(fl, ((0, 0), (0, LANES - FOX_HEADS)))
    return jnp.concatenate([rw, q, k, v, fl_pad], axis=1).astype(BF16)


def _even_layer(y, gain, w_cat, w_out, fb, rw, *, batch, seq, past, shift0, wkv0):
    n = y.shape[0]
    tm = min(512, n)
    proj = _norm_matmul(y, gain, w_cat, tm=min(1024, n), tn=_EVEN_COLS)
    fb_pad = jnp.pad(fb, (0, LANES - FOX_HEADS)).reshape(1, LANES)
    logf, ct = _fox_prep(proj, _COL_FL // LANES, fb_pad, t=min(256, n))
    k_new = proj[:, _COL_K:_COL_V]
    v_new = proj[:, _COL_V:_COL_FL]
    if past is None:
        qkv = proj[:, _COL_Q:_COL_FL].astype(BF16)
        o_fox = _fox_attn(qkv, ct, tq=1024)
        o_rwkv, shift_new, wkv_new = _rwkv_mix(proj, shift0[:, _RWKV_PERM], wkv0, rw, batch=batch,
                                               t_prep=256, t_scan=256, shifted=True)
    else:
        cache_k, cache_v, cache_logf, page_table = past
        o_fox = _fox_decode(proj[:, _COL_Q:_COL_K], k_new, v_new, logf, cache_k, cache_v, cache_logf, page_table,
                            pages=8)
        o_rwkv, shift_new, wkv_new = _rwkv_mix(proj, shift0[:, _RWKV_PERM], wkv0, rw, batch=batch,
                                               t_prep=n, t_scan=1, shifted=False)
    w_out_b = w_out.astype(BF16)
    y = _matmul_res([o_fox, o_rwkv], [w_out_b[:FOX_WIDTH], w_out_b[FOX_WIDTH:]], y, tm=tm, tn=y.shape[1])
    heads = lambda t: t.reshape(batch, seq, FOX_HEADS, FOX_HEAD_DIM)
    return y, heads(k_new), heads(v_new), logf.reshape(batch, seq, FOX_HEADS), wkv_new, shift_new


def kernel(x_prompt, x_sample, cache_k, cache_v, cache_logf, page_table, state_wkv, state_shift, state_conv, state_ssm, norm_mix, norm_ffn, norm_final, w_in_even, w_out_even, fox_fb, rwkv_mu, rwkv_w0, rwkv_w2, rwkv_a0, rwkv_a2, rwkv_g2, rwkv_kk, rwkv_ka, rwkv_rk, rwkv_lnw, rwkv_lnb, w_in_odd, w_out_odd, ssm_conv_w, ssm_conv_b, ssm_dt_bias, ssm_a_log, ssm_d, ssm_norm_w, peer_wq, peer_keys, peer_u, peer_v):
    bp, seq_p, dm = x_prompt.shape
    bs, seq_s, _ = x_sample.shape
    assert bp == 1 and seq_s == 1
    depth = norm_mix.shape[0]
    ssm_heads = ssm_dt_bias.shape[1]
    yp = x_prompt.reshape(bp * seq_p, dm)
    ys = x_sample.reshape(bs * seq_s, dm)
    outs_p = {name: [] for name in ("k", "v", "logf", "wkv", "shift", "conv", "ssm")}
    outs_s = {name: [] for name in outs_p}
    for layer in range(depth):
        if layer % 2 == 0:
            e = layer // 2
            rw = (rwkv_mu[e], rwkv_w0[e], rwkv_w2[e], rwkv_a0[e], rwkv_a2[e], rwkv_g2[e],
                  rwkv_kk[e], rwkv_ka[e], rwkv_rk[e], rwkv_lnw[e], rwkv_lnb[e])
            w_cat = _even_weights(w_in_even[e])
            yp, *res_p = _even_layer(
                yp, norm_mix[layer], w_cat, w_out_even[e], fox_fb[e], rw, batch=bp, seq=seq_p, past=None,
                shift0=jnp.zeros((bp, RWKV_PROJ), F32),
                wkv0=jnp.zeros((bp, RWKV_HEADS, RWKV_HEAD_DIM, RWKV_HEAD_DIM), F32))
            ys, *res_s = _even_layer(
                ys, norm_mix[layer], w_cat, w_out_even[e], fox_fb[e], rw, batch=bs, seq=seq_s,
                past=(cache_k[e], cache_v[e], cache_logf[e], page_table), shift0=state_shift[e], wkv0=state_wkv[e])
            for outs, res in ((outs_p, res_p), (outs_s, res_s)):
                for name, val in zip(("k", "v", "logf", "wkv", "shift"), res):
                    outs[name].append(val)
        else:
            o = layer // 2
            mprm = (ssm_conv_w[o], ssm_conv_b[o], ssm_dt_bias[o], ssm_a_log[o], ssm_d[o], ssm_norm_w[o])
            w_cat = jnp.pad(w_in_odd[o], ((0, 0), (0, LANES - ssm_heads))).astype(BF16)
            w_out = w_out_odd[o].astype(BF16)
            tn = w_cat.shape[1]
            proj_p = _norm_matmul(yp, norm_mix[layer], w_cat, tm=256, tn=tn)
            y_m, conv_p, h_p = _ssd(proj_p, *mprm, heads=ssm_heads)
            yp = _matmul_res([y_m], [w_out], yp, tm=512, tn=dm)
            proj_s = _norm_matmul(ys, norm_mix[layer], w_cat, tm=bs, tn=tn)
            y_s, conv_s, h_s = _ssm_step(proj_s, state_conv[o], state_ssm[o], *mprm, heads=ssm_heads)
            ys = _matmul_res([y_s], [w_out], ys, tm=bs, tn=dm)
            outs_p["conv"].append(conv_p[None])
            outs_p["ssm"].append(h_p[None])
            outs_s["conv"].append(conv_s)
            outs_s["ssm"].append(h_s)
        wq = peer_wq[layer].astype(BF16)
        keys_pad = _peer_keys_padded(peer_keys[layer])
        u_b = peer_u[layer].astype(BF16)
        v_b = peer_v[layer].astype(BF16)
        yp = _peer(yp, norm_ffn[layer], wq, keys_pad, u_b, v_b, tm_proj=512, tm_route=128, tm_mlp=512, te=2048)
        ys = _peer(ys, norm_ffn[layer], wq, keys_pad, u_b, v_b, tm_proj=bs, tm_route=bs, tm_mlp=bs, te=2048)
    y_prompt = _rmsnorm(yp, norm_final, tm=512).reshape(bp, seq_p, dm)
    y_sample = _rmsnorm(ys, norm_final, tm=bs).reshape(bs, seq_s, dm)
    order = ("k", "v", "logf", "wkv", "shift", "conv", "ssm")
    return ((y_prompt, y_sample) + tuple(jnp.stack(outs_p[name]) for name in order)
            + tuple(jnp.stack(outs_s[name]) for name in order))
```

```python
import functools
import math

import numpy as np
import jax
import jax.numpy as jnp
from jax import lax
from jax.experimental import pallas as pl
from jax.experimental.pallas import tpu as pltpu

F32 = jnp.float32
BF16 = jnp.bfloat16
I32 = jnp.int32

LANES = 128
SUBLANES = 8
V7X_VMEM_BYTES = 64 * 1024 * 1024

RMS_EPS = 1e-6
GN_EPS = 64e-5

FOX_HEADS = 8
FOX_HEAD_DIM = 64
FOX_WIDTH = FOX_HEADS * FOX_HEAD_DIM
RWKV_HEADS = 8
RWKV_HEAD_DIM = 64
RWKV_WIDTH = RWKV_HEADS * RWKV_HEAD_DIM
DECAY_LORA = 64
ICLR_LORA = 64
GATE_LORA = 128
RWKV_PROJ = 4 * RWKV_WIDTH - RWKV_WIDTH + DECAY_LORA + ICLR_LORA + GATE_LORA
HEAD_PAIRS = RWKV_HEADS // 2

SSM_HEAD_DIM = 64
D_STATE = 128
N_GROUPS = 8
CONV_W = 4
CHUNK = 128

N_KEYS = 128
PEER_HEADS = 8
PEER_TOPK = 16
D_KEY = 128


def _cparams(semantics, vmem_mb=48):
    return pltpu.CompilerParams(dimension_semantics=semantics, vmem_limit_bytes=vmem_mb * 1024 * 1024)


def _split3(x):
    hi = x.astype(BF16)
    r1 = x - hi.astype(F32)
    mid = r1.astype(BF16)
    lo = (r1 - mid.astype(F32)).astype(BF16)
    return hi, mid, lo


def _dot_f32_right(x, m_bf16):
    hi, mid, lo = _split3(x)
    d = lambda a: jnp.dot(a, m_bf16, preferred_element_type=F32)
    return d(hi) + d(mid) + d(lo)


def _dot_f32_left(m_bf16, x):
    hi, mid, lo = _split3(x)
    d = lambda a: jnp.dot(m_bf16, a, preferred_element_type=F32)
    return d(hi) + d(mid) + d(lo)


def _sigmoid(x):
    return 1.0 / (1.0 + jnp.exp(-x))


def _softplus(x):
    return jnp.maximum(x, 0.0) + jnp.log1p(jnp.exp(-jnp.abs(x)))


def _silu(x):
    return x * _sigmoid(x)


def _norm_matmul_kernel(x_ref, g_ref, w_ref, o_ref, xn_ref):
    @pl.when(pl.program_id(1) == 0)
    def _():
        x = x_ref[...]
        ms = jnp.mean(x * x, axis=-1, keepdims=True)
        xn_ref[...] = (x * lax.rsqrt(ms + RMS_EPS) * g_ref[...]).astype(BF16)

    o_ref[...] = jnp.dot(xn_ref[...], w_ref[...], preferred_element_type=F32).astype(o_ref.dtype)


def _norm_matmul(x, g, w_bf16, *, tm, tn, out_dtype=F32):
    m, k = x.shape
    n = w_bf16.shape[1]
    assert m % tm == 0 and n % tn == 0
    return pl.pallas_call(
        _norm_matmul_kernel,
        grid=(m // tm, n // tn),
        in_specs=[
            pl.BlockSpec((tm, k), lambda i, j: (i, 0)),
            pl.BlockSpec((1, k), lambda i, j: (0, 0)),
            pl.BlockSpec((k, tn), lambda i, j: (0, j)),
        ],
        out_specs=pl.BlockSpec((tm, tn), lambda i, j: (i, j)),
        out_shape=jax.ShapeDtypeStruct((m, n), out_dtype),
        scratch_shapes=[pltpu.VMEM((tm, k), BF16)],
        compiler_params=_cparams(("parallel", "arbitrary")),
        name="norm_matmul",
    )(x, g.reshape(1, k), w_bf16)


def _matmul_res_kernel(*refs, n_in):
    a_refs = refs[:n_in]
    w_refs = refs[n_in:2 * n_in]
    r_ref, o_ref = refs[2 * n_in], refs[2 * n_in + 1]
    acc = r_ref[...]
    for a_ref, w_ref in zip(a_refs, w_refs):
        acc = acc + jnp.dot(a_ref[...].astype(BF16), w_ref[...], preferred_element_type=F32)
    o_ref[...] = acc


def _matmul_res(a_list, w_list, res, *, tm, tn):
    m, n = res.shape
    n_in = len(a_list)
    assert m % tm == 0 and n % tn == 0
    in_specs = ([pl.BlockSpec((tm, a.shape[1]), lambda i, j: (i, 0)) for a in a_list]
                + [pl.BlockSpec((w.shape[0], tn), lambda i, j: (0, j)) for w in w_list]
                + [pl.BlockSpec((tm, tn), lambda i, j: (i, j))])
    return pl.pallas_call(
        functools.partial(_matmul_res_kernel, n_in=n_in),
        grid=(m // tm, n // tn),
        in_specs=in_specs,
        out_specs=pl.BlockSpec((tm, tn), lambda i, j: (i, j)),
        out_shape=jax.ShapeDtypeStruct((m, n), F32),
        compiler_params=_cparams(("parallel", "parallel")),
        name="matmul_res",
    )(*a_list, *w_list, res)


def _rmsnorm_kernel(x_ref, g_ref, o_ref):
    x = x_ref[...]
    ms = jnp.mean(x * x, axis=-1, keepdims=True)
    o_ref[...] = x * lax.rsqrt(ms + RMS_EPS) * g_ref[...]


def _rmsnorm(x, g, *, tm):
    m, k = x.shape
    return pl.pallas_call(
        _rmsnorm_kernel,
        grid=(m // tm,),
        in_specs=[pl.BlockSpec((tm, k), lambda i: (i, 0)), pl.BlockSpec((1, k), lambda i: (0, 0))],
        out_specs=pl.BlockSpec((tm, k), lambda i: (i, 0)),
        out_shape=jax.ShapeDtypeStruct((m, k), F32),
        compiler_params=_cparams(("parallel",)),
        name="rmsnorm",
    )(x, g.reshape(1, k))


def _fox_prep_kernel(fl_ref, fb_ref, lf_ref, ct_ref, carry_ref, *, t):
    @pl.when(pl.program_id(0) == 0)
    def _():
        carry_ref[...] = jnp.zeros_like(carry_ref)

    x = fl_ref[...] + fb_ref[...]
    lf = jnp.minimum(x, 0.0) - jnp.log1p(jnp.exp(-jnp.abs(x)))
    lf_ref[...] = lf[:, :FOX_HEADS]
    row = lax.broadcasted_iota(I32, (t, t), 0)
    col = lax.broadcasted_iota(I32, (t, t), 1)
    tri = jnp.where(col <= row, 1.0, 0.0).astype(BF16)
    c = _dot_f32_left(tri, lf) + carry_ref[...]
    carry_ref[...] = c[t - 1:t, :]
    ct_ref[...] = c.T[:FOX_HEADS, :]


def _fox_prep(proj, fl_block, fb_pad, *, t):
    n = proj.shape[0]
    return pl.pallas_call(
        functools.partial(_fox_prep_kernel, t=t),
        grid=(n // t,),
        in_specs=[pl.BlockSpec((t, LANES), lambda i: (i, fl_block)),
                  pl.BlockSpec((1, LANES), lambda i: (0, 0))],
        out_specs=[pl.BlockSpec((t, FOX_HEADS), lambda i: (i, 0)),
                   pl.BlockSpec((FOX_HEADS, t), lambda i: (0, i))],
        out_shape=[jax.ShapeDtypeStruct((n, FOX_HEADS), F32),
                   jax.ShapeDtypeStruct((FOX_HEADS, n), F32)],
        scratch_shapes=[pltpu.VMEM((1, LANES), F32)],
        compiler_params=_cparams(("arbitrary",)),
        name="fox_prep",
    )(proj, fb_pad)


_SKIP_LOGIT_GAP = 110.0
_SKIP_NORM_SLACK = 1.05


def _fox_bounds_kernel(q_ref, k_ref, ct_ref, sel_ref, jmin_ref, qmax_ref, kmax_ref, cs_ref, ce_ref, *, tq, nq):
    i = pl.program_id(0)
    lane = lax.broadcasted_iota(I32, (FOX_HEADS, LANES), 1)
    sub = lax.broadcasted_iota(I32, (FOX_HEADS, LANES), 0)

    @pl.when(i == 0)
    def _():
        qmax_ref[...] = jnp.zeros_like(qmax_ref)
        kmax_ref[...] = jnp.zeros_like(kmax_ref)
        cs_ref[...] = jnp.zeros_like(cs_ref)
        ce_ref[...] = jnp.zeros_like(ce_ref)

    def head_sq_norm_max(x_ref):
        x = x_ref[...].astype(F32)
        sq = jnp.dot((x * x).astype(BF16), sel_ref[...], preferred_element_type=F32)
        return jnp.max(sq, axis=0, keepdims=True)

    qmax_ref[...] = jnp.maximum(qmax_ref[...], head_sq_norm_max(q_ref))
    kmax_ref[...] = jnp.maximum(kmax_ref[...], head_sq_norm_max(k_ref))
    cs_ref[...] = jnp.where(lane == i, ct_ref[:, 0:1], cs_ref[...])
    ce_ref[...] = jnp.where(lane == i, ct_ref[:, tq - 1:tq], ce_ref[...])

    @pl.when(i == nq - 1)
    def _():
        def to_col(row):
            return jnp.sum(jnp.where(sub == lane, row, 0.0), axis=-1, keepdims=True)

        b2 = (2.0 * _SKIP_NORM_SLACK * FOX_HEAD_DIM ** -0.5) * jnp.sqrt(to_col(qmax_ref[...]) * to_col(kmax_ref[...]))
        ce = ce_ref[...]
        out = jnp.zeros((FOX_HEADS, LANES), F32)
        for qi in range(nq):
            gap = b2 + cs_ref[:, qi:qi + 1] - ce
            skip = jnp.where((gap < -_SKIP_LOGIT_GAP) & (lane < qi), 1.0, 0.0)
            for p in range(FOX_HEADS // 2):
                both = skip[2 * p:2 * p + 1, :] * skip[2 * p + 1:2 * p + 2, :]
                count = jnp.sum(both, axis=-1, keepdims=True)
                out = jnp.where((sub == p) & (lane == qi), count, out)
        jmin_ref[...] = out.astype(I32)


def _fox_bounds(qkv_bf16, ct, *, tq):
    n = qkv_bf16.shape[0]
    nq = n // tq
    assert nq <= LANES
    sel = jnp.asarray((np.arange(FOX_WIDTH)[:, None] // FOX_HEAD_DIM == np.arange(LANES)[None, :]).astype(np.float32), BF16)
    table = pl.pallas_call(
        functools.partial(_fox_bounds_kernel, tq=tq, nq=nq),
        grid=(nq,),
        in_specs=[pl.BlockSpec((tq, FOX_WIDTH), lambda i: (i, 0)),
                  pl.BlockSpec((tq, FOX_WIDTH), lambda i: (i, 1)),
                  pl.BlockSpec((FOX_HEADS, tq), lambda i: (0, i)),
                  pl.BlockSpec(sel.shape, lambda i: (0, 0))],
        out_specs=pl.BlockSpec((FOX_HEADS, LANES), lambda i: (0, 0)),
        out_shape=jax.ShapeDtypeStruct((FOX_HEADS, LANES), I32),
        scratch_shapes=[pltpu.VMEM((1, LANES), F32), pltpu.VMEM((1, LANES), F32),
                        pltpu.VMEM((FOX_HEADS, LANES), F32), pltpu.VMEM((FOX_HEADS, LANES), F32)],
        compiler_params=_cparams(("arbitrary",)),
        name="fox_bounds",
    )(qkv_bf16, qkv_bf16, ct, sel)
    return table[:FOX_HEADS // 2, :nq]


def _fox_attn_kernel(jmin_ref, q_ref, k_ref, v_ref, ct_ref, o_ref, qs_ref, m_ref, l_ref, acc_ref, *, tq):
    i = pl.program_id(1)
    lane = lax.broadcasted_iota(I32, (1, LANES), 1)
    left = lane < FOX_HEAD_DIM
    q = q_ref[...] * jnp.asarray(FOX_HEAD_DIM ** -0.5, BF16)
    zero = jnp.zeros_like(q)
    qs_ref[0] = jnp.where(left, q, zero)
    qs_ref[1] = jnp.where(left, zero, q)
    q0 = pl.multiple_of(i * tq, tq)
    cref = [ct_ref[h:h + 1, pl.ds(q0, LANES)][:, 0:1] for h in range(2)]

    m_ref[...] = jnp.full_like(m_ref, -jnp.inf)
    l_ref[...] = jnp.zeros_like(l_ref)
    acc_ref[...] = jnp.zeros_like(acc_ref)
    ncol = tq // LANES

    def step(j, masked):
        k0 = pl.multiple_of(j * tq, tq)
        kb = k_ref[pl.ds(k0, tq), :]
        vb = v_ref[pl.ds(k0, tq), :]
        bias = [cref[h] - ct_ref[h:h + 1, pl.ds(k0, tq)] for h in range(2)]
        for h in range(2):
            s = lax.dot_general(qs_ref[h], kb, (((1,), (1,)), ((), ())), preferred_element_type=F32)
            s = s + bias[h]
            if masked:
                r = lax.broadcasted_iota(I32, (tq, tq), 0)
                c = lax.broadcasted_iota(I32, (tq, tq), 1)
                s = jnp.where(c <= r, s, -jnp.inf)
            cols = [s[:, c * LANES:(c + 1) * LANES] for c in range(ncol)]
            lane_max = functools.reduce(jnp.maximum, cols)
            m_old = m_ref[h]
            m_new = jnp.maximum(m_old, jnp.max(lane_max, axis=-1, keepdims=True))
            alpha = jnp.exp(m_old - m_new)
            ps = [jnp.exp(c - m_new) for c in cols]
            l_ref[h] = alpha * l_ref[h] + functools.reduce(jnp.add, ps)
            m_ref[h] = m_new
            p = jnp.concatenate([x.astype(BF16) for x in ps], axis=1)
            acc_ref[h] = alpha * acc_ref[h] + jnp.dot(p, vb, preferred_element_type=F32)

    def body(j, carry):
        step(j, False)
        return carry

    lax.fori_loop(jmin_ref[pl.program_id(0), i], i, body, 0)
    step(i, True)
    l0 = jnp.sum(l_ref[0], axis=-1, keepdims=True)
    l1 = jnp.sum(l_ref[1], axis=-1, keepdims=True)
    o_ref[...] = jnp.where(left, acc_ref[0] / l0, acc_ref[1] / l1).astype(o_ref.dtype)


def _fox_attn(qkv_bf16, ct, *, tq):
    n = qkv_bf16.shape[0]
    nb = FOX_WIDTH // LANES
    jmin = _fox_bounds(qkv_bf16, ct, tq=tq)
    return pl.pallas_call(
        functools.partial(_fox_attn_kernel, tq=tq),
        grid_spec=pltpu.PrefetchScalarGridSpec(
            num_scalar_prefetch=1,
            grid=(nb, n // tq),
            in_specs=[
                pl.BlockSpec((tq, LANES), lambda p, i, jm: (i, p)),
                pl.BlockSpec((n, LANES), lambda p, i, jm: (0, nb + p)),
                pl.BlockSpec((n, LANES), lambda p, i, jm: (0, 2 * nb + p)),
                pl.BlockSpec((None, 2, n), lambda p, i, jm: (p, 0, 0)),
            ],
            out_specs=pl.BlockSpec((tq, LANES), lambda p, i, jm: (i, p)),
            scratch_shapes=[pltpu.VMEM((2, tq, LANES), BF16), pltpu.VMEM((2, tq, LANES), F32),
                            pltpu.VMEM((2, tq, LANES), F32), pltpu.VMEM((2, tq, LANES), F32)],
        ),
        out_shape=jax.ShapeDtypeStruct((n, FOX_WIDTH), BF16),
        compiler_params=_cparams(("arbitrary", "arbitrary")),
        name="fox_attn",
    )(jmin, qkv_bf16, qkv_bf16, qkv_bf16, ct.reshape(nb, 2, n))


_RW = RWKV_WIDTH
_RWKV_PERM = np.concatenate([
    np.arange(0, _RW),
    np.arange(_RW + DECAY_LORA, 2 * _RW + DECAY_LORA),
    np.arange(2 * _RW + DECAY_LORA, 3 * _RW + DECAY_LORA),
    np.arange(3 * _RW + DECAY_LORA + ICLR_LORA, RWKV_PROJ),
    np.arange(_RW, _RW + DECAY_LORA),
    np.arange(3 * _RW + DECAY_LORA, 3 * _RW + DECAY_LORA + ICLR_LORA),
])
_RWKV_INV_PERM = np.argsort(_RWKV_PERM)


def _head_block_ones(width, head_dim):
    idx = np.arange(width) // head_dim
    return jnp.asarray((idx[:, None] == idx[None, :]).astype(np.float32), BF16)


def _rwkv_prep_math(p, p_prev, mu, w0, w2p, a0, a2p, g2, k_k, k_a, bones):
    xs = p + mu * (p_prev - p)
    r = xs[:, 0:_RW]
    k = xs[:, _RW:2 * _RW]
    v = xs[:, 2 * _RW:3 * _RW]
    gl = xs[:, 3 * _RW:3 * _RW + GATE_LORA]
    wa = xs[:, 3 * _RW + GATE_LORA:]
    w = -_softplus(-(w0 + jnp.dot(jnp.tanh(wa).astype(BF16), w2p, preferred_element_type=F32))) - 0.5
    decay = jnp.exp(-jnp.exp(w))
    a = _sigmoid(a0 + jnp.dot(wa.astype(BF16), a2p, preferred_element_type=F32))
    g = jnp.dot(_sigmoid(gl).astype(BF16), g2, preferred_element_type=F32)
    kkr = k * k_k
    ss = _dot_f32_right(kkr * kkr, bones)
    kk = kkr / jnp.maximum(jnp.sqrt(ss), 1e-12)
    k2 = k * (1.0 + (a - 1.0) * k_a)
    return r, k2, v, kk, kk * a, decay, g


def _rwkv_prep_kernel(p_ref, prev_ref, mu_ref, w0_ref, w2_ref, a0_ref, a2_ref, g2_ref, kk_ref, ka_ref, bones_ref,
                      r_o, k_o, v_o, kk_o, b_o, d_o, g_o, last_o, buf_ref, *, t, shifted):
    p = p_ref[...]
    if shifted:
        @pl.when(pl.program_id(0) == 0)
        def _():
            buf_ref[SUBLANES - 1:SUBLANES, :] = prev_ref[...]

        buf_ref[SUBLANES:SUBLANES + t, :] = p
        p_prev = buf_ref[SUBLANES - 1:SUBLANES - 1 + t, :]
        buf_ref[SUBLANES - 1:SUBLANES, :] = p[t - 1:t, :]
        last_o[...] = p[t - 1:t, :]
    else:
        p_prev = prev_ref[...]
        last_o[...] = p
    outs = _rwkv_prep_math(p, p_prev, mu_ref[...], w0_ref[...], w2_ref[...], a0_ref[...], a2_ref[...], g2_ref[...],
                           kk_ref[...], ka_ref[...], bones_ref[...])
    for o_ref, val in zip((r_o, k_o, v_o, kk_o, b_o, d_o, g_o), outs):
        o_ref[...] = val


def _rwkv_prep(proj, prev, mu, w0, w2p, a0, a2p, g2, k_k, k_a, bones, *, t, shifted):
    n = proj.shape[0]
    row = lambda w: pl.BlockSpec((1, w), lambda i: (0, 0))
    full = lambda a: pl.BlockSpec(a.shape, lambda i: (0, 0))
    tok = lambda w: pl.BlockSpec((t, w), lambda i: (i, 0))
    prev_spec = row(RWKV_PROJ) if shifted else tok(RWKV_PROJ)
    last_spec = row(RWKV_PROJ) if shifted else tok(RWKV_PROJ)
    last_shape = (1, RWKV_PROJ) if shifted else (n, RWKV_PROJ)
    return pl.pallas_call(
        functools.partial(_rwkv_prep_kernel, t=t, shifted=shifted),
        grid=(n // t,),
        in_specs=[tok(RWKV_PROJ), prev_spec, row(RWKV_PROJ), row(_RW), full(w2p), row(_RW), full(a2p), full(g2),
                  row(_RW), row(_RW), full(bones)],
        out_specs=[tok(_RW)] * 7 + [last_spec],
        out_shape=[jax.ShapeDtypeStruct((n, _RW), F32)] * 7 + [jax.ShapeDtypeStruct(last_shape, F32)],
        scratch_shapes=[pltpu.VMEM((SUBLANES + t, RWKV_PROJ), F32)],
        compiler_params=_cparams(("arbitrary",)),
        name="rwkv_prep",
    )(proj, prev, mu, w0, w2p, a0, a2p, g2, k_k, k_a, bones)


def _rwkv_scan_kernel(r_ref, k_ref, v_ref, kk_ref, b_ref, d_ref, s0_ref, o_ref, st_ref, s_ref, *, t):
    j = pl.program_id(1)

    @pl.when(j == 0)
    def _():
        s_ref[...] = s0_ref[...]

    shape = (RWKV_HEAD_DIM, LANES)
    lane = lax.broadcasted_iota(I32, shape, 1)
    sub = lax.broadcasted_iota(I32, shape, 0)
    left = lane < RWKV_HEAD_DIM
    eye2 = (lane & (RWKV_HEAD_DIM - 1)) == sub

    def seg(x):
        s0 = jnp.sum(jnp.where(left, x, 0.0), axis=-1, keepdims=True)
        s1 = jnp.sum(jnp.where(left, 0.0, x), axis=-1, keepdims=True)
        return jnp.where(left, s0, s1)

    group = 2 * SUBLANES if t % (2 * SUBLANES) == 0 else t

    def readout(s, r_rows):
        s_bd = jnp.concatenate([jnp.where(left, s, 0.0), jnp.where(left, 0.0, s)], axis=0).astype(BF16)
        return lax.dot_general(r_rows, s_bd, (((1,), (1,)), ((), ())), preferred_element_type=F32)

    piece_pairs = ((0, 0), (0, 1), (0, 2), (1, 0), (1, 1), (2, 0))
    n_c = len(piece_pairs) * group
    c_lane = lax.broadcasted_iota(I32, (LANES, LANES), 1)
    c_step = c_lane % group
    c_live = c_lane < n_c
    pad_rows = jnp.zeros((LANES - n_c, LANES), F32)

    def value_columns(v_t):
        v_p = [x.astype(F32) for x in _split3(v_t)]
        v_rows = jnp.concatenate([v_p[a] for a, _ in piece_pairs] + [pad_rows], axis=0)
        return v_rows.T

    def key_rows(k_t):
        k_p = [x.astype(F32) for x in _split3(k_t)]
        return jnp.concatenate([k_p[b] for _, b in piece_pairs] + [pad_rows], axis=0).astype(BF16)

    def outer_product(v_cols, k_rows, u):
        lhs = jnp.where(c_live & (c_step == u), v_cols, 0.0).astype(BF16)
        vk = jnp.dot(lhs, k_rows, preferred_element_type=F32)
        return jnp.where(left, vk[:RWKV_HEAD_DIM], vk[RWKV_HEAD_DIM:])

    n_groups = t // group

    def group_value_columns(gi):
        base = gi * group if isinstance(gi, int) else pl.multiple_of(gi * group, group)
        return tuple(value_columns(v_ref[pl.ds(base, group), p * LANES:(p + 1) * LANES]) for p in range(HEAD_PAIRS))

    def body(gi, carry):
        states, v_cols = carry
        base = pl.multiple_of(gi * group, group)
        states = list(states)
        tiles = []
        for p in range(HEAD_PAIRS):
            sl = slice(p * LANES, (p + 1) * LANES)
            tiles.append(tuple(ref[pl.ds(base, group), sl] for ref in (kk_ref, v_ref, d_ref, b_ref, k_ref, r_ref)))
        k_rows = [key_rows(tiles[p][4]) for p in range(HEAD_PAIRS)]
        vk_tiles = [[None] * group for _ in range(HEAD_PAIRS)]
        for u in range(group):
            for p in range(HEAD_PAIRS):
                vk_tiles[p][u] = outer_product(v_cols[p], k_rows[p], u)
        v_cols_next = group_value_columns(jnp.minimum(gi + 1, n_groups - 1))
        o_rows = [[] for _ in range(HEAD_PAIRS)]
        for u in range(group):
            row = lambda x: x[u:u + 1, :]
            for p in range(HEAD_PAIRS):
                kk_t, v_t, d_t, b_t, k_t, r_t = tiles[p]
                s = states[p]
                sk = seg(s * row(kk_t))
                s = s * row(d_t) - sk * row(b_t) + vk_tiles[p][u]
                states[p] = s
                r_rows = jnp.broadcast_to(r_t, (SUBLANES, LANES)) if group == 1 else r_t
                o_rows[p].append(readout(s, r_rows.astype(BF16))[u:u + 1, :])
        for p in range(HEAD_PAIRS):
            sl = slice(p * LANES, (p + 1) * LANES)
            o_ref[pl.ds(base, group), sl] = jnp.concatenate(o_rows[p], axis=0) if group > 1 else o_rows[p][0]
        return tuple(states), v_cols_next

    states, _ = lax.fori_loop(0, n_groups, body,
                              (tuple(s_ref[p] for p in range(HEAD_PAIRS)), group_value_columns(0)))
    for p in range(HEAD_PAIRS):
        s_ref[p] = states[p]

    @pl.when(j == pl.num_programs(1) - 1)
    def _():
        st_ref[...] = s_ref[...]


def _rwkv_scan(r, k, v, kk, b, d, s0_pairs, *, batch, t):
    n = r.shape[0]
    seq = n // batch
    tok = pl.BlockSpec((None, t, _RW), lambda bi, j: (bi, j, 0))
    st = pl.BlockSpec((None, HEAD_PAIRS, RWKV_HEAD_DIM, LANES), lambda bi, j: (bi, 0, 0, 0))
    o, s_t = pl.pallas_call(
        functools.partial(_rwkv_scan_kernel, t=t),
        grid=(batch, seq // t),
        in_specs=[tok] * 6 + [st],
        out_specs=[tok, st],
        out_shape=[jax.ShapeDtypeStruct((batch, seq, _RW), F32),
                   jax.ShapeDtypeStruct((batch, HEAD_PAIRS, RWKV_HEAD_DIM, LANES), F32)],
        scratch_shapes=[pltpu.VMEM((HEAD_PAIRS, RWKV_HEAD_DIM, LANES), F32)],
        compiler_params=_cparams(("arbitrary", "arbitrary")),
        name="rwkv_scan",
    )(*(x.reshape(batch, seq, _RW) for x in (r, k, v, kk, b, d)), s0_pairs)
    return o.reshape(n, _RW), s_t


def _rwkv_post_kernel(o_ref, r_ref, k_ref, v_ref, g_ref, lnw_ref, lnb_ref, rk_ref, bones_ref, out_ref):
    bones = bones_ref[...]
    inv = 1.0 / RWKV_HEAD_DIM
    o = o_ref[...]
    mean = _dot_f32_right(o, bones) * inv
    cen = o - mean
    var = _dot_f32_right(cen * cen, bones) * inv
    gn = cen * lax.rsqrt(var + GN_EPS) * lnw_ref[...] + lnb_ref[...]
    bonus = _dot_f32_right(r_ref[...] * k_ref[...] * rk_ref[...], bones) * v_ref[...]
    out_ref[...] = ((gn + bonus) * g_ref[...]).astype(out_ref.dtype)


def _rwkv_post(o, r, k, v, g, lnw, lnb, rk, bones, *, t):
    n = o.shape[0]
    tok = pl.BlockSpec((t, _RW), lambda i: (i, 0))
    row = pl.BlockSpec((1, _RW), lambda i: (0, 0))
    return pl.pallas_call(
        _rwkv_post_kernel,
        grid=(n // t,),
        in_specs=[tok] * 5 + [row] * 3 + [pl.BlockSpec(bones.shape, lambda i: (0, 0))],
        out_specs=tok,
        out_shape=jax.ShapeDtypeStruct((n, _RW), BF16),
        compiler_params=_cparams(("parallel",)),
        name="rwkv_post",
    )(o, r, k, v, g, lnw, lnb, rk, bones)


def _to_pairs(s):
    b = s.shape[0]
    return s.reshape(b, HEAD_PAIRS, 2, RWKV_HEAD_DIM, RWKV_HEAD_DIM).transpose(0, 1, 3, 2, 4).reshape(
        b, HEAD_PAIRS, RWKV_HEAD_DIM, LANES)


def _from_pairs(s):
    b = s.shape[0]
    return s.reshape(b, HEAD_PAIRS, RWKV_HEAD_DIM, 2, RWKV_HEAD_DIM).transpose(0, 1, 3, 2, 4).reshape(
        b, RWKV_HEADS, RWKV_HEAD_DIM, RWKV_HEAD_DIM)


def _rwkv_mix(proj, prev, s0, rw, *, batch, t_prep, t_scan, shifted):
    mu, w0, w2, a0, a2, g2, k_k, k_a, r_k, ln_w, ln_b = rw
    bones = _head_block_ones(_RW, RWKV_HEAD_DIM)
    zeros = jnp.zeros((LANES - DECAY_LORA, _RW), F32)
    w2p = jnp.concatenate([w2, zeros], axis=0).astype(BF16)
    a2p = jnp.concatenate([zeros, a2], axis=0).astype(BF16)
    row = lambda x: x.reshape(1, -1)
    r, k, v, kk, b, d, g, last = _rwkv_prep(
        proj, prev, row(mu[_RWKV_PERM]), row(w0), w2p, row(a0), a2p, g2.astype(BF16), row(k_k), row(k_a), bones,
        t=t_prep, shifted=shifted)
    o, s_t = _rwkv_scan(r, k, v, kk, b, d, _to_pairs(s0), batch=batch, t=t_scan)
    out = _rwkv_post(o, r, k, v, g, row(ln_w), row(ln_b), row(r_k), bones, t=t_prep)
    return out, last[:, _RWKV_INV_PERM], _from_pairs(s_t)


def _top_rows(s, k, order):
    vals, ords = [], []
    for _ in range(k):
        m = jnp.max(s, axis=0, keepdims=True)
        pick = jnp.min(jnp.where(s == m, order, jnp.inf), axis=0, keepdims=True)
        vals.append(m)
        ords.append(pick)
        s = jnp.where(order == pick, -jnp.inf, s)
    return jnp.concatenate(vals, axis=0), jnp.concatenate(ords, axis=0)


def _take_rows(table, pos):
    out = jnp.zeros(pos.shape, table.dtype)
    for i in range(table.shape[0]):
        out = jnp.where(pos == float(i), table[i:i + 1, :], out)
    return out


_CAND_GROUPS = ((0, 2, 16), (2, 4, 8))
_CAND_TAIL_I0, _CAND_TAIL_J = 4, 3
_CAND_HEAD_ROWS = sum((hi - lo) * nj for lo, hi, nj in _CAND_GROUPS)
_CAND_ROWS = _CAND_HEAD_ROWS + _CAND_TAIL_J * PEER_TOPK


def _candidate_tables(tm):
    k = PEER_TOPK
    assert k == 16 and _CAND_GROUPS == ((0, 2, 16), (2, 4, 8))
    r = lax.broadcasted_iota(I32, (_CAND_ROWS, tm), 0)
    r2 = r - 2 * k
    flat2 = (2 + (r2 >> 3)) * k + (r2 & 7)
    r3 = r - _CAND_HEAD_ROWS
    i3 = r3 & (k - 1)
    flat3 = i3 * k + (r3 >> 4)
    flat = jnp.where(r < 2 * k, r, jnp.where(r < _CAND_HEAD_ROWS, flat2, flat3))
    valid = (r < _CAND_HEAD_ROWS) | (i3 >= _CAND_TAIL_I0)
    return flat.astype(F32), valid


def _peer_route_kernel(q_ref, keys_ref, g_ref, e1_ref, e2_ref, gv_ref, gt_ref, *, tm):
    k = PEER_TOPK
    flat, valid = _candidate_tables(tm)
    key_order = lax.broadcasted_iota(I32, (N_KEYS, tm), 0).astype(F32)

    e1s, e2s, gs = [], [], []
    for h in range(PEER_HEADS):
        qh = q_ref[:, h * D_KEY:(h + 1) * D_KEY].astype(BF16)
        score = lambda p: lax.dot_general(keys_ref[h, p], qh, (((1,), (1,)), ((), ())), preferred_element_type=F32)
        v1, i1 = _top_rows(score(0), k, key_order)
        v2, i2 = _top_rows(score(1), k, key_order)
        parts = [v1[i:i + 1, :] + v2[:nj, :] for lo, hi, nj in _CAND_GROUPS for i in range(lo, hi)]
        parts += [v1 + v2[j:j + 1, :] for j in range(_CAND_TAIL_J)]
        cand = jnp.where(valid, jnp.concatenate(parts, axis=0), -jnp.inf)
        sc, pos = _top_rows(cand, k, flat)
        pos_i = jnp.floor(pos * (1.0 / k))
        e1s.append(_take_rows(i1, pos_i))
        e2s.append(_take_rows(i2, pos - k * pos_i))
        ex = jnp.exp(sc - sc[0:1, :])
        gs.append(ex / jnp.sum(ex, axis=0, keepdims=True))
    e1_ref[...] = jnp.concatenate(e1s, axis=0).T
    e2_ref[...] = jnp.concatenate(e2s, axis=0).T
    gv_ref[...] = jnp.concatenate(gs, axis=0).T

    eid = lax.broadcasted_iota(I32, (N_KEYS, PEER_HEADS * k), 0).astype(F32)
    rows = _ROUTE_GROUP

    def body(gi, carry):
        base = pl.multiple_of(gi * rows, rows)
        e1_t = e1_ref[pl.ds(base, rows), :]
        e2_t = e2_ref[pl.ds(base, rows), :]
        gv_t = gv_ref[pl.ds(base, rows), :]
        for u in range(rows):
            a = jnp.where(eid == e1_t[u:u + 1, :], gv_t[u:u + 1, :], 0.0).astype(BF16)
            b = jnp.where(eid == e2_t[u:u + 1, :], 1.0, 0.0).astype(BF16)
            gt_ref[u * _ROUTE_PITCH:u * _ROUTE_PITCH + N_KEYS, :] = lax.dot_general(
                a, b, (((1,), (1,)), ((), ())), preferred_element_type=F32)
        for e1 in range(N_KEYS):
            slab = gt_ref[pl.ds(e1, rows, stride=_ROUTE_PITCH), :]
            g_ref[pl.ds(base, rows), e1 * N_KEYS:(e1 + 1) * N_KEYS] = slab.astype(g_ref.dtype)
        return carry

    lax.fori_loop(0, tm // rows, body, 0)


_ROUTE_GROUP = 2 * SUBLANES
_ROUTE_PITCH = N_KEYS + SUBLANES


def _peer_route(q, keys_pad, *, tm):
    n = q.shape[0]
    width = PEER_HEADS * PEER_TOPK
    assert tm % _ROUTE_GROUP == 0
    return pl.pallas_call(
        functools.partial(_peer_route_kernel, tm=tm),
        grid=(n // tm,),
        in_specs=[pl.BlockSpec((tm, PEER_HEADS * D_KEY), lambda i: (i, 0)),
                  pl.BlockSpec(keys_pad.shape, lambda i: (0, 0, 0, 0))],
        out_specs=pl.BlockSpec((tm, N_KEYS * N_KEYS), lambda i: (i, 0)),
        out_shape=jax.ShapeDtypeStruct((n, N_KEYS * N_KEYS), BF16),
        scratch_shapes=[pltpu.VMEM((tm, width), F32)] * 3 + [pltpu.VMEM((_ROUTE_GROUP * _ROUTE_PITCH, N_KEYS), F32)],
        compiler_params=_cparams(("parallel",)),
        name="peer_route",
    )(q, keys_pad)


def _peer_mlp_kernel(x_ref, gain_ref, u_ref, v_ref, g_ref, o_ref, xn_ref, acc_ref):
    j = pl.program_id(1)

    @pl.when(j == 0)
    def _():
        x = x_ref[...]
        ms = jnp.mean(x * x, axis=-1, keepdims=True)
        xn_ref[...] = (x * lax.rsqrt(ms + RMS_EPS) * gain_ref[...]).astype(BF16)
        acc_ref[...] = jnp.zeros_like(acc_ref)

    h = lax.dot_general(xn_ref[...], u_ref[...], (((1,), (1,)), ((), ())), preferred_element_type=F32)
    act = 0.5 * h * (1.0 + lax.erf(h * (2.0 ** -0.5)))
    w = (act * g_ref[...].astype(F32)).astype(BF16)
    acc_ref[...] += jnp.dot(w, v_ref[...], preferred_element_type=F32)

    @pl.when(j == pl.num_programs(1) - 1)
    def _():
        o_ref[...] = x_ref[...] + acc_ref[...]


def _peer_mlp(x, gain, u_bf16, v_bf16, gmap, *, tm, te):
    n, dm = x.shape
    ne = u_bf16.shape[0]
    return pl.pallas_call(
        _peer_mlp_kernel,
        grid=(n // tm, ne // te),
        in_specs=[pl.BlockSpec((tm, dm), lambda i, j: (i, 0)),
                  pl.BlockSpec((1, dm), lambda i, j: (0, 0)),
                  pl.BlockSpec((te, dm), lambda i, j: (j, 0)),
                  pl.BlockSpec((te, dm), lambda i, j: (j, 0)),
                  pl.BlockSpec((tm, te), lambda i, j: (i, j))],
        out_specs=pl.BlockSpec((tm, dm), lambda i, j: (i, 0)),
        out_shape=jax.ShapeDtypeStruct((n, dm), F32),
        scratch_shapes=[pltpu.VMEM((tm, dm), BF16), pltpu.VMEM((tm, dm), F32)],
        compiler_params=_cparams(("parallel", "arbitrary")),
        name="peer_mlp",
    )(x, gain.reshape(1, dm), u_bf16, v_bf16, gmap)


def _peer_keys_padded(keys):
    z = jnp.zeros_like(keys[:, 0])
    first = jnp.concatenate([keys[:, 0], z], axis=-1)
    second = jnp.concatenate([z, keys[:, 1]], axis=-1)
    return jnp.stack([first, second], axis=1).astype(BF16)


def _peer(y, gain, wq_bf16, keys_pad, u_bf16, v_bf16, *, tm_proj, tm_route, tm_mlp, te):
    n = y.shape[0]
    q = _norm_matmul(y, gain, wq_bf16, tm=tm_proj, tn=wq_bf16.shape[1])
    gmap = _peer_route(q, keys_pad, tm=tm_route)
    return _peer_mlp(y, gain, u_bf16, v_bf16, gmap, tm=tm_mlp, te=te)


def _conv_silu(x, buf_ref, w_ref, b_ref, q):
    buf_ref[SUBLANES:SUBLANES + q, :] = x
    acc = b_ref[...] + x * w_ref[CONV_W - 1:CONV_W, :]
    for back in range(1, CONV_W):
        acc = acc + buf_ref[SUBLANES - back:SUBLANES - back + q, :] * w_ref[CONV_W - 1 - back:CONV_W - back, :]
    tail = buf_ref[q + SUBLANES - (CONV_W - 1):q + SUBLANES, :]
    buf_ref[SUBLANES - (CONV_W - 1):SUBLANES, :] = tail
    return _silu(acc), tail


def _ssd_kernel(z_ref, x_ref, bc_ref, dt_ref, cwx_ref, cwbc_ref, cbx_ref, cbbc_ref, dtb_ref, alog_ref, dskip_ref,
                nw_ref, expand_ref, y_ref, convx_ref, convbc_ref, ht_ref, bufx_ref, bufbc_ref, h_ref, *, q, heads):
    hpg = heads // N_GROUPS
    gw = hpg * SSM_HEAD_DIM

    @pl.when(pl.program_id(0) == 0)
    def _():
        bufx_ref[...] = jnp.zeros_like(bufx_ref)
        bufbc_ref[...] = jnp.zeros_like(bufbc_ref)
        h_ref[...] = jnp.zeros_like(h_ref)

    xs, tail_x = _conv_silu(x_ref[...], bufx_ref, cwx_ref, cbx_ref, q)
    bc, tail_bc = _conv_silu(bc_ref[...], bufbc_ref, cwbc_ref, cbbc_ref, q)
    convx_ref[...] = tail_x
    convbc_ref[...] = tail_bc

    dt = _softplus(dt_ref[...] + dtb_ref[...])
    a = -jnp.exp(alog_ref[...])
    row = lax.broadcasted_iota(I32, (q, q), 0)
    col = lax.broadcasted_iota(I32, (q, q), 1)
    lower = col <= row
    tri = jnp.where(lower, 1.0, 0.0).astype(BF16)
    acs = _dot_f32_left(tri, dt * a)
    acs_t = acs.T
    expand = expand_ref[...]
    dt_e = _dot_f32_right(dt, expand)
    acs_e = _dot_f32_right(acs, expand)
    last_e = acs_e[q - 1:q, :]
    xc = xs * dt_e
    xcd = (xc * jnp.exp(last_e - acs_e)).astype(BF16)
    xc_b = xc.astype(BF16)
    grow = jnp.exp(acs_e)
    chunk_decay = jnp.exp(last_e)

    nbc = N_GROUPS * D_STATE
    for g in range(N_GROUPS):
        bm = bc[:, g * D_STATE:(g + 1) * D_STATE].astype(BF16)
        cm = bc[:, nbc + g * D_STATE:nbc + (g + 1) * D_STATE].astype(BF16)
        cb = lax.dot_general(cm, bm, (((1,), (1,)), ((), ())), preferred_element_type=F32)
        sl = slice(g * gw, (g + 1) * gw)
        h_prev = h_ref[g]
        y_off = jnp.dot(cm, h_prev.astype(BF16), preferred_element_type=F32) * grow[:, sl]
        y_diag = []
        for r in range(hpg):
            hd = g * hpg + r
            diff = acs[:, hd:hd + 1] - acs_t[hd:hd + 1, :]
            m = (cb * jnp.exp(jnp.where(lower, diff, -jnp.inf))).astype(BF16)
            y_diag.append(jnp.dot(m, xc_b[:, hd * SSM_HEAD_DIM:(hd + 1) * SSM_HEAD_DIM], preferred_element_type=F32))
        y_g = jnp.concatenate(y_diag, axis=1) + y_off + dskip_ref[:, sl] * xs[:, sl]
        states = jnp.dot(bm.T, xcd[:, sl], preferred_element_type=F32)
        h_ref[g] = h_prev * chunk_decay[:, sl] + states
        zg = z_ref[:, sl]
        y_g = y_g * _silu(zg)
        ms = jnp.mean(y_g * y_g, axis=-1, keepdims=True)
        y_ref[:, sl] = (y_g * lax.rsqrt(ms + RMS_EPS) * nw_ref[:, sl]).astype(y_ref.dtype)

    @pl.when(pl.program_id(0) == pl.num_programs(0) - 1)
    def _():
        ht_ref[...] = h_ref[...]


def _ssd(proj, conv_w, conv_b, dt_bias, a_log, d_skip, norm_w, *, heads):
    n = proj.shape[0]
    d_inner = heads * SSM_HEAD_DIM
    q = CHUNK
    nbc = 2 * N_GROUPS * D_STATE
    assert nbc == d_inner and n % q == 0
    gw = d_inner // N_GROUPS
    pad = lambda x: jnp.pad(x, (0, LANES - heads)).reshape(1, LANES)
    expand = jnp.asarray((np.arange(LANES)[:, None] == (np.arange(d_inner) // SSM_HEAD_DIM)[None, :]).astype(np.float32), BF16)
    blk = lambda w, j: pl.BlockSpec((q, w), lambda i, j=j: (i, j))
    full = lambda a: pl.BlockSpec(a.shape, lambda i: (0,) * a.ndim)
    small = [conv_w[:, :d_inner], conv_w[:, d_inner:], conv_b[:d_inner].reshape(1, -1), conv_b[d_inner:].reshape(1, -1),
             pad(dt_bias), pad(a_log), jnp.repeat(d_skip, SSM_HEAD_DIM).reshape(1, -1), norm_w.reshape(1, -1), expand]
    y, cx, cbc, ht = pl.pallas_call(
        functools.partial(_ssd_kernel, q=q, heads=heads),
        grid=(n // q,),
        in_specs=[blk(d_inner, 0), blk(d_inner, 1), blk(d_inner, 2), blk(LANES, 3 * d_inner // LANES)]
                 + [full(a) for a in small],
        out_specs=[pl.BlockSpec((q, d_inner), lambda i: (i, 0)),
                   pl.BlockSpec((CONV_W - 1, d_inner), lambda i: (0, 0)),
                   pl.BlockSpec((CONV_W - 1, nbc), lambda i: (0, 0)),
                   pl.BlockSpec((N_GROUPS, D_STATE, gw), lambda i: (0, 0, 0))],
        out_shape=[jax.ShapeDtypeStruct((n, d_inner), BF16),
                   jax.ShapeDtypeStruct((CONV_W - 1, d_inner), F32),
                   jax.ShapeDtypeStruct((CONV_W - 1, nbc), F32),
                   jax.ShapeDtypeStruct((N_GROUPS, D_STATE, gw), F32)],
        scratch_shapes=[pltpu.VMEM((SUBLANES + q, d_inner), F32), pltpu.VMEM((SUBLANES + q, nbc), F32),
                        pltpu.VMEM((N_GROUPS, D_STATE, gw), F32)],
        compiler_params=_cparams(("arbitrary",)),
        name="ssd",
    )(proj, proj, proj, proj, *small)
    hpg = heads // N_GROUPS
    h_t = ht.reshape(N_GROUPS, D_STATE, hpg, SSM_HEAD_DIM).transpose(0, 2, 3, 1).reshape(heads, SSM_HEAD_DIM, D_STATE)
    return y, jnp.concatenate([cx, cbc], axis=1), h_t


def _ssm_step_kernel(z_ref, x_ref, bc_ref, dt_ref, bufx_ref, bufbc_ref, h0_ref, cwx_ref, cwbc_ref, cbx_ref, cbbc_ref,
                     dtb_ref, alog_ref, dskip_ref, nw_ref, y_ref, nbufx_ref, nbufbc_ref, h_ref, *, heads):
    hpg = heads // N_GROUPS
    gw = hpg * SSM_HEAD_DIM
    nbc = N_GROUPS * D_STATE
    fullx = jnp.concatenate([bufx_ref[...], x_ref[...]], axis=1)
    nbufx_ref[...] = fullx[:, 1:]
    xs = _silu(cbx_ref[...] + jnp.sum(fullx * cwx_ref[...], axis=1, keepdims=True))
    fullbc = jnp.concatenate([bufbc_ref[...], bc_ref[...]], axis=0)
    nbufbc_ref[...] = fullbc[1:, :]
    bc = _silu(cbbc_ref[...] + jnp.sum(fullbc * cwbc_ref[...], axis=0, keepdims=True))
    dt = _softplus(dt_ref[...] + dtb_ref[...])
    dec = jnp.exp(dt * (-jnp.exp(alog_ref[...])))
    xdt = xs * dt
    for g in range(N_GROUPS):
        rows = slice(g * gw, (g + 1) * gw)
        bm = bc[:, g * D_STATE:(g + 1) * D_STATE]
        cm = bc[:, nbc + g * D_STATE:nbc + (g + 1) * D_STATE]
        h_new = h0_ref[rows, :] * dec[rows, :] + xdt[rows, :] * bm
        h_ref[rows, :] = h_new
        y = jnp.sum(h_new * cm, axis=-1, keepdims=True) + dskip_ref[rows, :] * xs[rows, :]
        y = y * _silu(z_ref[rows, :])
        ms = jnp.mean(y * y, axis=0, keepdims=True)
        y_ref[rows, :] = y * lax.rsqrt(ms + RMS_EPS) * nw_ref[rows, :]


def _ssm_step(proj, conv_buf, h0, conv_w, conv_b, dt_bias, a_log, d_skip, norm_w, *, heads):
    b = proj.shape[0]
    d_inner = heads * SSM_HEAD_DIM
    nbc2 = 2 * N_GROUPS * D_STATE
    col = lambda x: x.reshape(b, d_inner, 1)
    per_head = lambda v: jnp.repeat(v, SSM_HEAD_DIM, axis=-1)
    z = col(proj[:, :d_inner])
    x = col(proj[:, d_inner:2 * d_inner])
    bc = proj[:, 2 * d_inner:2 * d_inner + nbc2].reshape(b, 1, nbc2)
    dt = col(per_head(proj[:, 2 * d_inner + nbc2:2 * d_inner + nbc2 + heads]))
    bufx = conv_buf[:, :, :d_inner].transpose(0, 2, 1)
    bufbc = conv_buf[:, :, d_inner:]
    pcol = lambda v: v.reshape(d_inner, 1)
    params = [conv_w[:, :d_inner].T, conv_w[:, d_inner:], pcol(conv_b[:d_inner]), conv_b[d_inner:].reshape(1, nbc2),
              pcol(per_head(dt_bias)), pcol(per_head(a_log)), pcol(per_head(d_skip)), pcol(norm_w)]
    seq = lambda *shape: pl.BlockSpec((None,) + shape, lambda i: (i,) + (0,) * len(shape))
    full = lambda a: pl.BlockSpec(a.shape, lambda i: (0,) * a.ndim)
    y, nbx, nbbc, h = pl.pallas_call(
        functools.partial(_ssm_step_kernel, heads=heads),
        grid=(b,),
        in_specs=[seq(d_inner, 1), seq(d_inner, 1), seq(1, nbc2), seq(d_inner, 1), seq(d_inner, CONV_W - 1),
                  seq(CONV_W - 1, nbc2), seq(d_inner, D_STATE)] + [full(a) for a in params],
        out_specs=[seq(d_inner, 1), seq(d_inner, CONV_W - 1), seq(CONV_W - 1, nbc2), seq(d_inner, D_STATE)],
        out_shape=[jax.ShapeDtypeStruct((b, d_inner, 1), F32), jax.ShapeDtypeStruct((b, d_inner, CONV_W - 1), F32),
                   jax.ShapeDtypeStruct((b, CONV_W - 1, nbc2), F32), jax.ShapeDtypeStruct((b, d_inner, D_STATE), F32)],
        compiler_params=_cparams(("parallel",)),
        name="ssm_step",
    )(z, x, bc, dt, bufx, bufbc, h0.reshape(b, d_inner, D_STATE), *params)
    new_buf = jnp.concatenate([nbx.transpose(0, 2, 1), nbbc], axis=2)
    return y.reshape(b, d_inner), new_buf, h.reshape(b, heads, SSM_HEAD_DIM, D_STATE)


def _fox_decode_kernel(pt_ref, q_ref, kc_ref, vc_ref, lfc_ref, *refs, pages):
    k_refs = refs[:pages]
    v_refs = refs[pages:2 * pages]
    lf_refs = refs[2 * pages:3 * pages]
    o_ref, qb_ref, m_ref, l_ref, acc_ref, later_ref = refs[3 * pages:]
    j = pl.program_id(1)
    first_lane = lax.broadcasted_iota(I32, (1, LANES), 1) == 0

    @pl.when(j == 0)
    def _():
        s_rows = []
        for h in range(FOX_HEADS):
            qh = q_ref[h] * (FOX_HEAD_DIM ** -0.5)
            qb_ref[h] = jnp.broadcast_to(qh, (FOX_HEAD_DIM, LANES))
            s_rows.append(jnp.sum(qh * kc_ref[h], axis=0, keepdims=True))
            acc_ref[h] = jnp.where(first_lane, vc_ref[h], 0.0)
        m_ref[...] = jnp.broadcast_to(jnp.concatenate(s_rows, axis=0), m_ref.shape)
        l_ref[...] = jnp.broadcast_to(jnp.where(first_lane, 1.0, 0.0), l_ref.shape)
        later_ref[...] = jnp.broadcast_to(lfc_ref[...], later_ref.shape)

    urow = lax.broadcasted_iota(I32, (LANES, LANES), 0)
    tcol = lax.broadcasted_iota(I32, (LANES, LANES), 1)
    after = jnp.where(urow > tcol, 1.0, 0.0).astype(BF16)
    for u in range(pages):
        lf = lf_refs[u][...]
        later = later_ref[...]
        bias = _dot_f32_right(lf, after) + later
        rows = [jnp.sum(k_refs[u][h] * qb_ref[h], axis=0, keepdims=True) for h in range(FOX_HEADS)]
        s = jnp.concatenate(rows, axis=0) + bias
        m_old = m_ref[...]
        m_new = jnp.maximum(m_old, jnp.max(s, axis=-1, keepdims=True))
        alpha = jnp.exp(m_old - m_new)
        p = jnp.exp(s - m_new)
        l_ref[...] = alpha * l_ref[...] + p
        m_ref[...] = m_new
        for h in range(FOX_HEADS):
            acc_ref[h] = alpha[h:h + 1, :] * acc_ref[h] + p[h:h + 1, :] * v_refs[u][h]
        later_ref[...] = later + jnp.sum(lf, axis=-1, keepdims=True)

    @pl.when(j == pl.num_programs(1) - 1)
    def _():
        l = jnp.sum(l_ref[...], axis=-1, keepdims=True)
        for h in range(FOX_HEADS):
            o_ref[h] = jnp.sum(acc_ref[h], axis=-1, keepdims=True) / l[h:h + 1, :]


def _fox_decode(q, k_cur, v_cur, logf_cur, cache_k, cache_v, cache_logf, page_table, *, pages):
    b, n_pages = page_table.shape
    page = cache_k.shape[1]
    assert page == LANES and n_pages % pages == 0
    kt = cache_k.transpose(0, 2, 3, 1)
    vt = cache_v.transpose(0, 2, 3, 1)
    lft = cache_logf.transpose(0, 2, 1)
    col = lambda x: x.reshape(b, FOX_HEADS, FOX_HEAD_DIM, 1)
    tok = pl.BlockSpec((None, FOX_HEADS, FOX_HEAD_DIM, 1), lambda bi, j, pt: (bi, 0, 0, 0))
    newest_first = lambda bi, j, pt, u: pt[bi, n_pages - 1 - (j * pages + u)]
    kv_page = lambda u: pl.BlockSpec((None, FOX_HEADS, FOX_HEAD_DIM, page),
                                     lambda bi, j, pt, u=u: (newest_first(bi, j, pt, u), 0, 0, 0))
    lf_page = lambda u: pl.BlockSpec((None, FOX_HEADS, page), lambda bi, j, pt, u=u: (newest_first(bi, j, pt, u), 0, 0))
    out = pl.pallas_call(
        functools.partial(_fox_decode_kernel, pages=pages),
        grid_spec=pltpu.PrefetchScalarGridSpec(
            num_scalar_prefetch=1,
            grid=(b, n_pages // pages),
            in_specs=[tok, tok, tok, pl.BlockSpec((None, FOX_HEADS, 1), lambda bi, j, pt: (bi, 0, 0))]
                     + [kv_page(u) for u in range(pages)] * 2 + [lf_page(u) for u in range(pages)],
            out_specs=tok,
            scratch_shapes=[pltpu.VMEM((FOX_HEADS, FOX_HEAD_DIM, LANES), F32), pltpu.VMEM((FOX_HEADS, LANES), F32),
                            pltpu.VMEM((FOX_HEADS, LANES), F32), pltpu.VMEM((FOX_HEADS, FOX_HEAD_DIM, LANES), F32),
                            pltpu.VMEM((FOX_HEADS, LANES), F32)],
        ),
        out_shape=jax.ShapeDtypeStruct((b, FOX_HEADS, FOX_HEAD_DIM, 1), F32),
        compiler_params=_cparams(("parallel", "arbitrary")),
        name="fox_decode",
    )(page_table, col(q), col(k_cur), col(v_cur), logf_cur.reshape(b, FOX_HEADS, 1),
      *([kt] * pages), *([vt] * pages), *([lft] * pages))
    return out.reshape(b, FOX_WIDTH)


_COL_Q = RWKV_PROJ
_COL_K = _COL_Q + FOX_WIDTH
_COL_V = _COL_K + FOX_WIDTH
_COL_FL = _COL_V + FOX_WIDTH
_EVEN_COLS = _COL_FL + LANES


def _even_weights(w_in):
    q, k, v = (w_in[:, i * FOX_WIDTH:(i + 1) * FOX_WIDTH] for i in range(3))
    fl = w_in[:, 3 * FOX_WIDTH:3 * FOX_WIDTH + FOX_HEADS]
    rw = w_in[:, 3 * FOX_WIDTH + FOX_HEADS:][:, _RWKV_PERM]
    fl_pad = jnp.pad(fl, ((0, 0), (0, LANES - FOX_HEADS)))
    return jnp.concatenate([rw, q, k, v, fl_pad], axis=1).astype(BF16)


def _even_layer(y, gain, w_cat, w_out, fb, rw, *, batch, seq, past, shift0, wkv0):
    n = y.shape[0]
    tm = min(512, n)
    proj = _norm_matmul(y, gain, w_cat, tm=min(1024, n), tn=_EVEN_COLS)
    fb_pad = jnp.pad(fb, (0, LANES - FOX_HEADS)).reshape(1, LANES)
    logf, ct = _fox_prep(proj, _COL_FL // LANES, fb_pad, t=min(256, n))
    k_new = proj[:, _COL_K:_COL_V]
    v_new = proj[:, _COL_V:_COL_FL]
    if past is None:
        qkv = proj[:, _COL_Q:_COL_FL].astype(BF16)
        o_fox = _fox_attn(qkv, ct, tq=1024)
        o_rwkv, shift_new, wkv_new = _rwkv_mix(proj, shift0[:, _RWKV_PERM], wkv0, rw, batch=batch,
                                               t_prep=256, t_scan=256, shifted=True)
    else:
        cache_k, cache_v, cache_logf, page_table = past
        o_fox = _fox_decode(proj[:, _COL_Q:_COL_K], k_new, v_new, logf, cache_k, cache_v, cache_logf, page_table,
                            pages=8)
        o_rwkv, shift_new, wkv_new = _rwkv_mix(proj, shift0[:, _RWKV_PERM], wkv0, rw, batch=batch,
                                               t_prep=n, t_scan=1, shifted=False)
    w_out_b = w_out.astype(BF16)
    y = _matmul_res([o_fox, o_rwkv], [w_out_b[:FOX_WIDTH], w_out_b[FOX_WIDTH:]], y, tm=tm, tn=y.shape[1])
    heads = lambda t: t.reshape(batch, seq, FOX_HEADS, FOX_HEAD_DIM)
    return y, heads(k_new), heads(v_new), logf.reshape(batch, seq, FOX_HEADS), wkv_new, shift_new


def kernel(x_prompt, x_sample, cache_k, cache_v, cache_logf, page_table, state_wkv, state_shift, state_conv, state_ssm, norm_mix, norm_ffn, norm_final, w_in_even, w_out_even, fox_fb, rwkv_mu, rwkv_w0, rwkv_w2, rwkv_a0, rwkv_a2, rwkv_g2, rwkv_kk, rwkv_ka, rwkv_rk, rwkv_lnw, rwkv_lnb, w_in_odd, w_out_odd, ssm_conv_w, ssm_conv_b, ssm_dt_bias, ssm_a_log, ssm_d, ssm_norm_w, peer_wq, peer_keys, peer_u, peer_v):
    bp, seq_p, dm = x_prompt.shape
    bs, seq_s, _ = x_sample.shape
    assert bp == 1 and seq_s == 1
    depth = norm_mix.shape[0]
    ssm_heads = ssm_dt_bias.shape[1]
    yp = x_prompt.reshape(bp * seq_p, dm)
    ys = x_sample.reshape(bs * seq_s, dm)
    outs_p = {name: [] for name in ("k", "v", "logf", "wkv", "shift", "conv", "ssm")}
    outs_s = {name: [] for name in outs_p}
    for layer in range(depth):
        if layer % 2 == 0:
            e = layer // 2
            rw = (rwkv_mu[e], rwkv_w0[e], rwkv_w2[e], rwkv_a0[e], rwkv_a2[e], rwkv_g2[e],
                  rwkv_kk[e], rwkv_ka[e], rwkv_rk[e], rwkv_lnw[e], rwkv_lnb[e])
            w_cat = _even_weights(w_in_even[e])
            yp, *res_p = _even_layer(
                yp, norm_mix[layer], w_cat, w_out_even[e], fox_fb[e], rw, batch=bp, seq=seq_p, past=None,
                shift0=jnp.zeros((bp, RWKV_PROJ), F32),
                wkv0=jnp.zeros((bp, RWKV_HEADS, RWKV_HEAD_DIM, RWKV_HEAD_DIM), F32))
            ys, *res_s = _even_layer(
                ys, norm_mix[layer], w_cat, w_out_even[e], fox_fb[e], rw, batch=bs, seq=seq_s,
                past=(cache_k[e], cache_v[e], cache_logf[e], page_table), shift0=state_shift[e], wkv0=state_wkv[e])
            for outs, res in ((outs_p, res_p), (outs_s, res_s)):
                for name, val in zip(("k", "v", "logf", "wkv", "shift"), res):
                    outs[name].append(val)
        else:
            o = layer // 2
            mprm = (ssm_conv_w[o], ssm_conv_b[o], ssm_dt_bias[o], ssm_a_log[o], ssm_d[o], ssm_norm_w[o])
            w_cat = jnp.pad(w_in_odd[o], ((0, 0), (0, LANES - ssm_heads))).astype(BF16)
            w_out = w_out_odd[o].astype(BF16)
            tn = w_cat.shape[1]
            proj_p = _norm_matmul(yp, norm_mix[layer], w_cat, tm=256, tn=tn)
            y_m, conv_p, h_p = _ssd(proj_p, *mprm, heads=ssm_heads)
            yp = _matmul_res([y_m], [w_out], yp, tm=512, tn=dm)
            proj_s = _norm_matmul(ys, norm_mix[layer], w_cat, tm=bs, tn=tn)
            y_s, conv_s, h_s = _ssm_step(proj_s, state_conv[o], state_ssm[o], *mprm, heads=ssm_heads)
            ys = _matmul_res([y_s], [w_out], ys, tm=bs, tn=dm)
            outs_p["conv"].append(conv_p[None])
            outs_p["ssm"].append(h_p[None])
            outs_s["conv"].append(conv_s)
            outs_s["ssm"].append(h_s)
        wq = peer_wq[layer].astype(BF16)
        keys_pad = _peer_keys_padded(peer_keys[layer])
        u_b = peer_u[layer].astype(BF16)
        v_b = peer_v[layer].astype(BF16)
        yp = _peer(yp, norm_ffn[layer], wq, keys_pad, u_b, v_b, tm_proj=512, tm_route=128, tm_mlp=512, te=2048)
        ys = _peer(ys, norm_ffn[layer], wq, keys_pad, u_b, v_b, tm_proj=bs, tm_route=bs, tm_mlp=bs, te=2048)
    y_prompt = _rmsnorm(yp, norm_final, tm=512).reshape(bp, seq_p, dm)
    y_sample = _rmsnorm(ys, norm_final, tm=bs).reshape(bs, seq_s, dm)
    order = ("k", "v", "logf", "wkv", "shift", "conv", "ssm")
    return ((y_prompt, y_sample) + tuple(jnp.stack(outs_p[name]) for name in order)
            + tuple(jnp.stack(outs_s[name]) for name in order))
```

```python
import functools
import math

import numpy as np
import jax
import jax.numpy as jnp
from jax import lax
from jax.experimental import pallas as pl
from jax.experimental.pallas import tpu as pltpu

F32 = jnp.float32
BF16 = jnp.bfloat16
I32 = jnp.int32

LANES = 128
SUBLANES = 8
V7X_VMEM_BYTES = 64 * 1024 * 1024

RMS_EPS = 1e-6
GN_EPS = 64e-5

FOX_HEADS = 8
FOX_HEAD_DIM = 64
FOX_WIDTH = FOX_HEADS * FOX_HEAD_DIM
RWKV_HEADS = 8
RWKV_HEAD_DIM = 64
RWKV_WIDTH = RWKV_HEADS * RWKV_HEAD_DIM
DECAY_LORA = 64
ICLR_LORA = 64
GATE_LORA = 128
RWKV_PROJ = 4 * RWKV_WIDTH - RWKV_WIDTH + DECAY_LORA + ICLR_LORA + GATE_LORA
HEAD_PAIRS = RWKV_HEADS // 2

SSM_HEAD_DIM = 64
D_STATE = 128
N_GROUPS = 8
CONV_W = 4
CHUNK = 128

N_KEYS = 128
PEER_HEADS = 8
PEER_TOPK = 16
D_KEY = 128


def _cparams(semantics, vmem_mb=48):
    return pltpu.CompilerParams(dimension_semantics=semantics, vmem_limit_bytes=vmem_mb * 1024 * 1024)


def _split3(x):
    hi = x.astype(BF16)
    r1 = x - hi.astype(F32)
    mid = r1.astype(BF16)
    lo = (r1 - mid.astype(F32)).astype(BF16)
    return hi, mid, lo


def _dot_f32_right(x, m_bf16):
    hi, mid, lo = _split3(x)
    d = lambda a: jnp.dot(a, m_bf16, preferred_element_type=F32)
    return d(hi) + d(mid) + d(lo)


def _dot_f32_left(m_bf16, x):
    hi, mid, lo = _split3(x)
    d = lambda a: jnp.dot(m_bf16, a, preferred_element_type=F32)
    return d(hi) + d(mid) + d(lo)


def _sigmoid(x):
    return 1.0 / (1.0 + jnp.exp(-x))


def _softplus(x):
    return jnp.maximum(x, 0.0) + jnp.log1p(jnp.exp(-jnp.abs(x)))


def _silu(x):
    return x * _sigmoid(x)


def _norm_matmul_kernel(x_ref, g_ref, w_ref, o_ref, xn_ref):
    @pl.when(pl.program_id(1) == 0)
    def _():
        x = x_ref[...]
        ms = jnp.mean(x * x, axis=-1, keepdims=True)
        xn_ref[...] = (x * lax.rsqrt(ms + RMS_EPS) * g_ref[...]).astype(BF16)

    o_ref[...] = jnp.dot(xn_ref[...], w_ref[...], preferred_element_type=F32).astype(o_ref.dtype)


def _norm_matmul(x, g, w_bf16, *, tm, tn, out_dtype=F32):
    m, k = x.shape
    n = w_bf16.shape[1]
    assert m % tm == 0 and n % tn == 0
    return pl.pallas_call(
        _norm_matmul_kernel,
        grid=(m // tm, n // tn),
        in_specs=[
            pl.BlockSpec((tm, k), lambda i, j: (i, 0)),
            pl.BlockSpec((1, k), lambda i, j: (0, 0)),
            pl.BlockSpec((k, tn), lambda i, j: (0, j)),
        ],
        out_specs=pl.BlockSpec((tm, tn), lambda i, j: (i, j)),
        out_shape=jax.ShapeDtypeStruct((m, n), out_dtype),
        scratch_shapes=[pltpu.VMEM((tm, k), BF16)],
        compiler_params=_cparams(("parallel", "arbitrary")),
        name="norm_matmul",
    )(x, g.reshape(1, k), w_bf16)


def _matmul_res_kernel(*refs, n_in):
    a_refs = refs[:n_in]
    w_refs = refs[n_in:2 * n_in]
    r_ref, o_ref = refs[2 * n_in], refs[2 * n_in + 1]
    acc = r_ref[...]
    for a_ref, w_ref in zip(a_refs, w_refs):
        acc = acc + jnp.dot(a_ref[...].astype(BF16), w_ref[...], preferred_element_type=F32)
    o_ref[...] = acc


def _matmul_res(a_list, w_list, res, *, tm, tn):
    m, n = res.shape
    n_in = len(a_list)
    assert m % tm == 0 and n % tn == 0
    in_specs = ([pl.BlockSpec((tm, a.shape[1]), lambda i, j: (i, 0)) for a in a_list]
                + [pl.BlockSpec((w.shape[0], tn), lambda i, j: (0, j)) for w in w_list]
                + [pl.BlockSpec((tm, tn), lambda i, j: (i, j))])
    return pl.pallas_call(
        functools.partial(_matmul_res_kernel, n_in=n_in),
        grid=(m // tm, n // tn),
        in_specs=in_specs,
        out_specs=pl.BlockSpec((tm, tn), lambda i, j: (i, j)),
        out_shape=jax.ShapeDtypeStruct((m, n), F32),
        compiler_params=_cparams(("parallel", "parallel")),
        name="matmul_res",
    )(*a_list, *w_list, res)


def _rmsnorm_kernel(x_ref, g_ref, o_ref):
    x = x_ref[...]
    ms = jnp.mean(x * x, axis=-1, keepdims=True)
    o_ref[...] = x * lax.rsqrt(ms + RMS_EPS) * g_ref[...]


def _rmsnorm(x, g, *, tm):
    m, k = x.shape
    return pl.pallas_call(
        _rmsnorm_kernel,
        grid=(m // tm,),
        in_specs=[pl.BlockSpec((tm, k), lambda i: (i, 0)), pl.BlockSpec((1, k), lambda i: (0, 0))],
        out_specs=pl.BlockSpec((tm, k), lambda i: (i, 0)),
        out_shape=jax.ShapeDtypeStruct((m, k), F32),
        compiler_params=_cparams(("parallel",)),
        name="rmsnorm",
    )(x, g.reshape(1, k))


def _fox_prep_kernel(fl_ref, fb_ref, lf_ref, ct_ref, carry_ref, *, t):
    @pl.when(pl.program_id(0) == 0)
    def _():
        carry_ref[...] = jnp.zeros_like(carry_ref)

    x = fl_ref[...] + fb_ref[...]
    lf = jnp.minimum(x, 0.0) - jnp.log1p(jnp.exp(-jnp.abs(x)))
    lf_ref[...] = lf[:, :FOX_HEADS]
    row = lax.broadcasted_iota(I32, (t, t), 0)
    col = lax.broadcasted_iota(I32, (t, t), 1)
    tri = jnp.where(col <= row, 1.0, 0.0).astype(BF16)
    c = _dot_f32_left(tri, lf) + carry_ref[...]
    carry_ref[...] = c[t - 1:t, :]
    ct_ref[...] = c.T[:FOX_HEADS, :]


def _fox_prep(proj, fl_block, fb_pad, *, t):
    n = proj.shape[0]
    return pl.pallas_call(
        functools.partial(_fox_prep_kernel, t=t),
        grid=(n // t,),
        in_specs=[pl.BlockSpec((t, LANES), lambda i: (i, fl_block)),
                  pl.BlockSpec((1, LANES), lambda i: (0, 0))],
        out_specs=[pl.BlockSpec((t, FOX_HEADS), lambda i: (i, 0)),
                   pl.BlockSpec((FOX_HEADS, t), lambda i: (0, i))],
        out_shape=[jax.ShapeDtypeStruct((n, FOX_HEADS), F32),
                   jax.ShapeDtypeStruct((FOX_HEADS, n), F32)],
        scratch_shapes=[pltpu.VMEM((1, LANES), F32)],
        compiler_params=_cparams(("arbitrary",)),
        name="fox_prep",
    )(proj, fb_pad)


_SKIP_LOGIT_GAP = 110.0
_SKIP_NORM_SLACK = 1.05


def _fox_bounds_kernel(q_ref, k_ref, ct_ref, sel_ref, jmin_ref, qmax_ref, kmax_ref, cs_ref, ce_ref, *, tq, nq):
    i = pl.program_id(0)
    lane = lax.broadcasted_iota(I32, (FOX_HEADS, LANES), 1)
    sub = lax.broadcasted_iota(I32, (FOX_HEADS, LANES), 0)

    @pl.when(i == 0)
    def _():
        qmax_ref[...] = jnp.zeros_like(qmax_ref)
        kmax_ref[...] = jnp.zeros_like(kmax_ref)
        cs_ref[...] = jnp.zeros_like(cs_ref)
        ce_ref[...] = jnp.zeros_like(ce_ref)

    def head_sq_norm_max(x_ref):
        x = x_ref[...].astype(F32)
        sq = jnp.dot((x * x).astype(BF16), sel_ref[...], preferred_element_type=F32)
        return jnp.max(sq, axis=0, keepdims=True)

    qmax_ref[...] = jnp.maximum(qmax_ref[...], head_sq_norm_max(q_ref))
    kmax_ref[...] = jnp.maximum(kmax_ref[...], head_sq_norm_max(k_ref))
    cs_ref[...] = jnp.where(lane == i, ct_ref[:, 0:1], cs_ref[...])
    ce_ref[...] = jnp.where(lane == i, ct_ref[:, tq - 1:tq], ce_ref[...])

    @pl.when(i == nq - 1)
    def _():
        def to_col(row):
            return jnp.sum(jnp.where(sub == lane, row, 0.0), axis=-1, keepdims=True)

        b2 = (2.0 * _SKIP_NORM_SLACK * FOX_HEAD_DIM ** -0.5) * jnp.sqrt(to_col(qmax_ref[...]) * to_col(kmax_ref[...]))
        ce = ce_ref[...]
        out = jnp.zeros((FOX_HEADS, LANES), F32)
        for qi in range(nq):
            gap = b2 + cs_ref[:, qi:qi + 1] - ce
            skip = jnp.where((gap < -_SKIP_LOGIT_GAP) & (lane < qi), 1.0, 0.0)
            for p in range(FOX_HEADS // 2):
                both = skip[2 * p:2 * p + 1, :] * skip[2 * p + 1:2 * p + 2, :]
                count = jnp.sum(both, axis=-1, keepdims=True)
                out = jnp.where((sub == p) & (lane == qi), count, out)
        jmin_ref[...] = out.astype(I32)


def _fox_bounds(qkv_bf16, ct, *, tq):
    n = qkv_bf16.shape[0]
    nq = n // tq
    assert nq <= LANES
    sel = jnp.asarray((np.arange(FOX_WIDTH)[:, None] // FOX_HEAD_DIM == np.arange(LANES)[None, :]).astype(np.float32), BF16)
    table = pl.pallas_call(
        functools.partial(_fox_bounds_kernel, tq=tq, nq=nq),
        grid=(nq,),
        in_specs=[pl.BlockSpec((tq, FOX_WIDTH), lambda i: (i, 0)),
                  pl.BlockSpec((tq, FOX_WIDTH), lambda i: (i, 1)),
                  pl.BlockSpec((FOX_HEADS, tq), lambda i: (0, i)),
                  pl.BlockSpec(sel.shape, lambda i: (0, 0))],
        out_specs=pl.BlockSpec((FOX_HEADS, LANES), lambda i: (0, 0)),
        out_shape=jax.ShapeDtypeStruct((FOX_HEADS, LANES), I32),
        scratch_shapes=[pltpu.VMEM((1, LANES), F32), pltpu.VMEM((1, LANES), F32),
                        pltpu.VMEM((FOX_HEADS, LANES), F32), pltpu.VMEM((FOX_HEADS, LANES), F32)],
        compiler_params=_cparams(("arbitrary",)),
        name="fox_bounds",
    )(qkv_bf16, qkv_bf16, ct, sel)
    return table[:FOX_HEADS // 2, :nq]


def _fox_attn_kernel(jmin_ref, q_ref, k_ref, v_ref, ct_ref, o_ref, qs_ref, m_ref, l_ref, acc_ref, *, tq):
    i = pl.program_id(1)
    lane = lax.broadcasted_iota(I32, (1, LANES), 1)
    left = lane < FOX_HEAD_DIM
    q = q_ref[...] * jnp.asarray(FOX_HEAD_DIM ** -0.5, BF16)
    zero = jnp.zeros_like(q)
    qs_ref[0] = jnp.where(left, q, zero)
    qs_ref[1] = jnp.where(left, zero, q)
    q0 = pl.multiple_of(i * tq, tq)
    cref = [ct_ref[h:h + 1, pl.ds(q0, LANES)][:, 0:1] for h in range(2)]

    m_ref[...] = jnp.full_like(m_ref, -jnp.inf)
    l_ref[...] = jnp.zeros_like(l_ref)
    acc_ref[...] = jnp.zeros_like(acc_ref)
    ncol = tq // LANES

    def step(j, masked):
        k0 = pl.multiple_of(j * tq, tq)
        kb = k_ref[pl.ds(k0, tq), :]
        vb = v_ref[pl.ds(k0, tq), :]
        bias = [cref[h] - ct_ref[h:h + 1, pl.ds(k0, tq)] for h in range(2)]
        for h in range(2):
            s = lax.dot_general(qs_ref[h], kb, (((1,), (1,)), ((), ())), preferred_element_type=F32)
            s = s + bias[h]
            if masked:
                r = lax.broadcasted_iota(I32, (tq, tq), 0)
                c = lax.broadcasted_iota(I32, (tq, tq), 1)
                s = jnp.where(c <= r, s, -jnp.inf)
            cols = [s[:, c * LANES:(c + 1) * LANES] for c in range(ncol)]
            lane_max = functools.reduce(jnp.maximum, cols)
            m_old = m_ref[h]
            m_new = jnp.maximum(m_old, jnp.max(lane_max, axis=-1, keepdims=True))
            alpha = jnp.exp(m_old - m_new)
            ps = [jnp.exp(c - m_new) for c in cols]
            l_ref[h] = alpha * l_ref[h] + functools.reduce(jnp.add, ps)
            m_ref[h] = m_new
            p = jnp.concatenate([x.astype(BF16) for x in ps], axis=1)
            acc_ref[h] = alpha * acc_ref[h] + jnp.dot(p, vb, preferred_element_type=F32)

    def body(j, carry):
        step(j, False)
        return carry

    lax.fori_loop(jmin_ref[pl.program_id(0), i], i, body, 0)
    step(i, True)
    l0 = jnp.sum(l_ref[0], axis=-1, keepdims=True)
    l1 = jnp.sum(l_ref[1], axis=-1, keepdims=True)
    o_ref[...] = jnp.where(left, acc_ref[0] / l0, acc_ref[1] / l1).astype(o_ref.dtype)


def _fox_attn(qkv_bf16, ct, *, tq):
    n = qkv_bf16.shape[0]
    nb = FOX_WIDTH // LANES
    jmin = _fox_bounds(qkv_bf16, ct, tq=tq)
    return pl.pallas_call(
        functools.partial(_fox_attn_kernel, tq=tq),
        grid_spec=pltpu.PrefetchScalarGridSpec(
            num_scalar_prefetch=1,
            grid=(nb, n // tq),
            in_specs=[
                pl.BlockSpec((tq, LANES), lambda p, i, jm: (i, p)),
                pl.BlockSpec((n, LANES), lambda p, i, jm: (0, nb + p)),
                pl.BlockSpec((n, LANES), lambda p, i, jm: (0, 2 * nb + p)),
                pl.BlockSpec((None, 2, n), lambda p, i, jm: (p, 0, 0)),
            ],
            out_specs=pl.BlockSpec((tq, LANES), lambda p, i, jm: (i, p)),
            scratch_shapes=[pltpu.VMEM((2, tq, LANES), BF16), pltpu.VMEM((2, tq, LANES), F32),
                            pltpu.VMEM((2, tq, LANES), F32), pltpu.VMEM((2, tq, LANES), F32)],
        ),
        out_shape=jax.ShapeDtypeStruct((n, FOX_WIDTH), BF16),
        compiler_params=_cparams(("arbitrary", "arbitrary")),
        name="fox_attn",
    )(jmin, qkv_bf16, qkv_bf16, qkv_bf16, ct.reshape(nb, 2, n))


_RW = RWKV_WIDTH
_RWKV_PERM = np.concatenate([
    np.arange(0, _RW),
    np.arange(_RW + DECAY_LORA, 2 * _RW + DECAY_LORA),
    np.arange(2 * _RW + DECAY_LORA, 3 * _RW + DECAY_LORA),
    np.arange(3 * _RW + DECAY_LORA + ICLR_LORA, RWKV_PROJ),
    np.arange(_RW, _RW + DECAY_LORA),
    np.arange(3 * _RW + DECAY_LORA, 3 * _RW + DECAY_LORA + ICLR_LORA),
])
_RWKV_INV_PERM = np.argsort(_RWKV_PERM)


def _head_block_ones(width, head_dim):
    idx = np.arange(width) // head_dim
    return jnp.asarray((idx[:, None] == idx[None, :]).astype(np.float32), BF16)


def _rwkv_prep_math(p, p_prev, mu, w0, w2p, a0, a2p, g2, k_k, k_a, bones):
    xs = p + mu * (p_prev - p)
    r = xs[:, 0:_RW]
    k = xs[:, _RW:2 * _RW]
    v = xs[:, 2 * _RW:3 * _RW]
    gl = xs[:, 3 * _RW:3 * _RW + GATE_LORA]
    wa = xs[:, 3 * _RW + GATE_LORA:]
    w = -_softplus(-(w0 + jnp.dot(jnp.tanh(wa).astype(BF16), w2p, preferred_element_type=F32))) - 0.5
    decay = jnp.exp(-jnp.exp(w))
    a = _sigmoid(a0 + jnp.dot(wa.astype(BF16), a2p, preferred_element_type=F32))
    g = jnp.dot(_sigmoid(gl).astype(BF16), g2, preferred_element_type=F32)
    kkr = k * k_k
    ss = _dot_f32_right(kkr * kkr, bones)
    kk = kkr / jnp.maximum(jnp.sqrt(ss), 1e-12)
    k2 = k * (1.0 + (a - 1.0) * k_a)
    return r, k2, v, kk, kk * a, decay, g


def _rwkv_prep_kernel(p_ref, prev_ref, mu_ref, w0_ref, w2_ref, a0_ref, a2_ref, g2_ref, kk_ref, ka_ref, bones_ref,
                      r_o, k_o, v_o, kk_o, b_o, d_o, g_o, last_o, buf_ref, *, t, shifted):
    p = p_ref[...]
    if shifted:
        @pl.when(pl.program_id(0) == 0)
        def _():
            buf_ref[SUBLANES - 1:SUBLANES, :] = prev_ref[...]

        buf_ref[SUBLANES:SUBLANES + t, :] = p
        p_prev = buf_ref[SUBLANES - 1:SUBLANES - 1 + t, :]
        buf_ref[SUBLANES - 1:SUBLANES, :] = p[t - 1:t, :]
        last_o[...] = p[t - 1:t, :]
    else:
        p_prev = prev_ref[...]
        last_o[...] = p
    outs = _rwkv_prep_math(p, p_prev, mu_ref[...], w0_ref[...], w2_ref[...], a0_ref[...], a2_ref[...], g2_ref[...],
                           kk_ref[...], ka_ref[...], bones_ref[...])
    for o_ref, val in zip((r_o, k_o, v_o, kk_o, b_o, d_o, g_o), outs):
        o_ref[...] = val


def _rwkv_prep(proj, prev, mu, w0, w2p, a0, a2p, g2, k_k, k_a, bones, *, t, shifted):
    n = proj.shape[0]
    row = lambda w: pl.BlockSpec((1, w), lambda i: (0, 0))
    full = lambda a: pl.BlockSpec(a.shape, lambda i: (0, 0))
    tok = lambda w: pl.BlockSpec((t, w), lambda i: (i, 0))
    prev_spec = row(RWKV_PROJ) if shifted else tok(RWKV_PROJ)
    last_spec = row(RWKV_PROJ) if shifted else tok(RWKV_PROJ)
    last_shape = (1, RWKV_PROJ) if shifted else (n, RWKV_PROJ)
    return pl.pallas_call(
        functools.partial(_rwkv_prep_kernel, t=t, shifted=shifted),
        grid=(n // t,),
        in_specs=[tok(RWKV_PROJ), prev_spec, row(RWKV_PROJ), row(_RW), full(w2p), row(_RW), full(a2p), full(g2),
                  row(_RW), row(_RW), full(bones)],
        out_specs=[tok(_RW)] * 7 + [last_spec],
        out_shape=[jax.ShapeDtypeStruct((n, _RW), F32)] * 7 + [jax.ShapeDtypeStruct(last_shape, F32)],
        scratch_shapes=[pltpu.VMEM((SUBLANES + t, RWKV_PROJ), F32)],
        compiler_params=_cparams(("arbitrary",)),
        name="rwkv_prep",
    )(proj, prev, mu, w0, w2p, a0, a2p, g2, k_k, k_a, bones)


def _rwkv_scan_kernel(r_ref, k_ref, v_ref, kk_ref, b_ref, d_ref, s0_ref, o_ref, st_ref, s_ref, *, t):
    j = pl.program_id(1)

    @pl.when(j == 0)
    def _():
        s_ref[...] = s0_ref[...]

    left = lax.broadcasted_iota(I32, (RWKV_HEAD_DIM, LANES), 1) < RWKV_HEAD_DIM

    def seg(x):
        s0 = jnp.sum(jnp.where(left, x, 0.0), axis=-1, keepdims=True)
        s1 = jnp.sum(jnp.where(left, 0.0, x), axis=-1, keepdims=True)
        return jnp.where(left, s0, s1)

    group = 2 * SUBLANES if t % (2 * SUBLANES) == 0 else t

    def readout(s, r_rows):
        s_bd = jnp.concatenate([jnp.where(left, s, 0.0), jnp.where(left, 0.0, s)], axis=0).astype(BF16)
        return lax.dot_general(r_rows, s_bd, (((1,), (1,)), ((), ())), preferred_element_type=F32)

    piece_pairs = ((0, 0), (0, 1), (0, 2), (1, 0), (1, 1), (2, 0))
    n_c = len(piece_pairs) * group
    c_lane = lax.broadcasted_iota(I32, (LANES, LANES), 1)
    c_step = c_lane % group
    c_live = c_lane < n_c
    pad_rows = jnp.zeros((LANES - n_c, LANES), F32)

    def value_columns(v_t):
        v_p = [x.astype(F32) for x in _split3(v_t)]
        v_rows = jnp.concatenate([v_p[a] for a, _ in piece_pairs] + [pad_rows], axis=0)
        return v_rows.T

    def key_rows(k_t):
        k_p = [x.astype(F32) for x in _split3(k_t)]
        return jnp.concatenate([k_p[b] for _, b in piece_pairs] + [pad_rows], axis=0).astype(BF16)

    def outer_product(v_cols, k_rows, u):
        lhs = jnp.where(c_live & (c_step == u), v_cols, 0.0).astype(BF16)
        vk = jnp.dot(lhs, k_rows, preferred_element_type=F32)
        return jnp.where(left, vk[:RWKV_HEAD_DIM], vk[RWKV_HEAD_DIM:])

    n_groups = t // group

    def group_value_columns(gi):
        base = gi * group if isinstance(gi, int) else pl.multiple_of(gi * group, group)
        return tuple(value_columns(v_ref[pl.ds(base, group), p * LANES:(p + 1) * LANES]) for p in range(HEAD_PAIRS))

    def body(gi, carry):
        states, v_cols = carry
        base = pl.multiple_of(gi * group, group)
        states = list(states)
        tiles = []
        for p in range(HEAD_PAIRS):
            sl = slice(p * LANES, (p + 1) * LANES)
            tiles.append(tuple(ref[pl.ds(base, group), sl] for ref in (kk_ref, v_ref, d_ref, b_ref, k_ref, r_ref)))
        k_rows = [key_rows(tiles[p][4]) for p in range(HEAD_PAIRS)]
        vk_tiles = [[None] * group for _ in range(HEAD_PAIRS)]
        for u in range(group):
            for p in range(HEAD_PAIRS):
                vk_tiles[p][u] = outer_product(v_cols[p], k_rows[p], u)
        v_cols_next = group_value_columns(jnp.minimum(gi + 1, n_groups - 1))
        o_rows = [[] for _ in range(HEAD_PAIRS)]
        for u in range(group):
            row = lambda x: x[u:u + 1, :]
            for p in range(HEAD_PAIRS):
                kk_t, v_t, d_t, b_t, k_t, r_t = tiles[p]
                s = states[p]
                sk = seg(s * row(kk_t))
                s = s * row(d_t) - sk * row(b_t) + vk_tiles[p][u]
                states[p] = s
                r_rows = jnp.broadcast_to(r_t, (SUBLANES, LANES)) if group == 1 else r_t
                o_rows[p].append(readout(s, r_rows.astype(BF16))[u:u + 1, :])
        for p in range(HEAD_PAIRS):
            sl = slice(p * LANES, (p + 1) * LANES)
            o_ref[pl.ds(base, group), sl] = jnp.concatenate(o_rows[p], axis=0) if group > 1 else o_rows[p][0]
        return tuple(states), v_cols_next

    states, _ = lax.fori_loop(0, n_groups, body,
                              (tuple(s_ref[p] for p in range(HEAD_PAIRS)), group_value_columns(0)))
    for p in range(HEAD_PAIRS):
        s_ref[p] = states[p]

    @pl.when(j == pl.num_programs(1) - 1)
    def _():
        st_ref[...] = s_ref[...]


def _rwkv_scan(r, k, v, kk, b, d, s0_pairs, *, batch, t):
    n = r.shape[0]
    seq = n // batch
    tok = pl.BlockSpec((None, t, _RW), lambda bi, j: (bi, j, 0))
    st = pl.BlockSpec((None, HEAD_PAIRS, RWKV_HEAD_DIM, LANES), lambda bi, j: (bi, 0, 0, 0))
    o, s_t = pl.pallas_call(
        functools.partial(_rwkv_scan_kernel, t=t),
        grid=(batch, seq // t),
        in_specs=[tok] * 6 + [st],
        out_specs=[tok, st],
        out_shape=[jax.ShapeDtypeStruct((batch, seq, _RW), F32),
                   jax.ShapeDtypeStruct((batch, HEAD_PAIRS, RWKV_HEAD_DIM, LANES), F32)],
        scratch_shapes=[pltpu.VMEM((HEAD_PAIRS, RWKV_HEAD_DIM, LANES), F32)],
        compiler_params=_cparams(("arbitrary", "arbitrary")),
        name="rwkv_scan",
    )(*(x.reshape(batch, seq, _RW) for x in (r, k, v, kk, b, d)), s0_pairs)
    return o.reshape(n, _RW), s_t


def _rwkv_post_kernel(o_ref, r_ref, k_ref, v_ref, g_ref, lnw_ref, lnb_ref, rk_ref, bones_ref, out_ref):
    bones = bones_ref[...]
    inv = 1.0 / RWKV_HEAD_DIM
    o = o_ref[...]
    mean = _dot_f32_right(o, bones) * inv
    cen = o - mean
    var = _dot_f32_right(cen * cen, bones) * inv
    gn = cen * lax.rsqrt(var + GN_EPS) * lnw_ref[...] + lnb_ref[...]
    bonus = _dot_f32_right(r_ref[...] * k_ref[...] * rk_ref[...], bones) * v_ref[...]
    out_ref[...] = ((gn + bonus) * g_ref[...]).astype(out_ref.dtype)


def _rwkv_post(o, r, k, v, g, lnw, lnb, rk, bones, *, t):
    n = o.shape[0]
    tok = pl.BlockSpec((t, _RW), lambda i: (i, 0))
    row = pl.BlockSpec((1, _RW), lambda i: (0, 0))
    return pl.pallas_call(
        _rwkv_post_kernel,
        grid=(n // t,),
        in_specs=[tok] * 5 + [row] * 3 + [pl.BlockSpec(bones.shape, lambda i: (0, 0))],
        out_specs=tok,
        out_shape=jax.ShapeDtypeStruct((n, _RW), BF16),
        compiler_params=_cparams(("parallel",)),
        name="rwkv_post",
    )(o, r, k, v, g, lnw, lnb, rk, bones)


def _to_pairs(s):
    b = s.shape[0]
    return s.reshape(b, HEAD_PAIRS, 2, RWKV_HEAD_DIM, RWKV_HEAD_DIM).transpose(0, 1, 3, 2, 4).reshape(
        b, HEAD_PAIRS, RWKV_HEAD_DIM, LANES)


def _from_pairs(s):
    b = s.shape[0]
    return s.reshape(b, HEAD_PAIRS, RWKV_HEAD_DIM, 2, RWKV_HEAD_DIM).transpose(0, 1, 3, 2, 4).reshape(
        b, RWKV_HEADS, RWKV_HEAD_DIM, RWKV_HEAD_DIM)


def _rwkv_mix(proj, prev, s0, rw, *, batch, t_prep, t_scan, shifted):
    mu, w0, w2, a0, a2, g2, k_k, k_a, r_k, ln_w, ln_b = rw
    bones = _head_block_ones(_RW, RWKV_HEAD_DIM)
    zeros = jnp.zeros((LANES - DECAY_LORA, _RW), F32)
    w2p = jnp.concatenate([w2, zeros], axis=0).astype(BF16)
    a2p = jnp.concatenate([zeros, a2], axis=0).astype(BF16)
    row = lambda x: x.reshape(1, -1)
    r, k, v, kk, b, d, g, last = _rwkv_prep(
        proj, prev, row(mu[_RWKV_PERM]), row(w0), w2p, row(a0), a2p, g2.astype(BF16), row(k_k), row(k_a), bones,
        t=t_prep, shifted=shifted)
    o, s_t = _rwkv_scan(r, k, v, kk, b, d, _to_pairs(s0), batch=batch, t=t_scan)
    out = _rwkv_post(o, r, k, v, g, row(ln_w), row(ln_b), row(r_k), bones, t=t_prep)
    return out, last[:, _RWKV_INV_PERM], _from_pairs(s_t)


def _top_rows(s, k, order):
    vals, ords = [], []
    for _ in range(k):
        m = jnp.max(s, axis=0, keepdims=True)
        pick = jnp.min(jnp.where(s == m, order, jnp.inf), axis=0, keepdims=True)
        vals.append(m)
        ords.append(pick)
        s = jnp.where(order == pick, -jnp.inf, s)
    return jnp.concatenate(vals, axis=0), jnp.concatenate(ords, axis=0)


def _take_rows(table, pos):
    out = jnp.zeros(pos.shape, table.dtype)
    for i in range(table.shape[0]):
        out = jnp.where(pos == float(i), table[i:i + 1, :], out)
    return out


_CAND_GROUPS = ((0, 2, 16), (2, 4, 8))
_CAND_TAIL_I0, _CAND_TAIL_J = 4, 3
_CAND_HEAD_ROWS = sum((hi - lo) * nj for lo, hi, nj in _CAND_GROUPS)
_CAND_ROWS = _CAND_HEAD_ROWS + _CAND_TAIL_J * PEER_TOPK


def _candidate_tables(tm):
    k = PEER_TOPK
    assert k == 16 and _CAND_GROUPS == ((0, 2, 16), (2, 4, 8))
    r = lax.broadcasted_iota(I32, (_CAND_ROWS, tm), 0)
    r2 = r - 2 * k
    flat2 = (2 + (r2 >> 3)) * k + (r2 & 7)
    r3 = r - _CAND_HEAD_ROWS
    i3 = r3 & (k - 1)
    flat3 = i3 * k + (r3 >> 4)
    flat = jnp.where(r < 2 * k, r, jnp.where(r < _CAND_HEAD_ROWS, flat2, flat3))
    valid = (r < _CAND_HEAD_ROWS) | (i3 >= _CAND_TAIL_I0)
    return flat.astype(F32), valid


def _peer_route_kernel(q_ref, keys_ref, g_ref, e1_ref, e2_ref, gv_ref, gt_ref, *, tm):
    k = PEER_TOPK
    flat, valid = _candidate_tables(tm)
    key_order = lax.broadcasted_iota(I32, (N_KEYS, tm), 0).astype(F32)

    e1s, e2s, gs = [], [], []
    for h in range(PEER_HEADS):
        qh = q_ref[:, h * D_KEY:(h + 1) * D_KEY].astype(BF16)
        score = lambda p: lax.dot_general(keys_ref[h, p], qh, (((1,), (1,)), ((), ())), preferred_element_type=F32)
        v1, i1 = _top_rows(score(0), k, key_order)
        v2, i2 = _top_rows(score(1), k, key_order)
        parts = [v1[i:i + 1, :] + v2[:nj, :] for lo, hi, nj in _CAND_GROUPS for i in range(lo, hi)]
        parts += [v1 + v2[j:j + 1, :] for j in range(_CAND_TAIL_J)]
        cand = jnp.where(valid, jnp.concatenate(parts, axis=0), -jnp.inf)
        sc, pos = _top_rows(cand, k, flat)
        pos_i = jnp.floor(pos * (1.0 / k))
        e1s.append(_take_rows(i1, pos_i))
        e2s.append(_take_rows(i2, pos - k * pos_i))
        ex = jnp.exp(sc - sc[0:1, :])
        gs.append(ex / jnp.sum(ex, axis=0, keepdims=True))
    e1_ref[...] = jnp.concatenate(e1s, axis=0).T
    e2_ref[...] = jnp.concatenate(e2s, axis=0).T
    gv_ref[...] = jnp.concatenate(gs, axis=0).T

    eid = lax.broadcasted_iota(I32, (N_KEYS, PEER_HEADS * k), 0).astype(F32)
    rows = _ROUTE_GROUP

    def body(gi, carry):
        for half in range(_ROUTE_UNROLL):
            base = pl.multiple_of((gi * _ROUTE_UNROLL + half) * rows, rows)
            stage = half * rows * _ROUTE_PITCH
            e1_t = e1_ref[pl.ds(base, rows), :]
            e2_t = e2_ref[pl.ds(base, rows), :]
            gv_t = gv_ref[pl.ds(base, rows), :]
            for u in range(rows):
                a = jnp.where(eid == e1_t[u:u + 1, :], gv_t[u:u + 1, :], 0.0).astype(BF16)
                b = jnp.where(eid == e2_t[u:u + 1, :], 1.0, 0.0).astype(BF16)
                gt_ref[stage + u * _ROUTE_PITCH:stage + u * _ROUTE_PITCH + N_KEYS, :] = lax.dot_general(
                    a, b, (((1,), (1,)), ((), ())), preferred_element_type=F32)
        for half in range(_ROUTE_UNROLL):
            base = pl.multiple_of((gi * _ROUTE_UNROLL + half) * rows, rows)
            stage = half * rows * _ROUTE_PITCH
            for e1 in range(N_KEYS):
                slab = gt_ref[pl.ds(stage + e1, rows, stride=_ROUTE_PITCH), :]
                g_ref[pl.ds(base, rows), e1 * N_KEYS:(e1 + 1) * N_KEYS] = slab.astype(g_ref.dtype)
        return carry

    lax.fori_loop(0, tm // (rows * _ROUTE_UNROLL), body, 0)


_ROUTE_GROUP = 2 * SUBLANES
_ROUTE_UNROLL = 2
_ROUTE_PITCH = N_KEYS + SUBLANES


def _peer_route(q, keys_pad, *, tm):
    n = q.shape[0]
    width = PEER_HEADS * PEER_TOPK
    assert tm % (_ROUTE_GROUP * _ROUTE_UNROLL) == 0
    return pl.pallas_call(
        functools.partial(_peer_route_kernel, tm=tm),
        grid=(n // tm,),
        in_specs=[pl.BlockSpec((tm, PEER_HEADS * D_KEY), lambda i: (i, 0)),
                  pl.BlockSpec(keys_pad.shape, lambda i: (0, 0, 0, 0))],
        out_specs=pl.BlockSpec((tm, N_KEYS * N_KEYS), lambda i: (i, 0)),
        out_shape=jax.ShapeDtypeStruct((n, N_KEYS * N_KEYS), BF16),
        scratch_shapes=[pltpu.VMEM((tm, width), F32)] * 3
                       + [pltpu.VMEM((_ROUTE_UNROLL * _ROUTE_GROUP * _ROUTE_PITCH, N_KEYS), F32)],
        compiler_params=_cparams(("parallel",)),
        name="peer_route",
    )(q, keys_pad)


def _peer_mlp_kernel(x_ref, gain_ref, u_ref, v_ref, g_ref, o_ref, xn_ref, acc_ref):
    j = pl.program_id(1)

    @pl.when(j == 0)
    def _():
        x = x_ref[...]
        ms = jnp.mean(x * x, axis=-1, keepdims=True)
        xn_ref[...] = (x * lax.rsqrt(ms + RMS_EPS) * gain_ref[...]).astype(BF16)
        acc_ref[...] = jnp.zeros_like(acc_ref)

    h = lax.dot_general(xn_ref[...], u_ref[...], (((1,), (1,)), ((), ())), preferred_element_type=F32)
    act = 0.5 * h * (1.0 + lax.erf(h * (2.0 ** -0.5)))
    w = (act * g_ref[...].astype(F32)).astype(BF16)
    acc_ref[...] += jnp.dot(w, v_ref[...], preferred_element_type=F32)

    @pl.when(j == pl.num_programs(1) - 1)
    def _():
        o_ref[...] = x_ref[...] + acc_ref[...]


def _peer_mlp(x, gain, u_bf16, v_bf16, gmap, *, tm, te):
    n, dm = x.shape
    ne = u_bf16.shape[0]
    return pl.pallas_call(
        _peer_mlp_kernel,
        grid=(n // tm, ne // te),
        in_specs=[pl.BlockSpec((tm, dm), lambda i, j: (i, 0)),
                  pl.BlockSpec((1, dm), lambda i, j: (0, 0)),
                  pl.BlockSpec((te, dm), lambda i, j: (j, 0)),
                  pl.BlockSpec((te, dm), lambda i, j: (j, 0)),
                  pl.BlockSpec((tm, te), lambda i, j: (i, j))],
        out_specs=pl.BlockSpec((tm, dm), lambda i, j: (i, 0)),
        out_shape=jax.ShapeDtypeStruct((n, dm), F32),
        scratch_shapes=[pltpu.VMEM((tm, dm), BF16), pltpu.VMEM((tm, dm), F32)],
        compiler_params=_cparams(("parallel", "arbitrary")),
        name="peer_mlp",
    )(x, gain.reshape(1, dm), u_bf16, v_bf16, gmap)


def _peer_keys_padded(keys):
    z = jnp.zeros_like(keys[:, 0])
    first = jnp.concatenate([keys[:, 0], z], axis=-1)
    second = jnp.concatenate([z, keys[:, 1]], axis=-1)
    return jnp.stack([first, second], axis=1).astype(BF16)


def _peer(y, gain, wq_bf16, keys_pad, u_bf16, v_bf16, *, tm_proj, tm_route, tm_mlp, te):
    n = y.shape[0]
    q = _norm_matmul(y, gain, wq_bf16, tm=tm_proj, tn=wq_bf16.shape[1])
    gmap = _peer_route(q, keys_pad, tm=tm_route)
    return _peer_mlp(y, gain, u_bf16, v_bf16, gmap, tm=tm_mlp, te=te)


def _conv_silu(x, buf_ref, w_ref, b_ref, q):
    buf_ref[SUBLANES:SUBLANES + q, :] = x
    acc = b_ref[...] + x * w_ref[CONV_W - 1:CONV_W, :]
    for back in range(1, CONV_W):
        acc = acc + buf_ref[SUBLANES - back:SUBLANES - back + q, :] * w_ref[CONV_W - 1 - back:CONV_W - back, :]
    tail = buf_ref[q + SUBLANES - (CONV_W - 1):q + SUBLANES, :]
    buf_ref[SUBLANES - (CONV_W - 1):SUBLANES, :] = tail
    return _silu(acc), tail


def _ssd_kernel(z_ref, x_ref, bc_ref, dt_ref, cwx_ref, cwbc_ref, cbx_ref, cbbc_ref, dtb_ref, alog_ref, dskip_ref,
                nw_ref, expand_ref, y_ref, convx_ref, convbc_ref, ht_ref, bufx_ref, bufbc_ref, h_ref, *, q, heads):
    hpg = heads // N_GROUPS
    gw = hpg * SSM_HEAD_DIM

    @pl.when(pl.program_id(0) == 0)
    def _():
        bufx_ref[...] = jnp.zeros_like(bufx_ref)
        bufbc_ref[...] = jnp.zeros_like(bufbc_ref)
        h_ref[...] = jnp.zeros_like(h_ref)

    xs, tail_x = _conv_silu(x_ref[...], bufx_ref, cwx_ref, cbx_ref, q)
    bc, tail_bc = _conv_silu(bc_ref[...], bufbc_ref, cwbc_ref, cbbc_ref, q)
    convx_ref[...] = tail_x
    convbc_ref[...] = tail_bc

    dt = _softplus(dt_ref[...] + dtb_ref[...])
    a = -jnp.exp(alog_ref[...])
    row = lax.broadcasted_iota(I32, (q, q), 0)
    col = lax.broadcasted_iota(I32, (q, q), 1)
    lower = col <= row
    tri = jnp.where(lower, 1.0, 0.0).astype(BF16)
    acs = _dot_f32_left(tri, dt * a)
    acs_t = acs.T
    expand = expand_ref[...]
    dt_e = _dot_f32_right(dt, expand)
    acs_e = _dot_f32_right(acs, expand)
    last_e = acs_e[q - 1:q, :]
    xc = xs * dt_e
    xcd = (xc * jnp.exp(last_e - acs_e)).astype(BF16)
    xc_b = xc.astype(BF16)
    grow = jnp.exp(acs_e)
    chunk_decay = jnp.exp(last_e)

    nbc = N_GROUPS * D_STATE
    for g in range(N_GROUPS):
        bm = bc[:, g * D_STATE:(g + 1) * D_STATE].astype(BF16)
        cm = bc[:, nbc + g * D_STATE:nbc + (g + 1) * D_STATE].astype(BF16)
        cb = lax.dot_general(cm, bm, (((1,), (1,)), ((), ())), preferred_element_type=F32)
        sl = slice(g * gw, (g + 1) * gw)
        h_prev = h_ref[g]
        y_off = jnp.dot(cm, h_prev.astype(BF16), preferred_element_type=F32) * grow[:, sl]
        y_diag = []
        for r in range(hpg):
            hd = g * hpg + r
            diff = acs[:, hd:hd + 1] - acs_t[hd:hd + 1, :]
            m = (cb * jnp.exp(jnp.where(lower, diff, -jnp.inf))).astype(BF16)
            y_diag.append(jnp.dot(m, xc_b[:, hd * SSM_HEAD_DIM:(hd + 1) * SSM_HEAD_DIM], preferred_element_type=F32))
        y_g = jnp.concatenate(y_diag, axis=1) + y_off + dskip_ref[:, sl] * xs[:, sl]
        states = jnp.dot(bm.T, xcd[:, sl], preferred_element_type=F32)
        h_ref[g] = h_prev * chunk_decay[:, sl] + states
        zg = z_ref[:, sl]
        y_g = y_g * _silu(zg)
        ms = jnp.mean(y_g * y_g, axis=-1, keepdims=True)
        y_ref[:, sl] = (y_g * lax.rsqrt(ms + RMS_EPS) * nw_ref[:, sl]).astype(y_ref.dtype)

    @pl.when(pl.program_id(0) == pl.num_programs(0) - 1)
    def _():
        ht_ref[...] = h_ref[...]


def _ssd(proj, conv_w, conv_b, dt_bias, a_log, d_skip, norm_w, *, heads):
    n = proj.shape[0]
    d_inner = heads * SSM_HEAD_DIM
    q = CHUNK
    nbc = 2 * N_GROUPS * D_STATE
    assert nbc == d_inner and n % q == 0
    gw = d_inner // N_GROUPS
    pad = lambda x: jnp.pad(x, (0, LANES - heads)).reshape(1, LANES)
    expand = jnp.asarray((np.arange(LANES)[:, None] == (np.arange(d_inner) // SSM_HEAD_DIM)[None, :]).astype(np.float32), BF16)
    blk = lambda w, j: pl.BlockSpec((q, w), lambda i, j=j: (i, j))
    full = lambda a: pl.BlockSpec(a.shape, lambda i: (0,) * a.ndim)
    small = [conv_w[:, :d_inner], conv_w[:, d_inner:], conv_b[:d_inner].reshape(1, -1), conv_b[d_inner:].reshape(1, -1),
             pad(dt_bias), pad(a_log), jnp.repeat(d_skip, SSM_HEAD_DIM).reshape(1, -1), norm_w.reshape(1, -1), expand]
    y, cx, cbc, ht = pl.pallas_call(
        functools.partial(_ssd_kernel, q=q, heads=heads),
        grid=(n // q,),
        in_specs=[blk(d_inner, 0), blk(d_inner, 1), blk(d_inner, 2), blk(LANES, 3 * d_inner // LANES)]
                 + [full(a) for a in small],
        out_specs=[pl.BlockSpec((q, d_inner), lambda i: (i, 0)),
                   pl.BlockSpec((CONV_W - 1, d_inner), lambda i: (0, 0)),
                   pl.BlockSpec((CONV_W - 1, nbc), lambda i: (0, 0)),
                   pl.BlockSpec((N_GROUPS, D_STATE, gw), lambda i: (0, 0, 0))],
        out_shape=[jax.ShapeDtypeStruct((n, d_inner), BF16),
                   jax.ShapeDtypeStruct((CONV_W - 1, d_inner), F32),
                   jax.ShapeDtypeStruct((CONV_W - 1, nbc), F32),
                   jax.ShapeDtypeStruct((N_GROUPS, D_STATE, gw), F32)],
        scratch_shapes=[pltpu.VMEM((SUBLANES + q, d_inner), F32), pltpu.VMEM((SUBLANES + q, nbc), F32),
                        pltpu.VMEM((N_GROUPS, D_STATE, gw), F32)],
        compiler_params=_cparams(("arbitrary",)),
        name="ssd",
    )(proj, proj, proj, proj, *small)
    hpg = heads // N_GROUPS
    h_t = ht.reshape(N_GROUPS, D_STATE, hpg, SSM_HEAD_DIM).transpose(0, 2, 3, 1).reshape(heads, SSM_HEAD_DIM, D_STATE)
    return y, jnp.concatenate([cx, cbc], axis=1), h_t


def _ssm_step_kernel(z_ref, x_ref, bc_ref, dt_ref, bufx_ref, bufbc_ref, h0_ref, cwx_ref, cwbc_ref, cbx_ref, cbbc_ref,
                     dtb_ref, alog_ref, dskip_ref, nw_ref, y_ref, nbufx_ref, nbufbc_ref, h_ref, *, heads):
    hpg = heads // N_GROUPS
    gw = hpg * SSM_HEAD_DIM
    nbc = N_GROUPS * D_STATE
    fullx = jnp.concatenate([bufx_ref[...], x_ref[...]], axis=1)
    nbufx_ref[...] = fullx[:, 1:]
    xs = _silu(cbx_ref[...] + jnp.sum(fullx * cwx_ref[...], axis=1, keepdims=True))
    fullbc = jnp.concatenate([bufbc_ref[...], bc_ref[...]], axis=0)
    nbufbc_ref[...] = fullbc[1:, :]
    bc = _silu(cbbc_ref[...] + jnp.sum(fullbc * cwbc_ref[...], axis=0, keepdims=True))
    dt = _softplus(dt_ref[...] + dtb_ref[...])
    dec = jnp.exp(dt * (-jnp.exp(alog_ref[...])))
    xdt = xs * dt
    for g in range(N_GROUPS):
        rows = slice(g * gw, (g + 1) * gw)
        bm = bc[:, g * D_STATE:(g + 1) * D_STATE]
        cm = bc[:, nbc + g * D_STATE:nbc + (g + 1) * D_STATE]
        h_new = h0_ref[rows, :] * dec[rows, :] + xdt[rows, :] * bm
        h_ref[rows, :] = h_new
        y = jnp.sum(h_new * cm, axis=-1, keepdims=True) + dskip_ref[rows, :] * xs[rows, :]
        y = y * _silu(z_ref[rows, :])
        ms = jnp.mean(y * y, axis=0, keepdims=True)
        y_ref[rows, :] = y * lax.rsqrt(ms + RMS_EPS) * nw_ref[rows, :]


def _ssm_step(proj, conv_buf, h0, conv_w, conv_b, dt_bias, a_log, d_skip, norm_w, *, heads):
    b = proj.shape[0]
    d_inner = heads * SSM_HEAD_DIM
    nbc2 = 2 * N_GROUPS * D_STATE
    col = lambda x: x.reshape(b, d_inner, 1)
    per_head = lambda v: jnp.repeat(v, SSM_HEAD_DIM, axis=-1)
    z = col(proj[:, :d_inner])
    x = col(proj[:, d_inner:2 * d_inner])
    bc = proj[:, 2 * d_inner:2 * d_inner + nbc2].reshape(b, 1, nbc2)
    dt = col(per_head(proj[:, 2 * d_inner + nbc2:2 * d_inner + nbc2 + heads]))
    bufx = conv_buf[:, :, :d_inner].transpose(0, 2, 1)
    bufbc = conv_buf[:, :, d_inner:]
    pcol = lambda v: v.reshape(d_inner, 1)
    params = [conv_w[:, :d_inner].T, conv_w[:, d_inner:], pcol(conv_b[:d_inner]), conv_b[d_inner:].reshape(1, nbc2),
              pcol(per_head(dt_bias)), pcol(per_head(a_log)), pcol(per_head(d_skip)), pcol(norm_w)]
    seq = lambda *shape: pl.BlockSpec((None,) + shape, lambda i: (i,) + (0,) * len(shape))
    full = lambda a: pl.BlockSpec(a.shape, lambda i: (0,) * a.ndim)
    y, nbx, nbbc, h = pl.pallas_call(
        functools.partial(_ssm_step_kernel, heads=heads),
        grid=(b,),
        in_specs=[seq(d_inner, 1), seq(d_inner, 1), seq(1, nbc2), seq(d_inner, 1), seq(d_inner, CONV_W - 1),
                  seq(CONV_W - 1, nbc2), seq(d_inner, D_STATE)] + [full(a) for a in params],
        out_specs=[seq(d_inner, 1), seq(d_inner, CONV_W - 1), seq(CONV_W - 1, nbc2), seq(d_inner, D_STATE)],
        out_shape=[jax.ShapeDtypeStruct((b, d_inner, 1), F32), jax.ShapeDtypeStruct((b, d_inner, CONV_W - 1), F32),
                   jax.ShapeDtypeStruct((b, CONV_W - 1, nbc2), F32), jax.ShapeDtypeStruct((b, d_inner, D_STATE), F32)],
        compiler_params=_cparams(("parallel",)),
        name="ssm_step",
    )(z, x, bc, dt, bufx, bufbc, h0.reshape(b, d_inner, D_STATE), *params)
    new_buf = jnp.concatenate([nbx.transpose(0, 2, 1), nbbc], axis=2)
    return y.reshape(b, d_inner), new_buf, h.reshape(b, heads, SSM_HEAD_DIM, D_STATE)


def _fox_decode_kernel(pt_ref, q_ref, kc_ref, vc_ref, lfc_ref, *refs, pages):
    k_refs = refs[:pages]
    v_refs = refs[pages:2 * pages]
    lf_refs = refs[2 * pages:3 * pages]
    o_ref, qb_ref, m_ref, l_ref, acc_ref, later_ref = refs[3 * pages:]
    j = pl.program_id(1)
    first_lane = lax.broadcasted_iota(I32, (1, LANES), 1) == 0

    @pl.when(j == 0)
    def _():
        s_rows = []
        for h in range(FOX_HEADS):
            qh = q_ref[h] * (FOX_HEAD_DIM ** -0.5)
            qb_ref[h] = jnp.broadcast_to(qh, (FOX_HEAD_DIM, LANES))
            s_rows.append(jnp.sum(qh * kc_ref[h], axis=0, keepdims=True))
            acc_ref[h] = jnp.where(first_lane, vc_ref[h], 0.0)
        m_ref[...] = jnp.broadcast_to(jnp.concatenate(s_rows, axis=0), m_ref.shape)
        l_ref[...] = jnp.broadcast_to(jnp.where(first_lane, 1.0, 0.0), l_ref.shape)
        later_ref[...] = jnp.broadcast_to(lfc_ref[...], later_ref.shape)

    urow = lax.broadcasted_iota(I32, (LANES, LANES), 0)
    tcol = lax.broadcasted_iota(I32, (LANES, LANES), 1)
    after = jnp.where(urow > tcol, 1.0, 0.0).astype(BF16)
    for u in range(pages):
        lf = lf_refs[u][...]
        later = later_ref[...]
        bias = _dot_f32_right(lf, after) + later
        rows = [jnp.sum(k_refs[u][h] * qb_ref[h], axis=0, keepdims=True) for h in range(FOX_HEADS)]
        s = jnp.concatenate(rows, axis=0) + bias
        m_old = m_ref[...]
        m_new = jnp.maximum(m_old, jnp.max(s, axis=-1, keepdims=True))
        alpha = jnp.exp(m_old - m_new)
        p = jnp.exp(s - m_new)
        l_ref[...] = alpha * l_ref[...] + p
        m_ref[...] = m_new
        for h in range(FOX_HEADS):
            acc_ref[h] = alpha[h:h + 1, :] * acc_ref[h] + p[h:h + 1, :] * v_refs[u][h]
        later_ref[...] = later + jnp.sum(lf, axis=-1, keepdims=True)

    @pl.when(j == pl.num_programs(1) - 1)
    def _():
        l = jnp.sum(l_ref[...], axis=-1, keepdims=True)
        for h in range(FOX_HEADS):
            o_ref[h] = jnp.sum(acc_ref[h], axis=-1, keepdims=True) / l[h:h + 1, :]


def _fox_decode(q, k_cur, v_cur, logf_cur, cache_k, cache_v, cache_logf, page_table, *, pages):
    b, n_pages = page_table.shape
    page = cache_k.shape[1]
    assert page == LANES and n_pages % pages == 0
    kt = cache_k.transpose(0, 2, 3, 1)
    vt = cache_v.transpose(0, 2, 3, 1)
    lft = cache_logf.transpose(0, 2, 1)
    col = lambda x: x.reshape(b, FOX_HEADS, FOX_HEAD_DIM, 1)
    tok = pl.BlockSpec((None, FOX_HEADS, FOX_HEAD_DIM, 1), lambda bi, j, pt: (bi, 0, 0, 0))
    newest_first = lambda bi, j, pt, u: pt[bi, n_pages - 1 - (j * pages + u)]
    kv_page = lambda u: pl.BlockSpec((None, FOX_HEADS, FOX_HEAD_DIM, page),
                                     lambda bi, j, pt, u=u: (newest_first(bi, j, pt, u), 0, 0, 0))
    lf_page = lambda u: pl.BlockSpec((None, FOX_HEADS, page), lambda bi, j, pt, u=u: (newest_first(bi, j, pt, u), 0, 0))
    out = pl.pallas_call(
        functools.partial(_fox_decode_kernel, pages=pages),
        grid_spec=pltpu.PrefetchScalarGridSpec(
            num_scalar_prefetch=1,
            grid=(b, n_pages // pages),
            in_specs=[tok, tok, tok, pl.BlockSpec((None, FOX_HEADS, 1), lambda bi, j, pt: (bi, 0, 0))]
                     + [kv_page(u) for u in range(pages)] * 2 + [lf_page(u) for u in range(pages)],
            out_specs=tok,
            scratch_shapes=[pltpu.VMEM((FOX_HEADS, FOX_HEAD_DIM, LANES), F32), pltpu.VMEM((FOX_HEADS, LANES), F32),
                            pltpu.VMEM((FOX_HEADS, LANES), F32), pltpu.VMEM((FOX_HEADS, FOX_HEAD_DIM, LANES), F32),
                            pltpu.VMEM((FOX_HEADS, LANES), F32)],
        ),
        out_shape=jax.ShapeDtypeStruct((b, FOX_HEADS, FOX_HEAD_DIM, 1), F32),
        compiler_params=_cparams(("parallel", "arbitrary")),
        name="fox_decode",
    )(page_table, col(q), col(k_cur), col(v_cur), logf_cur.reshape(b, FOX_HEADS, 1),
      *([kt] * pages), *([vt] * pages), *([lft] * pages))
    return out.reshape(b, FOX_WIDTH)


_COL_Q = RWKV_PROJ
_COL_K = _COL_Q + FOX_WIDTH
_COL_V = _COL_K + FOX_WIDTH
_COL_FL = _COL_V + FOX_WIDTH
_EVEN_COLS = _COL_FL + LANES


def _even_weights(w_in):
    q, k, v = (w_in[:, i * FOX_WIDTH:(i + 1) * FOX_WIDTH] for i in range(3))
    fl = w_in[:, 3 * FOX_WIDTH:3 * FOX_WIDTH + FOX_HEADS]
    rw = w_in[:, 3 * FOX_WIDTH + FOX_HEADS:][:, _RWKV_PERM]
    fl_pad = jnp.pad(fl, ((0, 0), (0, LANES - FOX_HEADS)))
    return jnp.concatenate([rw, q, k, v, fl_pad], axis=1).astype(BF16)


def _even_layer(y, gain, w_cat, w_out, fb, rw, *, batch, seq, past, shift0, wkv0):
    n = y.shape[0]
    tm = min(512, n)
    proj = _norm_matmul(y, gain, w_cat, tm=min(1024, n), tn=_EVEN_COLS)
    fb_pad = jnp.pad(fb, (0, LANES - FOX_HEADS)).reshape(1, LANES)
    logf, ct = _fox_prep(proj, _COL_FL // LANES, fb_pad, t=min(256, n))
    k_new = proj[:, _COL_K:_COL_V]
    v_new = proj[:, _COL_V:_COL_FL]
    if past is None:
        qkv = proj[:, _COL_Q:_COL_FL].astype(BF16)
        o_fox = _fox_attn(qkv, ct, tq=1024)
        o_rwkv, shift_new, wkv_new = _rwkv_mix(proj, shift0[:, _RWKV_PERM], wkv0, rw, batch=batch,
                                               t_prep=256, t_scan=256, shifted=True)
    else:
        cache_k, cache_v, cache_logf, page_table = past
        o_fox = _fox_decode(proj[:, _COL_Q:_COL_K], k_new, v_new, logf, cache_k, cache_v, cache_logf, page_table,
                            pages=16)
        o_rwkv, shift_new, wkv_new = _rwkv_mix(proj, shift0[:, _RWKV_PERM], wkv0, rw, batch=batch,
                                               t_prep=n, t_scan=1, shifted=False)
    w_out_b = w_out.astype(BF16)
    y = _matmul_res([o_fox, o_rwkv], [w_out_b[:FOX_WIDTH], w_out_b[FOX_WIDTH:]], y, tm=tm, tn=y.shape[1])
    heads = lambda t: t.reshape(batch, seq, FOX_HEADS, FOX_HEAD_DIM)
    return y, heads(k_new), heads(v_new), logf.reshape(batch, seq, FOX_HEADS), wkv_new, shift_new


def kernel(x_prompt, x_sample, cache_k, cache_v, cache_logf, page_table, state_wkv, state_shift, state_conv, state_ssm, norm_mix, norm_ffn, norm_final, w_in_even, w_out_even, fox_fb, rwkv_mu, rwkv_w0, rwkv_w2, rwkv_a0, rwkv_a2, rwkv_g2, rwkv_kk, rwkv_ka, rwkv_rk, rwkv_lnw, rwkv_lnb, w_in_odd, w_out_odd, ssm_conv_w, ssm_conv_b, ssm_dt_bias, ssm_a_log, ssm_d, ssm_norm_w, peer_wq, peer_keys, peer_u, peer_v):
    bp, seq_p, dm = x_prompt.shape
    bs, seq_s, _ = x_sample.shape
    assert bp == 1 and seq_s == 1
    depth = norm_mix.shape[0]
    ssm_heads = ssm_dt_bias.shape[1]
    yp = x_prompt.reshape(bp * seq_p, dm)
    ys = x_sample.reshape(bs * seq_s, dm)
    outs_p = {name: [] for name in ("k", "v", "logf", "wkv", "shift", "conv", "ssm")}
    outs_s = {name: [] for name in outs_p}
    for layer in range(depth):
        if layer % 2 == 0:
            e = layer // 2
            rw = (rwkv_mu[e], rwkv_w0[e], rwkv_w2[e], rwkv_a0[e], rwkv_a2[e], rwkv_g2[e],
                  rwkv_kk[e], rwkv_ka[e], rwkv_rk[e], rwkv_lnw[e], rwkv_lnb[e])
            w_cat = _even_weights(w_in_even[e])
            yp, *res_p = _even_layer(
                yp, norm_mix[layer], w_cat, w_out_even[e], fox_fb[e], rw, batch=bp, seq=seq_p, past=None,
                shift0=jnp.zeros((bp, RWKV_PROJ), F32),
                wkv0=jnp.zeros((bp, RWKV_HEADS, RWKV_HEAD_DIM, RWKV_HEAD_DIM), F32))
            ys, *res_s = _even_layer(
                ys, norm_mix[layer], w_cat, w_out_even[e], fox_fb[e], rw, batch=bs, seq=seq_s,
                past=(cache_k[e], cache_v[e], cache_logf[e], page_table), shift0=state_shift[e], wkv0=state_wkv[e])
            for outs, res in ((outs_p, res_p), (outs_s, res_s)):
                for name, val in zip(("k", "v", "logf", "wkv", "shift"), res):
                    outs[name].append(val)
        else:
            o = layer // 2
            mprm = (ssm_conv_w[o], ssm_conv_b[o], ssm_dt_bias[o], ssm_a_log[o], ssm_d[o], ssm_norm_w[o])
            w_cat = jnp.pad(w_in_odd[o], ((0, 0), (0, LANES - ssm_heads))).astype(BF16)
            w_out = w_out_odd[o].astype(BF16)
            tn = w_cat.shape[1]
            proj_p = _norm_matmul(yp, norm_mix[layer], w_cat, tm=256, tn=tn)
            y_m, conv_p, h_p = _ssd(proj_p, *mprm, heads=ssm_heads)
            yp = _matmul_res([y_m], [w_out], yp, tm=512, tn=dm)
            proj_s = _norm_matmul(ys, norm_mix[layer], w_cat, tm=bs, tn=tn)
            y_s, conv_s, h_s = _ssm_step(proj_s, state_conv[o], state_ssm[o], *mprm, heads=ssm_heads)
            ys = _matmul_res([y_s], [w_out], ys, tm=bs, tn=dm)
            outs_p["conv"].append(conv_p[None])
            outs_p["ssm"].append(h_p[None])
            outs_s["conv"].append(conv_s)
            outs_s["ssm"].append(h_s)
        wq = peer_wq[layer].astype(BF16)
        keys_pad = _peer_keys_padded(peer_keys[layer])
        u_b = peer_u[layer].astype(BF16)
        v_b = peer_v[layer].astype(BF16)
        yp = _peer(yp, norm_ffn[layer], wq, keys_pad, u_b, v_b, tm_proj=512, tm_route=128, tm_mlp=512, te=2048)
        ys = _peer(ys, norm_ffn[layer], wq, keys_pad, u_b, v_b, tm_proj=bs, tm_route=bs, tm_mlp=bs, te=2048)
    y_prompt = _rmsnorm(yp, norm_final, tm=512).reshape(bp, seq_p, dm)
    y_sample = _rmsnorm(ys, norm_final, tm=bs).reshape(bs, seq_s, dm)
    order = ("k", "v", "logf", "wkv", "shift", "conv", "ssm")
    return ((y_prompt, y_sample) + tuple(jnp.stack(outs_p[name]) for name in order)
            + tuple(jnp.stack(outs_s[name]) for name in order))
```

```python
import functools
import math

import numpy as np
import jax
import jax.numpy as jnp
from jax import lax
from jax.experimental import pallas as pl
from jax.experimental.pallas import tpu as pltpu

F32 = jnp.float32
BF16 = jnp.bfloat16
I32 = jnp.int32

LANES = 128
SUBLANES = 8
V7X_VMEM_BYTES = 64 * 1024 * 1024

RMS_EPS = 1e-6
GN_EPS = 64e-5

FOX_HEADS = 8
FOX_HEAD_DIM = 64
FOX_WIDTH = FOX_HEADS * FOX_HEAD_DIM
RWKV_HEADS = 8
RWKV_HEAD_DIM = 64
RWKV_WIDTH = RWKV_HEADS * RWKV_HEAD_DIM
DECAY_LORA = 64
ICLR_LORA = 64
GATE_LORA = 128
RWKV_PROJ = 4 * RWKV_WIDTH - RWKV_WIDTH + DECAY_LORA + ICLR_LORA + GATE_LORA
HEAD_PAIRS = RWKV_HEADS // 2

SSM_HEAD_DIM = 64
D_STATE = 128
N_GROUPS = 8
CONV_W = 4
CHUNK = 128

N_KEYS = 128
PEER_HEADS = 8
PEER_TOPK = 16
D_KEY = 128


def _cparams(semantics, vmem_mb=48):
    return pltpu.CompilerParams(dimension_semantics=semantics, vmem_limit_bytes=vmem_mb * 1024 * 1024)


def _split3(x):
    hi = x.astype(BF16)
    r1 = x - hi.astype(F32)
    mid = r1.astype(BF16)
    lo = (r1 - mid.astype(F32)).astype(BF16)
    return hi, mid, lo


def _dot_f32_right(x, m_bf16):
    hi, mid, lo = _split3(x)
    d = lambda a: jnp.dot(a, m_bf16, preferred_element_type=F32)
    return d(hi) + d(mid) + d(lo)


def _dot_f32_left(m_bf16, x):
    hi, mid, lo = _split3(x)
    d = lambda a: jnp.dot(m_bf16, a, preferred_element_type=F32)
    return d(hi) + d(mid) + d(lo)


def _sigmoid(x):
    return 1.0 / (1.0 + jnp.exp(-x))


def _softplus(x):
    return jnp.maximum(x, 0.0) + jnp.log1p(jnp.exp(-jnp.abs(x)))


def _silu(x):
    return x * _sigmoid(x)


def _norm_matmul_kernel(x_ref, g_ref, w_ref, o_ref, xn_ref):
    @pl.when(pl.program_id(1) == 0)
    def _():
        x = x_ref[...]
        ms = jnp.mean(x * x, axis=-1, keepdims=True)
        xn_ref[...] = (x * lax.rsqrt(ms + RMS_EPS) * g_ref[...]).astype(BF16)

    o_ref[...] = jnp.dot(xn_ref[...], w_ref[...], preferred_element_type=F32).astype(o_ref.dtype)


def _norm_matmul(x, g, w_bf16, *, tm, tn, out_dtype=F32):
    m, k = x.shape
    n = w_bf16.shape[1]
    assert m % tm == 0 and n % tn == 0
    return pl.pallas_call(
        _norm_matmul_kernel,
        grid=(m // tm, n // tn),
        in_specs=[
            pl.BlockSpec((tm, k), lambda i, j: (i, 0)),
            pl.BlockSpec((1, k), lambda i, j: (0, 0)),
            pl.BlockSpec((k, tn), lambda i, j: (0, j)),
        ],
        out_specs=pl.BlockSpec((tm, tn), lambda i, j: (i, j)),
        out_shape=jax.ShapeDtypeStruct((m, n), out_dtype),
        scratch_shapes=[pltpu.VMEM((tm, k), BF16)],
        compiler_params=_cparams(("parallel", "arbitrary")),
        name="norm_matmul",
    )(x, g.reshape(1, k), w_bf16)


def _norm_matmul_t_kernel(x_ref, g_ref, wt_ref, *o_refs):
    x = x_ref[...]
    ms = jnp.mean(x * x, axis=-1, keepdims=True)
    xn = (x * lax.rsqrt(ms + RMS_EPS) * g_ref[...]).astype(BF16)
    res = lax.dot_general(wt_ref[...], xn, (((1,), (1,)), ((), ())), preferred_element_type=F32)
    rows = res.shape[0] // len(o_refs)
    for i, o_ref in enumerate(o_refs):
        o_ref[...] = res[i * rows:(i + 1) * rows]


def _norm_matmul_t(x, g, wt_bf16, *, tm, parts):
    m, k = x.shape
    c = wt_bf16.shape[0]
    assert c % parts == 0
    return pl.pallas_call(
        _norm_matmul_t_kernel,
        grid=(m // tm,),
        in_specs=[pl.BlockSpec((tm, k), lambda i: (i, 0)),
                  pl.BlockSpec((1, k), lambda i: (0, 0)),
                  pl.BlockSpec((c, k), lambda i: (0, 0))],
        out_specs=[pl.BlockSpec((c // parts, tm), lambda i: (0, i))] * parts,
        out_shape=[jax.ShapeDtypeStruct((c // parts, m), F32)] * parts,
        compiler_params=_cparams(("parallel",)),
        name="norm_matmul_t",
    )(x, g.reshape(1, k), wt_bf16)


def _matmul_res_kernel(*refs, n_in):
    a_refs = refs[:n_in]
    w_refs = refs[n_in:2 * n_in]
    r_ref, o_ref = refs[2 * n_in], refs[2 * n_in + 1]
    acc = r_ref[...]
    for a_ref, w_ref in zip(a_refs, w_refs):
        acc = acc + jnp.dot(a_ref[...].astype(BF16), w_ref[...], preferred_element_type=F32)
    o_ref[...] = acc


def _matmul_res(a_list, w_list, res, *, tm, tn):
    m, n = res.shape
    n_in = len(a_list)
    assert m % tm == 0 and n % tn == 0
    in_specs = ([pl.BlockSpec((tm, a.shape[1]), lambda i, j: (i, 0)) for a in a_list]
                + [pl.BlockSpec((w.shape[0], tn), lambda i, j: (0, j)) for w in w_list]
                + [pl.BlockSpec((tm, tn), lambda i, j: (i, j))])
    return pl.pallas_call(
        functools.partial(_matmul_res_kernel, n_in=n_in),
        grid=(m // tm, n // tn),
        in_specs=in_specs,
        out_specs=pl.BlockSpec((tm, tn), lambda i, j: (i, j)),
        out_shape=jax.ShapeDtypeStruct((m, n), F32),
        compiler_params=_cparams(("parallel", "parallel")),
        name="matmul_res",
    )(*a_list, *w_list, res)


def _rmsnorm_kernel(x_ref, g_ref, o_ref):
    x = x_ref[...]
    ms = jnp.mean(x * x, axis=-1, keepdims=True)
    o_ref[...] = x * lax.rsqrt(ms + RMS_EPS) * g_ref[...]


def _rmsnorm(x, g, *, tm):
    m, k = x.shape
    return pl.pallas_call(
        _rmsnorm_kernel,
        grid=(m // tm,),
        in_specs=[pl.BlockSpec((tm, k), lambda i: (i, 0)), pl.BlockSpec((1, k), lambda i: (0, 0))],
        out_specs=pl.BlockSpec((tm, k), lambda i: (i, 0)),
        out_shape=jax.ShapeDtypeStruct((m, k), F32),
        compiler_params=_cparams(("parallel",)),
        name="rmsnorm",
    )(x, g.reshape(1, k))


def _fox_prep_kernel(fl_ref, fb_ref, lf_ref, ct_ref, carry_ref, *, t):
    @pl.when(pl.program_id(0) == 0)
    def _():
        carry_ref[...] = jnp.zeros_like(carry_ref)

    x = fl_ref[...] + fb_ref[...]
    lf = jnp.minimum(x, 0.0) - jnp.log1p(jnp.exp(-jnp.abs(x)))
    lf_ref[...] = lf[:, :FOX_HEADS]
    row = lax.broadcasted_iota(I32, (t, t), 0)
    col = lax.broadcasted_iota(I32, (t, t), 1)
    tri = jnp.where(col <= row, 1.0, 0.0).astype(BF16)
    c = _dot_f32_left(tri, lf) + carry_ref[...]
    carry_ref[...] = c[t - 1:t, :]
    ct_ref[...] = c.T[:FOX_HEADS, :]


def _fox_prep(proj, fl_block, fb_pad, *, t):
    n = proj.shape[0]
    return pl.pallas_call(
        functools.partial(_fox_prep_kernel, t=t),
        grid=(n // t,),
        in_specs=[pl.BlockSpec((t, LANES), lambda i: (i, fl_block)),
                  pl.BlockSpec((1, LANES), lambda i: (0, 0))],
        out_specs=[pl.BlockSpec((t, FOX_HEADS), lambda i: (i, 0)),
                   pl.BlockSpec((FOX_HEADS, t), lambda i: (0, i))],
        out_shape=[jax.ShapeDtypeStruct((n, FOX_HEADS), F32),
                   jax.ShapeDtypeStruct((FOX_HEADS, n), F32)],
        scratch_shapes=[pltpu.VMEM((1, LANES), F32)],
        compiler_params=_cparams(("arbitrary",)),
        name="fox_prep",
    )(proj, fb_pad)


_SKIP_LOGIT_GAP = 110.0
_SKIP_NORM_SLACK = 1.05


def _fox_bounds_kernel(q_ref, k_ref, ct_ref, sel_ref, jmin_ref, qmax_ref, kmax_ref, cs_ref, ce_ref, *, tq, nq):
    i = pl.program_id(0)
    lane = lax.broadcasted_iota(I32, (FOX_HEADS, LANES), 1)
    sub = lax.broadcasted_iota(I32, (FOX_HEADS, LANES), 0)

    @pl.when(i == 0)
    def _():
        qmax_ref[...] = jnp.zeros_like(qmax_ref)
        kmax_ref[...] = jnp.zeros_like(kmax_ref)
        cs_ref[...] = jnp.zeros_like(cs_ref)
        ce_ref[...] = jnp.zeros_like(ce_ref)

    def head_sq_norm_max(x_ref):
        x = x_ref[...].astype(F32)
        sq = jnp.dot((x * x).astype(BF16), sel_ref[...], preferred_element_type=F32)
        return jnp.max(sq, axis=0, keepdims=True)

    qmax_ref[...] = jnp.maximum(qmax_ref[...], head_sq_norm_max(q_ref))
    kmax_ref[...] = jnp.maximum(kmax_ref[...], head_sq_norm_max(k_ref))
    cs_ref[...] = jnp.where(lane == i, ct_ref[:, 0:1], cs_ref[...])
    ce_ref[...] = jnp.where(lane == i, ct_ref[:, tq - 1:tq], ce_ref[...])

    @pl.when(i == nq - 1)
    def _():
        def to_col(row):
            return jnp.sum(jnp.where(sub == lane, row, 0.0), axis=-1, keepdims=True)

        b2 = (2.0 * _SKIP_NORM_SLACK * FOX_HEAD_DIM ** -0.5) * jnp.sqrt(to_col(qmax_ref[...]) * to_col(kmax_ref[...]))
        ce = ce_ref[...]
        out = jnp.zeros((FOX_HEADS, LANES), F32)
        for qi in range(nq):
            gap = b2 + cs_ref[:, qi:qi + 1] - ce
            skip = jnp.where((gap < -_SKIP_LOGIT_GAP) & (lane < qi), 1.0, 0.0)
            for p in range(FOX_HEADS // 2):
                both = skip[2 * p:2 * p + 1, :] * skip[2 * p + 1:2 * p + 2, :]
                count = jnp.sum(both, axis=-1, keepdims=True)
                out = jnp.where((sub == p) & (lane == qi), count, out)
        jmin_ref[...] = out.astype(I32)


def _fox_bounds(qkv_bf16, ct, *, tq):
    n = qkv_bf16.shape[0]
    nq = n // tq
    assert nq <= LANES
    sel = jnp.asarray((np.arange(FOX_WIDTH)[:, None] // FOX_HEAD_DIM == np.arange(LANES)[None, :]).astype(np.float32), BF16)
    table = pl.pallas_call(
        functools.partial(_fox_bounds_kernel, tq=tq, nq=nq),
        grid=(nq,),
        in_specs=[pl.BlockSpec((tq, FOX_WIDTH), lambda i: (i, 0)),
                  pl.BlockSpec((tq, FOX_WIDTH), lambda i: (i, 1)),
                  pl.BlockSpec((FOX_HEADS, tq), lambda i: (0, i)),
                  pl.BlockSpec(sel.shape, lambda i: (0, 0))],
        out_specs=pl.BlockSpec((FOX_HEADS, LANES), lambda i: (0, 0)),
        out_shape=jax.ShapeDtypeStruct((FOX_HEADS, LANES), I32),
        scratch_shapes=[pltpu.VMEM((1, LANES), F32), pltpu.VMEM((1, LANES), F32),
                        pltpu.VMEM((FOX_HEADS, LANES), F32), pltpu.VMEM((FOX_HEADS, LANES), F32)],
        compiler_params=_cparams(("arbitrary",)),
        name="fox_bounds",
    )(qkv_bf16, qkv_bf16, ct, sel)
    return table[:FOX_HEADS // 2, :nq]


def _fox_attn_kernel(jmin_ref, q_ref, k_ref, v_ref, ct_ref, o_ref, qs_ref, m_ref, l_ref, acc_ref, *, tq):
    i = pl.program_id(1)
    lane = lax.broadcasted_iota(I32, (1, LANES), 1)
    left = lane < FOX_HEAD_DIM
    q = q_ref[...] * jnp.asarray(FOX_HEAD_DIM ** -0.5, BF16)
    zero = jnp.zeros_like(q)
    qs_ref[0] = jnp.where(left, q, zero)
    qs_ref[1] = jnp.where(left, zero, q)
    q0 = pl.multiple_of(i * tq, tq)
    cref = [ct_ref[h:h + 1, pl.ds(q0, LANES)][:, 0:1] for h in range(2)]

    m_ref[...] = jnp.full_like(m_ref, -jnp.inf)
    l_ref[...] = jnp.zeros_like(l_ref)
    acc_ref[...] = jnp.zeros_like(acc_ref)
    ncol = tq // LANES

    def step(j, masked):
        k0 = pl.multiple_of(j * tq, tq)
        kb = k_ref[pl.ds(k0, tq), :]
        vb = v_ref[pl.ds(k0, tq), :]
        bias = [cref[h] - ct_ref[h:h + 1, pl.ds(k0, tq)] for h in range(2)]
        for h in range(2):
            s = lax.dot_general(qs_ref[h], kb, (((1,), (1,)), ((), ())), preferred_element_type=F32)
            s = s + bias[h]
            if masked:
                r = lax.broadcasted_iota(I32, (tq, tq), 0)
                c = lax.broadcasted_iota(I32, (tq, tq), 1)
                s = jnp.where(c <= r, s, -jnp.inf)
            cols = [s[:, c * LANES:(c + 1) * LANES] for c in range(ncol)]
            lane_max = functools.reduce(jnp.maximum, cols)
            m_old = m_ref[h]
            m_new = jnp.maximum(m_old, jnp.max(lane_max, axis=-1, keepdims=True))
            alpha = jnp.exp(m_old - m_new)
            ps = [jnp.exp(c - m_new) for c in cols]
            l_ref[h] = alpha * l_ref[h] + functools.reduce(jnp.add, ps)
            m_ref[h] = m_new
            p = jnp.concatenate([x.astype(BF16) for x in ps], axis=1)
            acc_ref[h] = alpha * acc_ref[h] + jnp.dot(p, vb, preferred_element_type=F32)

    def body(j, carry):
        step(j, False)
        return carry

    lax.fori_loop(jmin_ref[pl.program_id(0), i], i, body, 0)
    step(i, True)
    l0 = jnp.sum(l_ref[0], axis=-1, keepdims=True)
    l1 = jnp.sum(l_ref[1], axis=-1, keepdims=True)
    o_ref[...] = jnp.where(left, acc_ref[0] / l0, acc_ref[1] / l1).astype(o_ref.dtype)


def _fox_attn(qkv_bf16, ct, *, tq):
    n = qkv_bf16.shape[0]
    nb = FOX_WIDTH // LANES
    jmin = _fox_bounds(qkv_bf16, ct, tq=tq)
    return pl.pallas_call(
        functools.partial(_fox_attn_kernel, tq=tq),
        grid_spec=pltpu.PrefetchScalarGridSpec(
            num_scalar_prefetch=1,
            grid=(nb, n // tq),
            in_specs=[
                pl.BlockSpec((tq, LANES), lambda p, i, jm: (i, p)),
                pl.BlockSpec((n, LANES), lambda p, i, jm: (0, nb + p)),
                pl.BlockSpec((n, LANES), lambda p, i, jm: (0, 2 * nb + p)),
                pl.BlockSpec((None, 2, n), lambda p, i, jm: (p, 0, 0)),
            ],
            out_specs=pl.BlockSpec((tq, LANES), lambda p, i, jm: (i, p)),
            scratch_shapes=[pltpu.VMEM((2, tq, LANES), BF16), pltpu.VMEM((2, tq, LANES), F32),
                            pltpu.VMEM((2, tq, LANES), F32), pltpu.VMEM((2, tq, LANES), F32)],
        ),
        out_shape=jax.ShapeDtypeStruct((n, FOX_WIDTH), BF16),
        compiler_params=_cparams(("arbitrary", "arbitrary")),
        name="fox_attn",
    )(jmin, qkv_bf16, qkv_bf16, qkv_bf16, ct.reshape(nb, 2, n))


_RW = RWKV_WIDTH
_RWKV_PERM = np.concatenate([
    np.arange(0, _RW),
    np.arange(_RW + DECAY_LORA, 2 * _RW + DECAY_LORA),
    np.arange(2 * _RW + DECAY_LORA, 3 * _RW + DECAY_LORA),
    np.arange(3 * _RW + DECAY_LORA + ICLR_LORA, RWKV_PROJ),
    np.arange(_RW, _RW + DECAY_LORA),
    np.arange(3 * _RW + DECAY_LORA, 3 * _RW + DECAY_LORA + ICLR_LORA),
])
_RWKV_INV_PERM = np.argsort(_RWKV_PERM)


def _head_block_ones(width, head_dim):
    idx = np.arange(width) // head_dim
    return jnp.asarray((idx[:, None] == idx[None, :]).astype(np.float32), BF16)


def _rwkv_prep_math(p, p_prev, mu, w0, w2p, a0, a2p, g2, k_k, k_a, bones):
    xs = p + mu * (p_prev - p)
    r = xs[:, 0:_RW]
    k = xs[:, _RW:2 * _RW]
    v = xs[:, 2 * _RW:3 * _RW]
    gl = xs[:, 3 * _RW:3 * _RW + GATE_LORA]
    wa = xs[:, 3 * _RW + GATE_LORA:]
    w = -_softplus(-(w0 + jnp.dot(jnp.tanh(wa).astype(BF16), w2p, preferred_element_type=F32))) - 0.5
    decay = jnp.exp(-jnp.exp(w))
    a = _sigmoid(a0 + jnp.dot(wa.astype(BF16), a2p, preferred_element_type=F32))
    g = jnp.dot(_sigmoid(gl).astype(BF16), g2, preferred_element_type=F32)
    kkr = k * k_k
    ss = _dot_f32_right(kkr * kkr, bones)
    kk = kkr / jnp.maximum(jnp.sqrt(ss), 1e-12)
    k2 = k * (1.0 + (a - 1.0) * k_a)
    return r, k2, v, kk, kk * a, decay, g


def _rwkv_prep_kernel(p_ref, prev_ref, mu_ref, w0_ref, w2_ref, a0_ref, a2_ref, g2_ref, kk_ref, ka_ref, bones_ref,
                      r_o, k_o, v_o, kk_o, b_o, d_o, g_o, last_o, buf_ref, *, t, shifted):
    p = p_ref[...]
    if shifted:
        @pl.when(pl.program_id(0) == 0)
        def _():
            buf_ref[SUBLANES - 1:SUBLANES, :] = prev_ref[...]

        buf_ref[SUBLANES:SUBLANES + t, :] = p
        p_prev = buf_ref[SUBLANES - 1:SUBLANES - 1 + t, :]
        buf_ref[SUBLANES - 1:SUBLANES, :] = p[t - 1:t, :]
        last_o[...] = p[t - 1:t, :]
    else:
        p_prev = prev_ref[...]
        last_o[...] = p
    outs = _rwkv_prep_math(p, p_prev, mu_ref[...], w0_ref[...], w2_ref[...], a0_ref[...], a2_ref[...], g2_ref[...],
                           kk_ref[...], ka_ref[...], bones_ref[...])
    for o_ref, val in zip((r_o, k_o, v_o, kk_o, b_o, d_o, g_o), outs):
        o_ref[...] = val


def _rwkv_prep(proj, prev, mu, w0, w2p, a0, a2p, g2, k_k, k_a, bones, *, t, shifted):
    n = proj.shape[0]
    row = lambda w: pl.BlockSpec((1, w), lambda i: (0, 0))
    full = lambda a: pl.BlockSpec(a.shape, lambda i: (0, 0))
    tok = lambda w: pl.BlockSpec((t, w), lambda i: (i, 0))
    prev_spec = row(RWKV_PROJ) if shifted else tok(RWKV_PROJ)
    last_spec = row(RWKV_PROJ) if shifted else tok(RWKV_PROJ)
    last_shape = (1, RWKV_PROJ) if shifted else (n, RWKV_PROJ)
    return pl.pallas_call(
        functools.partial(_rwkv_prep_kernel, t=t, shifted=shifted),
        grid=(n // t,),
        in_specs=[tok(RWKV_PROJ), prev_spec, row(RWKV_PROJ), row(_RW), full(w2p), row(_RW), full(a2p), full(g2),
                  row(_RW), row(_RW), full(bones)],
        out_specs=[tok(_RW)] * 7 + [last_spec],
        out_shape=[jax.ShapeDtypeStruct((n, _RW), F32)] * 7 + [jax.ShapeDtypeStruct(last_shape, F32)],
        scratch_shapes=[pltpu.VMEM((SUBLANES + t, RWKV_PROJ), F32)],
        compiler_params=_cparams(("arbitrary",)),
        name="rwkv_prep",
    )(proj, prev, mu, w0, w2p, a0, a2p, g2, k_k, k_a, bones)


_SCAN_GROUP = 16


def _rwkv_scan_kernel(r_ref, k_ref, v_ref, kk_ref, b_ref, d_ref, s0_ref, o_ref, st_ref, s_ref, *, t):
    j = pl.program_id(1)

    @pl.when(j == 0)
    def _():
        s_ref[...] = s0_ref[...]

    left = lax.broadcasted_iota(I32, (RWKV_HEAD_DIM, LANES), 1) < RWKV_HEAD_DIM

    def seg(x):
        s0 = jnp.sum(jnp.where(left, x, 0.0), axis=-1, keepdims=True)
        s1 = jnp.sum(jnp.where(left, 0.0, x), axis=-1, keepdims=True)
        return jnp.where(left, s0, s1)

    group = _SCAN_GROUP if t % _SCAN_GROUP == 0 else t

    def readout(s, r_rows):
        s_bd = jnp.concatenate([jnp.where(left, s, 0.0), jnp.where(left, 0.0, s)], axis=0).astype(BF16)
        return lax.dot_general(r_rows, s_bd, (((1,), (1,)), ((), ())), preferred_element_type=F32)

    piece_pairs = ((0, 0), (0, 1), (0, 2), (1, 0), (1, 1), (2, 0))
    n_c = len(piece_pairs) * group
    kc = LANES * pl.cdiv(n_c, LANES)
    c_lane = lax.broadcasted_iota(I32, (LANES, kc), 1)
    c_step = c_lane % group
    c_live = c_lane < n_c
    pad_rows = jnp.zeros((kc - n_c, LANES), F32)

    def value_columns(v_t):
        v_p = [x.astype(F32) for x in _split3(v_t)]
        v_rows = jnp.concatenate([v_p[a] for a, _ in piece_pairs] + [pad_rows], axis=0)
        return v_rows.T

    def key_rows(k_t):
        k_p = [x.astype(F32) for x in _split3(k_t)]
        return jnp.concatenate([k_p[b] for _, b in piece_pairs] + [pad_rows], axis=0).astype(BF16)

    def outer_product(v_cols, k_rows, u):
        lhs = jnp.where(c_live & (c_step == u), v_cols, 0.0).astype(BF16)
        vk = jnp.dot(lhs, k_rows, preferred_element_type=F32)
        return jnp.where(left, vk[:RWKV_HEAD_DIM], vk[RWKV_HEAD_DIM:])

    n_groups = t // group

    def group_value_columns(gi):
        base = gi * group if isinstance(gi, int) else pl.multiple_of(gi * group, group)
        return tuple(value_columns(v_ref[pl.ds(base, group), p * LANES:(p + 1) * LANES]) for p in range(HEAD_PAIRS))

    def body(gi, carry):
        states, v_cols = carry
        base = pl.multiple_of(gi * group, group)
        states = list(states)
        tiles = []
        for p in range(HEAD_PAIRS):
            sl = slice(p * LANES, (p + 1) * LANES)
            tiles.append(tuple(ref[pl.ds(base, group), sl] for ref in (kk_ref, v_ref, d_ref, b_ref, k_ref, r_ref)))
        k_rows = [key_rows(tiles[p][4]) for p in range(HEAD_PAIRS)]
        vk_tiles = [[None] * group for _ in range(HEAD_PAIRS)]
        for u in range(group):
            for p in range(HEAD_PAIRS):
                vk_tiles[p][u] = outer_product(v_cols[p], k_rows[p], u)
        v_cols_next = group_value_columns(jnp.minimum(gi + 1, n_groups - 1))
        o_rows = [[] for _ in range(HEAD_PAIRS)]
        for u in range(group):
            row = lambda x: x[u:u + 1, :]
            for p in range(HEAD_PAIRS):
                kk_t, v_t, d_t, b_t, k_t, r_t = tiles[p]
                s = states[p]
                sk = seg(s * row(kk_t))
                s = s * row(d_t) - sk * row(b_t) + vk_tiles[p][u]
                states[p] = s
                r_rows = jnp.broadcast_to(r_t, (SUBLANES, LANES)) if group == 1 else r_t
                o_rows[p].append(readout(s, r_rows.astype(BF16))[u:u + 1, :])
        for p in range(HEAD_PAIRS):
            sl = slice(p * LANES, (p + 1) * LANES)
            o_ref[pl.ds(base, group), sl] = jnp.concatenate(o_rows[p], axis=0) if group > 1 else o_rows[p][0]
        return tuple(states), v_cols_next

    states, _ = lax.fori_loop(0, n_groups, body,
                              (tuple(s_ref[p] for p in range(HEAD_PAIRS)), group_value_columns(0)))
    for p in range(HEAD_PAIRS):
        s_ref[p] = states[p]

    @pl.when(j == pl.num_programs(1) - 1)
    def _():
        st_ref[...] = s_ref[...]


def _rwkv_scan(r, k, v, kk, b, d, s0_pairs, *, batch, t):
    n = r.shape[0]
    seq = n // batch
    tok = pl.BlockSpec((None, t, _RW), lambda bi, j: (bi, j, 0))
    st = pl.BlockSpec((None, HEAD_PAIRS, RWKV_HEAD_DIM, LANES), lambda bi, j: (bi, 0, 0, 0))
    o, s_t = pl.pallas_call(
        functools.partial(_rwkv_scan_kernel, t=t),
        grid=(batch, seq // t),
        in_specs=[tok] * 6 + [st],
        out_specs=[tok, st],
        out_shape=[jax.ShapeDtypeStruct((batch, seq, _RW), F32),
                   jax.ShapeDtypeStruct((batch, HEAD_PAIRS, RWKV_HEAD_DIM, LANES), F32)],
        scratch_shapes=[pltpu.VMEM((HEAD_PAIRS, RWKV_HEAD_DIM, LANES), F32)],
        compiler_params=_cparams(("arbitrary", "arbitrary")),
        name="rwkv_scan",
    )(*(x.reshape(batch, seq, _RW) for x in (r, k, v, kk, b, d)), s0_pairs)
    return o.reshape(n, _RW), s_t


def _rwkv_post_kernel(o_ref, r_ref, k_ref, v_ref, g_ref, lnw_ref, lnb_ref, rk_ref, bones_ref, out_ref):
    bones = bones_ref[...]
    inv = 1.0 / RWKV_HEAD_DIM
    o = o_ref[...]
    mean = _dot_f32_right(o, bones) * inv
    cen = o - mean
    var = _dot_f32_right(cen * cen, bones) * inv
    gn = cen * lax.rsqrt(var + GN_EPS) * lnw_ref[...] + lnb_ref[...]
    bonus = _dot_f32_right(r_ref[...] * k_ref[...] * rk_ref[...], bones) * v_ref[...]
    out_ref[...] = ((gn + bonus) * g_ref[...]).astype(out_ref.dtype)


def _rwkv_post(o, r, k, v, g, lnw, lnb, rk, bones, *, t):
    n = o.shape[0]
    tok = pl.BlockSpec((t, _RW), lambda i: (i, 0))
    row = pl.BlockSpec((1, _RW), lambda i: (0, 0))
    return pl.pallas_call(
        _rwkv_post_kernel,
        grid=(n // t,),
        in_specs=[tok] * 5 + [row] * 3 + [pl.BlockSpec(bones.shape, lambda i: (0, 0))],
        out_specs=tok,
        out_shape=jax.ShapeDtypeStruct((n, _RW), BF16),
        compiler_params=_cparams(("parallel",)),
        name="rwkv_post",
    )(o, r, k, v, g, lnw, lnb, rk, bones)


def _to_pairs(s):
    b = s.shape[0]
    return s.reshape(b, HEAD_PAIRS, 2, RWKV_HEAD_DIM, RWKV_HEAD_DIM).transpose(0, 1, 3, 2, 4).reshape(
        b, HEAD_PAIRS, RWKV_HEAD_DIM, LANES)


def _from_pairs(s):
    b = s.shape[0]
    return s.reshape(b, HEAD_PAIRS, RWKV_HEAD_DIM, 2, RWKV_HEAD_DIM).transpose(0, 1, 3, 2, 4).reshape(
        b, RWKV_HEADS, RWKV_HEAD_DIM, RWKV_HEAD_DIM)


def _rwkv_mix(proj, prev, s0, rw, *, batch, t_prep, t_scan, shifted):
    mu, w0, w2, a0, a2, g2, k_k, k_a, r_k, ln_w, ln_b = rw
    bones = _head_block_ones(_RW, RWKV_HEAD_DIM)
    zeros = jnp.zeros((LANES - DECAY_LORA, _RW), F32)
    w2p = jnp.concatenate([w2, zeros], axis=0).astype(BF16)
    a2p = jnp.concatenate([zeros, a2], axis=0).astype(BF16)
    row = lambda x: x.reshape(1, -1)
    r, k, v, kk, b, d, g, last = _rwkv_prep(
        proj, prev, row(mu[_RWKV_PERM]), row(w0), w2p, row(a0), a2p, g2.astype(BF16), row(k_k), row(k_a), bones,
        t=t_prep, shifted=shifted)
    o, s_t = _rwkv_scan(r, k, v, kk, b, d, _to_pairs(s0), batch=batch, t=t_scan)
    out = _rwkv_post(o, r, k, v, g, row(ln_w), row(ln_b), row(r_k), bones, t=t_prep)
    return out, last[:, _RWKV_INV_PERM], _from_pairs(s_t)


def _top_rows(s, k, order):
    vals, ords = [], []
    for _ in range(k):
        m = jnp.max(s, axis=0, keepdims=True)
        pick = jnp.min(jnp.where(s == m, order, jnp.inf), axis=0, keepdims=True)
        vals.append(m)
        ords.append(pick)
        s = jnp.where(order == pick, -jnp.inf, s)
    return jnp.concatenate(vals, axis=0), jnp.concatenate(ords, axis=0)


def _take_rows(table, pos):
    out = jnp.zeros(pos.shape, table.dtype)
    for i in range(table.shape[0]):
        out = jnp.where(pos == float(i), table[i:i + 1, :], out)
    return out


_CAND_GROUPS = ((0, 2, 16), (2, 4, 8))
_CAND_TAIL_I0, _CAND_TAIL_J = 4, 3
_CAND_HEAD_ROWS = sum((hi - lo) * nj for lo, hi, nj in _CAND_GROUPS)
_CAND_ROWS = _CAND_HEAD_ROWS + _CAND_TAIL_J * PEER_TOPK


def _candidate_tables(tm):
    k = PEER_TOPK
    assert k == 16 and _CAND_GROUPS == ((0, 2, 16), (2, 4, 8))
    r = lax.broadcasted_iota(I32, (_CAND_ROWS, tm), 0)
    r2 = r - 2 * k
    flat2 = (2 + (r2 >> 3)) * k + (r2 & 7)
    r3 = r - _CAND_HEAD_ROWS
    i3 = r3 & (k - 1)
    flat3 = i3 * k + (r3 >> 4)
    flat = jnp.where(r < 2 * k, r, jnp.where(r < _CAND_HEAD_ROWS, flat2, flat3))
    valid = (r < _CAND_HEAD_ROWS) | (i3 >= _CAND_TAIL_I0)
    return flat.astype(F32), valid


def _peer_route_kernel(q_ref, keys_ref, g_ref, e1_ref, e2_ref, gv_ref, gt_ref, *, tm):
    k = PEER_TOPK
    flat, valid = _candidate_tables(tm)
    key_order = lax.broadcasted_iota(I32, (N_KEYS, tm), 0).astype(F32)

    e1s, e2s, gs = [], [], []
    for h in range(PEER_HEADS):
        qh = q_ref[:, h * D_KEY:(h + 1) * D_KEY].astype(BF16)
        score = lambda p: lax.dot_general(keys_ref[h, p], qh, (((1,), (1,)), ((), ())), preferred_element_type=F32)
        v1, i1 = _top_rows(score(0), k, key_order)
        v2, i2 = _top_rows(score(1), k, key_order)
        parts = [v1[i:i + 1, :] + v2[:nj, :] for lo, hi, nj in _CAND_GROUPS for i in range(lo, hi)]
        parts += [v1 + v2[j:j + 1, :] for j in range(_CAND_TAIL_J)]
        cand = jnp.where(valid, jnp.concatenate(parts, axis=0), -jnp.inf)
        sc, pos = _top_rows(cand, k, flat)
        pos_i = jnp.floor(pos * (1.0 / k))
        e1s.append(_take_rows(i1, pos_i))
        e2s.append(_take_rows(i2, pos - k * pos_i))
        ex = jnp.exp(sc - sc[0:1, :])
        gs.append(ex / jnp.sum(ex, axis=0, keepdims=True))
    e1_ref[...] = jnp.concatenate(e1s, axis=0).T
    e2_ref[...] = jnp.concatenate(e2s, axis=0).T
    gv_ref[...] = jnp.concatenate(gs, axis=0).T

    eid = lax.broadcasted_iota(I32, (N_KEYS, PEER_HEADS * k), 0).astype(F32)
    rows = _ROUTE_GROUP

    def body(gi, carry):
        for half in range(_ROUTE_UNROLL):
            base = pl.multiple_of((gi * _ROUTE_UNROLL + half) * rows, rows)
            stage = half * rows * _ROUTE_PITCH
            e1_t = e1_ref[pl.ds(base, rows), :]
            e2_t = e2_ref[pl.ds(base, rows), :]
            gv_t = gv_ref[pl.ds(base, rows), :]
            for u in range(rows):
                a = jnp.where(eid == e1_t[u:u + 1, :], gv_t[u:u + 1, :], 0.0).astype(BF16)
                b = jnp.where(eid == e2_t[u:u + 1, :], 1.0, 0.0).astype(BF16)
                gt_ref[stage + u * _ROUTE_PITCH:stage + u * _ROUTE_PITCH + N_KEYS, :] = lax.dot_general(
                    a, b, (((1,), (1,)), ((), ())), preferred_element_type=F32)
        for half in range(_ROUTE_UNROLL):
            base = pl.multiple_of((gi * _ROUTE_UNROLL + half) * rows, rows)
            stage = half * rows * _ROUTE_PITCH
            for e1 in range(N_KEYS):
                slab = gt_ref[pl.ds(stage + e1, rows, stride=_ROUTE_PITCH), :]
                g_ref[pl.ds(base, rows), e1 * N_KEYS:(e1 + 1) * N_KEYS] = slab.astype(g_ref.dtype)
        return carry

    lax.fori_loop(0, tm // (rows * _ROUTE_UNROLL), body, 0)


_ROUTE_GROUP = 2 * SUBLANES
_ROUTE_UNROLL = 2
_ROUTE_PITCH = N_KEYS + SUBLANES


def _peer_route(q, keys_pad, *, tm):
    n = q.shape[0]
    width = PEER_HEADS * PEER_TOPK
    assert tm % (_ROUTE_GROUP * _ROUTE_UNROLL) == 0
    return pl.pallas_call(
        functools.partial(_peer_route_kernel, tm=tm),
        grid=(n // tm,),
        in_specs=[pl.BlockSpec((tm, PEER_HEADS * D_KEY), lambda i: (i, 0)),
                  pl.BlockSpec(keys_pad.shape, lambda i: (0, 0, 0, 0))],
        out_specs=pl.BlockSpec((tm, N_KEYS * N_KEYS), lambda i: (i, 0)),
        out_shape=jax.ShapeDtypeStruct((n, N_KEYS * N_KEYS), BF16),
        scratch_shapes=[pltpu.VMEM((tm, width), F32)] * 3
                       + [pltpu.VMEM((_ROUTE_UNROLL * _ROUTE_GROUP * _ROUTE_PITCH, N_KEYS), F32)],
        compiler_params=_cparams(("parallel",)),
        name="peer_route",
    )(q, keys_pad)


def _peer_mlp_kernel(x_ref, gain_ref, u_ref, v_ref, g_ref, o_ref, xn_ref, acc_ref):
    j = pl.program_id(1)

    @pl.when(j == 0)
    def _():
        x = x_ref[...]
        ms = jnp.mean(x * x, axis=-1, keepdims=True)
        xn_ref[...] = (x * lax.rsqrt(ms + RMS_EPS) * gain_ref[...]).astype(BF16)
        acc_ref[...] = jnp.zeros_like(acc_ref)

    h = lax.dot_general(xn_ref[...], u_ref[...], (((1,), (1,)), ((), ())), preferred_element_type=F32)
    act = 0.5 * h * (1.0 + lax.erf(h * (2.0 ** -0.5)))
    w = (act * g_ref[...].astype(F32)).astype(BF16)
    acc_ref[...] += jnp.dot(w, v_ref[...], preferred_element_type=F32)

    @pl.when(j == pl.num_programs(1) - 1)
    def _():
        o_ref[...] = x_ref[...] + acc_ref[...]


def _peer_mlp(x, gain, u_bf16, v_bf16, gmap, *, tm, te):
    n, dm = x.shape
    ne = u_bf16.shape[0]
    return pl.pallas_call(
        _peer_mlp_kernel,
        grid=(n // tm, ne // te),
        in_specs=[pl.BlockSpec((tm, dm), lambda i, j: (i, 0)),
                  pl.BlockSpec((1, dm), lambda i, j: (0, 0)),
                  pl.BlockSpec((te, dm), lambda i, j: (j, 0)),
                  pl.BlockSpec((te, dm), lambda i, j: (j, 0)),
                  pl.BlockSpec((tm, te), lambda i, j: (i, j))],
        out_specs=pl.BlockSpec((tm, dm), lambda i, j: (i, 0)),
        out_shape=jax.ShapeDtypeStruct((n, dm), F32),
        scratch_shapes=[pltpu.VMEM((tm, dm), BF16), pltpu.VMEM((tm, dm), F32)],
        compiler_params=_cparams(("parallel", "arbitrary")),
        name="peer_mlp",
    )(x, gain.reshape(1, dm), u_bf16, v_bf16, gmap)


def _peer_keys_padded(keys):
    z = jnp.zeros_like(keys[:, 0])
    first = jnp.concatenate([keys[:, 0], z], axis=-1)
    second = jnp.concatenate([z, keys[:, 1]], axis=-1)
    return jnp.stack([first, second], axis=1).astype(BF16)


def _peer(y, gain, wq_bf16, keys_pad, u_bf16, v_bf16, *, tm_proj, tm_route, tm_mlp, te):
    n = y.shape[0]
    q = _norm_matmul(y, gain, wq_bf16, tm=tm_proj, tn=wq_bf16.shape[1])
    gmap = _peer_route(q, keys_pad, tm=tm_route)
    return _peer_mlp(y, gain, u_bf16, v_bf16, gmap, tm=tm_mlp, te=te)


def _conv_silu(x, buf_ref, w_ref, b_ref, q):
    buf_ref[SUBLANES:SUBLANES + q, :] = x
    acc = b_ref[...] + x * w_ref[CONV_W - 1:CONV_W, :]
    for back in range(1, CONV_W):
        acc = acc + buf_ref[SUBLANES - back:SUBLANES - back + q, :] * w_ref[CONV_W - 1 - back:CONV_W - back, :]
    tail = buf_ref[q + SUBLANES - (CONV_W - 1):q + SUBLANES, :]
    buf_ref[SUBLANES - (CONV_W - 1):SUBLANES, :] = tail
    return _silu(acc), tail


def _ssd_kernel(z_ref, x_ref, bc_ref, dt_ref, cwx_ref, cwbc_ref, cbx_ref, cbbc_ref, dtb_ref, alog_ref, dskip_ref,
                nw_ref, expand_ref, y_ref, convx_ref, convbc_ref, ht_ref, bufx_ref, bufbc_ref, h_ref, *, q, heads):
    hpg = heads // N_GROUPS
    gw = hpg * SSM_HEAD_DIM

    @pl.when(pl.program_id(0) == 0)
    def _():
        bufx_ref[...] = jnp.zeros_like(bufx_ref)
        bufbc_ref[...] = jnp.zeros_like(bufbc_ref)
        h_ref[...] = jnp.zeros_like(h_ref)

    xs, tail_x = _conv_silu(x_ref[...], bufx_ref, cwx_ref, cbx_ref, q)
    bc, tail_bc = _conv_silu(bc_ref[...], bufbc_ref, cwbc_ref, cbbc_ref, q)
    convx_ref[...] = tail_x
    convbc_ref[...] = tail_bc

    dt = _softplus(dt_ref[...] + dtb_ref[...])
    a = -jnp.exp(alog_ref[...])
    row = lax.broadcasted_iota(I32, (q, q), 0)
    col = lax.broadcasted_iota(I32, (q, q), 1)
    lower = col <= row
    tri = jnp.where(lower, 1.0, 0.0).astype(BF16)
    acs = _dot_f32_left(tri, dt * a)
    acs_t = acs.T
    expand = expand_ref[...]
    dt_e = _dot_f32_right(dt, expand)
    acs_e = _dot_f32_right(acs, expand)
    last_e = acs_e[q - 1:q, :]
    xc = xs * dt_e
    xcd = (xc * jnp.exp(last_e - acs_e)).astype(BF16)
    xc_b = xc.astype(BF16)
    grow = jnp.exp(acs_e)
    chunk_decay = jnp.exp(last_e)

    nbc = N_GROUPS * D_STATE
    for g in range(N_GROUPS):
        bm = bc[:, g * D_STATE:(g + 1) * D_STATE].astype(BF16)
        cm = bc[:, nbc + g * D_STATE:nbc + (g + 1) * D_STATE].astype(BF16)
        cb = lax.dot_general(cm, bm, (((1,), (1,)), ((), ())), preferred_element_type=F32)
        sl = slice(g * gw, (g + 1) * gw)
        h_prev = h_ref[g]
        y_off = jnp.dot(cm, h_prev.astype(BF16), preferred_element_type=F32) * grow[:, sl]
        y_diag = []
        for r in range(hpg):
            hd = g * hpg + r
            diff = acs[:, hd:hd + 1] - acs_t[hd:hd + 1, :]
            m = (cb * jnp.exp(jnp.where(lower, diff, -jnp.inf))).astype(BF16)
            y_diag.append(jnp.dot(m, xc_b[:, hd * SSM_HEAD_DIM:(hd + 1) * SSM_HEAD_DIM], preferred_element_type=F32))
        y_g = jnp.concatenate(y_diag, axis=1) + y_off + dskip_ref[:, sl] * xs[:, sl]
        states = jnp.dot(bm.T, xcd[:, sl], preferred_element_type=F32)
        h_ref[g] = h_prev * chunk_decay[:, sl] + states
        zg = z_ref[:, sl]
        y_g = y_g * _silu(zg)
        ms = jnp.mean(y_g * y_g, axis=-1, keepdims=True)
        y_ref[:, sl] = (y_g * lax.rsqrt(ms + RMS_EPS) * nw_ref[:, sl]).astype(y_ref.dtype)

    @pl.when(pl.program_id(0) == pl.num_programs(0) - 1)
    def _():
        ht_ref[...] = h_ref[...]


def _ssd(proj, conv_w, conv_b, dt_bias, a_log, d_skip, norm_w, *, heads):
    n = proj.shape[0]
    d_inner = heads * SSM_HEAD_DIM
    q = CHUNK
    nbc = 2 * N_GROUPS * D_STATE
    assert nbc == d_inner and n % q == 0
    gw = d_inner // N_GROUPS
    pad = lambda x: jnp.pad(x, (0, LANES - heads)).reshape(1, LANES)
    expand = jnp.asarray((np.arange(LANES)[:, None] == (np.arange(d_inner) // SSM_HEAD_DIM)[None, :]).astype(np.float32), BF16)
    blk = lambda w, j: pl.BlockSpec((q, w), lambda i, j=j: (i, j))
    full = lambda a: pl.BlockSpec(a.shape, lambda i: (0,) * a.ndim)
    small = [conv_w[:, :d_inner], conv_w[:, d_inner:], conv_b[:d_inner].reshape(1, -1), conv_b[d_inner:].reshape(1, -1),
             pad(dt_bias), pad(a_log), jnp.repeat(d_skip, SSM_HEAD_DIM).reshape(1, -1), norm_w.reshape(1, -1), expand]
    y, cx, cbc, ht = pl.pallas_call(
        functools.partial(_ssd_kernel, q=q, heads=heads),
        grid=(n // q,),
        in_specs=[blk(d_inner, 0), blk(d_inner, 1), blk(d_inner, 2), blk(LANES, 3 * d_inner // LANES)]
                 + [full(a) for a in small],
        out_specs=[pl.BlockSpec((q, d_inner), lambda i: (i, 0)),
                   pl.BlockSpec((CONV_W - 1, d_inner), lambda i: (0, 0)),
                   pl.BlockSpec((CONV_W - 1, nbc), lambda i: (0, 0)),
                   pl.BlockSpec((N_GROUPS, D_STATE, gw), lambda i: (0, 0, 0))],
        out_shape=[jax.ShapeDtypeStruct((n, d_inner), BF16),
                   jax.ShapeDtypeStruct((CONV_W - 1, d_inner), F32),
                   jax.ShapeDtypeStruct((CONV_W - 1, nbc), F32),
                   jax.ShapeDtypeStruct((N_GROUPS, D_STATE, gw), F32)],
        scratch_shapes=[pltpu.VMEM((SUBLANES + q, d_inner), F32), pltpu.VMEM((SUBLANES + q, nbc), F32),
                        pltpu.VMEM((N_GROUPS, D_STATE, gw), F32)],
        compiler_params=_cparams(("arbitrary",)),
        name="ssd",
    )(proj, proj, proj, proj, *small)
    hpg = heads // N_GROUPS
    h_t = ht.reshape(N_GROUPS, D_STATE, hpg, SSM_HEAD_DIM).transpose(0, 2, 3, 1).reshape(heads, SSM_HEAD_DIM, D_STATE)
    return y, jnp.concatenate([cx, cbc], axis=1), h_t


def _ssm_step_kernel(z_ref, x_ref, bc_ref, dt_ref, bufx_ref, bufbc_ref, h0_ref, cwx_ref, cwbc_ref, cbx_ref, cbbc_ref,
                     dtb_ref, alog_ref, dskip_ref, nw_ref, y_ref, nbufx_ref, nbufbc_ref, h_ref, *, heads):
    hpg = heads // N_GROUPS
    gw = hpg * SSM_HEAD_DIM
    nbc = N_GROUPS * D_STATE
    fullx = jnp.concatenate([bufx_ref[...], x_ref[...]], axis=1)
    nbufx_ref[...] = fullx[:, 1:]
    xs = _silu(cbx_ref[...] + jnp.sum(fullx * cwx_ref[...], axis=1, keepdims=True))
    fullbc = jnp.concatenate([bufbc_ref[...], bc_ref[...]], axis=0)
    nbufbc_ref[...] = fullbc[1:, :]
    bc = _silu(cbbc_ref[...] + jnp.sum(fullbc * cwbc_ref[...], axis=0, keepdims=True))
    dt = _softplus(dt_ref[...] + dtb_ref[...])
    dec = jnp.exp(dt * (-jnp.exp(alog_ref[...])))
    xdt = xs * dt
    for g in range(N_GROUPS):
        rows = slice(g * gw, (g + 1) * gw)
        bm = bc[:, g * D_STATE:(g + 1) * D_STATE]
        cm = bc[:, nbc + g * D_STATE:nbc + (g + 1) * D_STATE]
        h_new = h0_ref[rows, :] * dec[rows, :] + xdt[rows, :] * bm
        h_ref[rows, :] = h_new
        y = jnp.sum(h_new * cm, axis=-1, keepdims=True) + dskip_ref[rows, :] * xs[rows, :]
        y = y * _silu(z_ref[rows, :])
        ms = jnp.mean(y * y, axis=0, keepdims=True)
        y_ref[rows, :] = y * lax.rsqrt(ms + RMS_EPS) * nw_ref[rows, :]


def _ssm_step(proj, conv_buf, h0, conv_w, conv_b, dt_bias, a_log, d_skip, norm_w, *, heads):
    b = proj.shape[0]
    d_inner = heads * SSM_HEAD_DIM
    nbc2 = 2 * N_GROUPS * D_STATE
    col = lambda x: x.reshape(b, d_inner, 1)
    per_head = lambda v: jnp.repeat(v, SSM_HEAD_DIM, axis=-1)
    z = col(proj[:, :d_inner])
    x = col(proj[:, d_inner:2 * d_inner])
    bc = proj[:, 2 * d_inner:2 * d_inner + nbc2].reshape(b, 1, nbc2)
    dt = col(per_head(proj[:, 2 * d_inner + nbc2:2 * d_inner + nbc2 + heads]))
    bufx = conv_buf[:, :, :d_inner].transpose(0, 2, 1)
    bufbc = conv_buf[:, :, d_inner:]
    pcol = lambda v: v.reshape(d_inner, 1)
    params = [conv_w[:, :d_inner].T, conv_w[:, d_inner:], pcol(conv_b[:d_inner]), conv_b[d_inner:].reshape(1, nbc2),
              pcol(per_head(dt_bias)), pcol(per_head(a_log)), pcol(per_head(d_skip)), pcol(norm_w)]
    seq = lambda *shape: pl.BlockSpec((None,) + shape, lambda i: (i,) + (0,) * len(shape))
    full = lambda a: pl.BlockSpec(a.shape, lambda i: (0,) * a.ndim)
    y, nbx, nbbc, h = pl.pallas_call(
        functools.partial(_ssm_step_kernel, heads=heads),
        grid=(b,),
        in_specs=[seq(d_inner, 1), seq(d_inner, 1), seq(1, nbc2), seq(d_inner, 1), seq(d_inner, CONV_W - 1),
                  seq(CONV_W - 1, nbc2), seq(d_inner, D_STATE)] + [full(a) for a in params],
        out_specs=[seq(d_inner, 1), seq(d_inner, CONV_W - 1), seq(CONV_W - 1, nbc2), seq(d_inner, D_STATE)],
        out_shape=[jax.ShapeDtypeStruct((b, d_inner, 1), F32), jax.ShapeDtypeStruct((b, d_inner, CONV_W - 1), F32),
                   jax.ShapeDtypeStruct((b, CONV_W - 1, nbc2), F32), jax.ShapeDtypeStruct((b, d_inner, D_STATE), F32)],
        compiler_params=_cparams(("parallel",)),
        name="ssm_step",
    )(z, x, bc, dt, bufx, bufbc, h0.reshape(b, d_inner, D_STATE), *params)
    new_buf = jnp.concatenate([nbx.transpose(0, 2, 1), nbbc], axis=2)
    return y.reshape(b, d_inner), new_buf, h.reshape(b, heads, SSM_HEAD_DIM, D_STATE)


def _fox_decode_kernel(pt_ref, q_ref, kc_ref, vc_ref, lfc_ref, *refs, pages):
    k_refs = refs[:pages]
    v_refs = refs[pages:2 * pages]
    lf_refs = refs[2 * pages:3 * pages]
    o_ref, qb_ref, m_ref, l_ref, acc_ref, later_ref = refs[3 * pages:]
    j = pl.program_id(1)
    first_lane = lax.broadcasted_iota(I32, (1, LANES), 1) == 0

    @pl.when(j == 0)
    def _():
        s_rows = []
        for h in range(FOX_HEADS):
            qh = q_ref[h] * (FOX_HEAD_DIM ** -0.5)
            qb_ref[h] = jnp.broadcast_to(qh, (FOX_HEAD_DIM, LANES))
            s_rows.append(jnp.sum(qh * kc_ref[h], axis=0, keepdims=True))
            acc_ref[h] = jnp.where(first_lane, vc_ref[h], 0.0)
        m_ref[...] = jnp.broadcast_to(jnp.concatenate(s_rows, axis=0), m_ref.shape)
        l_ref[...] = jnp.broadcast_to(jnp.where(first_lane, 1.0, 0.0), l_ref.shape)
        later_ref[...] = jnp.broadcast_to(lfc_ref[...], later_ref.shape)

    urow = lax.broadcasted_iota(I32, (LANES, LANES), 0)
    tcol = lax.broadcasted_iota(I32, (LANES, LANES), 1)
    after = jnp.where(urow > tcol, 1.0, 0.0).astype(BF16)
    for u in range(pages):
        lf = lf_refs[u][...]
        later = later_ref[...]
        bias = _dot_f32_right(lf, after) + later
        rows = [jnp.sum(k_refs[u][h] * qb_ref[h], axis=0, keepdims=True) for h in range(FOX_HEADS)]
        s = jnp.concatenate(rows, axis=0) + bias
        m_old = m_ref[...]
        m_new = jnp.maximum(m_old, jnp.max(s, axis=-1, keepdims=True))
        alpha = jnp.exp(m_old - m_new)
        p = jnp.exp(s - m_new)
        l_ref[...] = alpha * l_ref[...] + p
        m_ref[...] = m_new
        for h in range(FOX_HEADS):
            acc_ref[h] = alpha[h:h + 1, :] * acc_ref[h] + p[h:h + 1, :] * v_refs[u][h]
        later_ref[...] = later + jnp.sum(lf, axis=-1, keepdims=True)

    @pl.when(j == pl.num_programs(1) - 1)
    def _():
        l = jnp.sum(l_ref[...], axis=-1, keepdims=True)
        for h in range(FOX_HEADS):
            o_ref[h] = jnp.sum(acc_ref[h], axis=-1, keepdims=True) / l[h:h + 1, :]


def _fox_decode(q, k_cur, v_cur, logf_cur, cache_k, cache_v, cache_logf, page_table, *, pages):
    b, n_pages = page_table.shape
    page = cache_k.shape[1]
    assert page == LANES and n_pages % pages == 0
    kt = cache_k.transpose(0, 2, 3, 1)
    vt = cache_v.transpose(0, 2, 3, 1)
    lft = cache_logf.transpose(0, 2, 1)
    col = lambda x: x.reshape(b, FOX_HEADS, FOX_HEAD_DIM, 1)
    tok = pl.BlockSpec((None, FOX_HEADS, FOX_HEAD_DIM, 1), lambda bi, j, pt: (bi, 0, 0, 0))
    newest_first = lambda bi, j, pt, u: pt[bi, n_pages - 1 - (j * pages + u)]
    kv_page = lambda u: pl.BlockSpec((None, FOX_HEADS, FOX_HEAD_DIM, page),
                                     lambda bi, j, pt, u=u: (newest_first(bi, j, pt, u), 0, 0, 0))
    lf_page = lambda u: pl.BlockSpec((None, FOX_HEADS, page), lambda bi, j, pt, u=u: (newest_first(bi, j, pt, u), 0, 0))
    out = pl.pallas_call(
        functools.partial(_fox_decode_kernel, pages=pages),
        grid_spec=pltpu.PrefetchScalarGridSpec(
            num_scalar_prefetch=1,
            grid=(b, n_pages // pages),
            in_specs=[tok, tok, tok, pl.BlockSpec((None, FOX_HEADS, 1), lambda bi, j, pt: (bi, 0, 0))]
                     + [kv_page(u) for u in range(pages)] * 2 + [lf_page(u) for u in range(pages)],
            out_specs=tok,
            scratch_shapes=[pltpu.VMEM((FOX_HEADS, FOX_HEAD_DIM, LANES), F32), pltpu.VMEM((FOX_HEADS, LANES), F32),
                            pltpu.VMEM((FOX_HEADS, LANES), F32), pltpu.VMEM((FOX_HEADS, FOX_HEAD_DIM, LANES), F32),
                            pltpu.VMEM((FOX_HEADS, LANES), F32)],
        ),
        out_shape=jax.ShapeDtypeStruct((b, FOX_HEADS, FOX_HEAD_DIM, 1), F32),
        compiler_params=_cparams(("parallel", "arbitrary")),
        name="fox_decode",
    )(page_table, col(q), col(k_cur), col(v_cur), logf_cur.reshape(b, FOX_HEADS, 1),
      *([kt] * pages), *([vt] * pages), *([lft] * pages))
    return out.reshape(b, FOX_WIDTH)


_COL_Q = RWKV_PROJ
_COL_K = _COL_Q + FOX_WIDTH
_COL_V = _COL_K + FOX_WIDTH
_COL_FL = _COL_V + FOX_WIDTH
_EVEN_COLS = _COL_FL + LANES


def _even_weights(w_in):
    q, k, v = (w_in[:, i * FOX_WIDTH:(i + 1) * FOX_WIDTH] for i in range(3))
    fl = w_in[:, 3 * FOX_WIDTH:3 * FOX_WIDTH + FOX_HEADS]
    rw = w_in[:, 3 * FOX_WIDTH + FOX_HEADS:][:, _RWKV_PERM]
    fl_pad = jnp.pad(fl, ((0, 0), (0, LANES - FOX_HEADS)))
    return jnp.concatenate([rw, q, k, v, fl_pad], axis=1).astype(BF16)


def _even_layer(y, gain, w_cat, w_out, fb, rw, *, batch, seq, past, shift0, wkv0):
    n = y.shape[0]
    tm = min(512, n)
    proj = _norm_matmul(y, gain, w_cat, tm=min(1024, n), tn=_EVEN_COLS)
    fb_pad = jnp.pad(fb, (0, LANES - FOX_HEADS)).reshape(1, LANES)
    logf, ct = _fox_prep(proj, _COL_FL // LANES, fb_pad, t=min(256, n))
    heads = lambda t: t.reshape(batch, seq, FOX_HEADS, FOX_HEAD_DIM)
    if past is None:
        k_t, v_t = _norm_matmul_t(y, gain, w_cat[:, _COL_K:_COL_FL].T, tm=1024, parts=2)
        tok_major = lambda x: x.reshape(FOX_HEADS, FOX_HEAD_DIM, n).transpose(2, 0, 1).reshape(
            batch, seq, FOX_HEADS, FOX_HEAD_DIM)
        k_out, v_out = tok_major(k_t), tok_major(v_t)
    else:
        k_new = proj[:, _COL_K:_COL_V]
        v_new = proj[:, _COL_V:_COL_FL]
        k_out, v_out = heads(k_new), heads(v_new)
    if past is None:
        qkv = proj[:, _COL_Q:_COL_FL].astype(BF16)
        o_fox = _fox_attn(qkv, ct, tq=1024)
        o_rwkv, shift_new, wkv_new = _rwkv_mix(proj, shift0[:, _RWKV_PERM], wkv0, rw, batch=batch,
                                               t_prep=256, t_scan=256, shifted=True)
    else:
        cache_k, cache_v, cache_logf, page_table = past
        o_fox = _fox_decode(proj[:, _COL_Q:_COL_K], k_new, v_new, logf, cache_k, cache_v, cache_logf, page_table,
                            pages=16)
        o_rwkv, shift_new, wkv_new = _rwkv_mix(proj, shift0[:, _RWKV_PERM], wkv0, rw, batch=batch,
                                               t_prep=n, t_scan=1, shifted=False)
    w_out_b = w_out.astype(BF16)
    y = _matmul_res([o_fox, o_rwkv], [w_out_b[:FOX_WIDTH], w_out_b[FOX_WIDTH:]], y, tm=tm, tn=y.shape[1])
    return y, k_out, v_out, logf.reshape(batch, seq, FOX_HEADS), wkv_new, shift_new


def kernel(x_prompt, x_sample, cache_k, cache_v, cache_logf, page_table, state_wkv, state_shift, state_conv, state_ssm, norm_mix, norm_ffn, norm_final, w_in_even, w_out_even, fox_fb, rwkv_mu, rwkv_w0, rwkv_w2, rwkv_a0, rwkv_a2, rwkv_g2, rwkv_kk, rwkv_ka, rwkv_rk, rwkv_lnw, rwkv_lnb, w_in_odd, w_out_odd, ssm_conv_w, ssm_conv_b, ssm_dt_bias, ssm_a_log, ssm_d, ssm_norm_w, peer_wq, peer_keys, peer_u, peer_v):
    bp, seq_p, dm = x_prompt.shape
    bs, seq_s, _ = x_sample.shape
    assert bp == 1 and seq_s == 1
    depth = norm_mix.shape[0]
    ssm_heads = ssm_dt_bias.shape[1]
    yp = x_prompt.reshape(bp * seq_p, dm)
    ys = x_sample.reshape(bs * seq_s, dm)
    outs_p = {name: [] for name in ("k", "v", "logf", "wkv", "shift", "conv", "ssm")}
    outs_s = {name: [] for name in outs_p}
    for layer in range(depth):
        if layer % 2 == 0:
            e = layer // 2
            rw = (rwkv_mu[e], rwkv_w0[e], rwkv_w2[e], rwkv_a0[e], rwkv_a2[e], rwkv_g2[e],
                  rwkv_kk[e], rwkv_ka[e], rwkv_rk[e], rwkv_lnw[e], rwkv_lnb[e])
            w_cat = _even_weights(w_in_even[e])
            yp, *res_p = _even_layer(
                yp, norm_mix[layer], w_cat, w_out_even[e], fox_fb[e], rw, batch=bp, seq=seq_p, past=None,
                shift0=jnp.zeros((bp, RWKV_PROJ), F32),
                wkv0=jnp.zeros((bp, RWKV_HEADS, RWKV_HEAD_DIM, RWKV_HEAD_DIM), F32))
            ys, *res_s = _even_layer(
                ys, norm_mix[layer], w_cat, w_out_even[e], fox_fb[e], rw, batch=bs, seq=seq_s,
                past=(cache_k[e], cache_v[e], cache_logf[e], page_table), shift0=state_shift[e], wkv0=state_wkv[e])
            for outs, res in ((outs_p, res_p), (outs_s, res_s)):
                for name, val in zip(("k", "v", "logf", "wkv", "shift"), res):
                    outs[name].append(val)
        else:
            o = layer // 2
            mprm = (ssm_conv_w[o], ssm_conv_b[o], ssm_dt_bias[o], ssm_a_log[o], ssm_d[o], ssm_norm_w[o])
            w_cat = jnp.pad(w_in_odd[o], ((0, 0), (0, LANES - ssm_heads))).astype(BF16)
            w_out = w_out_odd[o].astype(BF16)
            tn = w_cat.shape[1]
            proj_p = _norm_matmul(yp, norm_mix[layer], w_cat, tm=256, tn=tn)
            y_m, conv_p, h_p = _ssd(proj_p, *mprm, heads=ssm_heads)
            yp = _matmul_res([y_m], [w_out], yp, tm=512, tn=dm)
            proj_s = _norm_matmul(ys, norm_mix[layer], w_cat, tm=bs, tn=tn)
            y_s, conv_s, h_s = _ssm_step(proj_s, state_conv[o], state_ssm[o], *mprm, heads=ssm_heads)
            ys = _matmul_res([y_s], [w_out], ys, tm=bs, tn=dm)
            outs_p["conv"].append(conv_p[None])
            outs_p["ssm"].append(h_p[None])
            outs_s["conv"].append(conv_s)
            outs_s["ssm"].append(h_s)
        wq = peer_wq[layer].astype(BF16)
        keys_pad = _peer_keys_padded(peer_keys[layer])
        u_b = peer_u[layer].astype(BF16)
        v_b = peer_v[layer].astype(BF16)
        yp = _peer(yp, norm_ffn[layer], wq, keys_pad, u_b, v_b, tm_proj=512, tm_route=128, tm_mlp=512, te=2048)
        ys = _peer(ys, norm_ffn[layer], wq, keys_pad, u_b, v_b, tm_proj=bs, tm_route=bs, tm_mlp=bs, te=2048)
    y_prompt = _rmsnorm(yp, norm_final, tm=512).reshape(bp, seq_p, dm)
    y_sample = _rmsnorm(ys, norm_final, tm=bs).reshape(bs, seq_s, dm)
    order = ("k", "v", "logf", "wkv", "shift", "conv", "ssm")
    return ((y_prompt, y_sample) + tuple(jnp.stack(outs_p[name]) for name in order)
            + tuple(jnp.stack(outs_s[name]) for name in order))
```

```python
import functools
import math

import numpy as np
import jax
import jax.numpy as jnp
from jax import lax
from jax.experimental import pallas as pl
from jax.experimental.pallas import tpu as pltpu

F32 = jnp.float32
BF16 = jnp.bfloat16
I32 = jnp.int32

LANES = 128
SUBLANES = 8
V7X_VMEM_BYTES = 64 * 1024 * 1024

RMS_EPS = 1e-6
GN_EPS = 64e-5

FOX_HEADS = 8
FOX_HEAD_DIM = 64
FOX_WIDTH = FOX_HEADS * FOX_HEAD_DIM
RWKV_HEADS = 8
RWKV_HEAD_DIM = 64
RWKV_WIDTH = RWKV_HEADS * RWKV_HEAD_DIM
DECAY_LORA = 64
ICLR_LORA = 64
GATE_LORA = 128
RWKV_PROJ = 4 * RWKV_WIDTH - RWKV_WIDTH + DECAY_LORA + ICLR_LORA + GATE_LORA
HEAD_PAIRS = RWKV_HEADS // 2

SSM_HEAD_DIM = 64
D_STATE = 128
N_GROUPS = 8
CONV_W = 4
CHUNK = 128

N_KEYS = 128
PEER_HEADS = 8
PEER_TOPK = 16
D_KEY = 128


def _cparams(semantics, vmem_mb=48):
    return pltpu.CompilerParams(dimension_semantics=semantics, vmem_limit_bytes=vmem_mb * 1024 * 1024)


def _split3(x):
    hi = x.astype(BF16)
    r1 = x - hi.astype(F32)
    mid = r1.astype(BF16)
    lo = (r1 - mid.astype(F32)).astype(BF16)
    return hi, mid, lo


def _dot_f32_right(x, m_bf16):
    hi, mid, lo = _split3(x)
    d = lambda a: jnp.dot(a, m_bf16, preferred_element_type=F32)
    return d(hi) + d(mid) + d(lo)


def _dot_f32_left(m_bf16, x):
    hi, mid, lo = _split3(x)
    d = lambda a: jnp.dot(m_bf16, a, preferred_element_type=F32)
    return d(hi) + d(mid) + d(lo)


def _sigmoid(x):
    return 1.0 / (1.0 + jnp.exp(-x))


def _softplus(x):
    return jnp.maximum(x, 0.0) + jnp.log1p(jnp.exp(-jnp.abs(x)))


def _silu(x):
    return x * _sigmoid(x)


def _norm_matmul_kernel(x_ref, g_ref, w_ref, o_ref, xn_ref):
    @pl.when(pl.program_id(1) == 0)
    def _():
        x = x_ref[...]
        ms = jnp.mean(x * x, axis=-1, keepdims=True)
        xn_ref[...] = (x * lax.rsqrt(ms + RMS_EPS) * g_ref[...]).astype(BF16)

    o_ref[...] = jnp.dot(xn_ref[...], w_ref[...], preferred_element_type=F32).astype(o_ref.dtype)


def _norm_matmul(x, g, w_bf16, *, tm, tn, out_dtype=F32):
    m, k = x.shape
    n = w_bf16.shape[1]
    assert m % tm == 0 and n % tn == 0
    return pl.pallas_call(
        _norm_matmul_kernel,
        grid=(m // tm, n // tn),
        in_specs=[
            pl.BlockSpec((tm, k), lambda i, j: (i, 0)),
            pl.BlockSpec((1, k), lambda i, j: (0, 0)),
            pl.BlockSpec((k, tn), lambda i, j: (0, j)),
        ],
        out_specs=pl.BlockSpec((tm, tn), lambda i, j: (i, j)),
        out_shape=jax.ShapeDtypeStruct((m, n), out_dtype),
        scratch_shapes=[pltpu.VMEM((tm, k), BF16)],
        compiler_params=_cparams(("parallel", "arbitrary")),
        name="norm_matmul",
    )(x, g.reshape(1, k), w_bf16)


def _norm_matmul_t_kernel(x_ref, g_ref, wt_ref, *o_refs):
    x = x_ref[...]
    ms = jnp.mean(x * x, axis=-1, keepdims=True)
    xn = (x * lax.rsqrt(ms + RMS_EPS) * g_ref[...]).astype(BF16)
    res = lax.dot_general(wt_ref[...], xn, (((1,), (1,)), ((), ())), preferred_element_type=F32)
    rows = res.shape[0] // len(o_refs)
    for i, o_ref in enumerate(o_refs):
        o_ref[...] = res[i * rows:(i + 1) * rows]


def _norm_matmul_t(x, g, wt_bf16, *, tm, parts):
    m, k = x.shape
    c = wt_bf16.shape[0]
    assert c % parts == 0
    return pl.pallas_call(
        _norm_matmul_t_kernel,
        grid=(m // tm,),
        in_specs=[pl.BlockSpec((tm, k), lambda i: (i, 0)),
                  pl.BlockSpec((1, k), lambda i: (0, 0)),
                  pl.BlockSpec((c, k), lambda i: (0, 0))],
        out_specs=[pl.BlockSpec((c // parts, tm), lambda i: (0, i))] * parts,
        out_shape=[jax.ShapeDtypeStruct((c // parts, m), F32)] * parts,
        compiler_params=_cparams(("parallel",)),
        name="norm_matmul_t",
    )(x, g.reshape(1, k), wt_bf16)


def _matmul_res_kernel(*refs, n_in):
    a_refs = refs[:n_in]
    w_refs = refs[n_in:2 * n_in]
    r_ref, o_ref = refs[2 * n_in], refs[2 * n_in + 1]
    acc = r_ref[...]
    for a_ref, w_ref in zip(a_refs, w_refs):
        acc = acc + jnp.dot(a_ref[...].astype(BF16), w_ref[...], preferred_element_type=F32)
    o_ref[...] = acc


def _matmul_res(a_list, w_list, res, *, tm, tn):
    m, n = res.shape
    n_in = len(a_list)
    assert m % tm == 0 and n % tn == 0
    in_specs = ([pl.BlockSpec((tm, a.shape[1]), lambda i, j: (i, 0)) for a in a_list]
                + [pl.BlockSpec((w.shape[0], tn), lambda i, j: (0, j)) for w in w_list]
                + [pl.BlockSpec((tm, tn), lambda i, j: (i, j))])
    return pl.pallas_call(
        functools.partial(_matmul_res_kernel, n_in=n_in),
        grid=(m // tm, n // tn),
        in_specs=in_specs,
        out_specs=pl.BlockSpec((tm, tn), lambda i, j: (i, j)),
        out_shape=jax.ShapeDtypeStruct((m, n), F32),
        compiler_params=_cparams(("parallel", "parallel")),
        name="matmul_res",
    )(*a_list, *w_list, res)


def _rmsnorm_kernel(x_ref, g_ref, o_ref):
    x = x_ref[...]
    ms = jnp.mean(x * x, axis=-1, keepdims=True)
    o_ref[...] = x * lax.rsqrt(ms + RMS_EPS) * g_ref[...]


def _rmsnorm(x, g, *, tm):
    m, k = x.shape
    return pl.pallas_call(
        _rmsnorm_kernel,
        grid=(m // tm,),
        in_specs=[pl.BlockSpec((tm, k), lambda i: (i, 0)), pl.BlockSpec((1, k), lambda i: (0, 0))],
        out_specs=pl.BlockSpec((tm, k), lambda i: (i, 0)),
        out_shape=jax.ShapeDtypeStruct((m, k), F32),
        compiler_params=_cparams(("parallel",)),
        name="rmsnorm",
    )(x, g.reshape(1, k))


def _fox_prep_kernel(fl_ref, fb_ref, lf_ref, ct_ref, carry_ref, *, t):
    @pl.when(pl.program_id(0) == 0)
    def _():
        carry_ref[...] = jnp.zeros_like(carry_ref)

    x = fl_ref[...] + fb_ref[...]
    lf = jnp.minimum(x, 0.0) - jnp.log1p(jnp.exp(-jnp.abs(x)))
    lf_ref[...] = lf[:, :FOX_HEADS]
    row = lax.broadcasted_iota(I32, (t, t), 0)
    col = lax.broadcasted_iota(I32, (t, t), 1)
    tri = jnp.where(col <= row, 1.0, 0.0).astype(BF16)
    c = _dot_f32_left(tri, lf) + carry_ref[...]
    carry_ref[...] = c[t - 1:t, :]
    ct_ref[...] = c.T[:FOX_HEADS, :]


def _fox_prep(proj, fl_block, fb_pad, *, t):
    n = proj.shape[0]
    return pl.pallas_call(
        functools.partial(_fox_prep_kernel, t=t),
        grid=(n // t,),
        in_specs=[pl.BlockSpec((t, LANES), lambda i: (i, fl_block)),
                  pl.BlockSpec((1, LANES), lambda i: (0, 0))],
        out_specs=[pl.BlockSpec((t, FOX_HEADS), lambda i: (i, 0)),
                   pl.BlockSpec((FOX_HEADS, t), lambda i: (0, i))],
        out_shape=[jax.ShapeDtypeStruct((n, FOX_HEADS), F32),
                   jax.ShapeDtypeStruct((FOX_HEADS, n), F32)],
        scratch_shapes=[pltpu.VMEM((1, LANES), F32)],
        compiler_params=_cparams(("arbitrary",)),
        name="fox_prep",
    )(proj, fb_pad)


_SKIP_LOGIT_GAP = 110.0
_SKIP_NORM_SLACK = 1.05


def _fox_bounds_kernel(q_ref, k_ref, ct_ref, sel_ref, jmin_ref, qmax_ref, kmax_ref, cs_ref, ce_ref, *, tq, nq):
    i = pl.program_id(0)
    lane = lax.broadcasted_iota(I32, (FOX_HEADS, LANES), 1)
    sub = lax.broadcasted_iota(I32, (FOX_HEADS, LANES), 0)

    @pl.when(i == 0)
    def _():
        qmax_ref[...] = jnp.zeros_like(qmax_ref)
        kmax_ref[...] = jnp.zeros_like(kmax_ref)
        cs_ref[...] = jnp.zeros_like(cs_ref)
        ce_ref[...] = jnp.zeros_like(ce_ref)

    def head_sq_norm_max(x_ref):
        x = x_ref[...].astype(F32)
        sq = jnp.dot((x * x).astype(BF16), sel_ref[...], preferred_element_type=F32)
        return jnp.max(sq, axis=0, keepdims=True)

    qmax_ref[...] = jnp.maximum(qmax_ref[...], head_sq_norm_max(q_ref))
    kmax_ref[...] = jnp.maximum(kmax_ref[...], head_sq_norm_max(k_ref))
    cs_ref[...] = jnp.where(lane == i, ct_ref[:, 0:1], cs_ref[...])
    ce_ref[...] = jnp.where(lane == i, ct_ref[:, tq - 1:tq], ce_ref[...])

    @pl.when(i == nq - 1)
    def _():
        def to_col(row):
            return jnp.sum(jnp.where(sub == lane, row, 0.0), axis=-1, keepdims=True)

        b2 = (2.0 * _SKIP_NORM_SLACK * FOX_HEAD_DIM ** -0.5) * jnp.sqrt(to_col(qmax_ref[...]) * to_col(kmax_ref[...]))
        ce = ce_ref[...]
        out = jnp.zeros((FOX_HEADS, LANES), F32)
        for qi in range(nq):
            gap = b2 + cs_ref[:, qi:qi + 1] - ce
            skip = jnp.where((gap < -_SKIP_LOGIT_GAP) & (lane < qi), 1.0, 0.0)
            for p in range(FOX_HEADS // 2):
                both = skip[2 * p:2 * p + 1, :] * skip[2 * p + 1:2 * p + 2, :]
                count = jnp.sum(both, axis=-1, keepdims=True)
                out = jnp.where((sub == p) & (lane == qi), count, out)
        jmin_ref[...] = out.astype(I32)


def _fox_bounds(qkv_bf16, ct, *, tq):
    n = qkv_bf16.shape[0]
    nq = n // tq
    assert nq <= LANES
    sel = jnp.asarray((np.arange(FOX_WIDTH)[:, None] // FOX_HEAD_DIM == np.arange(LANES)[None, :]).astype(np.float32), BF16)
    table = pl.pallas_call(
        functools.partial(_fox_bounds_kernel, tq=tq, nq=nq),
        grid=(nq,),
        in_specs=[pl.BlockSpec((tq, FOX_WIDTH), lambda i: (i, 0)),
                  pl.BlockSpec((tq, FOX_WIDTH), lambda i: (i, 1)),
                  pl.BlockSpec((FOX_HEADS, tq), lambda i: (0, i)),
                  pl.BlockSpec(sel.shape, lambda i: (0, 0))],
        out_specs=pl.BlockSpec((FOX_HEADS, LANES), lambda i: (0, 0)),
        out_shape=jax.ShapeDtypeStruct((FOX_HEADS, LANES), I32),
        scratch_shapes=[pltpu.VMEM((1, LANES), F32), pltpu.VMEM((1, LANES), F32),
                        pltpu.VMEM((FOX_HEADS, LANES), F32), pltpu.VMEM((FOX_HEADS, LANES), F32)],
        compiler_params=_cparams(("arbitrary",)),
        name="fox_bounds",
    )(qkv_bf16, qkv_bf16, ct, sel)
    return table[:FOX_HEADS // 2, :nq]


def _fox_attn_kernel(jmin_ref, q_ref, k_ref, v_ref, ct_ref, o_ref, qs_ref, m_ref, l_ref, acc_ref, *, tq):
    i = pl.program_id(1)
    lane = lax.broadcasted_iota(I32, (1, LANES), 1)
    left = lane < FOX_HEAD_DIM
    q = q_ref[...] * jnp.asarray(FOX_HEAD_DIM ** -0.5, BF16)
    zero = jnp.zeros_like(q)
    qs_ref[0] = jnp.where(left, q, zero)
    qs_ref[1] = jnp.where(left, zero, q)
    q0 = pl.multiple_of(i * tq, tq)
    cref = [ct_ref[h:h + 1, pl.ds(q0, LANES)][:, 0:1] for h in range(2)]

    m_ref[...] = jnp.full_like(m_ref, -jnp.inf)
    l_ref[...] = jnp.zeros_like(l_ref)
    acc_ref[...] = jnp.zeros_like(acc_ref)
    ncol = tq // LANES

    def step(j, masked):
        k0 = pl.multiple_of(j * tq, tq)
        kb = k_ref[pl.ds(k0, tq), :]
        vb = v_ref[pl.ds(k0, tq), :]
        bias = [cref[h] - ct_ref[h:h + 1, pl.ds(k0, tq)] for h in range(2)]
        for h in range(2):
            s = lax.dot_general(qs_ref[h], kb, (((1,), (1,)), ((), ())), preferred_element_type=F32)
            s = s + bias[h]
            if masked:
                r = lax.broadcasted_iota(I32, (tq, tq), 0)
                c = lax.broadcasted_iota(I32, (tq, tq), 1)
                s = jnp.where(c <= r, s, -jnp.inf)
            cols = [s[:, c * LANES:(c + 1) * LANES] for c in range(ncol)]
            lane_max = functools.reduce(jnp.maximum, cols)
            m_old = m_ref[h]
            m_new = jnp.maximum(m_old, jnp.max(lane_max, axis=-1, keepdims=True))
            alpha = jnp.exp(m_old - m_new)
            ps = [jnp.exp(c - m_new) for c in cols]
            l_ref[h] = alpha * l_ref[h] + functools.reduce(jnp.add, ps)
            m_ref[h] = m_new
            p = jnp.concatenate([x.astype(BF16) for x in ps], axis=1)
            acc_ref[h] = alpha * acc_ref[h] + jnp.dot(p, vb, preferred_element_type=F32)

    def body(j, carry):
        step(j, False)
        return carry

    lax.fori_loop(jmin_ref[pl.program_id(0), i], i, body, 0)
    step(i, True)
    l0 = jnp.sum(l_ref[0], axis=-1, keepdims=True)
    l1 = jnp.sum(l_ref[1], axis=-1, keepdims=True)
    o_ref[...] = jnp.where(left, acc_ref[0] / l0, acc_ref[1] / l1).astype(o_ref.dtype)


def _fox_attn(qkv_bf16, ct, *, tq):
    n = qkv_bf16.shape[0]
    nb = FOX_WIDTH // LANES
    jmin = _fox_bounds(qkv_bf16, ct, tq=tq)
    return pl.pallas_call(
        functools.partial(_fox_attn_kernel, tq=tq),
        grid_spec=pltpu.PrefetchScalarGridSpec(
            num_scalar_prefetch=1,
            grid=(nb, n // tq),
            in_specs=[
                pl.BlockSpec((tq, LANES), lambda p, i, jm: (i, p)),
                pl.BlockSpec((n, LANES), lambda p, i, jm: (0, nb + p)),
                pl.BlockSpec((n, LANES), lambda p, i, jm: (0, 2 * nb + p)),
                pl.BlockSpec((None, 2, n), lambda p, i, jm: (p, 0, 0)),
            ],
            out_specs=pl.BlockSpec((tq, LANES), lambda p, i, jm: (i, p)),
            scratch_shapes=[pltpu.VMEM((2, tq, LANES), BF16), pltpu.VMEM((2, tq, LANES), F32),
                            pltpu.VMEM((2, tq, LANES), F32), pltpu.VMEM((2, tq, LANES), F32)],
        ),
        out_shape=jax.ShapeDtypeStruct((n, FOX_WIDTH), BF16),
        compiler_params=_cparams(("arbitrary", "arbitrary")),
        name="fox_attn",
    )(jmin, qkv_bf16, qkv_bf16, qkv_bf16, ct.reshape(nb, 2, n))


_RW = RWKV_WIDTH
_RWKV_PERM = np.concatenate([
    np.arange(0, _RW),
    np.arange(_RW + DECAY_LORA, 2 * _RW + DECAY_LORA),
    np.arange(2 * _RW + DECAY_LORA, 3 * _RW + DECAY_LORA),
    np.arange(3 * _RW + DECAY_LORA + ICLR_LORA, RWKV_PROJ),
    np.arange(_RW, _RW + DECAY_LORA),
    np.arange(3 * _RW + DECAY_LORA, 3 * _RW + DECAY_LORA + ICLR_LORA),
])
_RWKV_INV_PERM = np.argsort(_RWKV_PERM)


def _head_block_ones(width, head_dim):
    idx = np.arange(width) // head_dim
    return jnp.asarray((idx[:, None] == idx[None, :]).astype(np.float32), BF16)


def _rwkv_prep_math(p, p_prev, mu, w0, w2p, a0, a2p, g2, k_k, k_a, bones):
    xs = p + mu * (p_prev - p)
    r = xs[:, 0:_RW]
    k = xs[:, _RW:2 * _RW]
    v = xs[:, 2 * _RW:3 * _RW]
    gl = xs[:, 3 * _RW:3 * _RW + GATE_LORA]
    wa = xs[:, 3 * _RW + GATE_LORA:]
    w = -_softplus(-(w0 + jnp.dot(jnp.tanh(wa).astype(BF16), w2p, preferred_element_type=F32))) - 0.5
    decay = jnp.exp(-jnp.exp(w))
    a = _sigmoid(a0 + jnp.dot(wa.astype(BF16), a2p, preferred_element_type=F32))
    g = jnp.dot(_sigmoid(gl).astype(BF16), g2, preferred_element_type=F32)
    kkr = k * k_k
    ss = _dot_f32_right(kkr * kkr, bones)
    kk = kkr / jnp.maximum(jnp.sqrt(ss), 1e-12)
    k2 = k * (1.0 + (a - 1.0) * k_a)
    return r, k2, v, kk, kk * a, decay, g


def _rwkv_prep_kernel(p_ref, prev_ref, mu_ref, w0_ref, w2_ref, a0_ref, a2_ref, g2_ref, kk_ref, ka_ref, bones_ref,
                      r_o, k_o, v_o, kk_o, b_o, d_o, g_o, last_o, buf_ref, *, t, shifted):
    p = p_ref[...]
    if shifted:
        @pl.when(pl.program_id(0) == 0)
        def _():
            buf_ref[SUBLANES - 1:SUBLANES, :] = prev_ref[...]

        buf_ref[SUBLANES:SUBLANES + t, :] = p
        p_prev = buf_ref[SUBLANES - 1:SUBLANES - 1 + t, :]
        buf_ref[SUBLANES - 1:SUBLANES, :] = p[t - 1:t, :]
        last_o[...] = p[t - 1:t, :]
    else:
        p_prev = prev_ref[...]
        last_o[...] = p
    outs = _rwkv_prep_math(p, p_prev, mu_ref[...], w0_ref[...], w2_ref[...], a0_ref[...], a2_ref[...], g2_ref[...],
                           kk_ref[...], ka_ref[...], bones_ref[...])
    for o_ref, val in zip((r_o, k_o, v_o, kk_o, b_o, d_o, g_o), outs):
        o_ref[...] = val


def _rwkv_prep(proj, prev, mu, w0, w2p, a0, a2p, g2, k_k, k_a, bones, *, t, shifted):
    n = proj.shape[0]
    row = lambda w: pl.BlockSpec((1, w), lambda i: (0, 0))
    full = lambda a: pl.BlockSpec(a.shape, lambda i: (0, 0))
    tok = lambda w: pl.BlockSpec((t, w), lambda i: (i, 0))
    prev_spec = row(RWKV_PROJ) if shifted else tok(RWKV_PROJ)
    last_spec = row(RWKV_PROJ) if shifted else tok(RWKV_PROJ)
    last_shape = (1, RWKV_PROJ) if shifted else (n, RWKV_PROJ)
    return pl.pallas_call(
        functools.partial(_rwkv_prep_kernel, t=t, shifted=shifted),
        grid=(n // t,),
        in_specs=[tok(RWKV_PROJ), prev_spec, row(RWKV_PROJ), row(_RW), full(w2p), row(_RW), full(a2p), full(g2),
                  row(_RW), row(_RW), full(bones)],
        out_specs=[tok(_RW)] * 7 + [last_spec],
        out_shape=[jax.ShapeDtypeStruct((n, _RW), F32)] * 7 + [jax.ShapeDtypeStruct(last_shape, F32)],
        scratch_shapes=[pltpu.VMEM((SUBLANES + t, RWKV_PROJ), F32)],
        compiler_params=_cparams(("arbitrary",)),
        name="rwkv_prep",
    )(proj, prev, mu, w0, w2p, a0, a2p, g2, k_k, k_a, bones)


_SCAN_GROUP = 16


def _rwkv_scan_kernel(r_ref, k_ref, v_ref, kk_ref, b_ref, d_ref, s0_ref, o_ref, st_ref, s_ref, *, t):
    j = pl.program_id(1)

    @pl.when(j == 0)
    def _():
        s_ref[...] = s0_ref[...]

    left = lax.broadcasted_iota(I32, (RWKV_HEAD_DIM, LANES), 1) < RWKV_HEAD_DIM

    def seg(x):
        s0 = jnp.sum(jnp.where(left, x, 0.0), axis=-1, keepdims=True)
        s1 = jnp.sum(jnp.where(left, 0.0, x), axis=-1, keepdims=True)
        return jnp.where(left, s0, s1)

    group = _SCAN_GROUP if t % _SCAN_GROUP == 0 else t

    def readout(s, r_rows):
        s_bd = jnp.concatenate([jnp.where(left, s, 0.0), jnp.where(left, 0.0, s)], axis=0).astype(BF16)
        return lax.dot_general(r_rows, s_bd, (((1,), (1,)), ((), ())), preferred_element_type=F32)

    piece_pairs = ((0, 0), (0, 1), (0, 2), (1, 0), (1, 1), (2, 0))
    n_c = len(piece_pairs) * group
    kc = LANES * pl.cdiv(n_c, LANES)
    c_lane = lax.broadcasted_iota(I32, (LANES, kc), 1)
    c_step = c_lane % group
    c_live = c_lane < n_c
    pad_rows = jnp.zeros((kc - n_c, LANES), F32)

    def value_columns(v_t):
        v_p = [x.astype(F32) for x in _split3(v_t)]
        v_rows = jnp.concatenate([v_p[a] for a, _ in piece_pairs] + [pad_rows], axis=0)
        return v_rows.T

    def key_rows(k_t):
        k_p = [x.astype(F32) for x in _split3(k_t)]
        return jnp.concatenate([k_p[b] for _, b in piece_pairs] + [pad_rows], axis=0).astype(BF16)

    def outer_product(v_cols, k_rows, u):
        lhs = jnp.where(c_live & (c_step == u), v_cols, 0.0).astype(BF16)
        vk = jnp.dot(lhs, k_rows, preferred_element_type=F32)
        return jnp.where(left, vk[:RWKV_HEAD_DIM], vk[RWKV_HEAD_DIM:])

    n_groups = t // group

    def group_value_columns(gi):
        base = gi * group if isinstance(gi, int) else pl.multiple_of(gi * group, group)
        return tuple(value_columns(v_ref[pl.ds(base, group), p * LANES:(p + 1) * LANES]) for p in range(HEAD_PAIRS))

    def body(gi, carry):
        states, v_cols = carry
        base = pl.multiple_of(gi * group, group)
        states = list(states)
        tiles = []
        for p in range(HEAD_PAIRS):
            sl = slice(p * LANES, (p + 1) * LANES)
            tiles.append(tuple(ref[pl.ds(base, group), sl] for ref in (kk_ref, v_ref, d_ref, b_ref, k_ref, r_ref)))
        k_rows = [key_rows(tiles[p][4]) for p in range(HEAD_PAIRS)]
        vk_tiles = [[None] * group for _ in range(HEAD_PAIRS)]
        for u in range(group):
            for p in range(HEAD_PAIRS):
                vk_tiles[p][u] = outer_product(v_cols[p], k_rows[p], u)
        v_cols_next = group_value_columns(jnp.minimum(gi + 1, n_groups - 1))
        o_rows = [[] for _ in range(HEAD_PAIRS)]
        for u in range(group):
            row = lambda x: x[u:u + 1, :]
            for p in range(HEAD_PAIRS):
                kk_t, v_t, d_t, b_t, k_t, r_t = tiles[p]
                s = states[p]
                sk = seg(s * row(kk_t))
                s = s * row(d_t) - sk * row(b_t) + vk_tiles[p][u]
                states[p] = s
                r_rows = jnp.broadcast_to(r_t, (SUBLANES, LANES)) if group == 1 else r_t
                o_rows[p].append(readout(s, r_rows.astype(BF16))[u:u + 1, :])
        for p in range(HEAD_PAIRS):
            sl = slice(p * LANES, (p + 1) * LANES)
            o_ref[pl.ds(base, group), sl] = jnp.concatenate(o_rows[p], axis=0) if group > 1 else o_rows[p][0]
        return tuple(states), v_cols_next

    states, _ = lax.fori_loop(0, n_groups, body,
                              (tuple(s_ref[p] for p in range(HEAD_PAIRS)), group_value_columns(0)))
    for p in range(HEAD_PAIRS):
        s_ref[p] = states[p]

    @pl.when(j == pl.num_programs(1) - 1)
    def _():
        st_ref[...] = s_ref[...]


def _rwkv_scan(r, k, v, kk, b, d, s0_pairs, *, batch, t):
    n = r.shape[0]
    seq = n // batch
    tok = pl.BlockSpec((None, t, _RW), lambda bi, j: (bi, j, 0))
    st = pl.BlockSpec((None, HEAD_PAIRS, RWKV_HEAD_DIM, LANES), lambda bi, j: (bi, 0, 0, 0))
    o, s_t = pl.pallas_call(
        functools.partial(_rwkv_scan_kernel, t=t),
        grid=(batch, seq // t),
        in_specs=[tok] * 6 + [st],
        out_specs=[tok, st],
        out_shape=[jax.ShapeDtypeStruct((batch, seq, _RW), F32),
                   jax.ShapeDtypeStruct((batch, HEAD_PAIRS, RWKV_HEAD_DIM, LANES), F32)],
        scratch_shapes=[pltpu.VMEM((HEAD_PAIRS, RWKV_HEAD_DIM, LANES), F32)],
        compiler_params=_cparams(("arbitrary", "arbitrary")),
        name="rwkv_scan",
    )(*(x.reshape(batch, seq, _RW) for x in (r, k, v, kk, b, d)), s0_pairs)
    return o.reshape(n, _RW), s_t


def _rwkv_post_kernel(o_ref, r_ref, k_ref, v_ref, g_ref, lnw_ref, lnb_ref, rk_ref, bones_ref, out_ref):
    bones = bones_ref[...]
    inv = 1.0 / RWKV_HEAD_DIM
    o = o_ref[...]
    mean = _dot_f32_right(o, bones) * inv
    cen = o - mean
    var = _dot_f32_right(cen * cen, bones) * inv
    gn = cen * lax.rsqrt(var + GN_EPS) * lnw_ref[...] + lnb_ref[...]
    bonus = _dot_f32_right(r_ref[...] * k_ref[...] * rk_ref[...], bones) * v_ref[...]
    out_ref[...] = ((gn + bonus) * g_ref[...]).astype(out_ref.dtype)


def _rwkv_post(o, r, k, v, g, lnw, lnb, rk, bones, *, t):
    n = o.shape[0]
    tok = pl.BlockSpec((t, _RW), lambda i: (i, 0))
    row = pl.BlockSpec((1, _RW), lambda i: (0, 0))
    return pl.pallas_call(
        _rwkv_post_kernel,
        grid=(n // t,),
        in_specs=[tok] * 5 + [row] * 3 + [pl.BlockSpec(bones.shape, lambda i: (0, 0))],
        out_specs=tok,
        out_shape=jax.ShapeDtypeStruct((n, _RW), BF16),
        compiler_params=_cparams(("parallel",)),
        name="rwkv_post",
    )(o, r, k, v, g, lnw, lnb, rk, bones)


def _to_pairs(s):
    b = s.shape[0]
    return s.reshape(b, HEAD_PAIRS, 2, RWKV_HEAD_DIM, RWKV_HEAD_DIM).transpose(0, 1, 3, 2, 4).reshape(
        b, HEAD_PAIRS, RWKV_HEAD_DIM, LANES)


def _from_pairs(s):
    b = s.shape[0]
    return s.reshape(b, HEAD_PAIRS, RWKV_HEAD_DIM, 2, RWKV_HEAD_DIM).transpose(0, 1, 3, 2, 4).reshape(
        b, RWKV_HEADS, RWKV_HEAD_DIM, RWKV_HEAD_DIM)


def _rwkv_mix(proj, prev, s0, rw, *, batch, t_prep, t_scan, shifted):
    mu, w0, w2, a0, a2, g2, k_k, k_a, r_k, ln_w, ln_b = rw
    bones = _head_block_ones(_RW, RWKV_HEAD_DIM)
    zeros = jnp.zeros((LANES - DECAY_LORA, _RW), F32)
    w2p = jnp.concatenate([w2, zeros], axis=0).astype(BF16)
    a2p = jnp.concatenate([zeros, a2], axis=0).astype(BF16)
    row = lambda x: x.reshape(1, -1)
    r, k, v, kk, b, d, g, last = _rwkv_prep(
        proj, prev, row(mu[_RWKV_PERM]), row(w0), w2p, row(a0), a2p, g2.astype(BF16), row(k_k), row(k_a), bones,
        t=t_prep, shifted=shifted)
    o, s_t = _rwkv_scan(r, k, v, kk, b, d, _to_pairs(s0), batch=batch, t=t_scan)
    out = _rwkv_post(o, r, k, v, g, row(ln_w), row(ln_b), row(r_k), bones, t=t_prep)
    return out, last[:, _RWKV_INV_PERM], _from_pairs(s_t)


def _top_rows(s, k, order):
    vals, ords = [], []
    for _ in range(k):
        m = jnp.max(s, axis=0, keepdims=True)
        pick = jnp.min(jnp.where(s == m, order, jnp.inf), axis=0, keepdims=True)
        vals.append(m)
        ords.append(pick)
        s = jnp.where(order == pick, -jnp.inf, s)
    return jnp.concatenate(vals, axis=0), jnp.concatenate(ords, axis=0)


def _take_rows(table, pos):
    out = jnp.zeros(pos.shape, table.dtype)
    for i in range(table.shape[0]):
        out = jnp.where(pos == float(i), table[i:i + 1, :], out)
    return out


_CAND_GROUPS = ((0, 2, 16), (2, 4, 8))
_CAND_TAIL_I0, _CAND_TAIL_J = 4, 3
_CAND_HEAD_ROWS = sum((hi - lo) * nj for lo, hi, nj in _CAND_GROUPS)
_CAND_ROWS = _CAND_HEAD_ROWS + _CAND_TAIL_J * PEER_TOPK


def _candidate_tables(tm):
    k = PEER_TOPK
    assert k == 16 and _CAND_GROUPS == ((0, 2, 16), (2, 4, 8))
    r = lax.broadcasted_iota(I32, (_CAND_ROWS, tm), 0)
    r2 = r - 2 * k
    flat2 = (2 + (r2 >> 3)) * k + (r2 & 7)
    r3 = r - _CAND_HEAD_ROWS
    i3 = r3 & (k - 1)
    flat3 = i3 * k + (r3 >> 4)
    flat = jnp.where(r < 2 * k, r, jnp.where(r < _CAND_HEAD_ROWS, flat2, flat3))
    valid = (r < _CAND_HEAD_ROWS) | (i3 >= _CAND_TAIL_I0)
    return flat.astype(F32), valid


def _peer_route_kernel(q_ref, keys_ref, g_ref, e1_ref, e2_ref, gv_ref, gt_ref, *, tm):
    k = PEER_TOPK
    flat, valid = _candidate_tables(tm)
    key_order = lax.broadcasted_iota(I32, (N_KEYS, tm), 0).astype(F32)

    e1s, e2s, gs = [], [], []
    for h in range(PEER_HEADS):
        qh = q_ref[:, h * D_KEY:(h + 1) * D_KEY].astype(BF16)
        score = lambda p: lax.dot_general(keys_ref[h, p], qh, (((1,), (1,)), ((), ())), preferred_element_type=F32)
        v1, i1 = _top_rows(score(0), k, key_order)
        v2, i2 = _top_rows(score(1), k, key_order)
        parts = [v1[i:i + 1, :] + v2[:nj, :] for lo, hi, nj in _CAND_GROUPS for i in range(lo, hi)]
        parts += [v1 + v2[j:j + 1, :] for j in range(_CAND_TAIL_J)]
        cand = jnp.where(valid, jnp.concatenate(parts, axis=0), -jnp.inf)
        sc, pos = _top_rows(cand, k, flat)
        pos_i = jnp.floor(pos * (1.0 / k))
        e1s.append(_take_rows(i1, pos_i))
        e2s.append(_take_rows(i2, pos - k * pos_i))
        ex = jnp.exp(sc - sc[0:1, :])
        gs.append(ex / jnp.sum(ex, axis=0, keepdims=True))
    e1_ref[...] = jnp.concatenate(e1s, axis=0).T
    e2_ref[...] = jnp.concatenate(e2s, axis=0).T
    gv_ref[...] = jnp.concatenate(gs, axis=0).T

    eid = lax.broadcasted_iota(I32, (N_KEYS, PEER_HEADS * k), 0).astype(F32)
    rows = _ROUTE_GROUP

    def body(gi, carry):
        for half in range(_ROUTE_UNROLL):
            base = pl.multiple_of((gi * _ROUTE_UNROLL + half) * rows, rows)
            stage = half * rows * _ROUTE_PITCH
            e1_t = e1_ref[pl.ds(base, rows), :]
            e2_t = e2_ref[pl.ds(base, rows), :]
            gv_t = gv_ref[pl.ds(base, rows), :]
            for u in range(rows):
                a = jnp.where(eid == e1_t[u:u + 1, :], gv_t[u:u + 1, :], 0.0).astype(BF16)
                b = jnp.where(eid == e2_t[u:u + 1, :], 1.0, 0.0).astype(BF16)
                gt_ref[stage + u * _ROUTE_PITCH:stage + u * _ROUTE_PITCH + N_KEYS, :] = lax.dot_general(
                    a, b, (((1,), (1,)), ((), ())), preferred_element_type=F32)
        for half in range(_ROUTE_UNROLL):
            base = pl.multiple_of((gi * _ROUTE_UNROLL + half) * rows, rows)
            stage = half * rows * _ROUTE_PITCH
            for e1 in range(N_KEYS):
                slab = gt_ref[pl.ds(stage + e1, rows, stride=_ROUTE_PITCH), :]
                g_ref[pl.ds(base, rows), e1 * N_KEYS:(e1 + 1) * N_KEYS] = slab.astype(g_ref.dtype)
        return carry

    lax.fori_loop(0, tm // (rows * _ROUTE_UNROLL), body, 0)


_ROUTE_GROUP = 2 * SUBLANES
_ROUTE_UNROLL = 2
_ROUTE_PITCH = N_KEYS + SUBLANES


def _peer_route(q, keys_pad, *, tm):
    n = q.shape[0]
    width = PEER_HEADS * PEER_TOPK
    assert tm % (_ROUTE_GROUP * _ROUTE_UNROLL) == 0
    return pl.pallas_call(
        functools.partial(_peer_route_kernel, tm=tm),
        grid=(n // tm,),
        in_specs=[pl.BlockSpec((tm, PEER_HEADS * D_KEY), lambda i: (i, 0)),
                  pl.BlockSpec(keys_pad.shape, lambda i: (0, 0, 0, 0))],
        out_specs=pl.BlockSpec((tm, N_KEYS * N_KEYS), lambda i: (i, 0)),
        out_shape=jax.ShapeDtypeStruct((n, N_KEYS * N_KEYS), BF16),
        scratch_shapes=[pltpu.VMEM((tm, width), F32)] * 3
                       + [pltpu.VMEM((_ROUTE_UNROLL * _ROUTE_GROUP * _ROUTE_PITCH, N_KEYS), F32)],
        compiler_params=_cparams(("parallel",)),
        name="peer_route",
    )(q, keys_pad)


def _peer_mlp_kernel(x_ref, gain_ref, u_ref, v_ref, g_ref, o_ref, xn_ref, acc_ref):
    j = pl.program_id(1)

    @pl.when(j == 0)
    def _():
        x = x_ref[...]
        ms = jnp.mean(x * x, axis=-1, keepdims=True)
        xn_ref[...] = (x * lax.rsqrt(ms + RMS_EPS) * gain_ref[...]).astype(BF16)
        acc_ref[...] = jnp.zeros_like(acc_ref)

    h = lax.dot_general(xn_ref[...], u_ref[...], (((1,), (1,)), ((), ())), preferred_element_type=F32)
    act = 0.5 * h * (1.0 + lax.erf(h * (2.0 ** -0.5)))
    w = (act * g_ref[...].astype(F32)).astype(BF16)
    acc_ref[...] += jnp.dot(w, v_ref[...], preferred_element_type=F32)

    @pl.when(j == pl.num_programs(1) - 1)
    def _():
        o_ref[...] = x_ref[...] + acc_ref[...]


def _peer_mlp(x, gain, u_bf16, v_bf16, gmap, *, layer, tm, te):
    n, dm = x.shape
    ne = u_bf16.shape[1]
    return pl.pallas_call(
        _peer_mlp_kernel,
        grid=(n // tm, ne // te),
        in_specs=[pl.BlockSpec((tm, dm), lambda i, j: (i, 0)),
                  pl.BlockSpec((1, dm), lambda i, j: (0, 0)),
                  pl.BlockSpec((None, te, dm), lambda i, j: (layer, j, 0)),
                  pl.BlockSpec((None, te, dm), lambda i, j: (layer, j, 0)),
                  pl.BlockSpec((tm, te), lambda i, j: (i, j))],
        out_specs=pl.BlockSpec((tm, dm), lambda i, j: (i, 0)),
        out_shape=jax.ShapeDtypeStruct((n, dm), F32),
        scratch_shapes=[pltpu.VMEM((tm, dm), BF16), pltpu.VMEM((tm, dm), F32)],
        compiler_params=_cparams(("parallel", "arbitrary")),
        name="peer_mlp",
    )(x, gain.reshape(1, dm), u_bf16, v_bf16, gmap)


def _peer_keys_padded(keys):
    z = jnp.zeros_like(keys[:, 0])
    first = jnp.concatenate([keys[:, 0], z], axis=-1)
    second = jnp.concatenate([z, keys[:, 1]], axis=-1)
    return jnp.stack([first, second], axis=1).astype(BF16)


def _peer(y, gain, wq_bf16, keys_pad, u_bf16, v_bf16, *, layer, tm_proj, tm_route, tm_mlp, te):
    q = _norm_matmul(y, gain, wq_bf16, tm=tm_proj, tn=wq_bf16.shape[1])
    gmap = _peer_route(q, keys_pad, tm=tm_route)
    return _peer_mlp(y, gain, u_bf16, v_bf16, gmap, layer=layer, tm=tm_mlp, te=te)


def _conv_silu(x, buf_ref, w_ref, b_ref, q):
    buf_ref[SUBLANES:SUBLANES + q, :] = x
    acc = b_ref[...] + x * w_ref[CONV_W - 1:CONV_W, :]
    for back in range(1, CONV_W):
        acc = acc + buf_ref[SUBLANES - back:SUBLANES - back + q, :] * w_ref[CONV_W - 1 - back:CONV_W - back, :]
    tail = buf_ref[q + SUBLANES - (CONV_W - 1):q + SUBLANES, :]
    buf_ref[SUBLANES - (CONV_W - 1):SUBLANES, :] = tail
    return _silu(acc), tail


def _ssd_kernel(z_ref, x_ref, bc_ref, dt_ref, cwx_ref, cwbc_ref, cbx_ref, cbbc_ref, dtb_ref, alog_ref, dskip_ref,
                nw_ref, expand_ref, y_ref, convx_ref, convbc_ref, ht_ref, bufx_ref, bufbc_ref, h_ref, *, q, heads):
    hpg = heads // N_GROUPS
    gw = hpg * SSM_HEAD_DIM

    @pl.when(pl.program_id(0) == 0)
    def _():
        bufx_ref[...] = jnp.zeros_like(bufx_ref)
        bufbc_ref[...] = jnp.zeros_like(bufbc_ref)
        h_ref[...] = jnp.zeros_like(h_ref)

    xs, tail_x = _conv_silu(x_ref[...], bufx_ref, cwx_ref, cbx_ref, q)
    bc, tail_bc = _conv_silu(bc_ref[...], bufbc_ref, cwbc_ref, cbbc_ref, q)
    convx_ref[...] = tail_x
    convbc_ref[...] = tail_bc

    dt = _softplus(dt_ref[...] + dtb_ref[...])
    a = -jnp.exp(alog_ref[...])
    row = lax.broadcasted_iota(I32, (q, q), 0)
    col = lax.broadcasted_iota(I32, (q, q), 1)
    lower = col <= row
    tri = jnp.where(lower, 1.0, 0.0).astype(BF16)
    acs = _dot_f32_left(tri, dt * a)
    acs_t = acs.T
    expand = expand_ref[...]
    dt_e = _dot_f32_right(dt, expand)
    acs_e = _dot_f32_right(acs, expand)
    last_e = acs_e[q - 1:q, :]
    xc = xs * dt_e
    xcd = (xc * jnp.exp(last_e - acs_e)).astype(BF16)
    xc_b = xc.astype(BF16)
    grow = jnp.exp(acs_e)
    chunk_decay = jnp.exp(last_e)

    nbc = N_GROUPS * D_STATE
    for g in range(N_GROUPS):
        bm = bc[:, g * D_STATE:(g + 1) * D_STATE].astype(BF16)
        cm = bc[:, nbc + g * D_STATE:nbc + (g + 1) * D_STATE].astype(BF16)
        cb = lax.dot_general(cm, bm, (((1,), (1,)), ((), ())), preferred_element_type=F32)
        sl = slice(g * gw, (g + 1) * gw)
        h_prev = h_ref[g]
        y_off = jnp.dot(cm, h_prev.astype(BF16), preferred_element_type=F32) * grow[:, sl]
        y_diag = []
        for r in range(hpg):
            hd = g * hpg + r
            diff = acs[:, hd:hd + 1] - acs_t[hd:hd + 1, :]
            m = (cb * jnp.exp(jnp.where(lower, diff, -jnp.inf))).astype(BF16)
            y_diag.append(jnp.dot(m, xc_b[:, hd * SSM_HEAD_DIM:(hd + 1) * SSM_HEAD_DIM], preferred_element_type=F32))
        y_g = jnp.concatenate(y_diag, axis=1) + y_off + dskip_ref[:, sl] * xs[:, sl]
        states = jnp.dot(bm.T, xcd[:, sl], preferred_element_type=F32)
        h_ref[g] = h_prev * chunk_decay[:, sl] + states
        zg = z_ref[:, sl]
        y_g = y_g * _silu(zg)
        ms = jnp.mean(y_g * y_g, axis=-1, keepdims=True)
        y_ref[:, sl] = (y_g * lax.rsqrt(ms + RMS_EPS) * nw_ref[:, sl]).astype(y_ref.dtype)

    @pl.when(pl.program_id(0) == pl.num_programs(0) - 1)
    def _():
        ht_ref[...] = h_ref[...]


def _ssd(proj, conv_w, conv_b, dt_bias, a_log, d_skip, norm_w, *, heads):
    n = proj.shape[0]
    d_inner = heads * SSM_HEAD_DIM
    q = CHUNK
    nbc = 2 * N_GROUPS * D_STATE
    assert nbc == d_inner and n % q == 0
    gw = d_inner // N_GROUPS
    pad = lambda x: jnp.pad(x, (0, LANES - heads)).reshape(1, LANES)
    expand = jnp.asarray((np.arange(LANES)[:, None] == (np.arange(d_inner) // SSM_HEAD_DIM)[None, :]).astype(np.float32), BF16)
    blk = lambda w, j: pl.BlockSpec((q, w), lambda i, j=j: (i, j))
    full = lambda a: pl.BlockSpec(a.shape, lambda i: (0,) * a.ndim)
    small = [conv_w[:, :d_inner], conv_w[:, d_inner:], conv_b[:d_inner].reshape(1, -1), conv_b[d_inner:].reshape(1, -1),
             pad(dt_bias), pad(a_log), jnp.repeat(d_skip, SSM_HEAD_DIM).reshape(1, -1), norm_w.reshape(1, -1), expand]
    y, cx, cbc, ht = pl.pallas_call(
        functools.partial(_ssd_kernel, q=q, heads=heads),
        grid=(n // q,),
        in_specs=[blk(d_inner, 0), blk(d_inner, 1), blk(d_inner, 2), blk(LANES, 3 * d_inner // LANES)]
                 + [full(a) for a in small],
        out_specs=[pl.BlockSpec((q, d_inner), lambda i: (i, 0)),
                   pl.BlockSpec((CONV_W - 1, d_inner), lambda i: (0, 0)),
                   pl.BlockSpec((CONV_W - 1, nbc), lambda i: (0, 0)),
                   pl.BlockSpec((N_GROUPS, D_STATE, gw), lambda i: (0, 0, 0))],
        out_shape=[jax.ShapeDtypeStruct((n, d_inner), BF16),
                   jax.ShapeDtypeStruct((CONV_W - 1, d_inner), F32),
                   jax.ShapeDtypeStruct((CONV_W - 1, nbc), F32),
                   jax.ShapeDtypeStruct((N_GROUPS, D_STATE, gw), F32)],
        scratch_shapes=[pltpu.VMEM((SUBLANES + q, d_inner), F32), pltpu.VMEM((SUBLANES + q, nbc), F32),
                        pltpu.VMEM((N_GROUPS, D_STATE, gw), F32)],
        compiler_params=_cparams(("arbitrary",)),
        name="ssd",
    )(proj, proj, proj, proj, *small)
    hpg = heads // N_GROUPS
    h_t = ht.reshape(N_GROUPS, D_STATE, hpg, SSM_HEAD_DIM).transpose(0, 2, 3, 1).reshape(heads, SSM_HEAD_DIM, D_STATE)
    return y, jnp.concatenate([cx, cbc], axis=1), h_t


def _ssm_step_kernel(z_ref, x_ref, bc_ref, dt_ref, bufx_ref, bufbc_ref, h0_ref, cwx_ref, cwbc_ref, cbx_ref, cbbc_ref,
                     dtb_ref, alog_ref, dskip_ref, nw_ref, y_ref, nbufx_ref, nbufbc_ref, h_ref, *, heads):
    hpg = heads // N_GROUPS
    gw = hpg * SSM_HEAD_DIM
    nbc = N_GROUPS * D_STATE
    fullx = jnp.concatenate([bufx_ref[...], x_ref[...]], axis=1)
    nbufx_ref[...] = fullx[:, 1:]
    xs = _silu(cbx_ref[...] + jnp.sum(fullx * cwx_ref[...], axis=1, keepdims=True))
    fullbc = jnp.concatenate([bufbc_ref[...], bc_ref[...]], axis=0)
    nbufbc_ref[...] = fullbc[1:, :]
    bc = _silu(cbbc_ref[...] + jnp.sum(fullbc * cwbc_ref[...], axis=0, keepdims=True))
    dt = _softplus(dt_ref[...] + dtb_ref[...])
    dec = jnp.exp(dt * (-jnp.exp(alog_ref[...])))
    xdt = xs * dt
    for g in range(N_GROUPS):
        rows = slice(g * gw, (g + 1) * gw)
        bm = bc[:, g * D_STATE:(g + 1) * D_STATE]
        cm = bc[:, nbc + g * D_STATE:nbc + (g + 1) * D_STATE]
        h_new = h0_ref[rows, :] * dec[rows, :] + xdt[rows, :] * bm
        h_ref[rows, :] = h_new
        y = jnp.sum(h_new * cm, axis=-1, keepdims=True) + dskip_ref[rows, :] * xs[rows, :]
        y = y * _silu(z_ref[rows, :])
        ms = jnp.mean(y * y, axis=0, keepdims=True)
        y_ref[rows, :] = y * lax.rsqrt(ms + RMS_EPS) * nw_ref[rows, :]


def _ssm_step(proj, conv_buf, h0, conv_w, conv_b, dt_bias, a_log, d_skip, norm_w, *, heads):
    b = proj.shape[0]
    d_inner = heads * SSM_HEAD_DIM
    nbc2 = 2 * N_GROUPS * D_STATE
    col = lambda x: x.reshape(b, d_inner, 1)
    per_head = lambda v: jnp.repeat(v, SSM_HEAD_DIM, axis=-1)
    z = col(proj[:, :d_inner])
    x = col(proj[:, d_inner:2 * d_inner])
    bc = proj[:, 2 * d_inner:2 * d_inner + nbc2].reshape(b, 1, nbc2)
    dt = col(per_head(proj[:, 2 * d_inner + nbc2:2 * d_inner + nbc2 + heads]))
    bufx = conv_buf[:, :, :d_inner].transpose(0, 2, 1)
    bufbc = conv_buf[:, :, d_inner:]
    pcol = lambda v: v.reshape(d_inner, 1)
    params = [conv_w[:, :d_inner].T, conv_w[:, d_inner:], pcol(conv_b[:d_inner]), conv_b[d_inner:].reshape(1, nbc2),
              pcol(per_head(dt_bias)), pcol(per_head(a_log)), pcol(per_head(d_skip)), pcol(norm_w)]
    seq = lambda *shape: pl.BlockSpec((None,) + shape, lambda i: (i,) + (0,) * len(shape))
    full = lambda a: pl.BlockSpec(a.shape, lambda i: (0,) * a.ndim)
    y, nbx, nbbc, h = pl.pallas_call(
        functools.partial(_ssm_step_kernel, heads=heads),
        grid=(b,),
        in_specs=[seq(d_inner, 1), seq(d_inner, 1), seq(1, nbc2), seq(d_inner, 1), seq(d_inner, CONV_W - 1),
                  seq(CONV_W - 1, nbc2), seq(d_inner, D_STATE)] + [full(a) for a in params],
        out_specs=[seq(d_inner, 1), seq(d_inner, CONV_W - 1), seq(CONV_W - 1, nbc2), seq(d_inner, D_STATE)],
        out_shape=[jax.ShapeDtypeStruct((b, d_inner, 1), F32), jax.ShapeDtypeStruct((b, d_inner, CONV_W - 1), F32),
                   jax.ShapeDtypeStruct((b, CONV_W - 1, nbc2), F32), jax.ShapeDtypeStruct((b, d_inner, D_STATE), F32)],
        compiler_params=_cparams(("parallel",)),
        name="ssm_step",
    )(z, x, bc, dt, bufx, bufbc, h0.reshape(b, d_inner, D_STATE), *params)
    new_buf = jnp.concatenate([nbx.transpose(0, 2, 1), nbbc], axis=2)
    return y.reshape(b, d_inner), new_buf, h.reshape(b, heads, SSM_HEAD_DIM, D_STATE)


def _fox_decode_kernel(pt_ref, q_ref, kc_ref, vc_ref, lfc_ref, *refs, pages):
    k_refs = refs[:pages]
    v_refs = refs[pages:2 * pages]
    lf_refs = refs[2 * pages:3 * pages]
    o_ref, qb_ref, m_ref, l_ref, acc_ref, later_ref = refs[3 * pages:]
    j = pl.program_id(1)
    first_lane = lax.broadcasted_iota(I32, (1, LANES), 1) == 0

    @pl.when(j == 0)
    def _():
        s_rows = []
        for h in range(FOX_HEADS):
            qh = q_ref[h] * (FOX_HEAD_DIM ** -0.5)
            qb_ref[h] = jnp.broadcast_to(qh, (FOX_HEAD_DIM, LANES))
            s_rows.append(jnp.sum(qh * kc_ref[h], axis=0, keepdims=True))
            acc_ref[h] = jnp.where(first_lane, vc_ref[h], 0.0)
        m_ref[...] = jnp.broadcast_to(jnp.concatenate(s_rows, axis=0), m_ref.shape)
        l_ref[...] = jnp.broadcast_to(jnp.where(first_lane, 1.0, 0.0), l_ref.shape)
        later_ref[...] = jnp.broadcast_to(lfc_ref[...], later_ref.shape)

    urow = lax.broadcasted_iota(I32, (LANES, LANES), 0)
    tcol = lax.broadcasted_iota(I32, (LANES, LANES), 1)
    after = jnp.where(urow > tcol, 1.0, 0.0).astype(BF16)
    for u in range(pages):
        lf = lf_refs[u][...]
        later = later_ref[...]
        bias = _dot_f32_right(lf, after) + later
        rows = [jnp.sum(k_refs[u][h] * qb_ref[h], axis=0, keepdims=True) for h in range(FOX_HEADS)]
        s = jnp.concatenate(rows, axis=0) + bias
        m_old = m_ref[...]
        m_new = jnp.maximum(m_old, jnp.max(s, axis=-1, keepdims=True))
        alpha = jnp.exp(m_old - m_new)
        p = jnp.exp(s - m_new)
        l_ref[...] = alpha * l_ref[...] + p
        m_ref[...] = m_new
        for h in range(FOX_HEADS):
            acc_ref[h] = alpha[h:h + 1, :] * acc_ref[h] + p[h:h + 1, :] * v_refs[u][h]
        later_ref[...] = later + jnp.sum(lf, axis=-1, keepdims=True)

    @pl.when(j == pl.num_programs(1) - 1)
    def _():
        l = jnp.sum(l_ref[...], axis=-1, keepdims=True)
        for h in range(FOX_HEADS):
            o_ref[h] = jnp.sum(acc_ref[h], axis=-1, keepdims=True) / l[h:h + 1, :]


def _fox_decode(q, k_cur, v_cur, logf_cur, cache_k, cache_v, cache_logf, page_table, *, pages):
    b, n_pages = page_table.shape
    page = cache_k.shape[1]
    assert page == LANES and n_pages % pages == 0
    kt = cache_k.transpose(0, 2, 3, 1)
    vt = cache_v.transpose(0, 2, 3, 1)
    lft = cache_logf.transpose(0, 2, 1)
    col = lambda x: x.reshape(b, FOX_HEADS, FOX_HEAD_DIM, 1)
    tok = pl.BlockSpec((None, FOX_HEADS, FOX_HEAD_DIM, 1), lambda bi, j, pt: (bi, 0, 0, 0))
    newest_first = lambda bi, j, pt, u: pt[bi, n_pages - 1 - (j * pages + u)]
    kv_page = lambda u: pl.BlockSpec((None, FOX_HEADS, FOX_HEAD_DIM, page),
                                     lambda bi, j, pt, u=u: (newest_first(bi, j, pt, u), 0, 0, 0))
    lf_page = lambda u: pl.BlockSpec((None, FOX_HEADS, page), lambda bi, j, pt, u=u: (newest_first(bi, j, pt, u), 0, 0))
    out = pl.pallas_call(
        functools.partial(_fox_decode_kernel, pages=pages),
        grid_spec=pltpu.PrefetchScalarGridSpec(
            num_scalar_prefetch=1,
            grid=(b, n_pages // pages),
            in_specs=[tok, tok, tok, pl.BlockSpec((None, FOX_HEADS, 1), lambda bi, j, pt: (bi, 0, 0))]
                     + [kv_page(u) for u in range(pages)] * 2 + [lf_page(u) for u in range(pages)],
            out_specs=tok,
            scratch_shapes=[pltpu.VMEM((FOX_HEADS, FOX_HEAD_DIM, LANES), F32), pltpu.VMEM((FOX_HEADS, LANES), F32),
                            pltpu.VMEM((FOX_HEADS, LANES), F32), pltpu.VMEM((FOX_HEADS, FOX_HEAD_DIM, LANES), F32),
                            pltpu.VMEM((FOX_HEADS, LANES), F32)],
        ),
        out_shape=jax.ShapeDtypeStruct((b, FOX_HEADS, FOX_HEAD_DIM, 1), F32),
        compiler_params=_cparams(("parallel", "arbitrary")),
        name="fox_decode",
    )(page_table, col(q), col(k_cur), col(v_cur), logf_cur.reshape(b, FOX_HEADS, 1),
      *([kt] * pages), *([vt] * pages), *([lft] * pages))
    return out.reshape(b, FOX_WIDTH)


_COL_Q = RWKV_PROJ
_COL_K = _COL_Q + FOX_WIDTH
_COL_V = _COL_K + FOX_WIDTH
_COL_FL = _COL_V + FOX_WIDTH
_EVEN_COLS = _COL_FL + LANES


def _even_weights(w_in):
    q, k, v = (w_in[:, i * FOX_WIDTH:(i + 1) * FOX_WIDTH] for i in range(3))
    fl = w_in[:, 3 * FOX_WIDTH:3 * FOX_WIDTH + FOX_HEADS]
    rw = w_in[:, 3 * FOX_WIDTH + FOX_HEADS:][:, _RWKV_PERM]
    fl_pad = jnp.pad(fl, ((0, 0), (0, LANES - FOX_HEADS)))
    return jnp.concatenate([rw, q, k, v, fl_pad], axis=1).astype(BF16)


def _even_layer(y, gain, w_cat, w_out, fb, rw, *, batch, seq, past, shift0, wkv0):
    n = y.shape[0]
    tm = min(512, n)
    proj = _norm_matmul(y, gain, w_cat, tm=min(1024, n), tn=_EVEN_COLS)
    fb_pad = jnp.pad(fb, (0, LANES - FOX_HEADS)).reshape(1, LANES)
    logf, ct = _fox_prep(proj, _COL_FL // LANES, fb_pad, t=min(256, n))
    heads = lambda t: t.reshape(batch, seq, FOX_HEADS, FOX_HEAD_DIM)
    if past is None:
        k_t, v_t = _norm_matmul_t(y, gain, w_cat[:, _COL_K:_COL_FL].T, tm=1024, parts=2)
        tok_major = lambda x: x.reshape(FOX_HEADS, FOX_HEAD_DIM, n).transpose(2, 0, 1).reshape(
            batch, seq, FOX_HEADS, FOX_HEAD_DIM)
        k_out, v_out = tok_major(k_t), tok_major(v_t)
    else:
        k_new = proj[:, _COL_K:_COL_V]
        v_new = proj[:, _COL_V:_COL_FL]
        k_out, v_out = heads(k_new), heads(v_new)
    if past is None:
        qkv = proj[:, _COL_Q:_COL_FL].astype(BF16)
        o_fox = _fox_attn(qkv, ct, tq=1024)
        o_rwkv, shift_new, wkv_new = _rwkv_mix(proj, shift0[:, _RWKV_PERM], wkv0, rw, batch=batch,
                                               t_prep=256, t_scan=256, shifted=True)
    else:
        cache_k, cache_v, cache_logf, page_table = past
        o_fox = _fox_decode(proj[:, _COL_Q:_COL_K], k_new, v_new, logf, cache_k, cache_v, cache_logf, page_table,
                            pages=16)
        o_rwkv, shift_new, wkv_new = _rwkv_mix(proj, shift0[:, _RWKV_PERM], wkv0, rw, batch=batch,
                                               t_prep=n, t_scan=1, shifted=False)
    w_out_b = w_out.astype(BF16)
    y = _matmul_res([o_fox, o_rwkv], [w_out_b[:FOX_WIDTH], w_out_b[FOX_WIDTH:]], y, tm=tm, tn=y.shape[1])
    return y, k_out, v_out, logf.reshape(batch, seq, FOX_HEADS), wkv_new, shift_new


def kernel(x_prompt, x_sample, cache_k, cache_v, cache_logf, page_table, state_wkv, state_shift, state_conv, state_ssm, norm_mix, norm_ffn, norm_final, w_in_even, w_out_even, fox_fb, rwkv_mu, rwkv_w0, rwkv_w2, rwkv_a0, rwkv_a2, rwkv_g2, rwkv_kk, rwkv_ka, rwkv_rk, rwkv_lnw, rwkv_lnb, w_in_odd, w_out_odd, ssm_conv_w, ssm_conv_b, ssm_dt_bias, ssm_a_log, ssm_d, ssm_norm_w, peer_wq, peer_keys, peer_u, peer_v):
    bp, seq_p, dm = x_prompt.shape
    bs, seq_s, _ = x_sample.shape
    assert bp == 1 and seq_s == 1
    depth = norm_mix.shape[0]
    ssm_heads = ssm_dt_bias.shape[1]
    yp = x_prompt.reshape(bp * seq_p, dm)
    ys = x_sample.reshape(bs * seq_s, dm)
    u_all = peer_u.astype(BF16)
    v_all = peer_v.astype(BF16)
    outs_p = {name: [] for name in ("k", "v", "logf", "wkv", "shift", "conv", "ssm")}
    outs_s = {name: [] for name in outs_p}
    for layer in range(depth):
        if layer % 2 == 0:
            e = layer // 2
            rw = (rwkv_mu[e], rwkv_w0[e], rwkv_w2[e], rwkv_a0[e], rwkv_a2[e], rwkv_g2[e],
                  rwkv_kk[e], rwkv_ka[e], rwkv_rk[e], rwkv_lnw[e], rwkv_lnb[e])
            w_cat = _even_weights(w_in_even[e])
            yp, *res_p = _even_layer(
                yp, norm_mix[layer], w_cat, w_out_even[e], fox_fb[e], rw, batch=bp, seq=seq_p, past=None,
                shift0=jnp.zeros((bp, RWKV_PROJ), F32),
                wkv0=jnp.zeros((bp, RWKV_HEADS, RWKV_HEAD_DIM, RWKV_HEAD_DIM), F32))
            ys, *res_s = _even_layer(
                ys, norm_mix[layer], w_cat, w_out_even[e], fox_fb[e], rw, batch=bs, seq=seq_s,
                past=(cache_k[e], cache_v[e], cache_logf[e], page_table), shift0=state_shift[e], wkv0=state_wkv[e])
            for outs, res in ((outs_p, res_p), (outs_s, res_s)):
                for name, val in zip(("k", "v", "logf", "wkv", "shift"), res):
                    outs[name].append(val)
        else:
            o = layer // 2
            mprm = (ssm_conv_w[o], ssm_conv_b[o], ssm_dt_bias[o], ssm_a_log[o], ssm_d[o], ssm_norm_w[o])
            w_cat = jnp.pad(w_in_odd[o], ((0, 0), (0, LANES - ssm_heads))).astype(BF16)
            w_out = w_out_odd[o].astype(BF16)
            tn = w_cat.shape[1]
            proj_p = _norm_matmul(yp, norm_mix[layer], w_cat, tm=256, tn=tn)
            y_m, conv_p, h_p = _ssd(proj_p, *mprm, heads=ssm_heads)
            yp = _matmul_res([y_m], [w_out], yp, tm=512, tn=dm)
            proj_s = _norm_matmul(ys, norm_mix[layer], w_cat, tm=bs, tn=tn)
            y_s, conv_s, h_s = _ssm_step(proj_s, state_conv[o], state_ssm[o], *mprm, heads=ssm_heads)
            ys = _matmul_res([y_s], [w_out], ys, tm=bs, tn=dm)
            outs_p["conv"].append(conv_p[None])
            outs_p["ssm"].append(h_p[None])
            outs_s["conv"].append(conv_s)
            outs_s["ssm"].append(h_s)
        wq = peer_wq[layer].astype(BF16)
        keys_pad = _peer_keys_padded(peer_keys[layer])
        yp = _peer(yp, norm_ffn[layer], wq, keys_pad, u_all, v_all, layer=layer,
                   tm_proj=512, tm_route=128, tm_mlp=512, te=2048)
        ys = _peer(ys, norm_ffn[layer], wq, keys_pad, u_all, v_all, layer=layer,
                   tm_proj=bs, tm_route=bs, tm_mlp=bs, te=2048)
    y_prompt = _rmsnorm(yp, norm_final, tm=512).reshape(bp, seq_p, dm)
    y_sample = _rmsnorm(ys, norm_final, tm=bs).reshape(bs, seq_s, dm)
    order = ("k", "v", "logf", "wkv", "shift", "conv", "ssm")
    return ((y_prompt, y_sample) + tuple(jnp.stack(outs_p[name]) for name in order)
            + tuple(jnp.stack(outs_s[name]) for name in order))
```

```python
import functools

import numpy as np
import jax
import jax.numpy as jnp
from jax import lax
from jax.experimental import pallas as pl
from jax.experimental.pallas import tpu as pltpu

F32 = jnp.float32
BF16 = jnp.bfloat16
I32 = jnp.int32

LANES = 128
SUBLANES = 8

RMS_EPS = 1e-6
GN_EPS = 64e-5

FOX_HEADS = 8
FOX_HEAD_DIM = 64
FOX_WIDTH = FOX_HEADS * FOX_HEAD_DIM
RWKV_HEADS = 8
RWKV_HEAD_DIM = 64
RWKV_WIDTH = RWKV_HEADS * RWKV_HEAD_DIM
DECAY_LORA = 64
ICLR_LORA = 64
GATE_LORA = 128
RWKV_PROJ = 4 * RWKV_WIDTH - RWKV_WIDTH + DECAY_LORA + ICLR_LORA + GATE_LORA
HEAD_PAIRS = RWKV_HEADS // 2

SSM_HEAD_DIM = 64
D_STATE = 128
N_GROUPS = 8
CONV_W = 4
CHUNK = 128

N_KEYS = 128
PEER_HEADS = 8
PEER_TOPK = 16
D_KEY = 128


def _cparams(semantics, vmem_mb=48):
    return pltpu.CompilerParams(dimension_semantics=semantics, vmem_limit_bytes=vmem_mb * 1024 * 1024)


def _split3(x):
    hi = x.astype(BF16)
    r1 = x - hi.astype(F32)
    mid = r1.astype(BF16)
    lo = (r1 - mid.astype(F32)).astype(BF16)
    return hi, mid, lo


def _dot_f32_right(x, m_bf16):
    hi, mid, lo = _split3(x)
    d = lambda a: jnp.dot(a, m_bf16, preferred_element_type=F32)
    return d(hi) + d(mid) + d(lo)


def _dot_f32_left(m_bf16, x):
    hi, mid, lo = _split3(x)
    d = lambda a: jnp.dot(m_bf16, a, preferred_element_type=F32)
    return d(hi) + d(mid) + d(lo)


def _sigmoid(x):
    return 1.0 / (1.0 + jnp.exp(-x))


def _softplus(x):
    return jnp.maximum(x, 0.0) + jnp.log1p(jnp.exp(-jnp.abs(x)))


def _silu(x):
    return x * _sigmoid(x)


def _norm_matmul_kernel(x_ref, g_ref, w_ref, o_ref, *rest, bf16_cols):
    xn_ref = rest[-1]

    @pl.when(pl.program_id(1) == 0)
    def _():
        x = x_ref[...]
        ms = jnp.mean(x * x, axis=-1, keepdims=True)
        xn_ref[...] = (x * lax.rsqrt(ms + RMS_EPS) * g_ref[...]).astype(BF16)

    res = jnp.dot(xn_ref[...], w_ref[...], preferred_element_type=F32)
    o_ref[...] = res
    if bf16_cols is not None:
        rest[0][...] = res[:, bf16_cols[0]:bf16_cols[1]].astype(BF16)


def _norm_matmul(x, g, w_bf16, *, tm, tn, bf16_cols=None):
    m, k = x.shape
    n = w_bf16.shape[1]
    assert m % tm == 0 and n % tn == 0
    out_specs = [pl.BlockSpec((tm, tn), lambda i, j: (i, j))]
    out_shape = [jax.ShapeDtypeStruct((m, n), F32)]
    if bf16_cols is not None:
        assert tn == n and bf16_cols[0] % LANES == 0 and bf16_cols[1] % LANES == 0
        width = bf16_cols[1] - bf16_cols[0]
        out_specs.append(pl.BlockSpec((tm, width), lambda i, j: (i, 0)))
        out_shape.append(jax.ShapeDtypeStruct((m, width), BF16))
    outs = pl.pallas_call(
        functools.partial(_norm_matmul_kernel, bf16_cols=bf16_cols),
        grid=(m // tm, n // tn),
        in_specs=[
            pl.BlockSpec((tm, k), lambda i, j: (i, 0)),
            pl.BlockSpec((1, k), lambda i, j: (0, 0)),
            pl.BlockSpec((k, tn), lambda i, j: (0, j)),
        ],
        out_specs=out_specs,
        out_shape=out_shape,
        scratch_shapes=[pltpu.VMEM((tm, k), BF16)],
        compiler_params=_cparams(("parallel", "arbitrary")),
        name="norm_matmul",
    )(x, g.reshape(1, k), w_bf16)
    return outs if bf16_cols is not None else outs[0]


def _norm_matmul_t_kernel(x_ref, g_ref, wt_ref, *o_refs):
    x = x_ref[...]
    ms = jnp.mean(x * x, axis=-1, keepdims=True)
    xn = (x * lax.rsqrt(ms + RMS_EPS) * g_ref[...]).astype(BF16)
    res = lax.dot_general(wt_ref[...], xn, (((1,), (1,)), ((), ())), preferred_element_type=F32)
    rows = res.shape[0] // len(o_refs)
    for i, o_ref in enumerate(o_refs):
        o_ref[...] = res[i * rows:(i + 1) * rows]


def _norm_matmul_t(x, g, wt_bf16, *, tm, parts):
    m, k = x.shape
    c = wt_bf16.shape[0]
    assert c % parts == 0
    return pl.pallas_call(
        _norm_matmul_t_kernel,
        grid=(m // tm,),
        in_specs=[pl.BlockSpec((tm, k), lambda i: (i, 0)),
                  pl.BlockSpec((1, k), lambda i: (0, 0)),
                  pl.BlockSpec((c, k), lambda i: (0, 0))],
        out_specs=[pl.BlockSpec((c // parts, tm), lambda i: (0, i))] * parts,
        out_shape=[jax.ShapeDtypeStruct((c // parts, m), F32)] * parts,
        compiler_params=_cparams(("parallel",)),
        name="norm_matmul_t",
    )(x, g.reshape(1, k), wt_bf16)


def _matmul_res_kernel(*refs, n_in):
    a_refs = refs[:n_in]
    w_refs = refs[n_in:2 * n_in]
    r_ref, o_ref = refs[2 * n_in], refs[2 * n_in + 1]
    acc = r_ref[...]
    for a_ref, w_ref in zip(a_refs, w_refs):
        acc = acc + jnp.dot(a_ref[...].astype(BF16), w_ref[...], preferred_element_type=F32)
    o_ref[...] = acc


def _matmul_res(a_list, w_list, res, *, tm, tn):
    m, n = res.shape
    n_in = len(a_list)
    assert m % tm == 0 and n % tn == 0
    in_specs = ([pl.BlockSpec((tm, a.shape[1]), lambda i, j: (i, 0)) for a in a_list]
                + [pl.BlockSpec((w.shape[0], tn), lambda i, j: (0, j)) for w in w_list]
                + [pl.BlockSpec((tm, tn), lambda i, j: (i, j))])
    return pl.pallas_call(
        functools.partial(_matmul_res_kernel, n_in=n_in),
        grid=(m // tm, n // tn),
        in_specs=in_specs,
        out_specs=pl.BlockSpec((tm, tn), lambda i, j: (i, j)),
        out_shape=jax.ShapeDtypeStruct((m, n), F32),
        compiler_params=_cparams(("parallel", "parallel")),
        name="matmul_res",
    )(*a_list, *w_list, res)


def _rmsnorm_kernel(x_ref, g_ref, o_ref):
    x = x_ref[...]
    ms = jnp.mean(x * x, axis=-1, keepdims=True)
    o_ref[...] = x * lax.rsqrt(ms + RMS_EPS) * g_ref[...]


def _rmsnorm(x, g, *, tm):
    m, k = x.shape
    return pl.pallas_call(
        _rmsnorm_kernel,
        grid=(m // tm,),
        in_specs=[pl.BlockSpec((tm, k), lambda i: (i, 0)), pl.BlockSpec((1, k), lambda i: (0, 0))],
        out_specs=pl.BlockSpec((tm, k), lambda i: (i, 0)),
        out_shape=jax.ShapeDtypeStruct((m, k), F32),
        compiler_params=_cparams(("parallel",)),
        name="rmsnorm",
    )(x, g.reshape(1, k))


def _fox_prep_kernel(fl_ref, fb_ref, lf_ref, ct_ref, carry_ref, *, t):
    @pl.when(pl.program_id(0) == 0)
    def _():
        carry_ref[...] = jnp.zeros_like(carry_ref)

    x = fl_ref[...] + fb_ref[...]
    lf = jnp.minimum(x, 0.0) - jnp.log1p(jnp.exp(-jnp.abs(x)))
    lf_ref[...] = lf[:, :FOX_HEADS]
    row = lax.broadcasted_iota(I32, (t, t), 0)
    col = lax.broadcasted_iota(I32, (t, t), 1)
    tri = jnp.where(col <= row, 1.0, 0.0).astype(BF16)
    c = _dot_f32_left(tri, lf) + carry_ref[...]
    carry_ref[...] = c[t - 1:t, :]
    ct_ref[...] = c.T[:FOX_HEADS, :]


def _fox_prep(proj, fl_block, fb_pad, *, t):
    n = proj.shape[0]
    return pl.pallas_call(
        functools.partial(_fox_prep_kernel, t=t),
        grid=(n // t,),
        in_specs=[pl.BlockSpec((t, LANES), lambda i: (i, fl_block)),
                  pl.BlockSpec((1, LANES), lambda i: (0, 0))],
        out_specs=[pl.BlockSpec((t, FOX_HEADS), lambda i: (i, 0)),
                   pl.BlockSpec((FOX_HEADS, t), lambda i: (0, i))],
        out_shape=[jax.ShapeDtypeStruct((n, FOX_HEADS), F32),
                   jax.ShapeDtypeStruct((FOX_HEADS, n), F32)],
        scratch_shapes=[pltpu.VMEM((1, LANES), F32)],
        compiler_params=_cparams(("arbitrary",)),
        name="fox_prep",
    )(proj, fb_pad)


_SKIP_LOGIT_GAP = 110.0
_SKIP_NORM_SLACK = 1.05


def _fox_bounds_kernel(q_ref, k_ref, ct_ref, sel_ref, jmin_ref, qmax_ref, kmax_ref, cs_ref, ce_ref, *, tq, nq):
    i = pl.program_id(0)
    lane = lax.broadcasted_iota(I32, (FOX_HEADS, LANES), 1)
    sub = lax.broadcasted_iota(I32, (FOX_HEADS, LANES), 0)

    @pl.when(i == 0)
    def _():
        qmax_ref[...] = jnp.zeros_like(qmax_ref)
        kmax_ref[...] = jnp.zeros_like(kmax_ref)
        cs_ref[...] = jnp.zeros_like(cs_ref)
        ce_ref[...] = jnp.zeros_like(ce_ref)

    def head_sq_norm_max(x_ref):
        x = x_ref[...].astype(F32)
        sq = jnp.dot((x * x).astype(BF16), sel_ref[...], preferred_element_type=F32)
        return jnp.max(sq, axis=0, keepdims=True)

    qmax_ref[...] = jnp.maximum(qmax_ref[...], head_sq_norm_max(q_ref))
    kmax_ref[...] = jnp.maximum(kmax_ref[...], head_sq_norm_max(k_ref))
    cs_ref[...] = jnp.where(lane == i, ct_ref[:, 0:1], cs_ref[...])
    ce_ref[...] = jnp.where(lane == i, ct_ref[:, tq - 1:tq], ce_ref[...])

    @pl.when(i == nq - 1)
    def _():
        def to_col(row):
            return jnp.sum(jnp.where(sub == lane, row, 0.0), axis=-1, keepdims=True)

        b2 = (2.0 * _SKIP_NORM_SLACK * FOX_HEAD_DIM ** -0.5) * jnp.sqrt(to_col(qmax_ref[...]) * to_col(kmax_ref[...]))
        ce = ce_ref[...]
        out = jnp.zeros((FOX_HEADS, LANES), F32)
        for qi in range(nq):
            gap = b2 + cs_ref[:, qi:qi + 1] - ce
            skip = jnp.where((gap < -_SKIP_LOGIT_GAP) & (lane < qi), 1.0, 0.0)
            for p in range(FOX_HEADS // 2):
                both = skip[2 * p:2 * p + 1, :] * skip[2 * p + 1:2 * p + 2, :]
                count = jnp.sum(both, axis=-1, keepdims=True)
                out = jnp.where((sub == p) & (lane == qi), count, out)
        jmin_ref[...] = out.astype(I32)


def _fox_bounds(qkv_bf16, ct, *, tq):
    n = qkv_bf16.shape[0]
    nq = n // tq
    assert nq <= LANES
    sel = jnp.asarray((np.arange(FOX_WIDTH)[:, None] // FOX_HEAD_DIM == np.arange(LANES)[None, :]).astype(np.float32), BF16)
    table = pl.pallas_call(
        functools.partial(_fox_bounds_kernel, tq=tq, nq=nq),
        grid=(nq,),
        in_specs=[pl.BlockSpec((tq, FOX_WIDTH), lambda i: (i, 0)),
                  pl.BlockSpec((tq, FOX_WIDTH), lambda i: (i, 1)),
                  pl.BlockSpec((FOX_HEADS, tq), lambda i: (0, i)),
                  pl.BlockSpec(sel.shape, lambda i: (0, 0))],
        out_specs=pl.BlockSpec((FOX_HEADS, LANES), lambda i: (0, 0)),
        out_shape=jax.ShapeDtypeStruct((FOX_HEADS, LANES), I32),
        scratch_shapes=[pltpu.VMEM((1, LANES), F32), pltpu.VMEM((1, LANES), F32),
                        pltpu.VMEM((FOX_HEADS, LANES), F32), pltpu.VMEM((FOX_HEADS, LANES), F32)],
        compiler_params=_cparams(("arbitrary",)),
        name="fox_bounds",
    )(qkv_bf16, qkv_bf16, ct, sel)
    return table[:FOX_HEADS // 2, :nq]


def _fox_attn_kernel(jmin_ref, q_ref, k_ref, v_ref, ct_ref, o_ref, qs_ref, m_ref, l_ref, acc_ref, *, tq):
    i = pl.program_id(1)
    lane = lax.broadcasted_iota(I32, (1, LANES), 1)
    left = lane < FOX_HEAD_DIM
    q = q_ref[...] * jnp.asarray(FOX_HEAD_DIM ** -0.5, BF16)
    zero = jnp.zeros_like(q)
    qs_ref[0] = jnp.where(left, q, zero)
    qs_ref[1] = jnp.where(left, zero, q)
    q0 = pl.multiple_of(i * tq, tq)
    cref = [ct_ref[h:h + 1, pl.ds(q0, LANES)][:, 0:1] for h in range(2)]

    m_ref[...] = jnp.full_like(m_ref, -jnp.inf)
    l_ref[...] = jnp.zeros_like(l_ref)
    acc_ref[...] = jnp.zeros_like(acc_ref)
    ncol = tq // LANES

    def step(j, masked):
        k0 = pl.multiple_of(j * tq, tq)
        kb = k_ref[pl.ds(k0, tq), :]
        vb = v_ref[pl.ds(k0, tq), :]
        bias = [cref[h] - ct_ref[h:h + 1, pl.ds(k0, tq)] for h in range(2)]
        for h in range(2):
            s = lax.dot_general(qs_ref[h], kb, (((1,), (1,)), ((), ())), preferred_element_type=F32)
            s = s + bias[h]
            if masked:
                r = lax.broadcasted_iota(I32, (tq, tq), 0)
                c = lax.broadcasted_iota(I32, (tq, tq), 1)
                s = jnp.where(c <= r, s, -jnp.inf)
            cols = [s[:, c * LANES:(c + 1) * LANES] for c in range(ncol)]
            lane_max = functools.reduce(jnp.maximum, cols)
            m_old = m_ref[h]
            m_new = jnp.maximum(m_old, jnp.max(lane_max, axis=-1, keepdims=True))
            alpha = jnp.exp(m_old - m_new)
            ps = [jnp.exp(c - m_new) for c in cols]
            l_ref[h] = alpha * l_ref[h] + functools.reduce(jnp.add, ps)
            m_ref[h] = m_new
            p = jnp.concatenate([x.astype(BF16) for x in ps], axis=1)
            acc_ref[h] = alpha * acc_ref[h] + jnp.dot(p, vb, preferred_element_type=F32)

    def body(j, carry):
        step(j, False)
        return carry

    lax.fori_loop(jmin_ref[pl.program_id(0), i], i, body, 0)
    step(i, True)
    l0 = jnp.sum(l_ref[0], axis=-1, keepdims=True)
    l1 = jnp.sum(l_ref[1], axis=-1, keepdims=True)
    o_ref[...] = jnp.where(left, acc_ref[0] / l0, acc_ref[1] / l1).astype(o_ref.dtype)


def _fox_attn(qkv_bf16, ct, *, tq):
    n = qkv_bf16.shape[0]
    nb = FOX_WIDTH // LANES
    jmin = _fox_bounds(qkv_bf16, ct, tq=tq)
    return pl.pallas_call(
        functools.partial(_fox_attn_kernel, tq=tq),
        grid_spec=pltpu.PrefetchScalarGridSpec(
            num_scalar_prefetch=1,
            grid=(nb, n // tq),
            in_specs=[
                pl.BlockSpec((tq, LANES), lambda p, i, jm: (i, p)),
                pl.BlockSpec((n, LANES), lambda p, i, jm: (0, nb + p)),
                pl.BlockSpec((n, LANES), lambda p, i, jm: (0, 2 * nb + p)),
                pl.BlockSpec((None, 2, n), lambda p, i, jm: (p, 0, 0)),
            ],
            out_specs=pl.BlockSpec((tq, LANES), lambda p, i, jm: (i, p)),
            scratch_shapes=[pltpu.VMEM((2, tq, LANES), BF16), pltpu.VMEM((2, tq, LANES), F32),
                            pltpu.VMEM((2, tq, LANES), F32), pltpu.VMEM((2, tq, LANES), F32)],
        ),
        out_shape=jax.ShapeDtypeStruct((n, FOX_WIDTH), BF16),
        compiler_params=_cparams(("arbitrary", "arbitrary")),
        name="fox_attn",
    )(jmin, qkv_bf16, qkv_bf16, qkv_bf16, ct.reshape(nb, 2, n))


_RW = RWKV_WIDTH
_RWKV_PERM = np.concatenate([
    np.arange(0, _RW),
    np.arange(_RW + DECAY_LORA, 2 * _RW + DECAY_LORA),
    np.arange(2 * _RW + DECAY_LORA, 3 * _RW + DECAY_LORA),
    np.arange(3 * _RW + DECAY_LORA + ICLR_LORA, RWKV_PROJ),
    np.arange(_RW, _RW + DECAY_LORA),
    np.arange(3 * _RW + DECAY_LORA, 3 * _RW + DECAY_LORA + ICLR_LORA),
])
_RWKV_INV_PERM = np.argsort(_RWKV_PERM)


def _head_block_ones(width, head_dim):
    idx = np.arange(width) // head_dim
    return jnp.asarray((idx[:, None] == idx[None, :]).astype(np.float32), BF16)


def _rwkv_prep_math(p, p_prev, mu, w0, w2p, a0, a2p, g2, k_k, k_a, bones):
    xs = p + mu * (p_prev - p)
    r = xs[:, 0:_RW]
    k = xs[:, _RW:2 * _RW]
    v = xs[:, 2 * _RW:3 * _RW]
    gl = xs[:, 3 * _RW:3 * _RW + GATE_LORA]
    wa = xs[:, 3 * _RW + GATE_LORA:]
    w = -_softplus(-(w0 + jnp.dot(jnp.tanh(wa).astype(BF16), w2p, preferred_element_type=F32))) - 0.5
    decay = jnp.exp(-jnp.exp(w))
    a = _sigmoid(a0 + jnp.dot(wa.astype(BF16), a2p, preferred_element_type=F32))
    g = jnp.dot(_sigmoid(gl).astype(BF16), g2, preferred_element_type=F32)
    kkr = k * k_k
    ss = _dot_f32_right(kkr * kkr, bones)
    kk = kkr / jnp.maximum(jnp.sqrt(ss), 1e-12)
    k2 = k * (1.0 + (a - 1.0) * k_a)
    return r, k2, v, kk, kk * a, decay, g


def _rwkv_prep_kernel(p_ref, prev_ref, mu_ref, w0_ref, w2_ref, a0_ref, a2_ref, g2_ref, kk_ref, ka_ref, bones_ref,
                      r_o, k_o, v_o, kk_o, b_o, d_o, g_o, last_o, buf_ref, *, t, shifted):
    p = p_ref[...]
    if shifted:
        @pl.when(pl.program_id(0) == 0)
        def _():
            buf_ref[SUBLANES - 1:SUBLANES, :] = prev_ref[...]

        buf_ref[SUBLANES:SUBLANES + t, :] = p
        p_prev = buf_ref[SUBLANES - 1:SUBLANES - 1 + t, :]
        buf_ref[SUBLANES - 1:SUBLANES, :] = p[t - 1:t, :]
        last_o[...] = p[t - 1:t, :]
    else:
        p_prev = prev_ref[...]
        last_o[...] = p
    outs = _rwkv_prep_math(p, p_prev, mu_ref[...], w0_ref[...], w2_ref[...], a0_ref[...], a2_ref[...], g2_ref[...],
                           kk_ref[...], ka_ref[...], bones_ref[...])
    for o_ref, val in zip((r_o, k_o, v_o, kk_o, b_o, d_o, g_o), outs):
        o_ref[...] = val


def _rwkv_prep(proj, prev, mu, w0, w2p, a0, a2p, g2, k_k, k_a, bones, *, t, shifted):
    n = proj.shape[0]
    row = lambda w: pl.BlockSpec((1, w), lambda i: (0, 0))
    full = lambda a: pl.BlockSpec(a.shape, lambda i: (0, 0))
    tok = lambda w: pl.BlockSpec((t, w), lambda i: (i, 0))
    prev_spec = row(RWKV_PROJ) if shifted else tok(RWKV_PROJ)
    last_spec = row(RWKV_PROJ) if shifted else tok(RWKV_PROJ)
    last_shape = (1, RWKV_PROJ) if shifted else (n, RWKV_PROJ)
    return pl.pallas_call(
        functools.partial(_rwkv_prep_kernel, t=t, shifted=shifted),
        grid=(n // t,),
        in_specs=[tok(RWKV_PROJ), prev_spec, row(RWKV_PROJ), row(_RW), full(w2p), row(_RW), full(a2p), full(g2),
                  row(_RW), row(_RW), full(bones)],
        out_specs=[tok(_RW)] * 7 + [last_spec],
        out_shape=[jax.ShapeDtypeStruct((n, _RW), F32)] * 7 + [jax.ShapeDtypeStruct(last_shape, F32)],
        scratch_shapes=[pltpu.VMEM((SUBLANES + t, RWKV_PROJ), F32)],
        compiler_params=_cparams(("arbitrary",)),
        name="rwkv_prep",
    )(proj, prev, mu, w0, w2p, a0, a2p, g2, k_k, k_a, bones)


_SCAN_GROUP = 16


def _rwkv_scan_kernel(r_ref, k_ref, v_ref, kk_ref, b_ref, d_ref, s0_ref, o_ref, st_ref, s_ref, *, t):
    j = pl.program_id(1)

    @pl.when(j == 0)
    def _():
        s_ref[...] = s0_ref[...]

    left = lax.broadcasted_iota(I32, (RWKV_HEAD_DIM, LANES), 1) < RWKV_HEAD_DIM

    def seg(x):
        s0 = jnp.sum(jnp.where(left, x, 0.0), axis=-1, keepdims=True)
        s1 = jnp.sum(jnp.where(left, 0.0, x), axis=-1, keepdims=True)
        return jnp.where(left, s0, s1)

    group = _SCAN_GROUP if t % _SCAN_GROUP == 0 else t

    def readout(s, r_rows):
        s_bd = jnp.concatenate([jnp.where(left, s, 0.0), jnp.where(left, 0.0, s)], axis=0).astype(BF16)
        return lax.dot_general(r_rows, s_bd, (((1,), (1,)), ((), ())), preferred_element_type=F32)

    piece_pairs = ((0, 0), (0, 1), (0, 2), (1, 0), (1, 1), (2, 0))
    n_c = len(piece_pairs) * group
    kc = LANES * pl.cdiv(n_c, LANES)
    c_lane = lax.broadcasted_iota(I32, (LANES, kc), 1)
    c_step = c_lane % group
    c_live = c_lane < n_c
    pad_rows = jnp.zeros((kc - n_c, LANES), F32)

    def value_columns(v_t):
        v_p = [x.astype(F32) for x in _split3(v_t)]
        v_rows = jnp.concatenate([v_p[a] for a, _ in piece_pairs] + [pad_rows], axis=0)
        return v_rows.T

    def key_rows(k_t):
        k_p = [x.astype(F32) for x in _split3(k_t)]
        return jnp.concatenate([k_p[b] for _, b in piece_pairs] + [pad_rows], axis=0).astype(BF16)

    def outer_product(v_cols, k_rows, u):
        lhs = jnp.where(c_live & (c_step == u), v_cols, 0.0).astype(BF16)
        vk = jnp.dot(lhs, k_rows, preferred_element_type=F32)
        return jnp.where(left, vk[:RWKV_HEAD_DIM], vk[RWKV_HEAD_DIM:])

    n_groups = t // group

    def group_value_columns(gi):
        base = gi * group if isinstance(gi, int) else pl.multiple_of(gi * group, group)
        return tuple(value_columns(v_ref[pl.ds(base, group), p * LANES:(p + 1) * LANES]) for p in range(HEAD_PAIRS))

    def body(gi, carry):
        states, v_cols = carry
        base = pl.multiple_of(gi * group, group)
        states = list(states)
        tiles = []
        for p in range(HEAD_PAIRS):
            sl = slice(p * LANES, (p + 1) * LANES)
            tiles.append(tuple(ref[pl.ds(base, group), sl] for ref in (kk_ref, v_ref, d_ref, b_ref, k_ref, r_ref)))
        k_rows = [key_rows(tiles[p][4]) for p in range(HEAD_PAIRS)]
        vk_tiles = [[None] * group for _ in range(HEAD_PAIRS)]
        for u in range(group):
            for p in range(HEAD_PAIRS):
                vk_tiles[p][u] = outer_product(v_cols[p], k_rows[p], u)
        v_cols_next = group_value_columns(jnp.minimum(gi + 1, n_groups - 1))
        o_rows = [[] for _ in range(HEAD_PAIRS)]
        for u in range(group):
            row = lambda x: x[u:u + 1, :]
            for p in range(HEAD_PAIRS):
                kk_t, v_t, d_t, b_t, k_t, r_t = tiles[p]
                s = states[p]
                sk = seg(s * row(kk_t))
                s = s * row(d_t) - sk * row(b_t) + vk_tiles[p][u]
                states[p] = s
                r_rows = jnp.broadcast_to(r_t, (SUBLANES, LANES)) if group == 1 else r_t
                o_rows[p].append(readout(s, r_rows.astype(BF16))[u:u + 1, :])
        for p in range(HEAD_PAIRS):
            sl = slice(p * LANES, (p + 1) * LANES)
            o_ref[pl.ds(base, group), sl] = jnp.concatenate(o_rows[p], axis=0) if group > 1 else o_rows[p][0]
        return tuple(states), v_cols_next

    states, _ = lax.fori_loop(0, n_groups, body,
                              (tuple(s_ref[p] for p in range(HEAD_PAIRS)), group_value_columns(0)))
    for p in range(HEAD_PAIRS):
        s_ref[p] = states[p]

    @pl.when(j == pl.num_programs(1) - 1)
    def _():
        st_ref[...] = s_ref[...]


def _rwkv_scan(r, k, v, kk, b, d, s0_pairs, *, batch, t):
    n = r.shape[0]
    seq = n // batch
    tok = pl.BlockSpec((None, t, _RW), lambda bi, j: (bi, j, 0))
    st = pl.BlockSpec((None, HEAD_PAIRS, RWKV_HEAD_DIM, LANES), lambda bi, j: (bi, 0, 0, 0))
    o, s_t = pl.pallas_call(
        functools.partial(_rwkv_scan_kernel, t=t),
        grid=(batch, seq // t),
        in_specs=[tok] * 6 + [st],
        out_specs=[tok, st],
        out_shape=[jax.ShapeDtypeStruct((batch, seq, _RW), F32),
                   jax.ShapeDtypeStruct((batch, HEAD_PAIRS, RWKV_HEAD_DIM, LANES), F32)],
        scratch_shapes=[pltpu.VMEM((HEAD_PAIRS, RWKV_HEAD_DIM, LANES), F32)],
        compiler_params=_cparams(("arbitrary", "arbitrary")),
        name="rwkv_scan",
    )(*(x.reshape(batch, seq, _RW) for x in (r, k, v, kk, b, d)), s0_pairs)
    return o.reshape(n, _RW), s_t


def _rwkv_post_kernel(o_ref, r_ref, k_ref, v_ref, g_ref, lnw_ref, lnb_ref, rk_ref, bones_ref, out_ref):
    bones = bones_ref[...]
    inv = 1.0 / RWKV_HEAD_DIM
    o = o_ref[...]
    mean = _dot_f32_right(o, bones) * inv
    cen = o - mean
    var = _dot_f32_right(cen * cen, bones) * inv
    gn = cen * lax.rsqrt(var + GN_EPS) * lnw_ref[...] + lnb_ref[...]
    bonus = _dot_f32_right(r_ref[...] * k_ref[...] * rk_ref[...], bones) * v_ref[...]
    out_ref[...] = ((gn + bonus) * g_ref[...]).astype(out_ref.dtype)


def _rwkv_post(o, r, k, v, g, lnw, lnb, rk, bones, *, t):
    n = o.shape[0]
    tok = pl.BlockSpec((t, _RW), lambda i: (i, 0))
    row = pl.BlockSpec((1, _RW), lambda i: (0, 0))
    return pl.pallas_call(
        _rwkv_post_kernel,
        grid=(n // t,),
        in_specs=[tok] * 5 + [row] * 3 + [pl.BlockSpec(bones.shape, lambda i: (0, 0))],
        out_specs=tok,
        out_shape=jax.ShapeDtypeStruct((n, _RW), BF16),
        compiler_params=_cparams(("parallel",)),
        name="rwkv_post",
    )(o, r, k, v, g, lnw, lnb, rk, bones)


def _to_pairs(s):
    b = s.shape[0]
    return s.reshape(b, HEAD_PAIRS, 2, RWKV_HEAD_DIM, RWKV_HEAD_DIM).transpose(0, 1, 3, 2, 4).reshape(
        b, HEAD_PAIRS, RWKV_HEAD_DIM, LANES)


def _from_pairs(s):
    b = s.shape[0]
    return s.reshape(b, HEAD_PAIRS, RWKV_HEAD_DIM, 2, RWKV_HEAD_DIM).transpose(0, 1, 3, 2, 4).reshape(
        b, RWKV_HEADS, RWKV_HEAD_DIM, RWKV_HEAD_DIM)


def _rwkv_mix(proj, prev, s0, rw, *, batch, t_prep, t_scan, shifted):
    mu, w0, w2, a0, a2, g2, k_k, k_a, r_k, ln_w, ln_b = rw
    bones = _head_block_ones(_RW, RWKV_HEAD_DIM)
    zeros = jnp.zeros((LANES - DECAY_LORA, _RW), F32)
    w2p = jnp.concatenate([w2, zeros], axis=0).astype(BF16)
    a2p = jnp.concatenate([zeros, a2], axis=0).astype(BF16)
    row = lambda x: x.reshape(1, -1)
    r, k, v, kk, b, d, g, last = _rwkv_prep(
        proj, prev, row(mu[_RWKV_PERM]), row(w0), w2p, row(a0), a2p, g2.astype(BF16), row(k_k), row(k_a), bones,
        t=t_prep, shifted=shifted)
    o, s_t = _rwkv_scan(r, k, v, kk, b, d, _to_pairs(s0), batch=batch, t=t_scan)
    out = _rwkv_post(o, r, k, v, g, row(ln_w), row(ln_b), row(r_k), bones, t=t_prep)
    return out, last[:, _RWKV_INV_PERM], _from_pairs(s_t)


def _top_rows(s, k, order):
    vals, ords = [], []
    for _ in range(k):
        m = jnp.max(s, axis=0, keepdims=True)
        pick = jnp.min(jnp.where(s == m, order, jnp.inf), axis=0, keepdims=True)
        vals.append(m)
        ords.append(pick)
        s = jnp.where(order == pick, -jnp.inf, s)
    return jnp.concatenate(vals, axis=0), jnp.concatenate(ords, axis=0)


def _take_rows(table, pos):
    out = jnp.zeros(pos.shape, table.dtype)
    for i in range(table.shape[0]):
        out = jnp.where(pos == float(i), table[i:i + 1, :], out)
    return out


_CAND_GROUPS = ((0, 2, 16), (2, 4, 8))
_CAND_TAIL_I0, _CAND_TAIL_J = 4, 3
_CAND_HEAD_ROWS = sum((hi - lo) * nj for lo, hi, nj in _CAND_GROUPS)
_CAND_ROWS = _CAND_HEAD_ROWS + _CAND_TAIL_J * PEER_TOPK


def _candidate_tables(tm):
    k = PEER_TOPK
    assert k == 16 and _CAND_GROUPS == ((0, 2, 16), (2, 4, 8))
    r = lax.broadcasted_iota(I32, (_CAND_ROWS, tm), 0)
    r2 = r - 2 * k
    flat2 = (2 + (r2 >> 3)) * k + (r2 & 7)
    r3 = r - _CAND_HEAD_ROWS
    i3 = r3 & (k - 1)
    flat3 = i3 * k + (r3 >> 4)
    flat = jnp.where(r < 2 * k, r, jnp.where(r < _CAND_HEAD_ROWS, flat2, flat3))
    valid = (r < _CAND_HEAD_ROWS) | (i3 >= _CAND_TAIL_I0)
    return flat.astype(F32), valid


def _peer_route_kernel(q_ref, keys_ref, g_ref, e1_ref, e2_ref, gv_ref, gt_ref, *, tm):
    k = PEER_TOPK
    flat, valid = _candidate_tables(tm)
    key_order = lax.broadcasted_iota(I32, (N_KEYS, tm), 0).astype(F32)

    e1s, e2s, gs = [], [], []
    for h in range(PEER_HEADS):
        qh = q_ref[:, h * D_KEY:(h + 1) * D_KEY].astype(BF16)
        score = lambda p: lax.dot_general(keys_ref[h, p], qh, (((1,), (1,)), ((), ())), preferred_element_type=F32)
        v1, i1 = _top_rows(score(0), k, key_order)
        v2, i2 = _top_rows(score(1), k, key_order)
        parts = [v1[i:i + 1, :] + v2[:nj, :] for lo, hi, nj in _CAND_GROUPS for i in range(lo, hi)]
        parts += [v1 + v2[j:j + 1, :] for j in range(_CAND_TAIL_J)]
        cand = jnp.where(valid, jnp.concatenate(parts, axis=0), -jnp.inf)
        sc, pos = _top_rows(cand, k, flat)
        pos_i = jnp.floor(pos * (1.0 / k))
        e1s.append(_take_rows(i1, pos_i))
        e2s.append(_take_rows(i2, pos - k * pos_i))
        ex = jnp.exp(sc - sc[0:1, :])
        gs.append(ex / jnp.sum(ex, axis=0, keepdims=True))
    e1_ref[...] = jnp.concatenate(e1s, axis=0).T
    e2_ref[...] = jnp.concatenate(e2s, axis=0).T
    gv_ref[...] = jnp.concatenate(gs, axis=0).T

    eid = lax.broadcasted_iota(I32, (N_KEYS, PEER_HEADS * k), 0).astype(F32)
    rows = _ROUTE_GROUP

    def body(gi, carry):
        for half in range(_ROUTE_UNROLL):
            base = pl.multiple_of((gi * _ROUTE_UNROLL + half) * rows, rows)
            stage = half * rows * _ROUTE_PITCH
            e1_t = e1_ref[pl.ds(base, rows), :]
            e2_t = e2_ref[pl.ds(base, rows), :]
            gv_t = gv_ref[pl.ds(base, rows), :]
            for u in range(rows):
                a = jnp.where(eid == e1_t[u:u + 1, :], gv_t[u:u + 1, :], 0.0).astype(BF16)
                b = jnp.where(eid == e2_t[u:u + 1, :], 1.0, 0.0).astype(BF16)
                gt_ref[stage + u * _ROUTE_PITCH:stage + u * _ROUTE_PITCH + N_KEYS, :] = lax.dot_general(
                    a, b, (((1,), (1,)), ((), ())), preferred_element_type=F32)
        for half in range(_ROUTE_UNROLL):
            base = pl.multiple_of((gi * _ROUTE_UNROLL + half) * rows, rows)
            stage = half * rows * _ROUTE_PITCH
            for e1 in range(N_KEYS):
                slab = gt_ref[pl.ds(stage + e1, rows, stride=_ROUTE_PITCH), :]
                g_ref[pl.ds(base, rows), e1 * N_KEYS:(e1 + 1) * N_KEYS] = slab.astype(g_ref.dtype)
        return carry

    lax.fori_loop(0, tm // (rows * _ROUTE_UNROLL), body, 0)


_ROUTE_GROUP = 2 * SUBLANES
_ROUTE_UNROLL = 2
_ROUTE_PITCH = N_KEYS + SUBLANES


def _peer_route(q, keys_pad, *, tm):
    n = q.shape[0]
    width = PEER_HEADS * PEER_TOPK
    assert tm % (_ROUTE_GROUP * _ROUTE_UNROLL) == 0
    return pl.pallas_call(
        functools.partial(_peer_route_kernel, tm=tm),
        grid=(n // tm,),
        in_specs=[pl.BlockSpec((tm, PEER_HEADS * D_KEY), lambda i: (i, 0)),
                  pl.BlockSpec(keys_pad.shape, lambda i: (0, 0, 0, 0))],
        out_specs=pl.BlockSpec((tm, N_KEYS * N_KEYS), lambda i: (i, 0)),
        out_shape=jax.ShapeDtypeStruct((n, N_KEYS * N_KEYS), BF16),
        scratch_shapes=[pltpu.VMEM((tm, width), F32)] * 3
                       + [pltpu.VMEM((_ROUTE_UNROLL * _ROUTE_GROUP * _ROUTE_PITCH, N_KEYS), F32)],
        compiler_params=_cparams(("parallel",)),
        name="peer_route",
    )(q, keys_pad)


def _peer_mlp_kernel(x_ref, gain_ref, u_ref, v_ref, g_ref, o_ref, xn_ref, acc_ref):
    j = pl.program_id(1)

    @pl.when(j == 0)
    def _():
        x = x_ref[...]
        ms = jnp.mean(x * x, axis=-1, keepdims=True)
        xn_ref[...] = (x * lax.rsqrt(ms + RMS_EPS) * gain_ref[...]).astype(BF16)
        acc_ref[...] = jnp.zeros_like(acc_ref)

    h = lax.dot_general(xn_ref[...], u_ref[...], (((1,), (1,)), ((), ())), preferred_element_type=F32)
    act = 0.5 * h * (1.0 + lax.erf(h * (2.0 ** -0.5)))
    w = (act * g_ref[...].astype(F32)).astype(BF16)
    acc_ref[...] += jnp.dot(w, v_ref[...], preferred_element_type=F32)

    @pl.when(j == pl.num_programs(1) - 1)
    def _():
        o_ref[...] = x_ref[...] + acc_ref[...]


def _peer_mlp(x, gain, u_bf16, v_bf16, gmap, *, layer, tm, te):
    n, dm = x.shape
    ne = u_bf16.shape[1]
    return pl.pallas_call(
        _peer_mlp_kernel,
        grid=(n // tm, ne // te),
        in_specs=[pl.BlockSpec((tm, dm), lambda i, j: (i, 0)),
                  pl.BlockSpec((1, dm), lambda i, j: (0, 0)),
                  pl.BlockSpec((None, te, dm), lambda i, j: (layer, j, 0)),
                  pl.BlockSpec((None, te, dm), lambda i, j: (layer, j, 0)),
                  pl.BlockSpec((tm, te), lambda i, j: (i, j))],
        out_specs=pl.BlockSpec((tm, dm), lambda i, j: (i, 0)),
        out_shape=jax.ShapeDtypeStruct((n, dm), F32),
        scratch_shapes=[pltpu.VMEM((tm, dm), BF16), pltpu.VMEM((tm, dm), F32)],
        compiler_params=_cparams(("parallel", "arbitrary")),
        name="peer_mlp",
    )(x, gain.reshape(1, dm), u_bf16, v_bf16, gmap)


def _peer_keys_padded(keys):
    z = jnp.zeros_like(keys[:, 0])
    first = jnp.concatenate([keys[:, 0], z], axis=-1)
    second = jnp.concatenate([z, keys[:, 1]], axis=-1)
    return jnp.stack([first, second], axis=1).astype(BF16)


def _peer(y, gain, wq_bf16, keys_pad, u_bf16, v_bf16, *, layer, tm_proj, tm_route, tm_mlp, te):
    q = _norm_matmul(y, gain, wq_bf16, tm=tm_proj, tn=wq_bf16.shape[1])
    gmap = _peer_route(q, keys_pad, tm=tm_route)
    return _peer_mlp(y, gain, u_bf16, v_bf16, gmap, layer=layer, tm=tm_mlp, te=te)


def _conv_silu(x, buf_ref, w_ref, b_ref, q):
    buf_ref[SUBLANES:SUBLANES + q, :] = x
    acc = b_ref[...] + x * w_ref[CONV_W - 1:CONV_W, :]
    for back in range(1, CONV_W):
        acc = acc + buf_ref[SUBLANES - back:SUBLANES - back + q, :] * w_ref[CONV_W - 1 - back:CONV_W - back, :]
    tail = buf_ref[q + SUBLANES - (CONV_W - 1):q + SUBLANES, :]
    buf_ref[SUBLANES - (CONV_W - 1):SUBLANES, :] = tail
    return _silu(acc), tail


def _ssd_kernel(z_ref, x_ref, bc_ref, dt_ref, cwx_ref, cwbc_ref, cbx_ref, cbbc_ref, dtb_ref, alog_ref, dskip_ref,
                nw_ref, expand_ref, y_ref, convx_ref, convbc_ref, ht_ref, bufx_ref, bufbc_ref, h_ref, *, q, heads):
    hpg = heads // N_GROUPS
    gw = hpg * SSM_HEAD_DIM

    @pl.when(pl.program_id(0) == 0)
    def _():
        bufx_ref[...] = jnp.zeros_like(bufx_ref)
        bufbc_ref[...] = jnp.zeros_like(bufbc_ref)
        h_ref[...] = jnp.zeros_like(h_ref)

    xs, tail_x = _conv_silu(x_ref[...], bufx_ref, cwx_ref, cbx_ref, q)
    bc, tail_bc = _conv_silu(bc_ref[...], bufbc_ref, cwbc_ref, cbbc_ref, q)
    convx_ref[...] = tail_x
    convbc_ref[...] = tail_bc

    dt = _softplus(dt_ref[...] + dtb_ref[...])
    a = -jnp.exp(alog_ref[...])
    row = lax.broadcasted_iota(I32, (q, q), 0)
    col = lax.broadcasted_iota(I32, (q, q), 1)
    lower = col <= row
    tri = jnp.where(lower, 1.0, 0.0).astype(BF16)
    acs = _dot_f32_left(tri, dt * a)
    acs_t = acs.T
    expand = expand_ref[...]
    dt_e = _dot_f32_right(dt, expand)
    acs_e = _dot_f32_right(acs, expand)
    last_e = acs_e[q - 1:q, :]
    xc = xs * dt_e
    xcd = (xc * jnp.exp(last_e - acs_e)).astype(BF16)
    xc_b = xc.astype(BF16)
    grow = jnp.exp(acs_e)
    chunk_decay = jnp.exp(last_e)

    nbc = N_GROUPS * D_STATE
    for g in range(N_GROUPS):
        bm = bc[:, g * D_STATE:(g + 1) * D_STATE].astype(BF16)
        cm = bc[:, nbc + g * D_STATE:nbc + (g + 1) * D_STATE].astype(BF16)
        cb = lax.dot_general(cm, bm, (((1,), (1,)), ((), ())), preferred_element_type=F32)
        sl = slice(g * gw, (g + 1) * gw)
        h_prev = h_ref[g]
        y_off = jnp.dot(cm, h_prev.astype(BF16), preferred_element_type=F32) * grow[:, sl]
        y_diag = []
        for r in range(hpg):
            hd = g * hpg + r
            diff = acs[:, hd:hd + 1] - acs_t[hd:hd + 1, :]
            m = (cb * jnp.exp(jnp.where(lower, diff, -jnp.inf))).astype(BF16)
            y_diag.append(jnp.dot(m, xc_b[:, hd * SSM_HEAD_DIM:(hd + 1) * SSM_HEAD_DIM], preferred_element_type=F32))
        y_g = jnp.concatenate(y_diag, axis=1) + y_off + dskip_ref[:, sl] * xs[:, sl]
        states = jnp.dot(bm.T, xcd[:, sl], preferred_element_type=F32)
        h_ref[g] = h_prev * chunk_decay[:, sl] + states
        zg = z_ref[:, sl]
        y_g = y_g * _silu(zg)
        ms = jnp.mean(y_g * y_g, axis=-1, keepdims=True)
        y_ref[:, sl] = (y_g * lax.rsqrt(ms + RMS_EPS) * nw_ref[:, sl]).astype(y_ref.dtype)

    @pl.when(pl.program_id(0) == pl.num_programs(0) - 1)
    def _():
        ht_ref[...] = h_ref[...]


def _ssd(proj, conv_w, conv_b, dt_bias, a_log, d_skip, norm_w, *, heads):
    n = proj.shape[0]
    d_inner = heads * SSM_HEAD_DIM
    q = CHUNK
    nbc = 2 * N_GROUPS * D_STATE
    assert nbc == d_inner and n % q == 0
    gw = d_inner // N_GROUPS
    pad = lambda x: jnp.pad(x, (0, LANES - heads)).reshape(1, LANES)
    expand = jnp.asarray((np.arange(LANES)[:, None] == (np.arange(d_inner) // SSM_HEAD_DIM)[None, :]).astype(np.float32), BF16)
    blk = lambda w, j: pl.BlockSpec((q, w), lambda i, j=j: (i, j))
    full = lambda a: pl.BlockSpec(a.shape, lambda i: (0,) * a.ndim)
    small = [conv_w[:, :d_inner], conv_w[:, d_inner:], conv_b[:d_inner].reshape(1, -1), conv_b[d_inner:].reshape(1, -1),
             pad(dt_bias), pad(a_log), jnp.repeat(d_skip, SSM_HEAD_DIM).reshape(1, -1), norm_w.reshape(1, -1), expand]
    y, cx, cbc, ht = pl.pallas_call(
        functools.partial(_ssd_kernel, q=q, heads=heads),
        grid=(n // q,),
        in_specs=[blk(d_inner, 0), blk(d_inner, 1), blk(d_inner, 2), blk(LANES, 3 * d_inner // LANES)]
                 + [full(a) for a in small],
        out_specs=[pl.BlockSpec((q, d_inner), lambda i: (i, 0)),
                   pl.BlockSpec((CONV_W - 1, d_inner), lambda i: (0, 0)),
                   pl.BlockSpec((CONV_W - 1, nbc), lambda i: (0, 0)),
                   pl.BlockSpec((N_GROUPS, D_STATE, gw), lambda i: (0, 0, 0))],
        out_shape=[jax.ShapeDtypeStruct((n, d_inner), BF16),
                   jax.ShapeDtypeStruct((CONV_W - 1, d_inner), F32),
                   jax.ShapeDtypeStruct((CONV_W - 1, nbc), F32),
                   jax.ShapeDtypeStruct((N_GROUPS, D_STATE, gw), F32)],
        scratch_shapes=[pltpu.VMEM((SUBLANES + q, d_inner), F32), pltpu.VMEM((SUBLANES + q, nbc), F32),
                        pltpu.VMEM((N_GROUPS, D_STATE, gw), F32)],
        compiler_params=_cparams(("arbitrary",)),
        name="ssd",
    )(proj, proj, proj, proj, *small)
    hpg = heads // N_GROUPS
    h_t = ht.reshape(N_GROUPS, D_STATE, hpg, SSM_HEAD_DIM).transpose(0, 2, 3, 1).reshape(heads, SSM_HEAD_DIM, D_STATE)
    return y, jnp.concatenate([cx, cbc], axis=1), h_t


def _ssm_step_kernel(z_ref, x_ref, bc_ref, dt_ref, bufx_ref, bufbc_ref, h0_ref, cwx_ref, cwbc_ref, cbx_ref, cbbc_ref,
                     dtb_ref, alog_ref, dskip_ref, nw_ref, y_ref, nbufx_ref, nbufbc_ref, h_ref, *, heads):
    hpg = heads // N_GROUPS
    gw = hpg * SSM_HEAD_DIM
    nbc = N_GROUPS * D_STATE
    fullx = jnp.concatenate([bufx_ref[...], x_ref[...]], axis=1)
    nbufx_ref[...] = fullx[:, 1:]
    xs = _silu(cbx_ref[...] + jnp.sum(fullx * cwx_ref[...], axis=1, keepdims=True))
    fullbc = jnp.concatenate([bufbc_ref[...], bc_ref[...]], axis=0)
    nbufbc_ref[...] = fullbc[1:, :]
    bc = _silu(cbbc_ref[...] + jnp.sum(fullbc * cwbc_ref[...], axis=0, keepdims=True))
    dt = _softplus(dt_ref[...] + dtb_ref[...])
    dec = jnp.exp(dt * (-jnp.exp(alog_ref[...])))
    xdt = xs * dt
    for g in range(N_GROUPS):
        rows = slice(g * gw, (g + 1) * gw)
        bm = bc[:, g * D_STATE:(g + 1) * D_STATE]
        cm = bc[:, nbc + g * D_STATE:nbc + (g + 1) * D_STATE]
        h_new = h0_ref[rows, :] * dec[rows, :] + xdt[rows, :] * bm
        h_ref[rows, :] = h_new
        y = jnp.sum(h_new * cm, axis=-1, keepdims=True) + dskip_ref[rows, :] * xs[rows, :]
        y = y * _silu(z_ref[rows, :])
        ms = jnp.mean(y * y, axis=0, keepdims=True)
        y_ref[rows, :] = y * lax.rsqrt(ms + RMS_EPS) * nw_ref[rows, :]


def _ssm_step(proj, conv_buf, h0, conv_w, conv_b, dt_bias, a_log, d_skip, norm_w, *, heads):
    b = proj.shape[0]
    d_inner = heads * SSM_HEAD_DIM
    nbc2 = 2 * N_GROUPS * D_STATE
    col = lambda x: x.reshape(b, d_inner, 1)
    per_head = lambda v: jnp.repeat(v, SSM_HEAD_DIM, axis=-1)
    z = col(proj[:, :d_inner])
    x = col(proj[:, d_inner:2 * d_inner])
    bc = proj[:, 2 * d_inner:2 * d_inner + nbc2].reshape(b, 1, nbc2)
    dt = col(per_head(proj[:, 2 * d_inner + nbc2:2 * d_inner + nbc2 + heads]))
    bufx = conv_buf[:, :, :d_inner].transpose(0, 2, 1)
    bufbc = conv_buf[:, :, d_inner:]
    pcol = lambda v: v.reshape(d_inner, 1)
    params = [conv_w[:, :d_inner].T, conv_w[:, d_inner:], pcol(conv_b[:d_inner]), conv_b[d_inner:].reshape(1, nbc2),
              pcol(per_head(dt_bias)), pcol(per_head(a_log)), pcol(per_head(d_skip)), pcol(norm_w)]
    seq = lambda *shape: pl.BlockSpec((None,) + shape, lambda i: (i,) + (0,) * len(shape))
    full = lambda a: pl.BlockSpec(a.shape, lambda i: (0,) * a.ndim)
    y, nbx, nbbc, h = pl.pallas_call(
        functools.partial(_ssm_step_kernel, heads=heads),
        grid=(b,),
        in_specs=[seq(d_inner, 1), seq(d_inner, 1), seq(1, nbc2), seq(d_inner, 1), seq(d_inner, CONV_W - 1),
                  seq(CONV_W - 1, nbc2), seq(d_inner, D_STATE)] + [full(a) for a in params],
        out_specs=[seq(d_inner, 1), seq(d_inner, CONV_W - 1), seq(CONV_W - 1, nbc2), seq(d_inner, D_STATE)],
        out_shape=[jax.ShapeDtypeStruct((b, d_inner, 1), F32), jax.ShapeDtypeStruct((b, d_inner, CONV_W - 1), F32),
                   jax.ShapeDtypeStruct((b, CONV_W - 1, nbc2), F32), jax.ShapeDtypeStruct((b, d_inner, D_STATE), F32)],
        compiler_params=_cparams(("parallel",)),
        name="ssm_step",
    )(z, x, bc, dt, bufx, bufbc, h0.reshape(b, d_inner, D_STATE), *params)
    new_buf = jnp.concatenate([nbx.transpose(0, 2, 1), nbbc], axis=2)
    return y.reshape(b, d_inner), new_buf, h.reshape(b, heads, SSM_HEAD_DIM, D_STATE)


def _fox_decode_kernel(pt_ref, q_ref, kc_ref, vc_ref, lfc_ref, *refs, pages):
    k_refs = refs[:pages]
    v_refs = refs[pages:2 * pages]
    lf_refs = refs[2 * pages:3 * pages]
    o_ref, qb_ref, m_ref, l_ref, acc_ref, later_ref = refs[3 * pages:]
    j = pl.program_id(1)
    first_lane = lax.broadcasted_iota(I32, (1, LANES), 1) == 0

    @pl.when(j == 0)
    def _():
        s_rows = []
        for h in range(FOX_HEADS):
            qh = q_ref[h] * (FOX_HEAD_DIM ** -0.5)
            qb_ref[h] = jnp.broadcast_to(qh, (FOX_HEAD_DIM, LANES))
            s_rows.append(jnp.sum(qh * kc_ref[h], axis=0, keepdims=True))
            acc_ref[h] = jnp.where(first_lane, vc_ref[h], 0.0)
        m_ref[...] = jnp.broadcast_to(jnp.concatenate(s_rows, axis=0), m_ref.shape)
        l_ref[...] = jnp.broadcast_to(jnp.where(first_lane, 1.0, 0.0), l_ref.shape)
        later_ref[...] = jnp.broadcast_to(lfc_ref[...], later_ref.shape)

    urow = lax.broadcasted_iota(I32, (LANES, LANES), 0)
    tcol = lax.broadcasted_iota(I32, (LANES, LANES), 1)
    after = jnp.where(urow > tcol, 1.0, 0.0).astype(BF16)
    for u in range(pages):
        lf = lf_refs[u][...]
        later = later_ref[...]
        bias = _dot_f32_right(lf, after) + later
        rows = [jnp.sum(k_refs[u][h] * qb_ref[h], axis=0, keepdims=True) for h in range(FOX_HEADS)]
        s = jnp.concatenate(rows, axis=0) + bias
        m_old = m_ref[...]
        m_new = jnp.maximum(m_old, jnp.max(s, axis=-1, keepdims=True))
        alpha = jnp.exp(m_old - m_new)
        p = jnp.exp(s - m_new)
        l_ref[...] = alpha * l_ref[...] + p
        m_ref[...] = m_new
        for h in range(FOX_HEADS):
            acc_ref[h] = alpha[h:h + 1, :] * acc_ref[h] + p[h:h + 1, :] * v_refs[u][h]
        later_ref[...] = later + jnp.sum(lf, axis=-1, keepdims=True)

    @pl.when(j == pl.num_programs(1) - 1)
    def _():
        l = jnp.sum(l_ref[...], axis=-1, keepdims=True)
        for h in range(FOX_HEADS):
            o_ref[h] = jnp.sum(acc_ref[h], axis=-1, keepdims=True) / l[h:h + 1, :]


def _fox_decode(q, k_cur, v_cur, logf_cur, cache_k, cache_v, cache_logf, page_table, *, pages):
    b, n_pages = page_table.shape
    page = cache_k.shape[1]
    assert page == LANES and n_pages % pages == 0
    kt = cache_k.transpose(0, 2, 3, 1)
    vt = cache_v.transpose(0, 2, 3, 1)
    lft = cache_logf.transpose(0, 2, 1)
    col = lambda x: x.reshape(b, FOX_HEADS, FOX_HEAD_DIM, 1)
    tok = pl.BlockSpec((None, FOX_HEADS, FOX_HEAD_DIM, 1), lambda bi, j, pt: (bi, 0, 0, 0))
    newest_first = lambda bi, j, pt, u: pt[bi, n_pages - 1 - (j * pages + u)]
    kv_page = lambda u: pl.BlockSpec((None, FOX_HEADS, FOX_HEAD_DIM, page),
                                     lambda bi, j, pt, u=u: (newest_first(bi, j, pt, u), 0, 0, 0))
    lf_page = lambda u: pl.BlockSpec((None, FOX_HEADS, page), lambda bi, j, pt, u=u: (newest_first(bi, j, pt, u), 0, 0))
    out = pl.pallas_call(
        functools.partial(_fox_decode_kernel, pages=pages),
        grid_spec=pltpu.PrefetchScalarGridSpec(
            num_scalar_prefetch=1,
            grid=(b, n_pages // pages),
            in_specs=[tok, tok, tok, pl.BlockSpec((None, FOX_HEADS, 1), lambda bi, j, pt: (bi, 0, 0))]
                     + [kv_page(u) for u in range(pages)] * 2 + [lf_page(u) for u in range(pages)],
            out_specs=tok,
            scratch_shapes=[pltpu.VMEM((FOX_HEADS, FOX_HEAD_DIM, LANES), F32), pltpu.VMEM((FOX_HEADS, LANES), F32),
                            pltpu.VMEM((FOX_HEADS, LANES), F32), pltpu.VMEM((FOX_HEADS, FOX_HEAD_DIM, LANES), F32),
                            pltpu.VMEM((FOX_HEADS, LANES), F32)],
        ),
        out_shape=jax.ShapeDtypeStruct((b, FOX_HEADS, FOX_HEAD_DIM, 1), F32),
        compiler_params=_cparams(("parallel", "arbitrary")),
        name="fox_decode",
    )(page_table, col(q), col(k_cur), col(v_cur), logf_cur.reshape(b, FOX_HEADS, 1),
      *([kt] * pages), *([vt] * pages), *([lft] * pages))
    return out.reshape(b, FOX_WIDTH)


_COL_Q = RWKV_PROJ
_COL_K = _COL_Q + FOX_WIDTH
_COL_V = _COL_K + FOX_WIDTH
_COL_FL = _COL_V + FOX_WIDTH
_EVEN_COLS = _COL_FL + LANES


def _even_weights(w_in):
    q, k, v = (w_in[:, i * FOX_WIDTH:(i + 1) * FOX_WIDTH] for i in range(3))
    fl = w_in[:, 3 * FOX_WIDTH:3 * FOX_WIDTH + FOX_HEADS]
    rw = w_in[:, 3 * FOX_WIDTH + FOX_HEADS:][:, _RWKV_PERM]
    fl_pad = jnp.pad(fl, ((0, 0), (0, LANES - FOX_HEADS)))
    return jnp.concatenate([rw, q, k, v, fl_pad], axis=1).astype(BF16)


def _even_layer(y, gain, w_cat, w_out, fb, rw, *, batch, seq, past, shift0, wkv0):
    n = y.shape[0]
    tm = min(512, n)
    fb_pad = jnp.pad(fb, (0, LANES - FOX_HEADS)).reshape(1, LANES)
    if past is None:
        proj, qkv = _norm_matmul(y, gain, w_cat, tm=512, tn=_EVEN_COLS, bf16_cols=(_COL_Q, _COL_FL))
        logf, ct = _fox_prep(proj, _COL_FL // LANES, fb_pad, t=256)
        k_t, v_t = _norm_matmul_t(y, gain, w_cat[:, _COL_K:_COL_FL].T, tm=1024, parts=2)
        tok_major = lambda x: x.reshape(FOX_HEADS, FOX_HEAD_DIM, n).transpose(2, 0, 1).reshape(
            batch, seq, FOX_HEADS, FOX_HEAD_DIM)
        k_out, v_out = tok_major(k_t), tok_major(v_t)
        o_fox = _fox_attn(qkv, ct, tq=1024)
        o_rwkv, shift_new, wkv_new = _rwkv_mix(proj, shift0[:, _RWKV_PERM], wkv0, rw, batch=batch,
                                               t_prep=256, t_scan=256, shifted=True)
    else:
        proj = _norm_matmul(y, gain, w_cat, tm=n, tn=_EVEN_COLS)
        logf, _ = _fox_prep(proj, _COL_FL // LANES, fb_pad, t=n)
        heads = lambda t: t.reshape(batch, seq, FOX_HEADS, FOX_HEAD_DIM)
        k_new = proj[:, _COL_K:_COL_V]
        v_new = proj[:, _COL_V:_COL_FL]
        k_out, v_out = heads(k_new), heads(v_new)
        cache_k, cache_v, cache_logf, page_table = past
        o_fox = _fox_decode(proj[:, _COL_Q:_COL_K], k_new, v_new, logf, cache_k, cache_v, cache_logf, page_table,
                            pages=16)
        o_rwkv, shift_new, wkv_new = _rwkv_mix(proj, shift0[:, _RWKV_PERM], wkv0, rw, batch=batch,
                                               t_prep=n, t_scan=1, shifted=False)
    w_out_b = w_out.astype(BF16)
    y = _matmul_res([o_fox, o_rwkv], [w_out_b[:FOX_WIDTH], w_out_b[FOX_WIDTH:]], y, tm=tm, tn=y.shape[1])
    return y, k_out, v_out, logf.reshape(batch, seq, FOX_HEADS), wkv_new, shift_new


def kernel(x_prompt, x_sample, cache_k, cache_v, cache_logf, page_table, state_wkv, state_shift, state_conv, state_ssm, norm_mix, norm_ffn, norm_final, w_in_even, w_out_even, fox_fb, rwkv_mu, rwkv_w0, rwkv_w2, rwkv_a0, rwkv_a2, rwkv_g2, rwkv_kk, rwkv_ka, rwkv_rk, rwkv_lnw, rwkv_lnb, w_in_odd, w_out_odd, ssm_conv_w, ssm_conv_b, ssm_dt_bias, ssm_a_log, ssm_d, ssm_norm_w, peer_wq, peer_keys, peer_u, peer_v):
    bp, seq_p, dm = x_prompt.shape
    bs, seq_s, _ = x_sample.shape
    assert bp == 1 and seq_s == 1
    depth = norm_mix.shape[0]
    ssm_heads = ssm_dt_bias.shape[1]
    yp = x_prompt.reshape(bp * seq_p, dm)
    ys = x_sample.reshape(bs * seq_s, dm)
    u_all = peer_u.astype(BF16)
    v_all = peer_v.astype(BF16)
    outs_p = {name: [] for name in ("k", "v", "logf", "wkv", "shift", "conv", "ssm")}
    outs_s = {name: [] for name in outs_p}
    for layer in range(depth):
        if layer % 2 == 0:
            e = layer // 2
            rw = (rwkv_mu[e], rwkv_w0[e], rwkv_w2[e], rwkv_a0[e], rwkv_a2[e], rwkv_g2[e],
                  rwkv_kk[e], rwkv_ka[e], rwkv_rk[e], rwkv_lnw[e], rwkv_lnb[e])
            w_cat = _even_weights(w_in_even[e])
            yp, *res_p = _even_layer(
                yp, norm_mix[layer], w_cat, w_out_even[e], fox_fb[e], rw, batch=bp, seq=seq_p, past=None,
                shift0=jnp.zeros((bp, RWKV_PROJ), F32),
                wkv0=jnp.zeros((bp, RWKV_HEADS, RWKV_HEAD_DIM, RWKV_HEAD_DIM), F32))
            ys, *res_s = _even_layer(
                ys, norm_mix[layer], w_cat, w_out_even[e], fox_fb[e], rw, batch=bs, seq=seq_s,
                past=(cache_k[e], cache_v[e], cache_logf[e], page_table), shift0=state_shift[e], wkv0=state_wkv[e])
            for outs, res in ((outs_p, res_p), (outs_s, res_s)):
                for name, val in zip(("k", "v", "logf", "wkv", "shift"), res):
                    outs[name].append(val)
        else:
            o = layer // 2
            mprm = (ssm_conv_w[o], ssm_conv_b[o], ssm_dt_bias[o], ssm_a_log[o], ssm_d[o], ssm_norm_w[o])
            w_cat = jnp.pad(w_in_odd[o], ((0, 0), (0, LANES - ssm_heads))).astype(BF16)
            w_out = w_out_odd[o].astype(BF16)
            tn = w_cat.shape[1]
            proj_p = _norm_matmul(yp, norm_mix[layer], w_cat, tm=256, tn=tn)
            y_m, conv_p, h_p = _ssd(proj_p, *mprm, heads=ssm_heads)
            yp = _matmul_res([y_m], [w_out], yp, tm=512, tn=dm)
            proj_s = _norm_matmul(ys, norm_mix[layer], w_cat, tm=bs, tn=tn)
            y_s, conv_s, h_s = _ssm_step(proj_s, state_conv[o], state_ssm[o], *mprm, heads=ssm_heads)
            ys = _matmul_res([y_s], [w_out], ys, tm=bs, tn=dm)
            outs_p["conv"].append(conv_p[None])
            outs_p["ssm"].append(h_p[None])
            outs_s["conv"].append(conv_s)
            outs_s["ssm"].append(h_s)
        wq = peer_wq[layer].astype(BF16)
        keys_pad = _peer_keys_padded(peer_keys[layer])
        yp = _peer(yp, norm_ffn[layer], wq, keys_pad, u_all, v_all, layer=layer,
                   tm_proj=512, tm_route=128, tm_mlp=512, te=2048)
        ys = _peer(ys, norm_ffn[layer], wq, keys_pad, u_all, v_all, layer=layer,
                   tm_proj=bs, tm_route=bs, tm_mlp=bs, te=2048)
    y_prompt = _rmsnorm(yp, norm_final, tm=512).reshape(bp, seq_p, dm)
    y_sample = _rmsnorm(ys, norm_final, tm=bs).reshape(bs, seq_s, dm)
    order = ("k", "v", "logf", "wkv", "shift", "conv", "ssm")
    return ((y_prompt, y_sample) + tuple(jnp.stack(outs_p[name]) for name in order)
            + tuple(jnp.stack(outs_s[name]) for name in order))
```

```python
import functools

import numpy as np
import jax
import jax.numpy as jnp
from jax import lax
from jax.experimental import pallas as pl
from jax.experimental.pallas import tpu as pltpu

F32 = jnp.float32
BF16 = jnp.bfloat16
I32 = jnp.int32

LANES = 128
SUBLANES = 8

RMS_EPS = 1e-6
GN_EPS = 64e-5

FOX_HEADS = 8
FOX_HEAD_DIM = 64
FOX_WIDTH = FOX_HEADS * FOX_HEAD_DIM
RWKV_HEADS = 8
RWKV_HEAD_DIM = 64
RWKV_WIDTH = RWKV_HEADS * RWKV_HEAD_DIM
DECAY_LORA = 64
ICLR_LORA = 64
GATE_LORA = 128
RWKV_PROJ = 4 * RWKV_WIDTH - RWKV_WIDTH + DECAY_LORA + ICLR_LORA + GATE_LORA
HEAD_PAIRS = RWKV_HEADS // 2

SSM_HEAD_DIM = 64
D_STATE = 128
N_GROUPS = 8
CONV_W = 4
CHUNK = 128

N_KEYS = 128
PEER_HEADS = 8
PEER_TOPK = 16
D_KEY = 128


def _cparams(semantics, vmem_mb=48):
    return pltpu.CompilerParams(dimension_semantics=semantics, vmem_limit_bytes=vmem_mb * 1024 * 1024)


def _split3(x):
    hi = x.astype(BF16)
    r1 = x - hi.astype(F32)
    mid = r1.astype(BF16)
    lo = (r1 - mid.astype(F32)).astype(BF16)
    return hi, mid, lo


def _dot_f32_right(x, m_bf16):
    hi, mid, lo = _split3(x)
    d = lambda a: jnp.dot(a, m_bf16, preferred_element_type=F32)
    return d(hi) + d(mid) + d(lo)


def _dot_f32_left(m_bf16, x):
    hi, mid, lo = _split3(x)
    d = lambda a: jnp.dot(m_bf16, a, preferred_element_type=F32)
    return d(hi) + d(mid) + d(lo)


def _sigmoid(x):
    return 1.0 / (1.0 + jnp.exp(-x))


def _softplus(x):
    return jnp.maximum(x, 0.0) + jnp.log1p(jnp.exp(-jnp.abs(x)))


def _silu(x):
    return x * _sigmoid(x)


def _norm_matmul_kernel(x_ref, g_ref, w_ref, o_ref, *rest, bf16_cols):
    xn_ref = rest[-1]

    @pl.when(pl.program_id(1) == 0)
    def _():
        x = x_ref[...]
        ms = jnp.mean(x * x, axis=-1, keepdims=True)
        xn_ref[...] = (x * lax.rsqrt(ms + RMS_EPS) * g_ref[...]).astype(BF16)

    res = jnp.dot(xn_ref[...], w_ref[...], preferred_element_type=F32)
    o_ref[...] = res
    if bf16_cols is not None:
        rest[0][...] = res[:, bf16_cols[0]:bf16_cols[1]].astype(BF16)


def _norm_matmul(x, g, w_bf16, *, tm, tn, bf16_cols=None):
    m, k = x.shape
    n = w_bf16.shape[1]
    assert m % tm == 0 and n % tn == 0
    out_specs = [pl.BlockSpec((tm, tn), lambda i, j: (i, j))]
    out_shape = [jax.ShapeDtypeStruct((m, n), F32)]
    if bf16_cols is not None:
        assert tn == n and bf16_cols[0] % LANES == 0 and bf16_cols[1] % LANES == 0
        width = bf16_cols[1] - bf16_cols[0]
        out_specs.append(pl.BlockSpec((tm, width), lambda i, j: (i, 0)))
        out_shape.append(jax.ShapeDtypeStruct((m, width), BF16))
    outs = pl.pallas_call(
        functools.partial(_norm_matmul_kernel, bf16_cols=bf16_cols),
        grid=(m // tm, n // tn),
        in_specs=[
            pl.BlockSpec((tm, k), lambda i, j: (i, 0)),
            pl.BlockSpec((1, k), lambda i, j: (0, 0)),
            pl.BlockSpec((k, tn), lambda i, j: (0, j)),
        ],
        out_specs=out_specs,
        out_shape=out_shape,
        scratch_shapes=[pltpu.VMEM((tm, k), BF16)],
        compiler_params=_cparams(("parallel", "arbitrary")),
        name="norm_matmul",
    )(x, g.reshape(1, k), w_bf16)
    return outs if bf16_cols is not None else outs[0]


def _norm_matmul_t_kernel(x_ref, g_ref, wt_ref, *o_refs):
    x = x_ref[...]
    ms = jnp.mean(x * x, axis=-1, keepdims=True)
    xn = (x * lax.rsqrt(ms + RMS_EPS) * g_ref[...]).astype(BF16)
    res = lax.dot_general(wt_ref[...], xn, (((1,), (1,)), ((), ())), preferred_element_type=F32)
    rows = res.shape[0] // len(o_refs)
    for i, o_ref in enumerate(o_refs):
        o_ref[...] = res[i * rows:(i + 1) * rows]


def _norm_matmul_t(x, g, wt_bf16, *, tm, parts):
    m, k = x.shape
    c = wt_bf16.shape[0]
    assert c % parts == 0
    return pl.pallas_call(
        _norm_matmul_t_kernel,
        grid=(m // tm,),
        in_specs=[pl.BlockSpec((tm, k), lambda i: (i, 0)),
                  pl.BlockSpec((1, k), lambda i: (0, 0)),
                  pl.BlockSpec((c, k), lambda i: (0, 0))],
        out_specs=[pl.BlockSpec((c // parts, tm), lambda i: (0, i))] * parts,
        out_shape=[jax.ShapeDtypeStruct((c // parts, m), F32)] * parts,
        compiler_params=_cparams(("parallel",)),
        name="norm_matmul_t",
    )(x, g.reshape(1, k), wt_bf16)


def _matmul_res_kernel(*refs, n_in):
    a_refs = refs[:n_in]
    w_refs = refs[n_in:2 * n_in]
    r_ref, o_ref = refs[2 * n_in], refs[2 * n_in + 1]
    acc = r_ref[...]
    for a_ref, w_ref in zip(a_refs, w_refs):
        acc = acc + jnp.dot(a_ref[...].astype(BF16), w_ref[...], preferred_element_type=F32)
    o_ref[...] = acc


def _matmul_res(a_list, w_list, res, *, tm, tn):
    m, n = res.shape
    n_in = len(a_list)
    assert m % tm == 0 and n % tn == 0
    in_specs = ([pl.BlockSpec((tm, a.shape[1]), lambda i, j: (i, 0)) for a in a_list]
                + [pl.BlockSpec((w.shape[0], tn), lambda i, j: (0, j)) for w in w_list]
                + [pl.BlockSpec((tm, tn), lambda i, j: (i, j))])
    return pl.pallas_call(
        functools.partial(_matmul_res_kernel, n_in=n_in),
        grid=(m // tm, n // tn),
        in_specs=in_specs,
        out_specs=pl.BlockSpec((tm, tn), lambda i, j: (i, j)),
        out_shape=jax.ShapeDtypeStruct((m, n), F32),
        compiler_params=_cparams(("parallel", "parallel")),
        name="matmul_res",
    )(*a_list, *w_list, res)


def _rmsnorm_kernel(x_ref, g_ref, o_ref):
    x = x_ref[...]
    ms = jnp.mean(x * x, axis=-1, keepdims=True)
    o_ref[...] = x * lax.rsqrt(ms + RMS_EPS) * g_ref[...]


def _rmsnorm(x, g, *, tm):
    m, k = x.shape
    return pl.pallas_call(
        _rmsnorm_kernel,
        grid=(m // tm,),
        in_specs=[pl.BlockSpec((tm, k), lambda i: (i, 0)), pl.BlockSpec((1, k), lambda i: (0, 0))],
        out_specs=pl.BlockSpec((tm, k), lambda i: (i, 0)),
        out_shape=jax.ShapeDtypeStruct((m, k), F32),
        compiler_params=_cparams(("parallel",)),
        name="rmsnorm",
    )(x, g.reshape(1, k))


def _fox_prep_kernel(fl_ref, fb_ref, lf_ref, ct_ref, carry_ref, *, t):
    @pl.when(pl.program_id(0) == 0)
    def _():
        carry_ref[...] = jnp.zeros_like(carry_ref)

    x = fl_ref[...] + fb_ref[...]
    lf = jnp.minimum(x, 0.0) - jnp.log1p(jnp.exp(-jnp.abs(x)))
    lf_ref[...] = lf[:, :FOX_HEADS]
    row = lax.broadcasted_iota(I32, (t, t), 0)
    col = lax.broadcasted_iota(I32, (t, t), 1)
    tri = jnp.where(col <= row, 1.0, 0.0).astype(BF16)
    c = _dot_f32_left(tri, lf) + carry_ref[...]
    carry_ref[...] = c[t - 1:t, :]
    ct_ref[...] = c.T[:FOX_HEADS, :]


def _fox_prep(proj, fl_block, fb_pad, *, t):
    n = proj.shape[0]
    return pl.pallas_call(
        functools.partial(_fox_prep_kernel, t=t),
        grid=(n // t,),
        in_specs=[pl.BlockSpec((t, LANES), lambda i: (i, fl_block)),
                  pl.BlockSpec((1, LANES), lambda i: (0, 0))],
        out_specs=[pl.BlockSpec((t, FOX_HEADS), lambda i: (i, 0)),
                   pl.BlockSpec((FOX_HEADS, t), lambda i: (0, i))],
        out_shape=[jax.ShapeDtypeStruct((n, FOX_HEADS), F32),
                   jax.ShapeDtypeStruct((FOX_HEADS, n), F32)],
        scratch_shapes=[pltpu.VMEM((1, LANES), F32)],
        compiler_params=_cparams(("arbitrary",)),
        name="fox_prep",
    )(proj, fb_pad)


_SKIP_LOGIT_GAP = 110.0
_SKIP_NORM_SLACK = 1.05


def _fox_bounds_kernel(q_ref, k_ref, ct_ref, sel_ref, jmin_ref, qmax_ref, kmax_ref, cs_ref, ce_ref, *, tq, nq):
    i = pl.program_id(0)
    lane = lax.broadcasted_iota(I32, (FOX_HEADS, LANES), 1)
    sub = lax.broadcasted_iota(I32, (FOX_HEADS, LANES), 0)

    @pl.when(i == 0)
    def _():
        qmax_ref[...] = jnp.zeros_like(qmax_ref)
        kmax_ref[...] = jnp.zeros_like(kmax_ref)
        cs_ref[...] = jnp.zeros_like(cs_ref)
        ce_ref[...] = jnp.zeros_like(ce_ref)

    def head_sq_norm_max(x_ref):
        x = x_ref[...].astype(F32)
        sq = jnp.dot((x * x).astype(BF16), sel_ref[...], preferred_element_type=F32)
        return jnp.max(sq, axis=0, keepdims=True)

    qmax_ref[...] = jnp.maximum(qmax_ref[...], head_sq_norm_max(q_ref))
    kmax_ref[...] = jnp.maximum(kmax_ref[...], head_sq_norm_max(k_ref))
    cs_ref[...] = jnp.where(lane == i, ct_ref[:, 0:1], cs_ref[...])
    ce_ref[...] = jnp.where(lane == i, ct_ref[:, tq - 1:tq], ce_ref[...])

    @pl.when(i == nq - 1)
    def _():
        def to_col(row):
            return jnp.sum(jnp.where(sub == lane, row, 0.0), axis=-1, keepdims=True)

        b2 = (2.0 * _SKIP_NORM_SLACK * FOX_HEAD_DIM ** -0.5) * jnp.sqrt(to_col(qmax_ref[...]) * to_col(kmax_ref[...]))
        ce = ce_ref[...]
        out = jnp.zeros((FOX_HEADS, LANES), F32)
        for qi in range(nq):
            gap = b2 + cs_ref[:, qi:qi + 1] - ce
            skip = jnp.where((gap < -_SKIP_LOGIT_GAP) & (lane < qi), 1.0, 0.0)
            for p in range(FOX_HEADS // 2):
                both = skip[2 * p:2 * p + 1, :] * skip[2 * p + 1:2 * p + 2, :]
                count = jnp.sum(both, axis=-1, keepdims=True)
                out = jnp.where((sub == p) & (lane == qi), count, out)
        jmin_ref[...] = out.astype(I32)


def _fox_bounds(qkv_bf16, ct, *, tq):
    n = qkv_bf16.shape[0]
    nq = n // tq
    assert nq <= LANES
    sel = jnp.asarray((np.arange(FOX_WIDTH)[:, None] // FOX_HEAD_DIM == np.arange(LANES)[None, :]).astype(np.float32), BF16)
    table = pl.pallas_call(
        functools.partial(_fox_bounds_kernel, tq=tq, nq=nq),
        grid=(nq,),
        in_specs=[pl.BlockSpec((tq, FOX_WIDTH), lambda i: (i, 0)),
                  pl.BlockSpec((tq, FOX_WIDTH), lambda i: (i, 1)),
                  pl.BlockSpec((FOX_HEADS, tq), lambda i: (0, i)),
                  pl.BlockSpec(sel.shape, lambda i: (0, 0))],
        out_specs=pl.BlockSpec((FOX_HEADS, LANES), lambda i: (0, 0)),
        out_shape=jax.ShapeDtypeStruct((FOX_HEADS, LANES), I32),
        scratch_shapes=[pltpu.VMEM((1, LANES), F32), pltpu.VMEM((1, LANES), F32),
                        pltpu.VMEM((FOX_HEADS, LANES), F32), pltpu.VMEM((FOX_HEADS, LANES), F32)],
        compiler_params=_cparams(("arbitrary",)),
        name="fox_bounds",
    )(qkv_bf16, qkv_bf16, ct, sel)
    return table[:FOX_HEADS // 2, :nq]


def _fox_attn_kernel(jmin_ref, q_ref, k_ref, v_ref, ct_ref, o_ref, qs_ref, m_ref, l_ref, acc_ref, *, tq):
    i = pl.program_id(1)
    lane = lax.broadcasted_iota(I32, (1, LANES), 1)
    left = lane < FOX_HEAD_DIM
    q = q_ref[...] * jnp.asarray(FOX_HEAD_DIM ** -0.5, BF16)
    zero = jnp.zeros_like(q)
    qs_ref[0] = jnp.where(left, q, zero)
    qs_ref[1] = jnp.where(left, zero, q)
    q0 = pl.multiple_of(i * tq, tq)
    cref = [ct_ref[h:h + 1, pl.ds(q0, LANES)][:, 0:1] for h in range(2)]

    m_ref[...] = jnp.full_like(m_ref, -jnp.inf)
    l_ref[...] = jnp.zeros_like(l_ref)
    acc_ref[...] = jnp.zeros_like(acc_ref)
    ncol = tq // LANES

    def step(j, masked):
        k0 = pl.multiple_of(j * tq, tq)
        kb = k_ref[pl.ds(k0, tq), :]
        vb = v_ref[pl.ds(k0, tq), :]
        bias = [cref[h] - ct_ref[h:h + 1, pl.ds(k0, tq)] for h in range(2)]
        for h in range(2):
            s = lax.dot_general(qs_ref[h], kb, (((1,), (1,)), ((), ())), preferred_element_type=F32)
            s = s + bias[h]
            if masked:
                r = lax.broadcasted_iota(I32, (tq, tq), 0)
                c = lax.broadcasted_iota(I32, (tq, tq), 1)
                s = jnp.where(c <= r, s, -jnp.inf)
            cols = [s[:, c * LANES:(c + 1) * LANES] for c in range(ncol)]
            lane_max = functools.reduce(jnp.maximum, cols)
            m_old = m_ref[h]
            m_new = jnp.maximum(m_old, jnp.max(lane_max, axis=-1, keepdims=True))
            alpha = jnp.exp(m_old - m_new)
            ps = [jnp.exp(c - m_new) for c in cols]
            l_ref[h] = alpha * l_ref[h] + functools.reduce(jnp.add, ps)
            m_ref[h] = m_new
            p = jnp.concatenate([x.astype(BF16) for x in ps], axis=1)
            acc_ref[h] = alpha * acc_ref[h] + jnp.dot(p, vb, preferred_element_type=F32)

    def body(j, carry):
        step(j, False)
        return carry

    lax.fori_loop(jmin_ref[pl.program_id(0), i], i, body, 0)
    step(i, True)
    l0 = jnp.sum(l_ref[0], axis=-1, keepdims=True)
    l1 = jnp.sum(l_ref[1], axis=-1, keepdims=True)
    o_ref[...] = jnp.where(left, acc_ref[0] / l0, acc_ref[1] / l1).astype(o_ref.dtype)


def _fox_attn(qkv_bf16, ct, *, tq):
    n = qkv_bf16.shape[0]
    nb = FOX_WIDTH // LANES
    jmin = _fox_bounds(qkv_bf16, ct, tq=tq)
    return pl.pallas_call(
        functools.partial(_fox_attn_kernel, tq=tq),
        grid_spec=pltpu.PrefetchScalarGridSpec(
            num_scalar_prefetch=1,
            grid=(nb, n // tq),
            in_specs=[
                pl.BlockSpec((tq, LANES), lambda p, i, jm: (i, p)),
                pl.BlockSpec((n, LANES), lambda p, i, jm: (0, nb + p)),
                pl.BlockSpec((n, LANES), lambda p, i, jm: (0, 2 * nb + p)),
                pl.BlockSpec((None, 2, n), lambda p, i, jm: (p, 0, 0)),
            ],
            out_specs=pl.BlockSpec((tq, LANES), lambda p, i, jm: (i, p)),
            scratch_shapes=[pltpu.VMEM((2, tq, LANES), BF16), pltpu.VMEM((2, tq, LANES), F32),
                            pltpu.VMEM((2, tq, LANES), F32), pltpu.VMEM((2, tq, LANES), F32)],
        ),
        out_shape=jax.ShapeDtypeStruct((n, FOX_WIDTH), BF16),
        compiler_params=_cparams(("arbitrary", "arbitrary")),
        name="fox_attn",
    )(jmin, qkv_bf16, qkv_bf16, qkv_bf16, ct.reshape(nb, 2, n))


_RW = RWKV_WIDTH
_RWKV_PERM = np.concatenate([
    np.arange(0, _RW),
    np.arange(_RW + DECAY_LORA, 2 * _RW + DECAY_LORA),
    np.arange(2 * _RW + DECAY_LORA, 3 * _RW + DECAY_LORA),
    np.arange(3 * _RW + DECAY_LORA + ICLR_LORA, RWKV_PROJ),
    np.arange(_RW, _RW + DECAY_LORA),
    np.arange(3 * _RW + DECAY_LORA, 3 * _RW + DECAY_LORA + ICLR_LORA),
])
_RWKV_INV_PERM = np.argsort(_RWKV_PERM)


def _head_block_ones(width, head_dim):
    idx = np.arange(width) // head_dim
    return jnp.asarray((idx[:, None] == idx[None, :]).astype(np.float32), BF16)


def _rwkv_prep_math(p, p_prev, mu, w0, w2p, a0, a2p, g2, k_k, k_a, bones):
    xs = p + mu * (p_prev - p)
    r = xs[:, 0:_RW]
    k = xs[:, _RW:2 * _RW]
    v = xs[:, 2 * _RW:3 * _RW]
    gl = xs[:, 3 * _RW:3 * _RW + GATE_LORA]
    wa = xs[:, 3 * _RW + GATE_LORA:]
    w = -_softplus(-(w0 + jnp.dot(jnp.tanh(wa).astype(BF16), w2p, preferred_element_type=F32))) - 0.5
    decay = jnp.exp(-jnp.exp(w))
    a = _sigmoid(a0 + jnp.dot(wa.astype(BF16), a2p, preferred_element_type=F32))
    g = jnp.dot(_sigmoid(gl).astype(BF16), g2, preferred_element_type=F32)
    kkr = k * k_k
    ss = _dot_f32_right(kkr * kkr, bones)
    kk = kkr / jnp.maximum(jnp.sqrt(ss), 1e-12)
    k2 = k * (1.0 + (a - 1.0) * k_a)
    return r, k2, v, kk, kk * a, decay, g


def _rwkv_prep_kernel(p_ref, prev_ref, mu_ref, w0_ref, w2_ref, a0_ref, a2_ref, g2_ref, kk_ref, ka_ref, bones_ref,
                      r_o, k_o, v_o, kk_o, b_o, d_o, g_o, last_o, buf_ref, *, t, shifted):
    p = p_ref[...]
    if shifted:
        @pl.when(pl.program_id(0) == 0)
        def _():
            buf_ref[SUBLANES - 1:SUBLANES, :] = prev_ref[...]

        buf_ref[SUBLANES:SUBLANES + t, :] = p
        p_prev = buf_ref[SUBLANES - 1:SUBLANES - 1 + t, :]
        buf_ref[SUBLANES - 1:SUBLANES, :] = p[t - 1:t, :]
        last_o[...] = p[t - 1:t, :]
    else:
        p_prev = prev_ref[...]
        last_o[...] = p
    outs = _rwkv_prep_math(p, p_prev, mu_ref[...], w0_ref[...], w2_ref[...], a0_ref[...], a2_ref[...], g2_ref[...],
                           kk_ref[...], ka_ref[...], bones_ref[...])
    for o_ref, val in zip((r_o, k_o, v_o, kk_o, b_o, d_o, g_o), outs):
        o_ref[...] = val


def _rwkv_prep(proj, prev, mu, w0, w2p, a0, a2p, g2, k_k, k_a, bones, *, t, shifted):
    n = proj.shape[0]
    row = lambda w: pl.BlockSpec((1, w), lambda i: (0, 0))
    full = lambda a: pl.BlockSpec(a.shape, lambda i: (0, 0))
    tok = lambda w: pl.BlockSpec((t, w), lambda i: (i, 0))
    prev_spec = row(RWKV_PROJ) if shifted else tok(RWKV_PROJ)
    last_spec = row(RWKV_PROJ) if shifted else tok(RWKV_PROJ)
    last_shape = (1, RWKV_PROJ) if shifted else (n, RWKV_PROJ)
    return pl.pallas_call(
        functools.partial(_rwkv_prep_kernel, t=t, shifted=shifted),
        grid=(n // t,),
        in_specs=[tok(RWKV_PROJ), prev_spec, row(RWKV_PROJ), row(_RW), full(w2p), row(_RW), full(a2p), full(g2),
                  row(_RW), row(_RW), full(bones)],
        out_specs=[tok(_RW)] * 7 + [last_spec],
        out_shape=[jax.ShapeDtypeStruct((n, _RW), F32)] * 7 + [jax.ShapeDtypeStruct(last_shape, F32)],
        scratch_shapes=[pltpu.VMEM((SUBLANES + t, RWKV_PROJ), F32)],
        compiler_params=_cparams(("arbitrary",)),
        name="rwkv_prep",
    )(proj, prev, mu, w0, w2p, a0, a2p, g2, k_k, k_a, bones)


_SCAN_GROUP = 16


def _rwkv_scan_kernel(r_ref, k_ref, v_ref, kk_ref, b_ref, d_ref, s0_ref, o_ref, st_ref, s_ref, *, t):
    j = pl.program_id(1)

    @pl.when(j == 0)
    def _():
        s_ref[...] = s0_ref[...]

    left = lax.broadcasted_iota(I32, (RWKV_HEAD_DIM, LANES), 1) < RWKV_HEAD_DIM

    def seg(x):
        s0 = jnp.sum(jnp.where(left, x, 0.0), axis=-1, keepdims=True)
        s1 = jnp.sum(jnp.where(left, 0.0, x), axis=-1, keepdims=True)
        return jnp.where(left, s0, s1)

    group = _SCAN_GROUP if t % _SCAN_GROUP == 0 else t

    def readout(s, r_rows):
        s_bd = jnp.concatenate([jnp.where(left, s, 0.0), jnp.where(left, 0.0, s)], axis=0).astype(BF16)
        return lax.dot_general(r_rows, s_bd, (((1,), (1,)), ((), ())), preferred_element_type=F32)

    piece_pairs = ((0, 0), (0, 1), (0, 2), (1, 0), (1, 1), (2, 0))
    n_c = len(piece_pairs) * group
    kc = LANES * pl.cdiv(n_c, LANES)
    c_lane = lax.broadcasted_iota(I32, (LANES, kc), 1)
    c_step = c_lane % group
    c_live = c_lane < n_c
    pad_rows = jnp.zeros((kc - n_c, LANES), F32)

    def value_columns(v_t):
        v_p = [x.astype(F32) for x in _split3(v_t)]
        v_rows = jnp.concatenate([v_p[a] for a, _ in piece_pairs] + [pad_rows], axis=0)
        return v_rows.T

    def key_rows(k_t):
        k_p = [x.astype(F32) for x in _split3(k_t)]
        return jnp.concatenate([k_p[b] for _, b in piece_pairs] + [pad_rows], axis=0).astype(BF16)

    def outer_product(v_cols, k_rows, u):
        lhs = jnp.where(c_live & (c_step == u), v_cols, 0.0).astype(BF16)
        vk = jnp.dot(lhs, k_rows, preferred_element_type=F32)
        return jnp.where(left, vk[:RWKV_HEAD_DIM], vk[RWKV_HEAD_DIM:])

    n_groups = t // group

    def group_value_columns(gi):
        base = gi * group if isinstance(gi, int) else pl.multiple_of(gi * group, group)
        return tuple(value_columns(v_ref[pl.ds(base, group), p * LANES:(p + 1) * LANES]) for p in range(HEAD_PAIRS))

    def body(gi, carry):
        states, v_cols = carry
        base = pl.multiple_of(gi * group, group)
        states = list(states)
        tiles = []
        for p in range(HEAD_PAIRS):
            sl = slice(p * LANES, (p + 1) * LANES)
            tiles.append(tuple(ref[pl.ds(base, group), sl] for ref in (kk_ref, v_ref, d_ref, b_ref, k_ref, r_ref)))
        k_rows = [key_rows(tiles[p][4]) for p in range(HEAD_PAIRS)]
        vk_tiles = [[None] * group for _ in range(HEAD_PAIRS)]
        for u in range(group):
            for p in range(HEAD_PAIRS):
                vk_tiles[p][u] = outer_product(v_cols[p], k_rows[p], u)
        v_cols_next = group_value_columns(jnp.minimum(gi + 1, n_groups - 1))
        o_rows = [[] for _ in range(HEAD_PAIRS)]
        for u in range(group):
            row = lambda x: x[u:u + 1, :]
            for p in range(HEAD_PAIRS):
                kk_t, v_t, d_t, b_t, k_t, r_t = tiles[p]
                s = states[p]
                sk = seg(s * row(kk_t))
                s = s * row(d_t) - sk * row(b_t) + vk_tiles[p][u]
                states[p] = s
                r_rows = jnp.broadcast_to(r_t, (SUBLANES, LANES)) if group == 1 else r_t
                o_rows[p].append(readout(s, r_rows.astype(BF16))[u:u + 1, :])
        for p in range(HEAD_PAIRS):
            sl = slice(p * LANES, (p + 1) * LANES)
            o_ref[pl.ds(base, group), sl] = jnp.concatenate(o_rows[p], axis=0) if group > 1 else o_rows[p][0]
        return tuple(states), v_cols_next

    states, _ = lax.fori_loop(0, n_groups, body,
                              (tuple(s_ref[p] for p in range(HEAD_PAIRS)), group_value_columns(0)))
    for p in range(HEAD_PAIRS):
        s_ref[p] = states[p]

    @pl.when(j == pl.num_programs(1) - 1)
    def _():
        st_ref[...] = s_ref[...]


def _rwkv_scan(r, k, v, kk, b, d, s0_pairs, *, batch, t):
    n = r.shape[0]
    seq = n // batch
    tok = pl.BlockSpec((None, t, _RW), lambda bi, j: (bi, j, 0))
    st = pl.BlockSpec((None, HEAD_PAIRS, RWKV_HEAD_DIM, LANES), lambda bi, j: (bi, 0, 0, 0))
    o, s_t = pl.pallas_call(
        functools.partial(_rwkv_scan_kernel, t=t),
        grid=(batch, seq // t),
        in_specs=[tok] * 6 + [st],
        out_specs=[tok, st],
        out_shape=[jax.ShapeDtypeStruct((batch, seq, _RW), F32),
                   jax.ShapeDtypeStruct((batch, HEAD_PAIRS, RWKV_HEAD_DIM, LANES), F32)],
        scratch_shapes=[pltpu.VMEM((HEAD_PAIRS, RWKV_HEAD_DIM, LANES), F32)],
        compiler_params=_cparams(("arbitrary", "arbitrary")),
        name="rwkv_scan",
    )(*(x.reshape(batch, seq, _RW) for x in (r, k, v, kk, b, d)), s0_pairs)
    return o.reshape(n, _RW), s_t


def _rwkv_post_kernel(o_ref, r_ref, k_ref, v_ref, g_ref, lnw_ref, lnb_ref, rk_ref, bones_ref, out_ref):
    bones = bones_ref[...]
    inv = 1.0 / RWKV_HEAD_DIM
    o = o_ref[...]
    mean = _dot_f32_right(o, bones) * inv
    cen = o - mean
    var = _dot_f32_right(cen * cen, bones) * inv
    gn = cen * lax.rsqrt(var + GN_EPS) * lnw_ref[...] + lnb_ref[...]
    bonus = _dot_f32_right(r_ref[...] * k_ref[...] * rk_ref[...], bones) * v_ref[...]
    out_ref[...] = ((gn + bonus) * g_ref[...]).astype(out_ref.dtype)


def _rwkv_post(o, r, k, v, g, lnw, lnb, rk, bones, *, t):
    n = o.shape[0]
    tok = pl.BlockSpec((t, _RW), lambda i: (i, 0))
    row = pl.BlockSpec((1, _RW), lambda i: (0, 0))
    return pl.pallas_call(
        _rwkv_post_kernel,
        grid=(n // t,),
        in_specs=[tok] * 5 + [row] * 3 + [pl.BlockSpec(bones.shape, lambda i: (0, 0))],
        out_specs=tok,
        out_shape=jax.ShapeDtypeStruct((n, _RW), BF16),
        compiler_params=_cparams(("parallel",)),
        name="rwkv_post",
    )(o, r, k, v, g, lnw, lnb, rk, bones)


def _to_pairs(s):
    b = s.shape[0]
    return s.reshape(b, HEAD_PAIRS, 2, RWKV_HEAD_DIM, RWKV_HEAD_DIM).transpose(0, 1, 3, 2, 4).reshape(
        b, HEAD_PAIRS, RWKV_HEAD_DIM, LANES)


def _from_pairs(s):
    b = s.shape[0]
    return s.reshape(b, HEAD_PAIRS, RWKV_HEAD_DIM, 2, RWKV_HEAD_DIM).transpose(0, 1, 3, 2, 4).reshape(
        b, RWKV_HEADS, RWKV_HEAD_DIM, RWKV_HEAD_DIM)


def _rwkv_mix(proj, prev, s0, rw, *, batch, t_prep, t_scan, shifted):
    mu, w0, w2, a0, a2, g2, k_k, k_a, r_k, ln_w, ln_b = rw
    bones = _head_block_ones(_RW, RWKV_HEAD_DIM)
    zeros = jnp.zeros((LANES - DECAY_LORA, _RW), F32)
    w2p = jnp.concatenate([w2, zeros], axis=0).astype(BF16)
    a2p = jnp.concatenate([zeros, a2], axis=0).astype(BF16)
    row = lambda x: x.reshape(1, -1)
    r, k, v, kk, b, d, g, last = _rwkv_prep(
        proj, prev, row(mu[_RWKV_PERM]), row(w0), w2p, row(a0), a2p, g2.astype(BF16), row(k_k), row(k_a), bones,
        t=t_prep, shifted=shifted)
    o, s_t = _rwkv_scan(r, k, v, kk, b, d, _to_pairs(s0), batch=batch, t=t_scan)
    out = _rwkv_post(o, r, k, v, g, row(ln_w), row(ln_b), row(r_k), bones, t=t_prep)
    return out, last[:, _RWKV_INV_PERM], _from_pairs(s_t)


def _top_rows(s, k, order):
    vals, ords = [], []
    for _ in range(k):
        m = jnp.max(s, axis=0, keepdims=True)
        pick = jnp.min(jnp.where(s == m, order, jnp.inf), axis=0, keepdims=True)
        vals.append(m)
        ords.append(pick)
        s = jnp.where(order == pick, -jnp.inf, s)
    return jnp.concatenate(vals, axis=0), jnp.concatenate(ords, axis=0)


def _take_rows(table, pos):
    out = jnp.zeros(pos.shape, table.dtype)
    for i in range(table.shape[0]):
        out = jnp.where(pos == float(i), table[i:i + 1, :], out)
    return out


_CAND_GROUPS = ((0, 2, 16), (2, 4, 8))
_CAND_TAIL_I0, _CAND_TAIL_J = 4, 3
_CAND_HEAD_ROWS = sum((hi - lo) * nj for lo, hi, nj in _CAND_GROUPS)
_CAND_ROWS = _CAND_HEAD_ROWS + _CAND_TAIL_J * PEER_TOPK


def _candidate_tables(tm):
    k = PEER_TOPK
    assert k == 16 and _CAND_GROUPS == ((0, 2, 16), (2, 4, 8))
    r = lax.broadcasted_iota(I32, (_CAND_ROWS, tm), 0)
    r2 = r - 2 * k
    flat2 = (2 + (r2 >> 3)) * k + (r2 & 7)
    r3 = r - _CAND_HEAD_ROWS
    i3 = r3 & (k - 1)
    flat3 = i3 * k + (r3 >> 4)
    flat = jnp.where(r < 2 * k, r, jnp.where(r < _CAND_HEAD_ROWS, flat2, flat3))
    valid = (r < _CAND_HEAD_ROWS) | (i3 >= _CAND_TAIL_I0)
    return flat.astype(F32), valid


def _peer_route_kernel(q_ref, keys_ref, g_ref, e1_ref, e2_ref, gv_ref, gt_ref, *, tm):
    k = PEER_TOPK
    flat, valid = _candidate_tables(tm)
    key_order = lax.broadcasted_iota(I32, (N_KEYS, tm), 0).astype(F32)

    e1s, e2s, gs = [], [], []
    for h in range(PEER_HEADS):
        qh = q_ref[:, h * D_KEY:(h + 1) * D_KEY].astype(BF16)
        score = lambda p: lax.dot_general(keys_ref[h, p], qh, (((1,), (1,)), ((), ())), preferred_element_type=F32)
        v1, i1 = _top_rows(score(0), k, key_order)
        v2, i2 = _top_rows(score(1), k, key_order)
        parts = [v1[i:i + 1, :] + v2[:nj, :] for lo, hi, nj in _CAND_GROUPS for i in range(lo, hi)]
        parts += [v1 + v2[j:j + 1, :] for j in range(_CAND_TAIL_J)]
        cand = jnp.where(valid, jnp.concatenate(parts, axis=0), -jnp.inf)
        sc, pos = _top_rows(cand, k, flat)
        pos_i = jnp.floor(pos * (1.0 / k))
        e1s.append(_take_rows(i1, pos_i))
        e2s.append(_take_rows(i2, pos - k * pos_i))
        ex = jnp.exp(sc - sc[0:1, :])
        gs.append(ex / jnp.sum(ex, axis=0, keepdims=True))
    e1_ref[...] = jnp.concatenate(e1s, axis=0).T
    e2_ref[...] = jnp.concatenate(e2s, axis=0).T
    gv_ref[...] = jnp.concatenate(gs, axis=0).T

    eid = lax.broadcasted_iota(I32, (N_KEYS, PEER_HEADS * k), 0).astype(F32)
    rows = _ROUTE_GROUP

    def body(gi, carry):
        for half in range(_ROUTE_UNROLL):
            base = pl.multiple_of((gi * _ROUTE_UNROLL + half) * rows, rows)
            stage = half * rows * _ROUTE_PITCH
            e1_t = e1_ref[pl.ds(base, rows), :]
            e2_t = e2_ref[pl.ds(base, rows), :]
            gv_t = gv_ref[pl.ds(base, rows), :]
            for u in range(rows):
                a = jnp.where(eid == e1_t[u:u + 1, :], gv_t[u:u + 1, :], 0.0).astype(BF16)
                b = jnp.where(eid == e2_t[u:u + 1, :], 1.0, 0.0).astype(BF16)
                gt_ref[stage + u * _ROUTE_PITCH:stage + u * _ROUTE_PITCH + N_KEYS, :] = lax.dot_general(
                    a, b, (((1,), (1,)), ((), ())), preferred_element_type=F32)
        for half in range(_ROUTE_UNROLL):
            base = pl.multiple_of((gi * _ROUTE_UNROLL + half) * rows, rows)
            stage = half * rows * _ROUTE_PITCH
            for e1 in range(N_KEYS):
                slab = gt_ref[pl.ds(stage + e1, rows, stride=_ROUTE_PITCH), :]
                g_ref[pl.ds(base, rows), e1 * N_KEYS:(e1 + 1) * N_KEYS] = slab.astype(g_ref.dtype)
        return carry

    lax.fori_loop(0, tm // (rows * _ROUTE_UNROLL), body, 0)


_ROUTE_GROUP = 2 * SUBLANES
_ROUTE_UNROLL = 2
_ROUTE_PITCH = N_KEYS + SUBLANES


def _peer_route(q, keys_pad, *, tm):
    n = q.shape[0]
    width = PEER_HEADS * PEER_TOPK
    assert tm % (_ROUTE_GROUP * _ROUTE_UNROLL) == 0
    return pl.pallas_call(
        functools.partial(_peer_route_kernel, tm=tm),
        grid=(n // tm,),
        in_specs=[pl.BlockSpec((tm, PEER_HEADS * D_KEY), lambda i: (i, 0)),
                  pl.BlockSpec(keys_pad.shape, lambda i: (0, 0, 0, 0))],
        out_specs=pl.BlockSpec((tm, N_KEYS * N_KEYS), lambda i: (i, 0)),
        out_shape=jax.ShapeDtypeStruct((n, N_KEYS * N_KEYS), BF16),
        scratch_shapes=[pltpu.VMEM((tm, width), F32)] * 3
                       + [pltpu.VMEM((_ROUTE_UNROLL * _ROUTE_GROUP * _ROUTE_PITCH, N_KEYS), F32)],
        compiler_params=_cparams(("parallel",)),
        name="peer_route",
    )(q, keys_pad)


def _peer_mlp_kernel(x_ref, gain_ref, u_ref, v_ref, g_ref, o_ref, xn_ref, acc_ref):
    j = pl.program_id(1)

    @pl.when(j == 0)
    def _():
        x = x_ref[...]
        ms = jnp.mean(x * x, axis=-1, keepdims=True)
        xn_ref[...] = (x * lax.rsqrt(ms + RMS_EPS) * gain_ref[...]).astype(BF16)
        acc_ref[...] = jnp.zeros_like(acc_ref)

    h = lax.dot_general(xn_ref[...], u_ref[...], (((1,), (1,)), ((), ())), preferred_element_type=F32)
    act = 0.5 * h * (1.0 + lax.erf(h * (2.0 ** -0.5)))
    w = (act * g_ref[...].astype(F32)).astype(BF16)
    acc_ref[...] += jnp.dot(w, v_ref[...], preferred_element_type=F32)

    @pl.when(j == pl.num_programs(1) - 1)
    def _():
        o_ref[...] = x_ref[...] + acc_ref[...]


def _peer_mlp(x, gain, u_bf16, v_bf16, gmap, *, layer, tm, te):
    n, dm = x.shape
    ne = u_bf16.shape[1]
    return pl.pallas_call(
        _peer_mlp_kernel,
        grid=(n // tm, ne // te),
        in_specs=[pl.BlockSpec((tm, dm), lambda i, j: (i, 0)),
                  pl.BlockSpec((1, dm), lambda i, j: (0, 0)),
                  pl.BlockSpec((None, te, dm), lambda i, j: (layer, j, 0)),
                  pl.BlockSpec((None, te, dm), lambda i, j: (layer, j, 0)),
                  pl.BlockSpec((tm, te), lambda i, j: (i, j))],
        out_specs=pl.BlockSpec((tm, dm), lambda i, j: (i, 0)),
        out_shape=jax.ShapeDtypeStruct((n, dm), F32),
        scratch_shapes=[pltpu.VMEM((tm, dm), BF16), pltpu.VMEM((tm, dm), F32)],
        compiler_params=_cparams(("parallel", "arbitrary")),
        name="peer_mlp",
    )(x, gain.reshape(1, dm), u_bf16, v_bf16, gmap)


def _peer_keys_padded(keys):
    z = jnp.zeros_like(keys[:, 0])
    first = jnp.concatenate([keys[:, 0], z], axis=-1)
    second = jnp.concatenate([z, keys[:, 1]], axis=-1)
    return jnp.stack([first, second], axis=1).astype(BF16)


def _peer(y, gain, wq_bf16, keys_pad, u_bf16, v_bf16, *, layer, tm_proj, tm_route, tm_mlp, te):
    q = _norm_matmul(y, gain, wq_bf16, tm=tm_proj, tn=wq_bf16.shape[1])
    gmap = _peer_route(q, keys_pad, tm=tm_route)
    return _peer_mlp(y, gain, u_bf16, v_bf16, gmap, layer=layer, tm=tm_mlp, te=te)


def _conv_silu(x, buf_ref, w_ref, b_ref, q):
    buf_ref[SUBLANES:SUBLANES + q, :] = x
    acc = b_ref[...] + x * w_ref[CONV_W - 1:CONV_W, :]
    for back in range(1, CONV_W):
        acc = acc + buf_ref[SUBLANES - back:SUBLANES - back + q, :] * w_ref[CONV_W - 1 - back:CONV_W - back, :]
    tail = buf_ref[q + SUBLANES - (CONV_W - 1):q + SUBLANES, :]
    buf_ref[SUBLANES - (CONV_W - 1):SUBLANES, :] = tail
    return _silu(acc), tail


def _ssd_kernel(z_ref, x_ref, bc_ref, dt_ref, cwx_ref, cwbc_ref, cbx_ref, cbbc_ref, dtb_ref, alog_ref, dskip_ref,
                nw_ref, expand_ref, y_ref, convx_ref, convbc_ref, ht_ref, bufx_ref, bufbc_ref, h_ref, *, q, heads):
    hpg = heads // N_GROUPS
    gw = hpg * SSM_HEAD_DIM

    @pl.when(pl.program_id(0) == 0)
    def _():
        bufx_ref[...] = jnp.zeros_like(bufx_ref)
        bufbc_ref[...] = jnp.zeros_like(bufbc_ref)
        h_ref[...] = jnp.zeros_like(h_ref)

    xs, tail_x = _conv_silu(x_ref[...], bufx_ref, cwx_ref, cbx_ref, q)
    bc, tail_bc = _conv_silu(bc_ref[...], bufbc_ref, cwbc_ref, cbbc_ref, q)
    convx_ref[...] = tail_x
    convbc_ref[...] = tail_bc

    dt = _softplus(dt_ref[...] + dtb_ref[...])
    a = -jnp.exp(alog_ref[...])
    row = lax.broadcasted_iota(I32, (q, q), 0)
    col = lax.broadcasted_iota(I32, (q, q), 1)
    lower = col <= row
    tri = jnp.where(lower, 1.0, 0.0).astype(BF16)
    acs = _dot_f32_left(tri, dt * a)
    acs_t = acs.T
    expand = expand_ref[...]
    dt_e = _dot_f32_right(dt, expand)
    acs_e = _dot_f32_right(acs, expand)
    last_e = acs_e[q - 1:q, :]
    xc = xs * dt_e
    xcd = (xc * jnp.exp(last_e - acs_e)).astype(BF16)
    xc_b = xc.astype(BF16)
    grow = jnp.exp(acs_e)
    chunk_decay = jnp.exp(last_e)

    nbc = N_GROUPS * D_STATE
    for g in range(N_GROUPS):
        bm = bc[:, g * D_STATE:(g + 1) * D_STATE].astype(BF16)
        cm = bc[:, nbc + g * D_STATE:nbc + (g + 1) * D_STATE].astype(BF16)
        cb = lax.dot_general(cm, bm, (((1,), (1,)), ((), ())), preferred_element_type=F32)
        sl = slice(g * gw, (g + 1) * gw)
        h_prev = h_ref[g]
        y_off = jnp.dot(cm, h_prev.astype(BF16), preferred_element_type=F32) * grow[:, sl]
        y_diag = []
        for r in range(hpg):
            hd = g * hpg + r
            diff = acs[:, hd:hd + 1] - acs_t[hd:hd + 1, :]
            m = (cb * jnp.exp(jnp.where(lower, diff, -jnp.inf))).astype(BF16)
            y_diag.append(jnp.dot(m, xc_b[:, hd * SSM_HEAD_DIM:(hd + 1) * SSM_HEAD_DIM], preferred_element_type=F32))
        y_g = jnp.concatenate(y_diag, axis=1) + y_off + dskip_ref[:, sl] * xs[:, sl]
        states = jnp.dot(bm.T, xcd[:, sl], preferred_element_type=F32)
        h_ref[g] = h_prev * chunk_decay[:, sl] + states
        zg = z_ref[:, sl]
        y_g = y_g * _silu(zg)
        ms = jnp.mean(y_g * y_g, axis=-1, keepdims=True)
        y_ref[:, sl] = (y_g * lax.rsqrt(ms + RMS_EPS) * nw_ref[:, sl]).astype(y_ref.dtype)

    @pl.when(pl.program_id(0) == pl.num_programs(0) - 1)
    def _():
        ht_ref[...] = h_ref[...]


def _ssd(proj, conv_w, conv_b, dt_bias, a_log, d_skip, norm_w, *, heads):
    n = proj.shape[0]
    d_inner = heads * SSM_HEAD_DIM
    q = CHUNK
    nbc = 2 * N_GROUPS * D_STATE
    assert nbc == d_inner and n % q == 0
    gw = d_inner // N_GROUPS
    pad = lambda x: jnp.pad(x, (0, LANES - heads)).reshape(1, LANES)
    expand = jnp.asarray((np.arange(LANES)[:, None] == (np.arange(d_inner) // SSM_HEAD_DIM)[None, :]).astype(np.float32), BF16)
    blk = lambda w, j: pl.BlockSpec((q, w), lambda i, j=j: (i, j))
    full = lambda a: pl.BlockSpec(a.shape, lambda i: (0,) * a.ndim)
    small = [conv_w[:, :d_inner], conv_w[:, d_inner:], conv_b[:d_inner].reshape(1, -1), conv_b[d_inner:].reshape(1, -1),
             pad(dt_bias), pad(a_log), jnp.repeat(d_skip, SSM_HEAD_DIM).reshape(1, -1), norm_w.reshape(1, -1), expand]
    y, cx, cbc, ht = pl.pallas_call(
        functools.partial(_ssd_kernel, q=q, heads=heads),
        grid=(n // q,),
        in_specs=[blk(d_inner, 0), blk(d_inner, 1), blk(d_inner, 2), blk(LANES, 3 * d_inner // LANES)]
                 + [full(a) for a in small],
        out_specs=[pl.BlockSpec((q, d_inner), lambda i: (i, 0)),
                   pl.BlockSpec((CONV_W - 1, d_inner), lambda i: (0, 0)),
                   pl.BlockSpec((CONV_W - 1, nbc), lambda i: (0, 0)),
                   pl.BlockSpec((N_GROUPS, D_STATE, gw), lambda i: (0, 0, 0))],
        out_shape=[jax.ShapeDtypeStruct((n, d_inner), BF16),
                   jax.ShapeDtypeStruct((CONV_W - 1, d_inner), F32),
                   jax.ShapeDtypeStruct((CONV_W - 1, nbc), F32),
                   jax.ShapeDtypeStruct((N_GROUPS, D_STATE, gw), F32)],
        scratch_shapes=[pltpu.VMEM((SUBLANES + q, d_inner), F32), pltpu.VMEM((SUBLANES + q, nbc), F32),
                        pltpu.VMEM((N_GROUPS, D_STATE, gw), F32)],
        compiler_params=_cparams(("arbitrary",)),
        name="ssd",
    )(proj, proj, proj, proj, *small)
    hpg = heads // N_GROUPS
    h_t = ht.reshape(N_GROUPS, D_STATE, hpg, SSM_HEAD_DIM).transpose(0, 2, 3, 1).reshape(heads, SSM_HEAD_DIM, D_STATE)
    return y, jnp.concatenate([cx, cbc], axis=1), h_t


def _ssm_step_kernel(z_ref, x_ref, bc_ref, dt_ref, bufx_ref, bufbc_ref, h0_ref, cwx_ref, cwbc_ref, cbx_ref, cbbc_ref,
                     dtb_ref, alog_ref, dskip_ref, nw_ref, y_ref, nbufx_ref, nbufbc_ref, h_ref, *, heads):
    hpg = heads // N_GROUPS
    gw = hpg * SSM_HEAD_DIM
    nbc = N_GROUPS * D_STATE
    fullx = jnp.concatenate([bufx_ref[...], x_ref[...]], axis=1)
    nbufx_ref[...] = fullx[:, 1:]
    xs = _silu(cbx_ref[...] + jnp.sum(fullx * cwx_ref[...], axis=1, keepdims=True))
    fullbc = jnp.concatenate([bufbc_ref[...], bc_ref[...]], axis=0)
    nbufbc_ref[...] = fullbc[1:, :]
    bc = _silu(cbbc_ref[...] + jnp.sum(fullbc * cwbc_ref[...], axis=0, keepdims=True))
    dt = _softplus(dt_ref[...] + dtb_ref[...])
    dec = jnp.exp(dt * (-jnp.exp(alog_ref[...])))
    xdt = xs * dt
    for g in range(N_GROUPS):
        rows = slice(g * gw, (g + 1) * gw)
        bm = bc[:, g * D_STATE:(g + 1) * D_STATE]
        cm = bc[:, nbc + g * D_STATE:nbc + (g + 1) * D_STATE]
        h_new = h0_ref[rows, :] * dec[rows, :] + xdt[rows, :] * bm
        h_ref[rows, :] = h_new
        y = jnp.sum(h_new * cm, axis=-1, keepdims=True) + dskip_ref[rows, :] * xs[rows, :]
        y = y * _silu(z_ref[rows, :])
        ms = jnp.mean(y * y, axis=0, keepdims=True)
        y_ref[rows, :] = y * lax.rsqrt(ms + RMS_EPS) * nw_ref[rows, :]


def _ssm_step(proj, conv_buf, h0, conv_w, conv_b, dt_bias, a_log, d_skip, norm_w, *, heads):
    b = proj.shape[0]
    d_inner = heads * SSM_HEAD_DIM
    nbc2 = 2 * N_GROUPS * D_STATE
    col = lambda x: x.reshape(b, d_inner, 1)
    per_head = lambda v: jnp.repeat(v, SSM_HEAD_DIM, axis=-1)
    z = col(proj[:, :d_inner])
    x = col(proj[:, d_inner:2 * d_inner])
    bc = proj[:, 2 * d_inner:2 * d_inner + nbc2].reshape(b, 1, nbc2)
    dt = col(per_head(proj[:, 2 * d_inner + nbc2:2 * d_inner + nbc2 + heads]))
    bufx = conv_buf[:, :, :d_inner].transpose(0, 2, 1)
    bufbc = conv_buf[:, :, d_inner:]
    pcol = lambda v: v.reshape(d_inner, 1)
    params = [conv_w[:, :d_inner].T, conv_w[:, d_inner:], pcol(conv_b[:d_inner]), conv_b[d_inner:].reshape(1, nbc2),
              pcol(per_head(dt_bias)), pcol(per_head(a_log)), pcol(per_head(d_skip)), pcol(norm_w)]
    seq = lambda *shape: pl.BlockSpec((None,) + shape, lambda i: (i,) + (0,) * len(shape))
    full = lambda a: pl.BlockSpec(a.shape, lambda i: (0,) * a.ndim)
    y, nbx, nbbc, h = pl.pallas_call(
        functools.partial(_ssm_step_kernel, heads=heads),
        grid=(b,),
        in_specs=[seq(d_inner, 1), seq(d_inner, 1), seq(1, nbc2), seq(d_inner, 1), seq(d_inner, CONV_W - 1),
                  seq(CONV_W - 1, nbc2), seq(d_inner, D_STATE)] + [full(a) for a in params],
        out_specs=[seq(d_inner, 1), seq(d_inner, CONV_W - 1), seq(CONV_W - 1, nbc2), seq(d_inner, D_STATE)],
        out_shape=[jax.ShapeDtypeStruct((b, d_inner, 1), F32), jax.ShapeDtypeStruct((b, d_inner, CONV_W - 1), F32),
                   jax.ShapeDtypeStruct((b, CONV_W - 1, nbc2), F32), jax.ShapeDtypeStruct((b, d_inner, D_STATE), F32)],
        compiler_params=_cparams(("parallel",)),
        name="ssm_step",
    )(z, x, bc, dt, bufx, bufbc, h0.reshape(b, d_inner, D_STATE), *params)
    new_buf = jnp.concatenate([nbx.transpose(0, 2, 1), nbbc], axis=2)
    return y.reshape(b, d_inner), new_buf, h.reshape(b, heads, SSM_HEAD_DIM, D_STATE)


def _fox_decode_kernel(pt_ref, q_ref, kc_ref, vc_ref, lfc_ref, *refs, pages):
    k_refs = refs[:pages]
    v_refs = refs[pages:2 * pages]
    lf_refs = refs[2 * pages:3 * pages]
    o_ref, qb_ref, m_ref, l_ref, acc_ref, later_ref = refs[3 * pages:]
    j = pl.program_id(1)
    first_lane = lax.broadcasted_iota(I32, (1, LANES), 1) == 0

    @pl.when(j == 0)
    def _():
        s_rows = []
        for h in range(FOX_HEADS):
            qh = q_ref[h] * (FOX_HEAD_DIM ** -0.5)
            qb_ref[h] = jnp.broadcast_to(qh, (FOX_HEAD_DIM, LANES))
            s_rows.append(jnp.sum(qh * kc_ref[h], axis=0, keepdims=True))
            acc_ref[h] = jnp.where(first_lane, vc_ref[h], 0.0)
        m_ref[...] = jnp.broadcast_to(jnp.concatenate(s_rows, axis=0), m_ref.shape)
        l_ref[...] = jnp.broadcast_to(jnp.where(first_lane, 1.0, 0.0), l_ref.shape)
        later_ref[...] = jnp.broadcast_to(lfc_ref[...], later_ref.shape)

    urow = lax.broadcasted_iota(I32, (LANES, LANES), 0)
    tcol = lax.broadcasted_iota(I32, (LANES, LANES), 1)
    after = jnp.where(urow > tcol, 1.0, 0.0).astype(BF16)
    for u in range(pages):
        lf = lf_refs[u][...]
        later = later_ref[...]
        bias = _dot_f32_right(lf, after) + later
        rows = [jnp.sum(k_refs[u][h] * qb_ref[h], axis=0, keepdims=True) for h in range(FOX_HEADS)]
        s = jnp.concatenate(rows, axis=0) + bias
        m_old = m_ref[...]
        m_new = jnp.maximum(m_old, jnp.max(s, axis=-1, keepdims=True))
        alpha = jnp.exp(m_old - m_new)
        p = jnp.exp(s - m_new)
        l_ref[...] = alpha * l_ref[...] + p
        m_ref[...] = m_new
        for h in range(FOX_HEADS):
            acc_ref[h] = alpha[h:h + 1, :] * acc_ref[h] + p[h:h + 1, :] * v_refs[u][h]
        later_ref[...] = later + jnp.sum(lf, axis=-1, keepdims=True)

    @pl.when(j == pl.num_programs(1) - 1)
    def _():
        l = jnp.sum(l_ref[...], axis=-1, keepdims=True)
        for h in range(FOX_HEADS):
            o_ref[h] = jnp.sum(acc_ref[h], axis=-1, keepdims=True) / l[h:h + 1, :]


def _fox_decode(q, k_cur, v_cur, logf_cur, cache_k, cache_v, cache_logf, page_table, *, pages):
    b, n_pages = page_table.shape
    page = cache_k.shape[1]
    assert page == LANES and n_pages % pages == 0
    kt = cache_k.transpose(0, 2, 3, 1)
    vt = cache_v.transpose(0, 2, 3, 1)
    lft = cache_logf.transpose(0, 2, 1)
    col = lambda x: x.reshape(b, FOX_HEADS, FOX_HEAD_DIM, 1)
    tok = pl.BlockSpec((None, FOX_HEADS, FOX_HEAD_DIM, 1), lambda bi, j, pt: (bi, 0, 0, 0))
    newest_first = lambda bi, j, pt, u: pt[bi, n_pages - 1 - (j * pages + u)]
    kv_page = lambda u: pl.BlockSpec((None, FOX_HEADS, FOX_HEAD_DIM, page),
                                     lambda bi, j, pt, u=u: (newest_first(bi, j, pt, u), 0, 0, 0))
    lf_page = lambda u: pl.BlockSpec((None, FOX_HEADS, page), lambda bi, j, pt, u=u: (newest_first(bi, j, pt, u), 0, 0))
    out = pl.pallas_call(
        functools.partial(_fox_decode_kernel, pages=pages),
        grid_spec=pltpu.PrefetchScalarGridSpec(
            num_scalar_prefetch=1,
            grid=(b, n_pages // pages),
            in_specs=[tok, tok, tok, pl.BlockSpec((None, FOX_HEADS, 1), lambda bi, j, pt: (bi, 0, 0))]
                     + [kv_page(u) for u in range(pages)] * 2 + [lf_page(u) for u in range(pages)],
            out_specs=tok,
            scratch_shapes=[pltpu.VMEM((FOX_HEADS, FOX_HEAD_DIM, LANES), F32), pltpu.VMEM((FOX_HEADS, LANES), F32),
                            pltpu.VMEM((FOX_HEADS, LANES), F32), pltpu.VMEM((FOX_HEADS, FOX_HEAD_DIM, LANES), F32),
                            pltpu.VMEM((FOX_HEADS, LANES), F32)],
        ),
        out_shape=jax.ShapeDtypeStruct((b, FOX_HEADS, FOX_HEAD_DIM, 1), F32),
        compiler_params=_cparams(("parallel", "arbitrary")),
        name="fox_decode",
    )(page_table, col(q), col(k_cur), col(v_cur), logf_cur.reshape(b, FOX_HEADS, 1),
      *([kt] * pages), *([vt] * pages), *([lft] * pages))
    return out.reshape(b, FOX_WIDTH)


_COL_Q = RWKV_PROJ
_COL_K = _COL_Q + FOX_WIDTH
_COL_V = _COL_K + FOX_WIDTH
_COL_FL = _COL_V + FOX_WIDTH
_EVEN_COLS = _COL_FL + LANES


def _even_weights(w_in):
    q, k, v = (w_in[:, i * FOX_WIDTH:(i + 1) * FOX_WIDTH] for i in range(3))
    fl = w_in[:, 3 * FOX_WIDTH:3 * FOX_WIDTH + FOX_HEADS]
    rw = w_in[:, 3 * FOX_WIDTH + FOX_HEADS:][:, _RWKV_PERM]
    fl_pad = jnp.pad(fl, ((0, 0), (0, LANES - FOX_HEADS)))
    return jnp.concatenate([rw, q, k, v, fl_pad], axis=1).astype(BF16)


def _even_layer(y, gain, w_cat, w_out, fb, rw, *, batch, seq, past, shift0, wkv0):
    n = y.shape[0]
    tm = min(512, n)
    fb_pad = jnp.pad(fb, (0, LANES - FOX_HEADS)).reshape(1, LANES)
    if past is None:
        proj, qkv = _norm_matmul(y, gain, w_cat, tm=512, tn=_EVEN_COLS, bf16_cols=(_COL_Q, _COL_FL))
        logf, ct = _fox_prep(proj, _COL_FL // LANES, fb_pad, t=256)
        k_t, v_t = _norm_matmul_t(y, gain, w_cat[:, _COL_K:_COL_FL].T, tm=1024, parts=2)
        tok_major = lambda x: x.reshape(FOX_HEADS, FOX_HEAD_DIM, n).transpose(2, 0, 1).reshape(
            batch, seq, FOX_HEADS, FOX_HEAD_DIM)
        k_out, v_out = tok_major(k_t), tok_major(v_t)
        o_fox = _fox_attn(qkv, ct, tq=1024)
        o_rwkv, shift_new, wkv_new = _rwkv_mix(proj, shift0[:, _RWKV_PERM], wkv0, rw, batch=batch,
                                               t_prep=256, t_scan=256, shifted=True)
    else:
        proj = _norm_matmul(y, gain, w_cat, tm=n, tn=_EVEN_COLS)
        logf, _ = _fox_prep(proj, _COL_FL // LANES, fb_pad, t=n)
        heads = lambda t: t.reshape(batch, seq, FOX_HEADS, FOX_HEAD_DIM)
        k_new = proj[:, _COL_K:_COL_V]
        v_new = proj[:, _COL_V:_COL_FL]
        k_out, v_out = heads(k_new), heads(v_new)
        cache_k, cache_v, cache_logf, page_table = past
        o_fox = _fox_decode(proj[:, _COL_Q:_COL_K], k_new, v_new, logf, cache_k, cache_v, cache_logf, page_table,
                            pages=16)
        o_rwkv, shift_new, wkv_new = _rwkv_mix(proj, shift0[:, _RWKV_PERM], wkv0, rw, batch=batch,
                                               t_prep=n, t_scan=1, shifted=False)
    w_out_b = w_out.astype(BF16)
    y = _matmul_res([o_fox, o_rwkv], [w_out_b[:FOX_WIDTH], w_out_b[FOX_WIDTH:]], y, tm=tm, tn=y.shape[1])
    return y, k_out, v_out, logf.reshape(batch, seq, FOX_HEADS), wkv_new, shift_new


def kernel(x_prompt, x_sample, cache_k, cache_v, cache_logf, page_table, state_wkv, state_shift, state_conv, state_ssm, norm_mix, norm_ffn, norm_final, w_in_even, w_out_even, fox_fb, rwkv_mu, rwkv_w0, rwkv_w2, rwkv_a0, rwkv_a2, rwkv_g2, rwkv_kk, rwkv_ka, rwkv_rk, rwkv_lnw, rwkv_lnb, w_in_odd, w_out_odd, ssm_conv_w, ssm_conv_b, ssm_dt_bias, ssm_a_log, ssm_d, ssm_norm_w, peer_wq, peer_keys, peer_u, peer_v):
    bp, seq_p, dm = x_prompt.shape
    bs, seq_s, _ = x_sample.shape
    assert bp == 1 and seq_s == 1
    depth = norm_mix.shape[0]
    ssm_heads = ssm_dt_bias.shape[1]
    yp = x_prompt.reshape(bp * seq_p, dm)
    ys = x_sample.reshape(bs * seq_s, dm)
    u_all = peer_u.astype(BF16)
    v_all = peer_v.astype(BF16)
    outs_p = {name: [] for name in ("k", "v", "logf", "wkv", "shift", "conv", "ssm")}
    outs_s = {name: [] for name in outs_p}
    for layer in range(depth):
        if layer % 2 == 0:
            e = layer // 2
            rw = (rwkv_mu[e], rwkv_w0[e], rwkv_w2[e], rwkv_a0[e], rwkv_a2[e], rwkv_g2[e],
                  rwkv_kk[e], rwkv_ka[e], rwkv_rk[e], rwkv_lnw[e], rwkv_lnb[e])
            w_cat = _even_weights(w_in_even[e])
            yp, *res_p = _even_layer(
                yp, norm_mix[layer], w_cat, w_out_even[e], fox_fb[e], rw, batch=bp, seq=seq_p, past=None,
                shift0=jnp.zeros((bp, RWKV_PROJ), F32),
                wkv0=jnp.zeros((bp, RWKV_HEADS, RWKV_HEAD_DIM, RWKV_HEAD_DIM), F32))
            ys, *res_s = _even_layer(
                ys, norm_mix[layer], w_cat, w_out_even[e], fox_fb[e], rw, batch=bs, seq=seq_s,
                past=(cache_k[e], cache_v[e], cache_logf[e], page_table), shift0=state_shift[e], wkv0=state_wkv[e])
            for outs, res in ((outs_p, res_p), (outs_s, res_s)):
                for name, val in zip(("k", "v", "logf", "wkv", "shift"), res):
                    outs[name].append(val)
        else:
            o = layer // 2
            mprm = (ssm_conv_w[o], ssm_conv_b[o], ssm_dt_bias[o], ssm_a_log[o], ssm_d[o], ssm_norm_w[o])
            w_cat = jnp.pad(w_in_odd[o], ((0, 0), (0, LANES - ssm_heads))).astype(BF16)
            w_out = w_out_odd[o].astype(BF16)
            tn = w_cat.shape[1]
            proj_p = _norm_matmul(yp, norm_mix[layer], w_cat, tm=256, tn=tn)
            y_m, conv_p, h_p = _ssd(proj_p, *mprm, heads=ssm_heads)
            yp = _matmul_res([y_m], [w_out], yp, tm=512, tn=dm)
            proj_s = _norm_matmul(ys, norm_mix[layer], w_cat, tm=bs, tn=tn)
            y_s, conv_s, h_s = _ssm_step(proj_s, state_conv[o], state_ssm[o], *mprm, heads=ssm_heads)
            ys = _matmul_res([y_s], [w_out], ys, tm=bs, tn=dm)
            outs_p["conv"].append(conv_p[None])
            outs_p["ssm"].append(h_p[None])
            outs_s["conv"].append(conv_s)
            outs_s["ssm"].append(h_s)
        wq = peer_wq[layer].astype(BF16)
        keys_pad = _peer_keys_padded(peer_keys[layer])
        yp = _peer(yp, norm_ffn[layer], wq, keys_pad, u_all, v_all, layer=layer,
                   tm_proj=512, tm_route=256, tm_mlp=512, te=2048)
        ys = _peer(ys, norm_ffn[layer], wq, keys_pad, u_all, v_all, layer=layer,
                   tm_proj=bs, tm_route=bs, tm_mlp=bs, te=2048)
    y_prompt = _rmsnorm(yp, norm_final, tm=512).reshape(bp, seq_p, dm)
    y_sample = _rmsnorm(ys, norm_final, tm=bs).reshape(bs, seq_s, dm)
    order = ("k", "v", "logf", "wkv", "shift", "conv", "ssm")
    return ((y_prompt, y_sample) + tuple(jnp.stack(outs_p[name]) for name in order)
            + tuple(jnp.stack(outs_s[name]) for name in order))
```

```python
import functools

import numpy as np
import jax
import jax.numpy as jnp
from jax import lax
from jax.experimental import pallas as pl
from jax.experimental.pallas import tpu as pltpu

F32 = jnp.float32
BF16 = jnp.bfloat16
I32 = jnp.int32

LANES = 128
SUBLANES = 8

RMS_EPS = 1e-6
GN_EPS = 64e-5

FOX_HEADS = 8
FOX_HEAD_DIM = 64
FOX_WIDTH = FOX_HEADS * FOX_HEAD_DIM
RWKV_HEADS = 8
RWKV_HEAD_DIM = 64
RWKV_WIDTH = RWKV_HEADS * RWKV_HEAD_DIM
DECAY_LORA = 64
ICLR_LORA = 64
GATE_LORA = 128
RWKV_PROJ = 4 * RWKV_WIDTH - RWKV_WIDTH + DECAY_LORA + ICLR_LORA + GATE_LORA
HEAD_PAIRS = RWKV_HEADS // 2

SSM_HEAD_DIM = 64
D_STATE = 128
N_GROUPS = 8
CONV_W = 4
CHUNK = 128

N_KEYS = 128
PEER_HEADS = 8
PEER_TOPK = 16
D_KEY = 128


def _cparams(semantics, vmem_mb=48):
    return pltpu.CompilerParams(dimension_semantics=semantics, vmem_limit_bytes=vmem_mb * 1024 * 1024)


def _split3(x):
    hi = x.astype(BF16)
    r1 = x - hi.astype(F32)
    mid = r1.astype(BF16)
    lo = (r1 - mid.astype(F32)).astype(BF16)
    return hi, mid, lo


def _dot_f32_right(x, m_bf16):
    hi, mid, lo = _split3(x)
    d = lambda a: jnp.dot(a, m_bf16, preferred_element_type=F32)
    return d(hi) + d(mid) + d(lo)


def _dot_f32_left(m_bf16, x):
    hi, mid, lo = _split3(x)
    d = lambda a: jnp.dot(m_bf16, a, preferred_element_type=F32)
    return d(hi) + d(mid) + d(lo)


def _sigmoid(x):
    return 1.0 / (1.0 + jnp.exp(-x))


def _softplus(x):
    return jnp.maximum(x, 0.0) + jnp.log1p(jnp.exp(-jnp.abs(x)))


def _silu(x):
    return x * _sigmoid(x)


def _norm_matmul_kernel(x_ref, g_ref, w_ref, o_ref, *rest, bf16_cols):
    xn_ref = rest[-1]

    @pl.when(pl.program_id(1) == 0)
    def _():
        x = x_ref[...]
        ms = jnp.mean(x * x, axis=-1, keepdims=True)
        xn_ref[...] = (x * lax.rsqrt(ms + RMS_EPS) * g_ref[...]).astype(BF16)

    res = jnp.dot(xn_ref[...], w_ref[...], preferred_element_type=F32)
    o_ref[...] = res
    if bf16_cols is not None:
        rest[0][...] = res[:, bf16_cols[0]:bf16_cols[1]].astype(BF16)


def _norm_matmul(x, g, w_bf16, *, tm, tn, bf16_cols=None):
    m, k = x.shape
    n = w_bf16.shape[1]
    assert m % tm == 0 and n % tn == 0
    out_specs = [pl.BlockSpec((tm, tn), lambda i, j: (i, j))]
    out_shape = [jax.ShapeDtypeStruct((m, n), F32)]
    if bf16_cols is not None:
        assert tn == n and bf16_cols[0] % LANES == 0 and bf16_cols[1] % LANES == 0
        width = bf16_cols[1] - bf16_cols[0]
        out_specs.append(pl.BlockSpec((tm, width), lambda i, j: (i, 0)))
        out_shape.append(jax.ShapeDtypeStruct((m, width), BF16))
    outs = pl.pallas_call(
        functools.partial(_norm_matmul_kernel, bf16_cols=bf16_cols),
        grid=(m // tm, n // tn),
        in_specs=[
            pl.BlockSpec((tm, k), lambda i, j: (i, 0)),
            pl.BlockSpec((1, k), lambda i, j: (0, 0)),
            pl.BlockSpec((k, tn), lambda i, j: (0, j)),
        ],
        out_specs=out_specs,
        out_shape=out_shape,
        scratch_shapes=[pltpu.VMEM((tm, k), BF16)],
        compiler_params=_cparams(("parallel", "arbitrary")),
        name="norm_matmul",
    )(x, g.reshape(1, k), w_bf16)
    return outs if bf16_cols is not None else outs[0]


def _norm_matmul_t_kernel(x_ref, g_ref, wt_ref, *o_refs):
    x = x_ref[...]
    ms = jnp.mean(x * x, axis=-1, keepdims=True)
    xn = (x * lax.rsqrt(ms + RMS_EPS) * g_ref[...]).astype(BF16)
    res = lax.dot_general(wt_ref[...], xn, (((1,), (1,)), ((), ())), preferred_element_type=F32)
    rows = res.shape[0] // len(o_refs)
    for i, o_ref in enumerate(o_refs):
        o_ref[...] = res[i * rows:(i + 1) * rows]


def _norm_matmul_t(x, g, wt_bf16, *, tm, parts):
    m, k = x.shape
    c = wt_bf16.shape[0]
    assert c % parts == 0
    return pl.pallas_call(
        _norm_matmul_t_kernel,
        grid=(m // tm,),
        in_specs=[pl.BlockSpec((tm, k), lambda i: (i, 0)),
                  pl.BlockSpec((1, k), lambda i: (0, 0)),
                  pl.BlockSpec((c, k), lambda i: (0, 0))],
        out_specs=[pl.BlockSpec((c // parts, tm), lambda i: (0, i))] * parts,
        out_shape=[jax.ShapeDtypeStruct((c // parts, m), F32)] * parts,
        compiler_params=_cparams(("parallel",)),
        name="norm_matmul_t",
    )(x, g.reshape(1, k), wt_bf16)


def _matmul_res_kernel(*refs, n_in):
    a_refs = refs[:n_in]
    w_refs = refs[n_in:2 * n_in]
    r_ref, o_ref = refs[2 * n_in], refs[2 * n_in + 1]
    acc = r_ref[...]
    for a_ref, w_ref in zip(a_refs, w_refs):
        acc = acc + jnp.dot(a_ref[...].astype(BF16), w_ref[...], preferred_element_type=F32)
    o_ref[...] = acc


def _matmul_res(a_list, w_list, res, *, tm, tn):
    m, n = res.shape
    n_in = len(a_list)
    assert m % tm == 0 and n % tn == 0
    in_specs = ([pl.BlockSpec((tm, a.shape[1]), lambda i, j: (i, 0)) for a in a_list]
                + [pl.BlockSpec((w.shape[0], tn), lambda i, j: (0, j)) for w in w_list]
                + [pl.BlockSpec((tm, tn), lambda i, j: (i, j))])
    return pl.pallas_call(
        functools.partial(_matmul_res_kernel, n_in=n_in),
        grid=(m // tm, n // tn),
        in_specs=in_specs,
        out_specs=pl.BlockSpec((tm, tn), lambda i, j: (i, j)),
        out_shape=jax.ShapeDtypeStruct((m, n), F32),
        compiler_params=_cparams(("parallel", "parallel")),
        name="matmul_res",
    )(*a_list, *w_list, res)


def _rmsnorm_kernel(x_ref, g_ref, o_ref):
    x = x_ref[...]
    ms = jnp.mean(x * x, axis=-1, keepdims=True)
    o_ref[...] = x * lax.rsqrt(ms + RMS_EPS) * g_ref[...]


def _rmsnorm(x, g, *, tm):
    m, k = x.shape
    return pl.pallas_call(
        _rmsnorm_kernel,
        grid=(m // tm,),
        in_specs=[pl.BlockSpec((tm, k), lambda i: (i, 0)), pl.BlockSpec((1, k), lambda i: (0, 0))],
        out_specs=pl.BlockSpec((tm, k), lambda i: (i, 0)),
        out_shape=jax.ShapeDtypeStruct((m, k), F32),
        compiler_params=_cparams(("parallel",)),
        name="rmsnorm",
    )(x, g.reshape(1, k))


def _fox_prep_kernel(fl_ref, fb_ref, lf_ref, ct_ref, carry_ref, *, t):
    @pl.when(pl.program_id(0) == 0)
    def _():
        carry_ref[...] = jnp.zeros_like(carry_ref)

    x = fl_ref[...] + fb_ref[...]
    lf = jnp.minimum(x, 0.0) - jnp.log1p(jnp.exp(-jnp.abs(x)))
    lf_ref[...] = lf[:, :FOX_HEADS]
    row = lax.broadcasted_iota(I32, (t, t), 0)
    col = lax.broadcasted_iota(I32, (t, t), 1)
    tri = jnp.where(col <= row, 1.0, 0.0).astype(BF16)
    c = _dot_f32_left(tri, lf) + carry_ref[...]
    carry_ref[...] = c[t - 1:t, :]
    ct_ref[...] = c.T[:FOX_HEADS, :]


def _fox_prep(proj, fl_block, fb_pad, *, t):
    n = proj.shape[0]
    return pl.pallas_call(
        functools.partial(_fox_prep_kernel, t=t),
        grid=(n // t,),
        in_specs=[pl.BlockSpec((t, LANES), lambda i: (i, fl_block)),
                  pl.BlockSpec((1, LANES), lambda i: (0, 0))],
        out_specs=[pl.BlockSpec((t, FOX_HEADS), lambda i: (i, 0)),
                   pl.BlockSpec((FOX_HEADS, t), lambda i: (0, i))],
        out_shape=[jax.ShapeDtypeStruct((n, FOX_HEADS), F32),
                   jax.ShapeDtypeStruct((FOX_HEADS, n), F32)],
        scratch_shapes=[pltpu.VMEM((1, LANES), F32)],
        compiler_params=_cparams(("arbitrary",)),
        name="fox_prep",
    )(proj, fb_pad)


_SKIP_LOGIT_GAP = 110.0
_SKIP_NORM_SLACK = 1.05


def _fox_bounds_kernel(q_ref, k_ref, ct_ref, sel_ref, jmin_ref, qmax_ref, kmax_ref, cs_ref, ce_ref, *, tq, nq):
    i = pl.program_id(0)
    lane = lax.broadcasted_iota(I32, (FOX_HEADS, LANES), 1)
    sub = lax.broadcasted_iota(I32, (FOX_HEADS, LANES), 0)

    @pl.when(i == 0)
    def _():
        qmax_ref[...] = jnp.zeros_like(qmax_ref)
        kmax_ref[...] = jnp.zeros_like(kmax_ref)
        cs_ref[...] = jnp.zeros_like(cs_ref)
        ce_ref[...] = jnp.zeros_like(ce_ref)

    def head_sq_norm_max(x_ref):
        x = x_ref[...].astype(F32)
        sq = jnp.dot((x * x).astype(BF16), sel_ref[...], preferred_element_type=F32)
        return jnp.max(sq, axis=0, keepdims=True)

    qmax_ref[...] = jnp.maximum(qmax_ref[...], head_sq_norm_max(q_ref))
    kmax_ref[...] = jnp.maximum(kmax_ref[...], head_sq_norm_max(k_ref))
    cs_ref[...] = jnp.where(lane == i, ct_ref[:, 0:1], cs_ref[...])
    ce_ref[...] = jnp.where(lane == i, ct_ref[:, tq - 1:tq], ce_ref[...])

    @pl.when(i == nq - 1)
    def _():
        def to_col(row):
            return jnp.sum(jnp.where(sub == lane, row, 0.0), axis=-1, keepdims=True)

        b2 = (2.0 * _SKIP_NORM_SLACK * FOX_HEAD_DIM ** -0.5) * jnp.sqrt(to_col(qmax_ref[...]) * to_col(kmax_ref[...]))
        ce = ce_ref[...]
        out = jnp.zeros((FOX_HEADS, LANES), F32)
        for qi in range(nq):
            gap = b2 + cs_ref[:, qi:qi + 1] - ce
            skip = jnp.where((gap < -_SKIP_LOGIT_GAP) & (lane < qi), 1.0, 0.0)
            for p in range(FOX_HEADS // 2):
                both = skip[2 * p:2 * p + 1, :] * skip[2 * p + 1:2 * p + 2, :]
                count = jnp.sum(both, axis=-1, keepdims=True)
                out = jnp.where((sub == p) & (lane == qi), count, out)
        jmin_ref[...] = out.astype(I32)


def _fox_bounds(qkv_bf16, ct, *, tq):
    n = qkv_bf16.shape[0]
    nq = n // tq
    assert nq <= LANES
    sel = jnp.asarray((np.arange(FOX_WIDTH)[:, None] // FOX_HEAD_DIM == np.arange(LANES)[None, :]).astype(np.float32), BF16)
    table = pl.pallas_call(
        functools.partial(_fox_bounds_kernel, tq=tq, nq=nq),
        grid=(nq,),
        in_specs=[pl.BlockSpec((tq, FOX_WIDTH), lambda i: (i, 0)),
                  pl.BlockSpec((tq, FOX_WIDTH), lambda i: (i, 1)),
                  pl.BlockSpec((FOX_HEADS, tq), lambda i: (0, i)),
                  pl.BlockSpec(sel.shape, lambda i: (0, 0))],
        out_specs=pl.BlockSpec((FOX_HEADS, LANES), lambda i: (0, 0)),
        out_shape=jax.ShapeDtypeStruct((FOX_HEADS, LANES), I32),
        scratch_shapes=[pltpu.VMEM((1, LANES), F32), pltpu.VMEM((1, LANES), F32),
                        pltpu.VMEM((FOX_HEADS, LANES), F32), pltpu.VMEM((FOX_HEADS, LANES), F32)],
        compiler_params=_cparams(("arbitrary",)),
        name="fox_bounds",
    )(qkv_bf16, qkv_bf16, ct, sel)
    return table[:FOX_HEADS // 2, :nq]


def _fox_attn_kernel(jmin_ref, q_ref, k_ref, v_ref, ct_ref, o_ref, qs_ref, m_ref, l_ref, acc_ref, *, tq):
    i = pl.program_id(1)
    lane = lax.broadcasted_iota(I32, (1, LANES), 1)
    left = lane < FOX_HEAD_DIM
    q = q_ref[...] * jnp.asarray(FOX_HEAD_DIM ** -0.5, BF16)
    zero = jnp.zeros_like(q)
    qs_ref[0] = jnp.where(left, q, zero)
    qs_ref[1] = jnp.where(left, zero, q)
    q0 = pl.multiple_of(i * tq, tq)
    cref = [ct_ref[h:h + 1, pl.ds(q0, LANES)][:, 0:1] for h in range(2)]

    m_ref[...] = jnp.full_like(m_ref, -jnp.inf)
    l_ref[...] = jnp.zeros_like(l_ref)
    acc_ref[...] = jnp.zeros_like(acc_ref)
    ncol = tq // LANES

    def step(j, masked):
        k0 = pl.multiple_of(j * tq, tq)
        kb = k_ref[pl.ds(k0, tq), :]
        vb = v_ref[pl.ds(k0, tq), :]
        bias = [cref[h] - ct_ref[h:h + 1, pl.ds(k0, tq)] for h in range(2)]
        for h in range(2):
            s = lax.dot_general(qs_ref[h], kb, (((1,), (1,)), ((), ())), preferred_element_type=F32)
            s = s + bias[h]
            if masked:
                r = lax.broadcasted_iota(I32, (tq, tq), 0)
                c = lax.broadcasted_iota(I32, (tq, tq), 1)
                s = jnp.where(c <= r, s, -jnp.inf)
            cols = [s[:, c * LANES:(c + 1) * LANES] for c in range(ncol)]
            lane_max = functools.reduce(jnp.maximum, cols)
            m_old = m_ref[h]
            m_new = jnp.maximum(m_old, jnp.max(lane_max, axis=-1, keepdims=True))
            alpha = jnp.exp(m_old - m_new)
            ps = [jnp.exp(c - m_new) for c in cols]
            l_ref[h] = alpha * l_ref[h] + functools.reduce(jnp.add, ps)
            m_ref[h] = m_new
            p = jnp.concatenate([x.astype(BF16) for x in ps], axis=1)
            acc_ref[h] = alpha * acc_ref[h] + jnp.dot(p, vb, preferred_element_type=F32)

    def body(j, carry):
        step(j, False)
        return carry

    lax.fori_loop(jmin_ref[pl.program_id(0), i], i, body, 0)
    step(i, True)
    l0 = jnp.sum(l_ref[0], axis=-1, keepdims=True)
    l1 = jnp.sum(l_ref[1], axis=-1, keepdims=True)
    o_ref[...] = jnp.where(left, acc_ref[0] / l0, acc_ref[1] / l1).astype(o_ref.dtype)


def _fox_attn(qkv_bf16, ct, *, tq):
    n = qkv_bf16.shape[0]
    nb = FOX_WIDTH // LANES
    jmin = _fox_bounds(qkv_bf16, ct, tq=tq)
    return pl.pallas_call(
        functools.partial(_fox_attn_kernel, tq=tq),
        grid_spec=pltpu.PrefetchScalarGridSpec(
            num_scalar_prefetch=1,
            grid=(nb, n // tq),
            in_specs=[
                pl.BlockSpec((tq, LANES), lambda p, i, jm: (i, p)),
                pl.BlockSpec((n, LANES), lambda p, i, jm: (0, nb + p)),
                pl.BlockSpec((n, LANES), lambda p, i, jm: (0, 2 * nb + p)),
                pl.BlockSpec((None, 2, n), lambda p, i, jm: (p, 0, 0)),
            ],
            out_specs=pl.BlockSpec((tq, LANES), lambda p, i, jm: (i, p)),
            scratch_shapes=[pltpu.VMEM((2, tq, LANES), BF16), pltpu.VMEM((2, tq, LANES), F32),
                            pltpu.VMEM((2, tq, LANES), F32), pltpu.VMEM((2, tq, LANES), F32)],
        ),
        out_shape=jax.ShapeDtypeStruct((n, FOX_WIDTH), BF16),
        compiler_params=_cparams(("arbitrary", "arbitrary")),
        name="fox_attn",
    )(jmin, qkv_bf16, qkv_bf16, qkv_bf16, ct.reshape(nb, 2, n))


_RW = RWKV_WIDTH
_RWKV_PERM = np.concatenate([
    np.arange(0, _RW),
    np.arange(_RW + DECAY_LORA, 2 * _RW + DECAY_LORA),
    np.arange(2 * _RW + DECAY_LORA, 3 * _RW + DECAY_LORA),
    np.arange(3 * _RW + DECAY_LORA + ICLR_LORA, RWKV_PROJ),
    np.arange(_RW, _RW + DECAY_LORA),
    np.arange(3 * _RW + DECAY_LORA, 3 * _RW + DECAY_LORA + ICLR_LORA),
])
_RWKV_INV_PERM = np.argsort(_RWKV_PERM)


def _head_block_ones(width, head_dim):
    idx = np.arange(width) // head_dim
    return jnp.asarray((idx[:, None] == idx[None, :]).astype(np.float32), BF16)


def _rwkv_prep_math(p, p_prev, mu, w0, w2p, a0, a2p, g2, k_k, k_a, bones):
    xs = p + mu * (p_prev - p)
    r = xs[:, 0:_RW]
    k = xs[:, _RW:2 * _RW]
    v = xs[:, 2 * _RW:3 * _RW]
    gl = xs[:, 3 * _RW:3 * _RW + GATE_LORA]
    wa = xs[:, 3 * _RW + GATE_LORA:]
    w = -_softplus(-(w0 + jnp.dot(jnp.tanh(wa).astype(BF16), w2p, preferred_element_type=F32))) - 0.5
    decay = jnp.exp(-jnp.exp(w))
    a = _sigmoid(a0 + jnp.dot(wa.astype(BF16), a2p, preferred_element_type=F32))
    g = jnp.dot(_sigmoid(gl).astype(BF16), g2, preferred_element_type=F32)
    kkr = k * k_k
    ss = _dot_f32_right(kkr * kkr, bones)
    kk = kkr / jnp.maximum(jnp.sqrt(ss), 1e-12)
    k2 = k * (1.0 + (a - 1.0) * k_a)
    return r, k2, v, kk, kk * a, decay, g


def _rwkv_prep_kernel(p_ref, prev_ref, mu_ref, w0_ref, w2_ref, a0_ref, a2_ref, g2_ref, kk_ref, ka_ref, bones_ref,
                      r_o, k_o, v_o, kk_o, b_o, d_o, g_o, last_o, buf_ref, *, t, shifted):
    p = p_ref[...]
    if shifted:
        @pl.when(pl.program_id(0) == 0)
        def _():
            buf_ref[SUBLANES - 1:SUBLANES, :] = prev_ref[...]

        buf_ref[SUBLANES:SUBLANES + t, :] = p
        p_prev = buf_ref[SUBLANES - 1:SUBLANES - 1 + t, :]
        buf_ref[SUBLANES - 1:SUBLANES, :] = p[t - 1:t, :]
        last_o[...] = p[t - 1:t, :]
    else:
        p_prev = prev_ref[...]
        last_o[...] = p
    outs = _rwkv_prep_math(p, p_prev, mu_ref[...], w0_ref[...], w2_ref[...], a0_ref[...], a2_ref[...], g2_ref[...],
                           kk_ref[...], ka_ref[...], bones_ref[...])
    for o_ref, val in zip((r_o, k_o, v_o, kk_o, b_o, d_o, g_o), outs):
        o_ref[...] = val


def _rwkv_prep(proj, prev, mu, w0, w2p, a0, a2p, g2, k_k, k_a, bones, *, t, shifted):
    n = proj.shape[0]
    row = lambda w: pl.BlockSpec((1, w), lambda i: (0, 0))
    full = lambda a: pl.BlockSpec(a.shape, lambda i: (0, 0))
    tok = lambda w: pl.BlockSpec((t, w), lambda i: (i, 0))
    prev_spec = row(RWKV_PROJ) if shifted else tok(RWKV_PROJ)
    last_spec = row(RWKV_PROJ) if shifted else tok(RWKV_PROJ)
    last_shape = (1, RWKV_PROJ) if shifted else (n, RWKV_PROJ)
    return pl.pallas_call(
        functools.partial(_rwkv_prep_kernel, t=t, shifted=shifted),
        grid=(n // t,),
        in_specs=[tok(RWKV_PROJ), prev_spec, row(RWKV_PROJ), row(_RW), full(w2p), row(_RW), full(a2p), full(g2),
                  row(_RW), row(_RW), full(bones)],
        out_specs=[tok(_RW)] * 7 + [last_spec],
        out_shape=[jax.ShapeDtypeStruct((n, _RW), F32)] * 7 + [jax.ShapeDtypeStruct(last_shape, F32)],
        scratch_shapes=[pltpu.VMEM((SUBLANES + t, RWKV_PROJ), F32)],
        compiler_params=_cparams(("arbitrary",)),
        name="rwkv_prep",
    )(proj, prev, mu, w0, w2p, a0, a2p, g2, k_k, k_a, bones)


_SCAN_GROUP = 16


def _rwkv_scan_kernel(r_ref, k_ref, v_ref, kk_ref, b_ref, d_ref, s0_ref, o_ref, st_ref, s_ref, *, t):
    j = pl.program_id(1)

    @pl.when(j == 0)
    def _():
        s_ref[...] = s0_ref[...]

    left = lax.broadcasted_iota(I32, (RWKV_HEAD_DIM, LANES), 1) < RWKV_HEAD_DIM

    def seg(x):
        s0 = jnp.sum(jnp.where(left, x, 0.0), axis=-1, keepdims=True)
        s1 = jnp.sum(jnp.where(left, 0.0, x), axis=-1, keepdims=True)
        return jnp.where(left, s0, s1)

    group = _SCAN_GROUP if t % _SCAN_GROUP == 0 else t

    def readout(s, r_rows):
        s_bd = jnp.concatenate([jnp.where(left, s, 0.0), jnp.where(left, 0.0, s)], axis=0).astype(BF16)
        return lax.dot_general(r_rows, s_bd, (((1,), (1,)), ((), ())), preferred_element_type=F32)

    piece_pairs = ((0, 0), (0, 1), (0, 2), (1, 0), (1, 1), (2, 0))
    n_c = len(piece_pairs) * group
    kc = LANES * pl.cdiv(n_c, LANES)
    c_lane = lax.broadcasted_iota(I32, (LANES, kc), 1)
    c_step = c_lane % group
    c_live = c_lane < n_c
    pad_rows = jnp.zeros((kc - n_c, LANES), F32)

    def value_columns(v_t):
        v_p = [x.astype(F32) for x in _split3(v_t)]
        v_rows = jnp.concatenate([v_p[a] for a, _ in piece_pairs] + [pad_rows], axis=0)
        return v_rows.T

    def key_rows(k_t):
        k_p = [x.astype(F32) for x in _split3(k_t)]
        return jnp.concatenate([k_p[b] for _, b in piece_pairs] + [pad_rows], axis=0).astype(BF16)

    def outer_product(v_cols, k_rows, u):
        lhs = jnp.where(c_live & (c_step == u), v_cols, 0.0).astype(BF16)
        vk = jnp.dot(lhs, k_rows, preferred_element_type=F32)
        return jnp.where(left, vk[:RWKV_HEAD_DIM], vk[RWKV_HEAD_DIM:])

    n_groups = t // group

    def group_value_columns(gi):
        base = gi * group if isinstance(gi, int) else pl.multiple_of(gi * group, group)
        return tuple(value_columns(v_ref[pl.ds(base, group), p * LANES:(p + 1) * LANES]) for p in range(HEAD_PAIRS))

    def body(gi, carry):
        states, v_cols = carry
        base = pl.multiple_of(gi * group, group)
        states = list(states)
        tiles = []
        for p in range(HEAD_PAIRS):
            sl = slice(p * LANES, (p + 1) * LANES)
            tiles.append(tuple(ref[pl.ds(base, group), sl] for ref in (kk_ref, v_ref, d_ref, b_ref, k_ref, r_ref)))
        k_rows = [key_rows(tiles[p][4]) for p in range(HEAD_PAIRS)]
        vk_tiles = [[None] * group for _ in range(HEAD_PAIRS)]
        for u in range(group):
            for p in range(HEAD_PAIRS):
                vk_tiles[p][u] = outer_product(v_cols[p], k_rows[p], u)
        v_cols_next = group_value_columns(jnp.minimum(gi + 1, n_groups - 1))
        o_rows = [[] for _ in range(HEAD_PAIRS)]
        for u in range(group):
            row = lambda x: x[u:u + 1, :]
            for p in range(HEAD_PAIRS):
                kk_t, v_t, d_t, b_t, k_t, r_t = tiles[p]
                s = states[p]
                sk = seg(s * row(kk_t))
                s = s * row(d_t) - sk * row(b_t) + vk_tiles[p][u]
                states[p] = s
                r_rows = jnp.broadcast_to(r_t, (SUBLANES, LANES)) if group == 1 else r_t
                o_rows[p].append(readout(s, r_rows.astype(BF16))[u:u + 1, :])
        for p in range(HEAD_PAIRS):
            sl = slice(p * LANES, (p + 1) * LANES)
            o_ref[pl.ds(base, group), sl] = jnp.concatenate(o_rows[p], axis=0) if group > 1 else o_rows[p][0]
        return tuple(states), v_cols_next

    states, _ = lax.fori_loop(0, n_groups, body,
                              (tuple(s_ref[p] for p in range(HEAD_PAIRS)), group_value_columns(0)))
    for p in range(HEAD_PAIRS):
        s_ref[p] = states[p]

    @pl.when(j == pl.num_programs(1) - 1)
    def _():
        st_ref[...] = s_ref[...]


def _rwkv_scan(r, k, v, kk, b, d, s0_pairs, *, batch, t):
    n = r.shape[0]
    seq = n // batch
    tok = pl.BlockSpec((None, t, _RW), lambda bi, j: (bi, j, 0))
    st = pl.BlockSpec((None, HEAD_PAIRS, RWKV_HEAD_DIM, LANES), lambda bi, j: (bi, 0, 0, 0))
    o, s_t = pl.pallas_call(
        functools.partial(_rwkv_scan_kernel, t=t),
        grid=(batch, seq // t),
        in_specs=[tok] * 6 + [st],
        out_specs=[tok, st],
        out_shape=[jax.ShapeDtypeStruct((batch, seq, _RW), F32),
                   jax.ShapeDtypeStruct((batch, HEAD_PAIRS, RWKV_HEAD_DIM, LANES), F32)],
        scratch_shapes=[pltpu.VMEM((HEAD_PAIRS, RWKV_HEAD_DIM, LANES), F32)],
        compiler_params=_cparams(("arbitrary", "arbitrary")),
        name="rwkv_scan",
    )(*(x.reshape(batch, seq, _RW) for x in (r, k, v, kk, b, d)), s0_pairs)
    return o.reshape(n, _RW), s_t


def _rwkv_post_kernel(o_ref, r_ref, k_ref, v_ref, g_ref, lnw_ref, lnb_ref, rk_ref, bones_ref, out_ref):
    bones = bones_ref[...]
    inv = 1.0 / RWKV_HEAD_DIM
    o = o_ref[...]
    mean = _dot_f32_right(o, bones) * inv
    cen = o - mean
    var = _dot_f32_right(cen * cen, bones) * inv
    gn = cen * lax.rsqrt(var + GN_EPS) * lnw_ref[...] + lnb_ref[...]
    bonus = _dot_f32_right(r_ref[...] * k_ref[...] * rk_ref[...], bones) * v_ref[...]
    out_ref[...] = ((gn + bonus) * g_ref[...]).astype(out_ref.dtype)


def _rwkv_post(o, r, k, v, g, lnw, lnb, rk, bones, *, t):
    n = o.shape[0]
    tok = pl.BlockSpec((t, _RW), lambda i: (i, 0))
    row = pl.BlockSpec((1, _RW), lambda i: (0, 0))
    return pl.pallas_call(
        _rwkv_post_kernel,
        grid=(n // t,),
        in_specs=[tok] * 5 + [row] * 3 + [pl.BlockSpec(bones.shape, lambda i: (0, 0))],
        out_specs=tok,
        out_shape=jax.ShapeDtypeStruct((n, _RW), BF16),
        compiler_params=_cparams(("parallel",)),
        name="rwkv_post",
    )(o, r, k, v, g, lnw, lnb, rk, bones)


def _to_pairs(s):
    b = s.shape[0]
    return s.reshape(b, HEAD_PAIRS, 2, RWKV_HEAD_DIM, RWKV_HEAD_DIM).transpose(0, 1, 3, 2, 4).reshape(
        b, HEAD_PAIRS, RWKV_HEAD_DIM, LANES)


def _from_pairs(s):
    b = s.shape[0]
    return s.reshape(b, HEAD_PAIRS, RWKV_HEAD_DIM, 2, RWKV_HEAD_DIM).transpose(0, 1, 3, 2, 4).reshape(
        b, RWKV_HEADS, RWKV_HEAD_DIM, RWKV_HEAD_DIM)


def _rwkv_mix(proj, prev, s0, rw, *, batch, t_prep, t_scan, shifted):
    mu, w0, w2, a0, a2, g2, k_k, k_a, r_k, ln_w, ln_b = rw
    bones = _head_block_ones(_RW, RWKV_HEAD_DIM)
    zeros = jnp.zeros((LANES - DECAY_LORA, _RW), F32)
    w2p = jnp.concatenate([w2, zeros], axis=0).astype(BF16)
    a2p = jnp.concatenate([zeros, a2], axis=0).astype(BF16)
    row = lambda x: x.reshape(1, -1)
    r, k, v, kk, b, d, g, last = _rwkv_prep(
        proj, prev, row(mu[_RWKV_PERM]), row(w0), w2p, row(a0), a2p, g2.astype(BF16), row(k_k), row(k_a), bones,
        t=t_prep, shifted=shifted)
    o, s_t = _rwkv_scan(r, k, v, kk, b, d, _to_pairs(s0), batch=batch, t=t_scan)
    out = _rwkv_post(o, r, k, v, g, row(ln_w), row(ln_b), row(r_k), bones, t=t_prep)
    return out, last[:, _RWKV_INV_PERM], _from_pairs(s_t)


def _top_rows(s, k, order):
    vals, ords = [], []
    for _ in range(k):
        m = jnp.max(s, axis=0, keepdims=True)
        pick = jnp.min(jnp.where(s == m, order, jnp.inf), axis=0, keepdims=True)
        vals.append(m)
        ords.append(pick)
        s = jnp.where(order == pick, -jnp.inf, s)
    return jnp.concatenate(vals, axis=0), jnp.concatenate(ords, axis=0)


def _take_rows(table, pos):
    out = jnp.zeros(pos.shape, table.dtype)
    for i in range(table.shape[0]):
        out = jnp.where(pos == float(i), table[i:i + 1, :], out)
    return out


_CAND_GROUPS = ((0, 2, 16), (2, 4, 8))
_CAND_TAIL_I0, _CAND_TAIL_J = 4, 3
_CAND_HEAD_ROWS = sum((hi - lo) * nj for lo, hi, nj in _CAND_GROUPS)
_CAND_ROWS = _CAND_HEAD_ROWS + _CAND_TAIL_J * PEER_TOPK


def _candidate_tables(tm):
    k = PEER_TOPK
    assert k == 16 and _CAND_GROUPS == ((0, 2, 16), (2, 4, 8))
    r = lax.broadcasted_iota(I32, (_CAND_ROWS, tm), 0)
    r2 = r - 2 * k
    flat2 = (2 + (r2 >> 3)) * k + (r2 & 7)
    r3 = r - _CAND_HEAD_ROWS
    i3 = r3 & (k - 1)
    flat3 = i3 * k + (r3 >> 4)
    flat = jnp.where(r < 2 * k, r, jnp.where(r < _CAND_HEAD_ROWS, flat2, flat3))
    valid = (r < _CAND_HEAD_ROWS) | (i3 >= _CAND_TAIL_I0)
    return flat.astype(F32), valid


def _peer_route_kernel(q_ref, keys_ref, g_ref, e1_ref, e2_ref, gv_ref, gt_ref, *, tm):
    k = PEER_TOPK
    flat, valid = _candidate_tables(tm)
    key_order = lax.broadcasted_iota(I32, (N_KEYS, tm), 0).astype(F32)

    e1s, e2s, gs = [], [], []
    for h in range(PEER_HEADS):
        qh = q_ref[:, h * D_KEY:(h + 1) * D_KEY].astype(BF16)
        score = lambda p: lax.dot_general(keys_ref[h, p], qh, (((1,), (1,)), ((), ())), preferred_element_type=F32)
        v1, i1 = _top_rows(score(0), k, key_order)
        v2, i2 = _top_rows(score(1), k, key_order)
        parts = [v1[i:i + 1, :] + v2[:nj, :] for lo, hi, nj in _CAND_GROUPS for i in range(lo, hi)]
        parts += [v1 + v2[j:j + 1, :] for j in range(_CAND_TAIL_J)]
        cand = jnp.where(valid, jnp.concatenate(parts, axis=0), -jnp.inf)
        sc, pos = _top_rows(cand, k, flat)
        pos_i = jnp.floor(pos * (1.0 / k))
        e1s.append(_take_rows(i1, pos_i))
        e2s.append(_take_rows(i2, pos - k * pos_i))
        ex = jnp.exp(sc - sc[0:1, :])
        gs.append(ex / jnp.sum(ex, axis=0, keepdims=True))
    e1_ref[...] = jnp.concatenate(e1s, axis=0).T
    e2_ref[...] = jnp.concatenate(e2s, axis=0).T
    gv_ref[...] = jnp.concatenate(gs, axis=0).T

    eid = lax.broadcasted_iota(I32, (N_KEYS, PEER_HEADS * k), 0).astype(F32)
    rows = _ROUTE_GROUP

    def body(gi, carry):
        for half in range(_ROUTE_UNROLL):
            base = pl.multiple_of((gi * _ROUTE_UNROLL + half) * rows, rows)
            stage = half * rows * _ROUTE_PITCH
            e1_t = e1_ref[pl.ds(base, rows), :]
            e2_t = e2_ref[pl.ds(base, rows), :]
            gv_t = gv_ref[pl.ds(base, rows), :]
            for u in range(rows):
                a = jnp.where(eid == e1_t[u:u + 1, :], gv_t[u:u + 1, :], 0.0).astype(BF16)
                b = jnp.where(eid == e2_t[u:u + 1, :], 1.0, 0.0).astype(BF16)
                gt_ref[stage + u * _ROUTE_PITCH:stage + u * _ROUTE_PITCH + N_KEYS, :] = lax.dot_general(
                    a, b, (((1,), (1,)), ((), ())), preferred_element_type=F32)
        for half in range(_ROUTE_UNROLL):
            base = pl.multiple_of((gi * _ROUTE_UNROLL + half) * rows, rows)
            stage = half * rows * _ROUTE_PITCH
            for e1 in range(N_KEYS):
                slab = gt_ref[pl.ds(stage + e1, rows, stride=_ROUTE_PITCH), :]
                g_ref[pl.ds(base, rows), e1 * N_KEYS:(e1 + 1) * N_KEYS] = slab.astype(g_ref.dtype)
        return carry

    lax.fori_loop(0, tm // (rows * _ROUTE_UNROLL), body, 0)


_ROUTE_GROUP = 2 * SUBLANES
_ROUTE_UNROLL = 2
_ROUTE_PITCH = N_KEYS + SUBLANES


def _peer_route(q, keys_pad, *, tm):
    n = q.shape[0]
    width = PEER_HEADS * PEER_TOPK
    assert tm % (_ROUTE_GROUP * _ROUTE_UNROLL) == 0
    return pl.pallas_call(
        functools.partial(_peer_route_kernel, tm=tm),
        grid=(n // tm,),
        in_specs=[pl.BlockSpec((tm, PEER_HEADS * D_KEY), lambda i: (i, 0)),
                  pl.BlockSpec(keys_pad.shape, lambda i: (0, 0, 0, 0))],
        out_specs=pl.BlockSpec((tm, N_KEYS * N_KEYS), lambda i: (i, 0)),
        out_shape=jax.ShapeDtypeStruct((n, N_KEYS * N_KEYS), BF16),
        scratch_shapes=[pltpu.VMEM((tm, width), F32)] * 3
                       + [pltpu.VMEM((_ROUTE_UNROLL * _ROUTE_GROUP * _ROUTE_PITCH, N_KEYS), F32)],
        compiler_params=_cparams(("parallel",)),
        name="peer_route",
    )(q, keys_pad)


def _peer_mlp_kernel(x_ref, gain_ref, u_ref, v_ref, g_ref, o_ref, xn_ref, acc_ref):
    j = pl.program_id(1)

    @pl.when(j == 0)
    def _():
        x = x_ref[...]
        ms = jnp.mean(x * x, axis=-1, keepdims=True)
        xn_ref[...] = (x * lax.rsqrt(ms + RMS_EPS) * gain_ref[...]).astype(BF16)
        acc_ref[...] = jnp.zeros_like(acc_ref)

    h = lax.dot_general(xn_ref[...], u_ref[...], (((1,), (1,)), ((), ())), preferred_element_type=F32)
    act = 0.5 * h * (1.0 + lax.erf(h * (2.0 ** -0.5)))
    w = (act * g_ref[...].astype(F32)).astype(BF16)
    acc_ref[...] += jnp.dot(w, v_ref[...], preferred_element_type=F32)

    @pl.when(j == pl.num_programs(1) - 1)
    def _():
        o_ref[...] = x_ref[...] + acc_ref[...]


def _peer_mlp(x, gain, u_bf16, v_bf16, gmap, *, layer, tm, te):
    n, dm = x.shape
    ne = u_bf16.shape[1]
    return pl.pallas_call(
        _peer_mlp_kernel,
        grid=(n // tm, ne // te),
        in_specs=[pl.BlockSpec((tm, dm), lambda i, j: (i, 0)),
                  pl.BlockSpec((1, dm), lambda i, j: (0, 0)),
                  pl.BlockSpec((None, te, dm), lambda i, j: (layer, j, 0)),
                  pl.BlockSpec((None, te, dm), lambda i, j: (layer, j, 0)),
                  pl.BlockSpec((tm, te), lambda i, j: (i, j))],
        out_specs=pl.BlockSpec((tm, dm), lambda i, j: (i, 0)),
        out_shape=jax.ShapeDtypeStruct((n, dm), F32),
        scratch_shapes=[pltpu.VMEM((tm, dm), BF16), pltpu.VMEM((tm, dm), F32)],
        compiler_params=_cparams(("parallel", "arbitrary")),
        name="peer_mlp",
    )(x, gain.reshape(1, dm), u_bf16, v_bf16, gmap)


def _peer_keys_padded(keys):
    z = jnp.zeros_like(keys[:, 0])
    first = jnp.concatenate([keys[:, 0], z], axis=-1)
    second = jnp.concatenate([z, keys[:, 1]], axis=-1)
    return jnp.stack([first, second], axis=1).astype(BF16)


def _peer(y, gain, wq_bf16, keys_pad, u_bf16, v_bf16, *, layer, tm_proj, tm_route, tm_mlp, te):
    q = _norm_matmul(y, gain, wq_bf16, tm=tm_proj, tn=wq_bf16.shape[1])
    gmap = _peer_route(q, keys_pad, tm=tm_route)
    return _peer_mlp(y, gain, u_bf16, v_bf16, gmap, layer=layer, tm=tm_mlp, te=te)


def _conv_silu(x, buf_ref, w_ref, b_ref, q):
    buf_ref[SUBLANES:SUBLANES + q, :] = x
    acc = b_ref[...] + x * w_ref[CONV_W - 1:CONV_W, :]
    for back in range(1, CONV_W):
        acc = acc + buf_ref[SUBLANES - back:SUBLANES - back + q, :] * w_ref[CONV_W - 1 - back:CONV_W - back, :]
    tail = buf_ref[q + SUBLANES - (CONV_W - 1):q + SUBLANES, :]
    buf_ref[SUBLANES - (CONV_W - 1):SUBLANES, :] = tail
    return _silu(acc), tail


def _ssd_kernel(z_ref, x_ref, bc_ref, dt_ref, cwx_ref, cwbc_ref, cbx_ref, cbbc_ref, dtb_ref, alog_ref, dskip_ref,
                nw_ref, expand_ref, y_ref, convx_ref, convbc_ref, ht_ref, bufx_ref, bufbc_ref, h_ref, *, q, heads):
    hpg = heads // N_GROUPS
    gw = hpg * SSM_HEAD_DIM

    @pl.when(pl.program_id(0) == 0)
    def _():
        bufx_ref[...] = jnp.zeros_like(bufx_ref)
        bufbc_ref[...] = jnp.zeros_like(bufbc_ref)
        h_ref[...] = jnp.zeros_like(h_ref)

    xs, tail_x = _conv_silu(x_ref[...], bufx_ref, cwx_ref, cbx_ref, q)
    bc, tail_bc = _conv_silu(bc_ref[...], bufbc_ref, cwbc_ref, cbbc_ref, q)
    convx_ref[...] = tail_x
    convbc_ref[...] = tail_bc

    dt = _softplus(dt_ref[...] + dtb_ref[...])
    a = -jnp.exp(alog_ref[...])
    row = lax.broadcasted_iota(I32, (q, q), 0)
    col = lax.broadcasted_iota(I32, (q, q), 1)
    lower = col <= row
    tri = jnp.where(lower, 1.0, 0.0).astype(BF16)
    acs = _dot_f32_left(tri, dt * a)
    acs_t = acs.T
    expand = expand_ref[...]
    dt_e = _dot_f32_right(dt, expand)
    acs_e = _dot_f32_right(acs, expand)
    last_e = acs_e[q - 1:q, :]
    xc = xs * dt_e
    xcd = (xc * jnp.exp(last_e - acs_e)).astype(BF16)
    xc_b = xc.astype(BF16)
    grow = jnp.exp(acs_e)
    chunk_decay = jnp.exp(last_e)

    nbc = N_GROUPS * D_STATE
    for g in range(N_GROUPS):
        bm = bc[:, g * D_STATE:(g + 1) * D_STATE].astype(BF16)
        cm = bc[:, nbc + g * D_STATE:nbc + (g + 1) * D_STATE].astype(BF16)
        cb = lax.dot_general(cm, bm, (((1,), (1,)), ((), ())), preferred_element_type=F32)
        sl = slice(g * gw, (g + 1) * gw)
        h_prev = h_ref[g]
        y_off = jnp.dot(cm, h_prev.astype(BF16), preferred_element_type=F32) * grow[:, sl]
        y_diag = []
        for r in range(hpg):
            hd = g * hpg + r
            diff = acs[:, hd:hd + 1] - acs_t[hd:hd + 1, :]
            m = (cb * jnp.exp(jnp.where(lower, diff, -jnp.inf))).astype(BF16)
            y_diag.append(jnp.dot(m, xc_b[:, hd * SSM_HEAD_DIM:(hd + 1) * SSM_HEAD_DIM], preferred_element_type=F32))
        y_g = jnp.concatenate(y_diag, axis=1) + y_off + dskip_ref[:, sl] * xs[:, sl]
        states = jnp.dot(bm.T, xcd[:, sl], preferred_element_type=F32)
        h_ref[g] = h_prev * chunk_decay[:, sl] + states
        zg = z_ref[:, sl]
        y_g = y_g * _silu(zg)
        ms = jnp.mean(y_g * y_g, axis=-1, keepdims=True)
        y_ref[:, sl] = (y_g * lax.rsqrt(ms + RMS_EPS) * nw_ref[:, sl]).astype(y_ref.dtype)

    @pl.when(pl.program_id(0) == pl.num_programs(0) - 1)
    def _():
        ht_ref[...] = h_ref[...]


def _ssd(proj, conv_w, conv_b, dt_bias, a_log, d_skip, norm_w, *, heads):
    n = proj.shape[0]
    d_inner = heads * SSM_HEAD_DIM
    q = CHUNK
    nbc = 2 * N_GROUPS * D_STATE
    assert nbc == d_inner and n % q == 0
    gw = d_inner // N_GROUPS
    pad = lambda x: jnp.pad(x, (0, LANES - heads)).reshape(1, LANES)
    expand = jnp.asarray((np.arange(LANES)[:, None] == (np.arange(d_inner) // SSM_HEAD_DIM)[None, :]).astype(np.float32), BF16)
    blk = lambda w, j: pl.BlockSpec((q, w), lambda i, j=j: (i, j))
    full = lambda a: pl.BlockSpec(a.shape, lambda i: (0,) * a.ndim)
    small = [conv_w[:, :d_inner], conv_w[:, d_inner:], conv_b[:d_inner].reshape(1, -1), conv_b[d_inner:].reshape(1, -1),
             pad(dt_bias), pad(a_log), jnp.repeat(d_skip, SSM_HEAD_DIM).reshape(1, -1), norm_w.reshape(1, -1), expand]
    y, cx, cbc, ht = pl.pallas_call(
        functools.partial(_ssd_kernel, q=q, heads=heads),
        grid=(n // q,),
        in_specs=[blk(d_inner, 0), blk(d_inner, 1), blk(d_inner, 2), blk(LANES, 3 * d_inner // LANES)]
                 + [full(a) for a in small],
        out_specs=[pl.BlockSpec((q, d_inner), lambda i: (i, 0)),
                   pl.BlockSpec((CONV_W - 1, d_inner), lambda i: (0, 0)),
                   pl.BlockSpec((CONV_W - 1, nbc), lambda i: (0, 0)),
                   pl.BlockSpec((N_GROUPS, D_STATE, gw), lambda i: (0, 0, 0))],
        out_shape=[jax.ShapeDtypeStruct((n, d_inner), BF16),
                   jax.ShapeDtypeStruct((CONV_W - 1, d_inner), F32),
                   jax.ShapeDtypeStruct((CONV_W - 1, nbc), F32),
                   jax.ShapeDtypeStruct((N_GROUPS, D_STATE, gw), F32)],
        scratch_shapes=[pltpu.VMEM((SUBLANES + q, d_inner), F32), pltpu.VMEM((SUBLANES + q, nbc), F32),
                        pltpu.VMEM((N_GROUPS, D_STATE, gw), F32)],
        compiler_params=_cparams(("arbitrary",)),
        name="ssd",
    )(proj, proj, proj, proj, *small)
    hpg = heads // N_GROUPS
    h_t = ht.reshape(N_GROUPS, D_STATE, hpg, SSM_HEAD_DIM).transpose(0, 2, 3, 1).reshape(heads, SSM_HEAD_DIM, D_STATE)
    return y, jnp.concatenate([cx, cbc], axis=1), h_t


def _ssm_step_kernel(z_ref, x_ref, bc_ref, dt_ref, bufx_ref, bufbc_ref, h0_ref, cwx_ref, cwbc_ref, cbx_ref, cbbc_ref,
                     dtb_ref, alog_ref, dskip_ref, nw_ref, y_ref, nbufx_ref, nbufbc_ref, h_ref, *, heads):
    hpg = heads // N_GROUPS
    gw = hpg * SSM_HEAD_DIM
    nbc = N_GROUPS * D_STATE
    fullx = jnp.concatenate([bufx_ref[...], x_ref[...]], axis=1)
    nbufx_ref[...] = fullx[:, 1:]
    xs = _silu(cbx_ref[...] + jnp.sum(fullx * cwx_ref[...], axis=1, keepdims=True))
    fullbc = jnp.concatenate([bufbc_ref[...], bc_ref[...]], axis=0)
    nbufbc_ref[...] = fullbc[1:, :]
    bc = _silu(cbbc_ref[...] + jnp.sum(fullbc * cwbc_ref[...], axis=0, keepdims=True))
    dt = _softplus(dt_ref[...] + dtb_ref[...])
    dec = jnp.exp(dt * (-jnp.exp(alog_ref[...])))
    xdt = xs * dt
    for g in range(N_GROUPS):
        rows = slice(g * gw, (g + 1) * gw)
        bm = bc[:, g * D_STATE:(g + 1) * D_STATE]
        cm = bc[:, nbc + g * D_STATE:nbc + (g + 1) * D_STATE]
        h_new = h0_ref[rows, :] * dec[rows, :] + xdt[rows, :] * bm
        h_ref[rows, :] = h_new
        y = jnp.sum(h_new * cm, axis=-1, keepdims=True) + dskip_ref[rows, :] * xs[rows, :]
        y = y * _silu(z_ref[rows, :])
        ms = jnp.mean(y * y, axis=0, keepdims=True)
        y_ref[rows, :] = y * lax.rsqrt(ms + RMS_EPS) * nw_ref[rows, :]


def _ssm_step(proj, conv_buf, h0, conv_w, conv_b, dt_bias, a_log, d_skip, norm_w, *, heads):
    b = proj.shape[0]
    d_inner = heads * SSM_HEAD_DIM
    nbc2 = 2 * N_GROUPS * D_STATE
    col = lambda x: x.reshape(b, d_inner, 1)
    per_head = lambda v: jnp.repeat(v, SSM_HEAD_DIM, axis=-1)
    z = col(proj[:, :d_inner])
    x = col(proj[:, d_inner:2 * d_inner])
    bc = proj[:, 2 * d_inner:2 * d_inner + nbc2].reshape(b, 1, nbc2)
    dt = col(per_head(proj[:, 2 * d_inner + nbc2:2 * d_inner + nbc2 + heads]))
    bufx = conv_buf[:, :, :d_inner].transpose(0, 2, 1)
    bufbc = conv_buf[:, :, d_inner:]
    pcol = lambda v: v.reshape(d_inner, 1)
    params = [conv_w[:, :d_inner].T, conv_w[:, d_inner:], pcol(conv_b[:d_inner]), conv_b[d_inner:].reshape(1, nbc2),
              pcol(per_head(dt_bias)), pcol(per_head(a_log)), pcol(per_head(d_skip)), pcol(norm_w)]
    seq = lambda *shape: pl.BlockSpec((None,) + shape, lambda i: (i,) + (0,) * len(shape))
    full = lambda a: pl.BlockSpec(a.shape, lambda i: (0,) * a.ndim)
    y, nbx, nbbc, h = pl.pallas_call(
        functools.partial(_ssm_step_kernel, heads=heads),
        grid=(b,),
        in_specs=[seq(d_inner, 1), seq(d_inner, 1), seq(1, nbc2), seq(d_inner, 1), seq(d_inner, CONV_W - 1),
                  seq(CONV_W - 1, nbc2), seq(d_inner, D_STATE)] + [full(a) for a in params],
        out_specs=[seq(d_inner, 1), seq(d_inner, CONV_W - 1), seq(CONV_W - 1, nbc2), seq(d_inner, D_STATE)],
        out_shape=[jax.ShapeDtypeStruct((b, d_inner, 1), F32), jax.ShapeDtypeStruct((b, d_inner, CONV_W - 1), F32),
                   jax.ShapeDtypeStruct((b, CONV_W - 1, nbc2), F32), jax.ShapeDtypeStruct((b, d_inner, D_STATE), F32)],
        compiler_params=_cparams(("parallel",)),
        name="ssm_step",
    )(z, x, bc, dt, bufx, bufbc, h0.reshape(b, d_inner, D_STATE), *params)
    new_buf = jnp.concatenate([nbx.transpose(0, 2, 1), nbbc], axis=2)
    return y.reshape(b, d_inner), new_buf, h.reshape(b, heads, SSM_HEAD_DIM, D_STATE)


def _fox_decode_kernel(pt_ref, q_ref, kc_ref, vc_ref, lfc_ref, *refs, pages):
    k_refs = refs[:pages]
    v_refs = refs[pages:2 * pages]
    lf_refs = refs[2 * pages:3 * pages]
    o_ref, qb_ref, m_ref, l_ref, acc_ref, later_ref = refs[3 * pages:]
    j = pl.program_id(1)
    first_lane = lax.broadcasted_iota(I32, (1, LANES), 1) == 0

    @pl.when(j == 0)
    def _():
        s_rows = []
        for h in range(FOX_HEADS):
            qh = q_ref[h] * (FOX_HEAD_DIM ** -0.5)
            qb_ref[h] = jnp.broadcast_to(qh, (FOX_HEAD_DIM, LANES))
            s_rows.append(jnp.sum(qh * kc_ref[h], axis=0, keepdims=True))
            acc_ref[h] = jnp.where(first_lane, vc_ref[h], 0.0)
        m_ref[...] = jnp.broadcast_to(jnp.concatenate(s_rows, axis=0), m_ref.shape)
        l_ref[...] = jnp.broadcast_to(jnp.where(first_lane, 1.0, 0.0), l_ref.shape)
        later_ref[...] = jnp.broadcast_to(lfc_ref[...], later_ref.shape)

    urow = lax.broadcasted_iota(I32, (LANES, LANES), 0)
    tcol = lax.broadcasted_iota(I32, (LANES, LANES), 1)
    after = jnp.where(urow > tcol, 1.0, 0.0).astype(BF16)
    for u in range(pages):
        lf = lf_refs[u][...]
        later = later_ref[...]
        bias = _dot_f32_right(lf, after) + later
        rows = [jnp.sum(k_refs[u][h] * qb_ref[h], axis=0, keepdims=True) for h in range(FOX_HEADS)]
        s = jnp.concatenate(rows, axis=0) + bias
        m_old = m_ref[...]
        m_new = jnp.maximum(m_old, jnp.max(s, axis=-1, keepdims=True))
        alpha = jnp.exp(m_old - m_new)
        p = jnp.exp(s - m_new)
        l_ref[...] = alpha * l_ref[...] + p
        m_ref[...] = m_new
        for h in range(FOX_HEADS):
            acc_ref[h] = alpha[h:h + 1, :] * acc_ref[h] + p[h:h + 1, :] * v_refs[u][h]
        later_ref[...] = later + jnp.sum(lf, axis=-1, keepdims=True)

    @pl.when(j == pl.num_programs(1) - 1)
    def _():
        l = jnp.sum(l_ref[...], axis=-1, keepdims=True)
        for h in range(FOX_HEADS):
            o_ref[h] = jnp.sum(acc_ref[h], axis=-1, keepdims=True) / l[h:h + 1, :]


def _fox_decode(q, k_cur, v_cur, logf_cur, cache_k, cache_v, cache_logf, page_table, *, pages):
    b, n_pages = page_table.shape
    page = cache_k.shape[1]
    assert page == LANES and n_pages % pages == 0
    kt = cache_k.transpose(0, 2, 3, 1)
    vt = cache_v.transpose(0, 2, 3, 1)
    lft = cache_logf.transpose(0, 2, 1)
    col = lambda x: x.reshape(b, FOX_HEADS, FOX_HEAD_DIM, 1)
    tok = pl.BlockSpec((None, FOX_HEADS, FOX_HEAD_DIM, 1), lambda bi, j, pt: (bi, 0, 0, 0))
    newest_first = lambda bi, j, pt, u: pt[bi, n_pages - 1 - (j * pages + u)]
    kv_page = lambda u: pl.BlockSpec((None, FOX_HEADS, FOX_HEAD_DIM, page),
                                     lambda bi, j, pt, u=u: (newest_first(bi, j, pt, u), 0, 0, 0))
    lf_page = lambda u: pl.BlockSpec((None, FOX_HEADS, page), lambda bi, j, pt, u=u: (newest_first(bi, j, pt, u), 0, 0))
    out = pl.pallas_call(
        functools.partial(_fox_decode_kernel, pages=pages),
        grid_spec=pltpu.PrefetchScalarGridSpec(
            num_scalar_prefetch=1,
            grid=(b, n_pages // pages),
            in_specs=[tok, tok, tok, pl.BlockSpec((None, FOX_HEADS, 1), lambda bi, j, pt: (bi, 0, 0))]
                     + [kv_page(u) for u in range(pages)] * 2 + [lf_page(u) for u in range(pages)],
            out_specs=tok,
            scratch_shapes=[pltpu.VMEM((FOX_HEADS, FOX_HEAD_DIM, LANES), F32), pltpu.VMEM((FOX_HEADS, LANES), F32),
                            pltpu.VMEM((FOX_HEADS, LANES), F32), pltpu.VMEM((FOX_HEADS, FOX_HEAD_DIM, LANES), F32),
                            pltpu.VMEM((FOX_HEADS, LANES), F32)],
        ),
        out_shape=jax.ShapeDtypeStruct((b, FOX_HEADS, FOX_HEAD_DIM, 1), F32),
        compiler_params=_cparams(("parallel", "arbitrary")),
        name="fox_decode",
    )(page_table, col(q), col(k_cur), col(v_cur), logf_cur.reshape(b, FOX_HEADS, 1),
      *([kt] * pages), *([vt] * pages), *([lft] * pages))
    return out.reshape(b, FOX_WIDTH)


_COL_Q = RWKV_PROJ
_COL_K = _COL_Q + FOX_WIDTH
_COL_V = _COL_K + FOX_WIDTH
_COL_FL = _COL_V + FOX_WIDTH
_EVEN_COLS = _COL_FL + LANES


def _even_weights(w_in):
    q, k, v = (w_in[:, i * FOX_WIDTH:(i + 1) * FOX_WIDTH] for i in range(3))
    fl = w_in[:, 3 * FOX_WIDTH:3 * FOX_WIDTH + FOX_HEADS]
    rw = w_in[:, 3 * FOX_WIDTH + FOX_HEADS:][:, _RWKV_PERM]
    fl_pad = jnp.pad(fl, ((0, 0), (0, LANES - FOX_HEADS)))
    return jnp.concatenate([rw, q, k, v, fl_pad], axis=1).astype(BF16)


def _even_layer(y, gain, w_cat, w_out, fb, rw, *, batch, seq, past, shift0, wkv0):
    n = y.shape[0]
    tm = min(512, n)
    fb_pad = jnp.pad(fb, (0, LANES - FOX_HEADS)).reshape(1, LANES)
    if past is None:
        proj, qkv = _norm_matmul(y, gain, w_cat, tm=512, tn=_EVEN_COLS, bf16_cols=(_COL_Q, _COL_FL))
        logf, ct = _fox_prep(proj, _COL_FL // LANES, fb_pad, t=256)
        k_t, v_t = _norm_matmul_t(y, gain, w_cat[:, _COL_K:_COL_FL].T, tm=1024, parts=2)
        tok_major = lambda x: x.reshape(FOX_HEADS, FOX_HEAD_DIM, n).transpose(2, 0, 1).reshape(
            batch, seq, FOX_HEADS, FOX_HEAD_DIM)
        k_out, v_out = tok_major(k_t), tok_major(v_t)
        o_fox = _fox_attn(qkv, ct, tq=1024)
        o_rwkv, shift_new, wkv_new = _rwkv_mix(proj, shift0[:, _RWKV_PERM], wkv0, rw, batch=batch,
                                               t_prep=256, t_scan=256, shifted=True)
    else:
        proj = _norm_matmul(y, gain, w_cat, tm=n, tn=_EVEN_COLS)
        logf, _ = _fox_prep(proj, _COL_FL // LANES, fb_pad, t=n)
        heads = lambda t: t.reshape(batch, seq, FOX_HEADS, FOX_HEAD_DIM)
        k_new = proj[:, _COL_K:_COL_V]
        v_new = proj[:, _COL_V:_COL_FL]
        k_out, v_out = heads(k_new), heads(v_new)
        cache_k, cache_v, cache_logf, page_table = past
        o_fox = _fox_decode(proj[:, _COL_Q:_COL_K], k_new, v_new, logf, cache_k, cache_v, cache_logf, page_table,
                            pages=32)
        o_rwkv, shift_new, wkv_new = _rwkv_mix(proj, shift0[:, _RWKV_PERM], wkv0, rw, batch=batch,
                                               t_prep=n, t_scan=1, shifted=False)
    w_out_b = w_out.astype(BF16)
    y = _matmul_res([o_fox, o_rwkv], [w_out_b[:FOX_WIDTH], w_out_b[FOX_WIDTH:]], y, tm=tm, tn=y.shape[1])
    return y, k_out, v_out, logf.reshape(batch, seq, FOX_HEADS), wkv_new, shift_new


def kernel(x_prompt, x_sample, cache_k, cache_v, cache_logf, page_table, state_wkv, state_shift, state_conv, state_ssm, norm_mix, norm_ffn, norm_final, w_in_even, w_out_even, fox_fb, rwkv_mu, rwkv_w0, rwkv_w2, rwkv_a0, rwkv_a2, rwkv_g2, rwkv_kk, rwkv_ka, rwkv_rk, rwkv_lnw, rwkv_lnb, w_in_odd, w_out_odd, ssm_conv_w, ssm_conv_b, ssm_dt_bias, ssm_a_log, ssm_d, ssm_norm_w, peer_wq, peer_keys, peer_u, peer_v):
    bp, seq_p, dm = x_prompt.shape
    bs, seq_s, _ = x_sample.shape
    assert bp == 1 and seq_s == 1
    depth = norm_mix.shape[0]
    ssm_heads = ssm_dt_bias.shape[1]
    yp = x_prompt.reshape(bp * seq_p, dm)
    ys = x_sample.reshape(bs * seq_s, dm)
    u_all = peer_u.astype(BF16)
    v_all = peer_v.astype(BF16)
    outs_p = {name: [] for name in ("k", "v", "logf", "wkv", "shift", "conv", "ssm")}
    outs_s = {name: [] for name in outs_p}
    for layer in range(depth):
        if layer % 2 == 0:
            e = layer // 2
            rw = (rwkv_mu[e], rwkv_w0[e], rwkv_w2[e], rwkv_a0[e], rwkv_a2[e], rwkv_g2[e],
                  rwkv_kk[e], rwkv_ka[e], rwkv_rk[e], rwkv_lnw[e], rwkv_lnb[e])
            w_cat = _even_weights(w_in_even[e])
            yp, *res_p = _even_layer(
                yp, norm_mix[layer], w_cat, w_out_even[e], fox_fb[e], rw, batch=bp, seq=seq_p, past=None,
                shift0=jnp.zeros((bp, RWKV_PROJ), F32),
                wkv0=jnp.zeros((bp, RWKV_HEADS, RWKV_HEAD_DIM, RWKV_HEAD_DIM), F32))
            ys, *res_s = _even_layer(
                ys, norm_mix[layer], w_cat, w_out_even[e], fox_fb[e], rw, batch=bs, seq=seq_s,
                past=(cache_k[e], cache_v[e], cache_logf[e], page_table), shift0=state_shift[e], wkv0=state_wkv[e])
            for outs, res in ((outs_p, res_p), (outs_s, res_s)):
                for name, val in zip(("k", "v", "logf", "wkv", "shift"), res):
                    outs[name].append(val)
        else:
            o = layer // 2
            mprm = (ssm_conv_w[o], ssm_conv_b[o], ssm_dt_bias[o], ssm_a_log[o], ssm_d[o], ssm_norm_w[o])
            w_cat = jnp.pad(w_in_odd[o], ((0, 0), (0, LANES - ssm_heads))).astype(BF16)
            w_out = w_out_odd[o].astype(BF16)
            tn = w_cat.shape[1]
            proj_p = _norm_matmul(yp, norm_mix[layer], w_cat, tm=256, tn=tn)
            y_m, conv_p, h_p = _ssd(proj_p, *mprm, heads=ssm_heads)
            yp = _matmul_res([y_m], [w_out], yp, tm=512, tn=dm)
            proj_s = _norm_matmul(ys, norm_mix[layer], w_cat, tm=bs, tn=tn)
            y_s, conv_s, h_s = _ssm_step(proj_s, state_conv[o], state_ssm[o], *mprm, heads=ssm_heads)
            ys = _matmul_res([y_s], [w_out], ys, tm=bs, tn=dm)
            outs_p["conv"].append(conv_p[None])
            outs_p["ssm"].append(h_p[None])
            outs_s["conv"].append(conv_s)
            outs_s["ssm"].append(h_s)
        wq = peer_wq[layer].astype(BF16)
        keys_pad = _peer_keys_padded(peer_keys[layer])
        yp = _peer(yp, norm_ffn[layer], wq, keys_pad, u_all, v_all, layer=layer,
                   tm_proj=512, tm_route=256, tm_mlp=512, te=2048)
        ys = _peer(ys, norm_ffn[layer], wq, keys_pad, u_all, v_all, layer=layer,
                   tm_proj=bs, tm_route=bs, tm_mlp=bs, te=2048)
    y_prompt = _rmsnorm(yp, norm_final, tm=512).reshape(bp, seq_p, dm)
    y_sample = _rmsnorm(ys, norm_final, tm=bs).reshape(bs, seq_s, dm)
    order = ("k", "v", "logf", "wkv", "shift", "conv", "ssm")
    return ((y_prompt, y_sample) + tuple(jnp.stack(outs_p[name]) for name in order)
            + tuple(jnp.stack(outs_s[name]) for name in order))
```
